```python
import jax, jax.numpy as jnp
from jax import lax
import numpy as np

D_MODEL = 2048
BATCH = 4
SEQ = 2048
DEPTH = 2

PLE_DIM = 256
N_EVEN = (DEPTH + 1) // 2
N_ODD = DEPTH // 2
CHUNK = 64
CONV_W = 4
EPS = 1e-6
A_DK = 128
A_WIDTH = D_MODEL // 2
A_HEADS = A_WIDTH // A_DK
A_DV = A_WIDTH // A_HEADS
A_QK = A_HEADS * A_DK
B_WIDTH = D_MODEL // 2
B_BLOCK = 128
B_BLOCKS = B_WIDTH // B_BLOCK
LRU_C = 8.0
C_DK = 128
C_DV = 256
C_WIDTH = D_MODEL
C_HEADS = C_WIDTH // C_DV
C_QK = C_HEADS * C_DK
EVEN_IN = 2 * A_QK + 2 * A_WIDTH + 2 * B_WIDTH
EVEN_MIX = A_WIDTH + B_WIDTH
ODD_IN = 2 * C_QK + 3 * C_WIDTH + 2 * C_HEADS
ODD_MIX = C_WIDTH

kernel_name = "hybrid_hgrn2_rglru_mlstm_trunk"


def rmsnorm(x, g):
    xf = x.astype(jnp.float32)
    y = xf * lax.rsqrt(jnp.mean(xf * xf, axis=-1, keepdims=True) + EPS)
    return (y * g.astype(jnp.float32)).astype(x.dtype)


def head_rmsnorm(y, g):
    b, s, h, d = y.shape
    yn = y * lax.rsqrt(jnp.mean(y * y, axis=-1, keepdims=True) + EPS)
    return (yn * g.astype(jnp.float32).reshape(h, d)).reshape(b, s, h * d)


def causal_conv(u, w, bias):
    k_w = w.shape[0]
    s = u.shape[1]
    up = jnp.pad(u, ((0, 0), (k_w - 1, 0), (0, 0)))
    out = bias
    for k in range(k_w):
        out = out + up[:, k:k + s, :] * w[k]
    return out


def to_chunks(t):
    b, s = t.shape[:2]
    t = t.reshape((b, s // CHUNK, CHUNK) + t.shape[2:])
    return jnp.moveaxis(t, (1, 3), (0, 2))


def from_chunks(t):
    nc, b, h, c, d = t.shape
    return t.transpose(1, 0, 3, 2, 4).reshape(b, nc * c, h, d)


def hgrn2(q_pre, f_pre, v, lb):
    bsz, s, h, dk = q_pre.shape
    dv = v.shape[-1]
    lb = lb.reshape(h, dk)
    fp = f_pre.astype(jnp.float32)
    q = jax.nn.silu(q_pre.astype(jnp.float32))
    f = lb + (1.0 - lb) * jax.nn.sigmoid(fp)
    log_f = jnp.log(f)
    k = (1.0 - lb) * jax.nn.sigmoid(-fp)
    causal = jnp.tril(jnp.ones((CHUNK, CHUNK), dtype=bool))

    def step(state, inp):
        qc, kc, vc, gc = inp
        b = jnp.cumsum(gc, axis=2)
        diff = b[:, :, :, None, :] - b[:, :, None, :, :]
        w = jnp.exp(jnp.where(causal[:, :, None], diff, -jnp.inf))
        attn = jnp.einsum('bhtd,bhsd,bhtsd->bhts', qc, kc, w)
        o = (jnp.einsum('bhtd,bhde->bhte', qc * jnp.exp(b), state)
             + jnp.einsum('bhts,bhse->bhte', attn, vc))
        b_last = b[:, :, -1:, :]
        new_state = (jnp.exp(b_last[:, :, 0, :, None]) * state
                     + jnp.einsum('bhsd,bhse->bhde', kc * jnp.exp(b_last - b), vc))
        return new_state, o

    s0 = jnp.zeros((bsz, h, dk, dv), jnp.float32)
    xs = (to_chunks(q), to_chunks(k), to_chunks(v.astype(jnp.float32)), to_chunks(log_f))
    _, o = lax.scan(step, s0, xs)
    return from_chunks(o)


def rg_lru(xc, w_r, b_r, w_i, b_i, lam):
    bsz, s, wdt = xc.shape
    xf = xc.astype(jnp.float32)
    xh = xf.reshape(bsz, s, B_BLOCKS, B_BLOCK)
    r = jax.nn.sigmoid(jnp.einsum('bsnc,ncd->bsnd', xh, w_r.astype(jnp.float32)).reshape(bsz, s, wdt)
                       + b_r.astype(jnp.float32))
    ig = jax.nn.sigmoid(jnp.einsum('bsnc,ncd->bsnd', xh, w_i.astype(jnp.float32)).reshape(bsz, s, wdt)
                        + b_i.astype(jnp.float32))
    log_a = -LRU_C * r * jax.nn.softplus(-lam.astype(jnp.float32))
    a = jnp.exp(log_a)
    u = jnp.sqrt(-jnp.expm1(2.0 * log_a)) * (ig * xf)

    def combine(left, right):
        a1, b1 = left
        a2, b2 = right
        return a1 * a2, a2 * b1 + b2

    _, hs = lax.associative_scan(combine, (a, u), axis=1)
    return hs


def mlstm(q, k, v, li, lf):
    bsz, s, h, dk = q.shape
    dv = v.shape[-1]
    q = q.astype(jnp.float32) * (dk ** -0.5)
    causal = jnp.tril(jnp.ones((CHUNK, CHUNK), dtype=bool))

    def step(carry, inp):
        c_st, n_st, m_st = carry
        qc, kc, vc, lic, lfc = inp
        b = jnp.cumsum(lfc, axis=-1)
        dmat = jnp.where(causal, b[..., :, None] - b[..., None, :] + lic[..., None, :], -jnp.inf)
        inter = b + m_st[..., None]
        m_t = jnp.maximum(jnp.max(dmat, axis=-1), inter)
        wts = jnp.exp(dmat - m_t[..., None])
        g_inter = jnp.exp(inter - m_t)
        sc = jnp.einsum('bhtd,bhsd->bhts', qc, kc) * wts
        num = (g_inter[..., None] * jnp.einsum('bhtd,bhde->bhte', qc, c_st)
               + jnp.einsum('bhts,bhse->bhte', sc, vc))
        den = g_inter * jnp.einsum('bhtd,bhd->bht', qc, n_st) + jnp.sum(sc, axis=-1)
        hout = num / jnp.maximum(jnp.abs(den), jnp.exp(-m_t))[..., None]
        b_last = b[..., -1]
        logw = b_last[..., None] - b + lic
        m_new = jnp.maximum(b_last + m_st, jnp.max(logw, axis=-1))
        decay = jnp.exp(b_last + m_st - m_new)
        kw = kc * jnp.exp(logw - m_new[..., None])[..., None]
        c_new = decay[..., None, None] * c_st + jnp.einsum('bhsd,bhse->bhde', kw, vc)
        n_new = decay[..., None] * n_st + jnp.sum(kw, axis=2)
        return (c_new, n_new, m_new), hout

    init = (jnp.zeros((bsz, h, dk, dv), jnp.float32),
            jnp.zeros((bsz, h, dk), jnp.float32),
            jnp.zeros((bsz, h), jnp.float32))
    xs = (to_chunks(q), to_chunks(k.astype(jnp.float32)), to_chunks(v.astype(jnp.float32)),
          to_chunks(li), to_chunks(lf))
    _, hs = lax.scan(step, init, xs)
    return from_chunks(hs)


def setup_inputs(seed: int = 0) -> dict:
    key = jax.random.key(seed)
    ks = jax.random.split(key, 32)

    def nrm(k, shape, scale):
        return jax.random.normal(k, shape, jnp.float32) * scale

    u = jax.random.uniform(ks[12], (N_EVEN, B_WIDTH), jnp.float32, minval=0.9, maxval=0.999)
    a0 = u ** (1.0 / LRU_C)
    return {
        "x": nrm(ks[0], (BATCH, SEQ, D_MODEL), 1.0),
        "p": nrm(ks[1], (DEPTH, BATCH, SEQ, PLE_DIM), 1.0),
        "e_norm": 1.0 + nrm(ks[2], (N_EVEN, D_MODEL), 0.02),
        "e_w_in": nrm(ks[3], (N_EVEN, D_MODEL, EVEN_IN), D_MODEL ** -0.5),
        "a_lb_logits": nrm(ks[4], (N_EVEN + 1, A_QK), 0.1),
        "a_norm": 1.0 + nrm(ks[5], (N_EVEN, A_WIDTH), 0.02),
        "b_conv_w": nrm(ks[6], (N_EVEN, CONV_W, B_WIDTH), CONV_W ** -0.5),
        "b_conv_b": nrm(ks[7], (N_EVEN, B_WIDTH), 0.01),
        "b_w_r": nrm(ks[8], (N_EVEN, B_BLOCKS, B_BLOCK, B_BLOCK), B_BLOCK ** -0.5),
        "b_b_r": nrm(ks[9], (N_EVEN, B_WIDTH), 0.1),
        "b_w_i": nrm(ks[10], (N_EVEN, B_BLOCKS, B_BLOCK, B_BLOCK), B_BLOCK ** -0.5),
        "b_b_i": nrm(ks[11], (N_EVEN, B_WIDTH), 0.1),
        "b_lambda": jnp.log(a0) - jnp.log1p(-a0),
        "e_w_out": nrm(ks[13], (N_EVEN, EVEN_MIX, D_MODEL), EVEN_MIX ** -0.5),
        "o_norm": 1.0 + nrm(ks[14], (N_ODD, D_MODEL), 0.02),
        "o_w_in": nrm(ks[15], (N_ODD, D_MODEL, ODD_IN), D_MODEL ** -0.5),
        "c_conv_w": nrm(ks[16], (N_ODD, CONV_W, 2 * C_QK), CONV_W ** -0.5),
        "c_conv_b": nrm(ks[17], (N_ODD, 2 * C_QK), 0.01),
        "c_b_i": nrm(ks[18], (N_ODD, C_HEADS), 0.1),
        "c_b_f": jnp.linspace(3.0, 6.0, C_HEADS, dtype=jnp.float32)[None, :] + nrm(ks[19], (N_ODD, C_HEADS), 0.1),
        "c_norm": 1.0 + nrm(ks[20], (N_ODD, C_WIDTH), 0.02),
        "o_w_out": nrm(ks[21], (N_ODD, ODD_MIX, D_MODEL), ODD_MIX ** -0.5),
        "ple_w": nrm(ks[22], (DEPTH, PLE_DIM, D_MODEL), PLE_DIM ** -0.5),
        "ple_norm": 1.0 + nrm(ks[23], (DEPTH, D_MODEL), 0.02),
        "ple_gate_w": nrm(ks[24], (DEPTH, D_MODEL, D_MODEL), D_MODEL ** -0.5),
        "final_norm": 1.0 + nrm(ks[25], (D_MODEL,), 0.02),
    }


def reference(x, p, e_norm, e_w_in, a_lb_logits, a_norm, b_conv_w, b_conv_b, b_w_r, b_b_r,
              b_w_i, b_b_i, b_lambda, e_w_out, o_norm, o_w_in, c_conv_w, c_conv_b, c_b_i,
              c_b_f, c_norm, o_w_out, ple_w, ple_norm, ple_gate_w, final_norm):
    bsz, s, _ = x.shape
    lbs = jnp.cumsum(jax.nn.softmax(a_lb_logits.astype(jnp.float32), axis=0), axis=0)
    h = x
    for i in range(DEPTH):
        j = i // 2
        hn = rmsnorm(h, e_norm[j] if i % 2 == 0 else o_norm[j])
        if i % 2 == 0:
            u = hn @ e_w_in[j]
            o0 = 0
            qa = u[..., o0:o0 + A_QK]; o0 += A_QK
            fa = u[..., o0:o0 + A_QK]; o0 += A_QK
            ia = u[..., o0:o0 + A_WIDTH]; o0 += A_WIDTH
            za = u[..., o0:o0 + A_WIDTH]; o0 += A_WIDTH
            xb = u[..., o0:o0 + B_WIDTH]; o0 += B_WIDTH
            zb = u[..., o0:o0 + B_WIDTH]
            ya = hgrn2(qa.reshape(bsz, s, A_HEADS, A_DK), fa.reshape(bsz, s, A_HEADS, A_DK),
                       ia.reshape(bsz, s, A_HEADS, A_DV), lbs[j])
            ya = head_rmsnorm(ya, a_norm[j]) * jax.nn.silu(za.astype(jnp.float32))
            xc = causal_conv(xb, b_conv_w[j], b_conv_b[j])
            yb = rg_lru(xc, b_w_r[j], b_b_r[j], b_w_i[j], b_b_i[j], b_lambda[j])
            yb = yb * jax.nn.silu(zb.astype(jnp.float32))
            y = jnp.concatenate([ya, yb], axis=-1).astype(h.dtype)
            mix = y @ e_w_out[j]
        else:
            u = hn @ o_w_in[j]
            o0 = 0
            qk = u[..., o0:o0 + 2 * C_QK]; o0 += 2 * C_QK
            v = u[..., o0:o0 + C_WIDTH]; o0 += C_WIDTH
            og = u[..., o0:o0 + C_WIDTH]; o0 += C_WIDTH
            z = u[..., o0:o0 + C_WIDTH]; o0 += C_WIDTH
            ig = u[..., o0:o0 + C_HEADS]; o0 += C_HEADS
            fg = u[..., o0:o0 + C_HEADS]
            qk = jax.nn.silu(causal_conv(qk, c_conv_w[j], c_conv_b[j]))
            q = qk[..., :C_QK].reshape(bsz, s, C_HEADS, C_DK)
            k = qk[..., C_QK:].reshape(bsz, s, C_HEADS, C_DK)
            li = ig.astype(jnp.float32) + c_b_i[j].astype(jnp.float32)
            lf = jax.nn.log_sigmoid(fg.astype(jnp.float32) + c_b_f[j].astype(jnp.float32))
            yc = mlstm(q, k, v.reshape(bsz, s, C_HEADS, C_DV), li, lf)
            yc = (head_rmsnorm(yc, c_norm[j]) * jax.nn.sigmoid(og.astype(jnp.float32))
                  * jax.nn.silu(z.astype(jnp.float32)))
            mix = yc.astype(h.dtype) @ o_w_out[j]
        h = h + mix
        pl = rmsnorm(p[i].astype(h.dtype) @ ple_w[i], ple_norm[i])
        h = h + jax.nn.sigmoid(h @ ple_gate_w[i]) * pl
    return rmsnorm(h, final_norm)
```

```python
import functools

import jax
import jax.numpy as jnp
from jax import lax
from jax.experimental import pallas as pl
from jax.experimental.pallas import tpu as pltpu

EPS = 1e-6
LRU_C = 8.0
CONV_W = 4
LANES = 128
SUBLANES = 8
VMEM_LIMIT = 56 * 1024 * 1024
BF16 = jnp.bfloat16
F32 = jnp.float32


def _sigmoid(x):
    return jax.nn.sigmoid(x)


def _silu(x):
    return x * jax.nn.sigmoid(x)


def _softplus(x):
    return jnp.maximum(x, 0.0) + jnp.log1p(jnp.exp(-jnp.abs(x)))


def _log_sigmoid(x):
    return -_softplus(-x)


def _shift_rows(x, s, row):
    return jnp.where(row >= s, pltpu.roll(x, s, 0), 0.0)


def _cumsum_rows(x, row):
    s = 1
    while s < x.shape[0]:
        x = x + _shift_rows(x, s, row)
        s *= 2
    return x


def _cummax_rows(x, row):
    s = 1
    while s < x.shape[0]:
        x = jnp.maximum(x, jnp.where(row >= s, pltpu.roll(x, s, 0), -jnp.inf))
        s *= 2
    return x


def _dot(a, b):
    return jnp.dot(a.astype(BF16), b.astype(BF16), preferred_element_type=F32)


def _dot_nt(a, b):
    return lax.dot_general(a.astype(BF16), b.astype(BF16), (((1,), (1,)), ((), ())),
                           preferred_element_type=F32)


def _dot_tn(a, b):
    return lax.dot_general(a.astype(BF16), b.astype(BF16), (((0,), (0,)), ((), ())),
                           preferred_element_type=F32)


NORM_ROWS = 256


def _in_proj_kernel(x_ref, g_ref, w_ref, *rest, with_gates):
    if with_gates:
        wg_ref, o_ref, og_ref, hn_ref = rest
    else:
        o_ref, hn_ref = rest

    @pl.when(pl.program_id(1) == 0)
    def _():
        def body(i, carry):
            r0 = pl.multiple_of(i * NORM_ROWS, NORM_ROWS)
            x = x_ref[pl.ds(r0, NORM_ROWS), :]
            ms = jnp.mean(x * x, axis=-1, keepdims=True)
            hn_ref[pl.ds(r0, NORM_ROWS), :] = (x * lax.rsqrt(ms + EPS) * g_ref[...]).astype(BF16)
            return carry
        lax.fori_loop(0, x_ref.shape[0] // NORM_ROWS, body, 0)
        if with_gates:
            og_ref[...] = jnp.dot(hn_ref[...], wg_ref[...], preferred_element_type=F32)

    o_ref[...] = jnp.dot(hn_ref[...], w_ref[...], preferred_element_type=F32)


def _in_proj(x, g, w, wg=None, *, tm=1024, tn=512):
    m, d = x.shape
    n = w.shape[1]
    with_gates = wg is not None
    in_specs = [
        pl.BlockSpec((tm, d), lambda i, j: (i, 0)),
        pl.BlockSpec((1, d), lambda i, j: (0, 0)),
        pl.BlockSpec((d, tn), lambda i, j: (0, j)),
    ]
    out_shape = [jax.ShapeDtypeStruct((m, n), F32)]
    out_specs = [pl.BlockSpec((tm, tn), lambda i, j: (i, j))]
    args = [x, g.reshape(1, d), w]
    if with_gates:
        in_specs.append(pl.BlockSpec((d, LANES), lambda i, j: (0, 0)))
        out_shape.append(jax.ShapeDtypeStruct((m, LANES), F32))
        out_specs.append(pl.BlockSpec((tm, LANES), lambda i, j: (i, 0)))
        args.append(wg)
    res = pl.pallas_call(
        functools.partial(_in_proj_kernel, with_gates=with_gates),
        grid=(m // tm, n // tn),
        in_specs=in_specs,
        out_specs=out_specs,
        out_shape=out_shape,
        scratch_shapes=[pltpu.VMEM((tm, d), BF16)],
        compiler_params=pltpu.CompilerParams(
            dimension_semantics=("parallel", "arbitrary"), vmem_limit_bytes=VMEM_LIMIT),
        name="in_proj_gates" if with_gates else "in_proj",
    )(*args)
    return res if with_gates else res[0]


HG_CHUNK = 64
HG_SUB = SUBLANES


def _hgrn2_kernel(q_ref, f_ref, i_ref, z_ref, lbl_ref, an_ref, o_ref, st_ref, *, layer):
    s_len = q_ref.shape[1]
    dk = q_ref.shape[2]
    dv = i_ref.shape[2]
    c = HG_CHUNK
    nsub = c // HG_SUB

    lg = lbl_ref[...]
    e = jnp.exp(lg - jnp.max(lg, axis=0, keepdims=True))
    sm = e / jnp.sum(e, axis=0, keepdims=True)
    lb = jnp.sum(sm[:layer + 1], axis=0, keepdims=True)
    an = an_ref[...]

    row = lax.broadcasted_iota(jnp.int32, (c, dk), 0)
    rowc = lax.broadcasted_iota(jnp.int32, (c, c), 0)
    colc = lax.broadcasted_iota(jnp.int32, (c, c), 1)
    grp = row // HG_SUB
    sub_pos = row % HG_SUB
    off_mask = (colc // HG_SUB) < (rowc // HG_SUB)

    st_ref[...] = jnp.zeros_like(st_ref)

    def chunk(ci, carry):
        r0 = pl.multiple_of(ci * c, c)
        qp = q_ref[0, pl.ds(r0, c), :]
        fp = f_ref[0, pl.ds(r0, c), :]
        v = i_ref[0, pl.ds(r0, c), :]
        z = z_ref[0, pl.ds(r0, c), :]
        q = _silu(qp)
        f = lb + (1.0 - lb) * _sigmoid(fp)
        k = (1.0 - lb) * _sigmoid(-fp)
        b = _cumsum_rows(jnp.log(f), row)
        st = st_ref[...]

        o = _dot_nt(q * jnp.exp(b), st)

        b_prev = _shift_rows(b, 1, row)
        beta = jnp.broadcast_to(b_prev.reshape(nsub, HG_SUB, dk)[:, 0:1, :],
                                (nsub, HG_SUB, dk)).reshape(c, dk)
        qd = q * jnp.exp(b - beta)
        q_parts, k_parts = [], []
        for i in range(1, nsub):
            beta_i = b[i * HG_SUB - 1:i * HG_SUB, :]
            q_parts.append(jnp.where(grp == i, qd, 0.0).astype(BF16))
            k_parts.append((k * jnp.exp(jnp.minimum(beta_i - b, 0.0))).astype(BF16))
        a_off = _dot_nt(jnp.concatenate(q_parts, axis=1), jnp.concatenate(k_parts, axis=1))
        attn = jnp.where(off_mask, a_off, 0.0)

        for d in range(HG_SUB):
            if d == 0:
                p = q * k
            else:
                ksh = pltpu.roll(k, d, 0)
                bsh = pltpu.roll(b, d, 0)
                p = jnp.where(sub_pos >= d, q * ksh * jnp.exp(jnp.minimum(b - bsh, 0.0)), 0.0)
            dsum = jnp.sum(p, axis=1, keepdims=True)
            attn = attn + jnp.where(colc == rowc - d, dsum, 0.0)

        o = o + _dot(attn, v)

        yn = o * lax.rsqrt(jnp.mean(o * o, axis=-1, keepdims=True) + EPS) * an
        o_ref[0, pl.ds(r0, c), :] = (yn * _silu(z)).astype(o_ref.dtype)

        b_last = b[c - 1:c, :]
        kt = k * jnp.exp(b_last - b)
        st_ref[...] = st * jnp.exp(b_last) + _dot_tn(v, kt)
        return carry

    lax.fori_loop(0, s_len // c, chunk, 0)


def _hgrn2(u, lb_logits, a_norm, *, layer, heads, dk):
    bsz, s_len, _ = u.shape
    nl = lb_logits.shape[0]
    blk = lambda off: pl.BlockSpec((1, s_len, dk), lambda b, h, off=off: (b, 0, off + h))
    return pl.pallas_call(
        functools.partial(_hgrn2_kernel, layer=layer),
        grid=(bsz, heads),
        in_specs=[blk(0), blk(heads), blk(2 * heads), blk(3 * heads),
                  pl.BlockSpec((nl, dk), lambda b, h: (0, h)),
                  pl.BlockSpec((1, dk), lambda b, h: (0, h))],
        out_specs=pl.BlockSpec((1, s_len, dk), lambda b, h: (b, 0, h)),
        out_shape=jax.ShapeDtypeStruct((bsz, s_len, heads * dk), BF16),
        scratch_shapes=[pltpu.VMEM((dk, dk), F32)],
        compiler_params=pltpu.CompilerParams(
            dimension_semantics=("parallel", "parallel"), vmem_limit_bytes=VMEM_LIMIT),
        name="hgrn2",
    )(u, u, u, u, lb_logits, a_norm.reshape(1, -1))


def _rglru_kernel(x_ref, z_ref, cw_ref, cb_ref, wr_ref, br_ref, wi_ref, bi_ref, lam_ref,
                  o_ref, a_ref, u_ref, h_ref):
    s_len = x_ref.shape[1]
    w = x_ref.shape[2]
    row = lax.broadcasted_iota(jnp.int32, (s_len, w), 0)
    x = x_ref[0]
    xc = cb_ref[...] + x * cw_ref[CONV_W - 1:CONV_W, :]
    for j in range(1, CONV_W):
        xc = xc + _shift_rows(x, j, row) * cw_ref[CONV_W - 1 - j:CONV_W - j, :]
    r = _sigmoid(_dot(xc, wr_ref[0]) + br_ref[...])
    ig = _sigmoid(_dot(xc, wi_ref[0]) + bi_ref[...])
    log_a = -LRU_C * r * _softplus(-lam_ref[...])
    a = jnp.exp(log_a)
    u = jnp.sqrt(-jnp.tanh(log_a) * (a * a + 1.0)) * (ig * xc)

    pos = row % SUBLANES
    s = 1
    while s < SUBLANES:
        m = pos >= s
        u = u + a * jnp.where(m, pltpu.roll(u, s, 0), 0.0)
        a = a * jnp.where(m, pltpu.roll(a, s, 0), 1.0)
        s *= 2
    a_ref[...] = a
    u_ref[...] = u

    def group(gi, carry):
        r0 = pl.multiple_of(gi * SUBLANES, SUBLANES)
        hg = a_ref[pl.ds(r0, SUBLANES), :] * carry + u_ref[pl.ds(r0, SUBLANES), :]
        h_ref[pl.ds(r0, SUBLANES), :] = hg
        return jnp.broadcast_to(hg[SUBLANES - 1:SUBLANES, :], (SUBLANES, w))

    lax.fori_loop(0, s_len // SUBLANES, group, jnp.zeros((SUBLANES, w), F32), unroll=8)
    o_ref[0] = (h_ref[...] * _silu(z_ref[0])).astype(o_ref.dtype)


def _rglru(u, conv_w, conv_b, w_r, b_r, w_i, b_i, lam, *, x_off, z_off):
    bsz, s_len, _ = u.shape
    nblk, blk, _ = w_r.shape
    vec = lambda rows: pl.BlockSpec((rows, blk), lambda b, n: (0, n))
    mat = pl.BlockSpec((1, blk, blk), lambda b, n: (n, 0, 0))
    return pl.pallas_call(
        _rglru_kernel,
        grid=(bsz, nblk),
        in_specs=[pl.BlockSpec((1, s_len, blk), lambda b, n: (b, 0, x_off + n)),
                  pl.BlockSpec((1, s_len, blk), lambda b, n: (b, 0, z_off + n)),
                  vec(CONV_W), vec(1), mat, vec(1), mat, vec(1), vec(1)],
        out_specs=pl.BlockSpec((1, s_len, blk), lambda b, n: (b, 0, n)),
        out_shape=jax.ShapeDtypeStruct((bsz, s_len, nblk * blk), BF16),
        scratch_shapes=[pltpu.VMEM((s_len, blk), F32)] * 3,
        compiler_params=pltpu.CompilerParams(
            dimension_semantics=("parallel", "parallel"), vmem_limit_bytes=VMEM_LIMIT),
        name="rglru",
    )(u, u, conv_w, conv_b.reshape(1, -1), w_r, b_r.reshape(1, -1), w_i, b_i.reshape(1, -1),
      lam.reshape(1, -1))


ML_CHUNK = 128


def _mlstm_kernel(q_ref, k_ref, v_ref, og_ref, z_ref, g_ref, gb_ref, cwq_ref, cbq_ref,
                  cwk_ref, cbk_ref, cn_ref, o_ref, qs_ref, ks_ref, cst_ref, *, heads):
    s_len = q_ref.shape[1]
    dk = q_ref.shape[2]
    dv = v_ref.shape[2]
    c = ML_CHUNK
    assert c == dk == LANES
    h = pl.program_id(1)

    rows = lax.broadcasted_iota(jnp.int32, (s_len, dk), 0)

    def conv_silu(x, cw_ref, cb_ref):
        acc = cb_ref[...] + x * cw_ref[CONV_W - 1:CONV_W, :]
        for j in range(1, CONV_W):
            acc = acc + _shift_rows(x, j, rows) * cw_ref[CONV_W - 1 - j:CONV_W - j, :]
        return _silu(acc)

    qs_ref[...] = conv_silu(q_ref[0], cwq_ref, cbq_ref) * (dk ** -0.5)
    ks_ref[...] = conv_silu(k_ref[0], cwk_ref, cbk_ref)
    cst_ref[...] = jnp.zeros_like(cst_ref)

    row = lax.broadcasted_iota(jnp.int32, (c, c), 0)
    col = lax.broadcasted_iota(jnp.int32, (c, c), 1)
    causal = col <= row
    cn = cn_ref[...]
    gb = gb_ref[...]

    def chunk(ci, carry):
        n_st, m_prev = carry
        r0 = pl.multiple_of(ci * c, c)
        q = qs_ref[pl.ds(r0, c), :]
        k = ks_ref[pl.ds(r0, c), :]
        v = v_ref[0, pl.ds(r0, c), :]
        t = g_ref[0, pl.ds(r0, c), :] + gb
        li = jnp.sum(jnp.where(col == h, t, 0.0), axis=1, keepdims=True)
        fpre = jnp.sum(jnp.where(col == h + heads, t, 0.0), axis=1, keepdims=True)
        lf = _log_sigmoid(fpre)
        b = _cumsum_rows(jnp.broadcast_to(lf, (c, c)), row)
        r = jnp.broadcast_to(li, (c, c)) - b
        rho = _cummax_rows(r, row)
        r_row = r.T
        mu = jnp.maximum(rho, m_prev)
        wts = jnp.where(causal, jnp.exp(jnp.minimum(r_row - mu, 0.0)), 0.0)
        g_inter = jnp.exp(m_prev - mu)[:, 0:1]
        e_negm = jnp.exp(-(b + mu))[:, 0:1]

        cst = cst_ref[...]
        sc = _dot_nt(q, k) * wts
        num = g_inter * _dot(q, cst) + _dot(sc, v)
        den = (g_inter * jnp.sum(q * n_st, axis=1, keepdims=True)
               + jnp.sum(sc, axis=1, keepdims=True))
        hout = num / jnp.maximum(jnp.abs(den), e_negm)

        yn = hout * lax.rsqrt(jnp.mean(hout * hout, axis=-1, keepdims=True) + EPS) * cn
        og = og_ref[0, pl.ds(r0, c), :]
        z = z_ref[0, pl.ds(r0, c), :]
        o_ref[0, pl.ds(r0, c), :] = (yn * _sigmoid(og) * _silu(z)).astype(o_ref.dtype)

        b_last = b[c - 1:c, :]
        m_new = b_last + jnp.maximum(m_prev, rho[c - 1:c, :])
        decay = jnp.exp(b_last + m_prev - m_new)
        kw = k * jnp.exp(b_last + r - m_new)
        cst_ref[...] = decay[:, 0:1] * cst + _dot_tn(kw, v)
        n_new = decay * n_st + jnp.sum(kw, axis=0, keepdims=True)
        return n_new, m_new

    lax.fori_loop(0, s_len // c, chunk,
                  (jnp.zeros((1, dk), F32), jnp.zeros((1, c), F32)))


def _mlstm(u, gates, gate_bias, conv_w, conv_b, c_norm, *, heads, dk, dv):
    bsz, s_len, _ = u.shape
    qk_blk = lambda off: pl.BlockSpec((1, s_len, dk), lambda b, h, off=off: (b, 0, off + h))
    v_blk = lambda off: pl.BlockSpec((1, s_len, dv), lambda b, h, off=off: (b, 0, off + h))
    cw = lambda off: pl.BlockSpec((CONV_W, dk), lambda b, h, off=off: (0, off + h))
    cb = lambda off: pl.BlockSpec((1, dk), lambda b, h, off=off: (0, off + h))
    nqk = 2 * heads * dk // dv
    return pl.pallas_call(
        functools.partial(_mlstm_kernel, heads=heads),
        grid=(bsz, heads),
        in_specs=[qk_blk(0), qk_blk(heads),
                  v_blk(nqk), v_blk(nqk + heads), v_blk(nqk + 2 * heads),
                  pl.BlockSpec((1, s_len, LANES), lambda b, h: (b, 0, 0)),
                  pl.BlockSpec((1, LANES), lambda b, h: (0, 0)),
                  cw(0), cb(0), cw(heads), cb(heads),
                  pl.BlockSpec((1, dv), lambda b, h: (0, h))],
        out_specs=pl.BlockSpec((1, s_len, dv), lambda b, h: (b, 0, h)),
        out_shape=jax.ShapeDtypeStruct((bsz, s_len, heads * dv), BF16),
        scratch_shapes=[pltpu.VMEM((s_len, dk), F32), pltpu.VMEM((s_len, dk), F32),
                        pltpu.VMEM((dk, dv), F32)],
        compiler_params=pltpu.CompilerParams(
            dimension_semantics=("parallel", "parallel"), vmem_limit_bytes=VMEM_LIMIT),
        name="mlstm",
    )(u, u, u, u, u, gates, gate_bias, conv_w, conv_b.reshape(1, -1), conv_w,
      conv_b.reshape(1, -1), c_norm.reshape(1, -1))


def _out_kernel(h_ref, *rest, n_y, final):
    y_refs = rest[:n_y]
    wo_refs = rest[n_y:2 * n_y]
    p_ref, pw_ref, pn_ref, gw_ref = rest[2 * n_y:2 * n_y + 4]
    rest = rest[2 * n_y + 4:]
    if final:
        fn_ref, o_ref = rest
    else:
        (o_ref,) = rest
    mix = jnp.dot(y_refs[0][...], wo_refs[0][...], preferred_element_type=F32)
    for y_ref, wo_ref in zip(y_refs[1:], wo_refs[1:]):
        mix = mix + jnp.dot(y_ref[...], wo_ref[...], preferred_element_type=F32)
    h1 = h_ref[...] + mix
    pe = _dot(p_ref[...], pw_ref[...])
    pe = pe * lax.rsqrt(jnp.mean(pe * pe, axis=-1, keepdims=True) + EPS) * pn_ref[...]
    h2 = h1 + _sigmoid(_dot(h1, gw_ref[...])) * pe
    if final:
        h2 = h2 * lax.rsqrt(jnp.mean(h2 * h2, axis=-1, keepdims=True) + EPS) * fn_ref[...]
    o_ref[...] = h2


def _out_proj(h, ys, wos, p, ple_w, ple_norm, gate_w, final_norm=None, *, tm=256):
    m, d = h.shape
    final = final_norm is not None
    const = lambda shape: pl.BlockSpec(shape, lambda i: (0, 0), pipeline_mode=pl.Buffered(1))
    in_specs = [pl.BlockSpec((tm, d), lambda i: (i, 0))]
    in_specs += [pl.BlockSpec((tm, y.shape[1]), lambda i: (i, 0)) for y in ys]
    in_specs += [const(w.shape) for w in wos]
    in_specs += [pl.BlockSpec((tm, p.shape[1]), lambda i: (i, 0)),
                 const(ple_w.shape), const((1, d)), const(gate_w.shape)]
    args = [h, *ys, *wos, p, ple_w, ple_norm.reshape(1, d), gate_w]
    if final:
        in_specs.append(const((1, d)))
        args.append(final_norm.reshape(1, d))
    return pl.pallas_call(
        functools.partial(_out_kernel, n_y=len(ys), final=final),
        grid=(m // tm,),
        in_specs=in_specs,
        out_specs=pl.BlockSpec((tm, d), lambda i: (i, 0)),
        out_shape=jax.ShapeDtypeStruct((m, d), F32),
        compiler_params=pltpu.CompilerParams(
            dimension_semantics=("parallel",), vmem_limit_bytes=VMEM_LIMIT),
        name="out_proj_final" if final else "out_proj",
    )(*args)


def kernel(x, p, e_norm, e_w_in, a_lb_logits, a_norm, b_conv_w, b_conv_b, b_w_r, b_b_r, b_w_i, b_b_i, b_lambda, e_w_out, o_norm, o_w_in, c_conv_w, c_conv_b, c_b_i, c_b_f, c_norm, o_w_out, ple_w, ple_norm, ple_gate_w, final_norm):
    bsz, s_len, d = x.shape
    depth = p.shape[0]
    m = bsz * s_len
    a_width = a_norm.shape[1]
    b_width = b_lambda.shape[1]
    a_dk = LANES
    a_heads = a_width // a_dk
    c_heads = c_b_i.shape[1]
    c_dv = c_norm.shape[1] // c_heads
    c_dk = c_conv_w.shape[2] // (2 * c_heads)
    c_main = o_w_in.shape[2] - 2 * c_heads

    h = x.reshape(m, d)
    for i in range(depth):
        j = i // 2
        last = i == depth - 1
        if i % 2 == 0:
            u = _in_proj(h, e_norm[j], e_w_in[j].astype(BF16)).reshape(bsz, s_len, -1)
            ya = _hgrn2(u, a_lb_logits, a_norm[j], layer=j, heads=a_heads, dk=a_dk)
            x_off = (2 * a_heads * a_dk + 2 * a_width) // LANES
            yb = _rglru(u, b_conv_w[j], b_conv_b[j], b_w_r[j].astype(BF16), b_b_r[j],
                        b_w_i[j].astype(BF16), b_b_i[j], b_lambda[j],
                        x_off=x_off, z_off=x_off + b_width // LANES)
            w_out = e_w_out[j].astype(BF16)
            ys = [ya.reshape(m, a_width), yb.reshape(m, b_width)]
            wos = [w_out[:a_width], w_out[a_width:]]
        else:
            w_in = o_w_in[j]
            wg = jnp.pad(w_in[:, c_main:], ((0, 0), (0, LANES - 2 * c_heads))).astype(BF16)
            u, gates = _in_proj(h, o_norm[j], w_in[:, :c_main].astype(BF16), wg)
            gate_bias = jnp.pad(jnp.concatenate([c_b_i[j], c_b_f[j]]),
                                (0, LANES - 2 * c_heads)).reshape(1, LANES)
            yc = _mlstm(u.reshape(bsz, s_len, -1), gates.reshape(bsz, s_len, LANES), gate_bias,
                        c_conv_w[j], c_conv_b[j], c_norm[j], heads=c_heads, dk=c_dk, dv=c_dv)
            ys = [yc.reshape(m, -1)]
            wos = [o_w_out[j].astype(BF16)]
        h = _out_proj(h, ys, wos, p[i].reshape(m, -1), ple_w[i].astype(BF16), ple_norm[i],
                      ple_gate_w[i].astype(BF16), final_norm if last else None)
    return h.reshape(bsz, s_len, d)
```

```python
import functools

import jax
import jax.numpy as jnp
from jax import lax
from jax.experimental import pallas as pl
from jax.experimental.pallas import tpu as pltpu

EPS = 1e-6
LRU_C = 8.0
CONV_W = 4
LANES = 128
SUBLANES = 8
VMEM_LIMIT = 56 * 1024 * 1024
BF16 = jnp.bfloat16
F32 = jnp.float32


def _sigmoid(x):
    return jax.nn.sigmoid(x)


def _silu(x):
    return x * jax.nn.sigmoid(x)


def _softplus(x):
    return jnp.maximum(x, 0.0) + jnp.log1p(jnp.exp(-jnp.abs(x)))


def _log_sigmoid(x):
    return -_softplus(-x)


def _shift_rows(x, s, row):
    return jnp.where(row >= s, pltpu.roll(x, s, 0), 0.0)


def _cumsum_rows(x, row):
    s = 1
    while s < x.shape[0]:
        x = x + _shift_rows(x, s, row)
        s *= 2
    return x


def _cummax_rows(x, row):
    s = 1
    while s < x.shape[0]:
        x = jnp.maximum(x, jnp.where(row >= s, pltpu.roll(x, s, 0), -jnp.inf))
        s *= 2
    return x


def _dot(a, b):
    return jnp.dot(a.astype(BF16), b.astype(BF16), preferred_element_type=F32)


def _dot_nt(a, b):
    return lax.dot_general(a.astype(BF16), b.astype(BF16), (((1,), (1,)), ((), ())),
                           preferred_element_type=F32)


def _dot_tn(a, b):
    return lax.dot_general(a.astype(BF16), b.astype(BF16), (((0,), (0,)), ((), ())),
                           preferred_element_type=F32)


NORM_ROWS = 256


def _in_proj_kernel(x_ref, g_ref, w_ref, *rest, with_gates):
    if with_gates:
        wg_ref, o_ref, og_ref, hn_ref = rest
    else:
        o_ref, hn_ref = rest

    @pl.when(pl.program_id(1) == 0)
    def _():
        def body(i, carry):
            r0 = pl.multiple_of(i * NORM_ROWS, NORM_ROWS)
            x = x_ref[pl.ds(r0, NORM_ROWS), :]
            ms = jnp.mean(x * x, axis=-1, keepdims=True)
            hn_ref[pl.ds(r0, NORM_ROWS), :] = (x * lax.rsqrt(ms + EPS) * g_ref[...]).astype(BF16)
            return carry
        lax.fori_loop(0, x_ref.shape[0] // NORM_ROWS, body, 0)
        if with_gates:
            og_ref[...] = jnp.dot(hn_ref[...], wg_ref[...], preferred_element_type=F32)

    o_ref[...] = jnp.dot(hn_ref[...], w_ref[...], preferred_element_type=F32)


def _in_proj(x, g, w, wg=None, *, tm=1024, tn=512):
    m, d = x.shape
    n = w.shape[1]
    with_gates = wg is not None
    in_specs = [
        pl.BlockSpec((tm, d), lambda i, j: (i, 0)),
        pl.BlockSpec((1, d), lambda i, j: (0, 0)),
        pl.BlockSpec((d, tn), lambda i, j: (0, j)),
    ]
    out_shape = [jax.ShapeDtypeStruct((m, n), F32)]
    out_specs = [pl.BlockSpec((tm, tn), lambda i, j: (i, j))]
    args = [x, g.reshape(1, d), w]
    if with_gates:
        gw = wg.shape[1]
        in_specs.append(pl.BlockSpec((d, gw), lambda i, j: (0, 0)))
        out_shape.append(jax.ShapeDtypeStruct((m, gw), F32))
        out_specs.append(pl.BlockSpec((tm, gw), lambda i, j: (i, 0)))
        args.append(wg)
    res = pl.pallas_call(
        functools.partial(_in_proj_kernel, with_gates=with_gates),
        grid=(m // tm, n // tn),
        in_specs=in_specs,
        out_specs=out_specs,
        out_shape=out_shape,
        scratch_shapes=[pltpu.VMEM((tm, d), BF16)],
        compiler_params=pltpu.CompilerParams(
            dimension_semantics=("parallel", "arbitrary"), vmem_limit_bytes=VMEM_LIMIT),
        name="in_proj_gates" if with_gates else "in_proj",
    )(*args)
    return res if with_gates else res[0]


HG_CHUNK = 64
HG_SUB = SUBLANES


def _hgrn2_kernel(q_ref, f_ref, i_ref, z_ref, lbl_ref, an_ref, o_ref, st_ref, *, layer):
    s_len = q_ref.shape[1]
    dk = q_ref.shape[2]
    dv = i_ref.shape[2]
    c = HG_CHUNK
    nsub = c // HG_SUB

    lg = lbl_ref[...]
    e = jnp.exp(lg - jnp.max(lg, axis=0, keepdims=True))
    sm = e / jnp.sum(e, axis=0, keepdims=True)
    lb = jnp.sum(sm[:layer + 1], axis=0, keepdims=True)
    an = an_ref[...]

    row = lax.broadcasted_iota(jnp.int32, (c, dk), 0)
    rowc = lax.broadcasted_iota(jnp.int32, (c, c), 0)
    colc = lax.broadcasted_iota(jnp.int32, (c, c), 1)
    grp = row // HG_SUB
    sub_pos = row % HG_SUB
    off_mask = (colc // HG_SUB) < (rowc // HG_SUB)

    st_ref[...] = jnp.zeros_like(st_ref)

    def chunk(ci, carry):
        r0 = pl.multiple_of(ci * c, c)
        qp = q_ref[0, pl.ds(r0, c), :]
        fp = f_ref[0, pl.ds(r0, c), :]
        v = i_ref[0, pl.ds(r0, c), :]
        z = z_ref[0, pl.ds(r0, c), :]
        q = _silu(qp)
        f = lb + (1.0 - lb) * _sigmoid(fp)
        k = (1.0 - lb) * _sigmoid(-fp)
        b = _cumsum_rows(jnp.log(f), row)
        st = st_ref[...]

        o = _dot_nt(q * jnp.exp(b), st)

        b_prev = _shift_rows(b, 1, row)
        beta = jnp.broadcast_to(b_prev.reshape(nsub, HG_SUB, dk)[:, 0:1, :],
                                (nsub, HG_SUB, dk)).reshape(c, dk)
        qd = q * jnp.exp(b - beta)
        q_parts, k_parts = [], []
        for i in range(1, nsub):
            beta_i = b[i * HG_SUB - 1:i * HG_SUB, :]
            q_parts.append(jnp.where(grp == i, qd, 0.0).astype(BF16))
            k_parts.append((k * jnp.exp(jnp.minimum(beta_i - b, 0.0))).astype(BF16))
        a_off = _dot_nt(jnp.concatenate(q_parts, axis=1), jnp.concatenate(k_parts, axis=1))
        attn = jnp.where(off_mask, a_off, 0.0)

        for d in range(HG_SUB):
            if d == 0:
                p = q * k
            else:
                ksh = pltpu.roll(k, d, 0)
                bsh = pltpu.roll(b, d, 0)
                p = jnp.where(sub_pos >= d, q * ksh * jnp.exp(jnp.minimum(b - bsh, 0.0)), 0.0)
            dsum = jnp.sum(p, axis=1, keepdims=True)
            attn = attn + jnp.where(colc == rowc - d, dsum, 0.0)

        o = o + _dot(attn, v)

        yn = o * lax.rsqrt(jnp.mean(o * o, axis=-1, keepdims=True) + EPS) * an
        o_ref[0, pl.ds(r0, c), :] = (yn * _silu(z)).astype(o_ref.dtype)

        b_last = b[c - 1:c, :]
        kt = k * jnp.exp(b_last - b)
        st_ref[...] = st * jnp.exp(b_last) + _dot_tn(v, kt)
        return carry

    lax.fori_loop(0, s_len // c, chunk, 0, unroll=4)


def _hgrn2(u, lb_logits, a_norm, *, layer, heads, dk):
    bsz, s_len, _ = u.shape
    nl = lb_logits.shape[0]
    blk = lambda off: pl.BlockSpec((1, s_len, dk), lambda b, h, off=off: (b, 0, off + h))
    return pl.pallas_call(
        functools.partial(_hgrn2_kernel, layer=layer),
        grid=(bsz, heads),
        in_specs=[blk(0), blk(heads), blk(2 * heads), blk(3 * heads),
                  pl.BlockSpec((nl, dk), lambda b, h: (0, h)),
                  pl.BlockSpec((1, dk), lambda b, h: (0, h))],
        out_specs=pl.BlockSpec((1, s_len, dk), lambda b, h: (b, 0, h)),
        out_shape=jax.ShapeDtypeStruct((bsz, s_len, heads * dk), BF16),
        scratch_shapes=[pltpu.VMEM((dk, dk), F32)],
        compiler_params=pltpu.CompilerParams(
            dimension_semantics=("parallel", "parallel"), vmem_limit_bytes=VMEM_LIMIT),
        name="hgrn2",
    )(u, u, u, u, lb_logits, a_norm.reshape(1, -1))


def _rglru_kernel(x_ref, z_ref, cw_ref, cb_ref, wr_ref, br_ref, wi_ref, bi_ref, lam_ref,
                  o_ref, a_ref, u_ref, h_ref):
    s_len = x_ref.shape[1]
    w = x_ref.shape[2]
    row = lax.broadcasted_iota(jnp.int32, (s_len, w), 0)
    x = x_ref[0]
    xc = cb_ref[...] + x * cw_ref[CONV_W - 1:CONV_W, :]
    for j in range(1, CONV_W):
        xc = xc + _shift_rows(x, j, row) * cw_ref[CONV_W - 1 - j:CONV_W - j, :]
    r = _sigmoid(_dot(xc, wr_ref[0]) + br_ref[...])
    ig = _sigmoid(_dot(xc, wi_ref[0]) + bi_ref[...])
    log_a = -LRU_C * r * _softplus(-lam_ref[...])
    a = jnp.exp(log_a)
    u = jnp.sqrt(-jnp.tanh(log_a) * (a * a + 1.0)) * (ig * xc)

    pos = row % SUBLANES
    s = 1
    while s < SUBLANES:
        m = pos >= s
        u = u + a * jnp.where(m, pltpu.roll(u, s, 0), 0.0)
        a = a * jnp.where(m, pltpu.roll(a, s, 0), 1.0)
        s *= 2
    a_ref[...] = a
    u_ref[...] = u

    def group(gi, carry):
        r0 = pl.multiple_of(gi * SUBLANES, SUBLANES)
        hg = a_ref[pl.ds(r0, SUBLANES), :] * carry + u_ref[pl.ds(r0, SUBLANES), :]
        h_ref[pl.ds(r0, SUBLANES), :] = hg
        return jnp.broadcast_to(hg[SUBLANES - 1:SUBLANES, :], (SUBLANES, w))

    lax.fori_loop(0, s_len // SUBLANES, group, jnp.zeros((SUBLANES, w), F32), unroll=8)
    o_ref[0] = (h_ref[...] * _silu(z_ref[0])).astype(o_ref.dtype)


def _rglru(u, conv_w, conv_b, w_r, b_r, w_i, b_i, lam, *, x_off, z_off):
    bsz, s_len, _ = u.shape
    nblk, blk, _ = w_r.shape
    vec = lambda rows: pl.BlockSpec((rows, blk), lambda b, n: (0, n))
    mat = pl.BlockSpec((1, blk, blk), lambda b, n: (n, 0, 0))
    return pl.pallas_call(
        _rglru_kernel,
        grid=(bsz, nblk),
        in_specs=[pl.BlockSpec((1, s_len, blk), lambda b, n: (b, 0, x_off + n)),
                  pl.BlockSpec((1, s_len, blk), lambda b, n: (b, 0, z_off + n)),
                  vec(CONV_W), vec(1), mat, vec(1), mat, vec(1), vec(1)],
        out_specs=pl.BlockSpec((1, s_len, blk), lambda b, n: (b, 0, n)),
        out_shape=jax.ShapeDtypeStruct((bsz, s_len, nblk * blk), BF16),
        scratch_shapes=[pltpu.VMEM((s_len, blk), F32)] * 3,
        compiler_params=pltpu.CompilerParams(
            dimension_semantics=("parallel", "parallel"), vmem_limit_bytes=VMEM_LIMIT),
        name="rglru",
    )(u, u, conv_w, conv_b.reshape(1, -1), w_r, b_r.reshape(1, -1), w_i, b_i.reshape(1, -1),
      lam.reshape(1, -1))


ML_CHUNK = 128


def _mlstm_gates_kernel(g_ref, gb_ref, mu_ref, gi_ref, en_ref, ks_ref, dec_ref, rt_ref):
    s_len = g_ref.shape[1]
    c = ML_CHUNK
    t = g_ref[0] + gb_ref[...]
    li = t[:, :LANES]
    lf = _log_sigmoid(t[:, LANES:])
    row = lax.broadcasted_iota(jnp.int32, (s_len, LANES), 0) % c

    def scan(x, op, fill):
        s = 1
        while s < c:
            x = op(x, jnp.where(row >= s, pltpu.roll(x, s, 0), fill))
            s *= 2
        return x

    b = scan(lf, jnp.add, 0.0)
    r = li - b
    rho = scan(r, jnp.maximum, -jnp.inf)
    m = jnp.zeros((1, LANES), F32)
    for ci in range(s_len // c):
        sl = slice(ci * c, (ci + 1) * c)
        b_c, r_c, rho_c = b[sl], r[sl], rho[sl]
        b_last = b_c[c - 1:c, :]
        m_new = b_last + jnp.maximum(m, rho_c[c - 1:c, :])
        mu = jnp.maximum(rho_c, m)
        mu_ref[0, sl, :] = mu
        gi_ref[0, sl, :] = jnp.exp(m - mu)
        en_ref[0, sl, :] = jnp.exp(-(b_c + mu))
        ks_ref[0, sl, :] = jnp.exp(b_last + r_c - m_new)
        dec_ref[0, ci:ci + 1, :] = jnp.exp(b_last + m - m_new)
        rt_ref[0, :, sl] = r_c.T
        m = m_new


def _mlstm_gates(gates, gate_bias):
    bsz, s_len, gw = gates.shape
    nc = s_len // ML_CHUNK
    col = jax.ShapeDtypeStruct((bsz, s_len, LANES), F32)
    col_spec = pl.BlockSpec((1, s_len, LANES), lambda b: (b, 0, 0))
    return pl.pallas_call(
        _mlstm_gates_kernel,
        grid=(bsz,),
        in_specs=[pl.BlockSpec((1, s_len, gw), lambda b: (b, 0, 0)),
                  pl.BlockSpec((1, gw), lambda b: (0, 0))],
        out_specs=[col_spec] * 4 + [pl.BlockSpec((1, nc, LANES), lambda b: (b, 0, 0)),
                                    pl.BlockSpec((1, LANES, s_len), lambda b: (b, 0, 0))],
        out_shape=[col] * 4 + [jax.ShapeDtypeStruct((bsz, nc, LANES), F32),
                               jax.ShapeDtypeStruct((bsz, LANES, s_len), F32)],
        compiler_params=pltpu.CompilerParams(
            dimension_semantics=("parallel",), vmem_limit_bytes=VMEM_LIMIT),
        name="mlstm_gates",
    )(gates, gate_bias)


def _mlstm_kernel(q_ref, k_ref, v_ref, og_ref, z_ref, mu_ref, gi_ref, en_ref, ks_ref, dec_ref,
                  rt_ref, cwq_ref, cbq_ref, cwk_ref, cbk_ref, cn_ref, o_ref, qs_ref, ks_s_ref,
                  cst_ref, pad_ref):
    s_len = q_ref.shape[1]
    dk = q_ref.shape[2]
    c = ML_CHUNK
    assert c == dk == LANES
    h = pl.program_id(1)

    def conv_silu(x_ref, cw_ref, cb_ref):
        pad_ref[0:SUBLANES, :] = jnp.zeros((SUBLANES, dk), F32)
        pad_ref[SUBLANES:, :] = x_ref[0]
        acc = cb_ref[...] + x_ref[0] * cw_ref[CONV_W - 1:CONV_W, :]
        for j in range(1, CONV_W):
            acc = acc + (pad_ref[SUBLANES - j:SUBLANES - j + s_len, :]
                         * cw_ref[CONV_W - 1 - j:CONV_W - j, :])
        return _silu(acc)

    qs_ref[...] = conv_silu(q_ref, cwq_ref, cbq_ref) * (dk ** -0.5)
    ks_s_ref[...] = conv_silu(k_ref, cwk_ref, cbk_ref)
    cst_ref[...] = jnp.zeros_like(cst_ref)

    row = lax.broadcasted_iota(jnp.int32, (c, c), 0)
    col = lax.broadcasted_iota(jnp.int32, (c, c), 1)
    causal = col <= row
    head = col == h
    cn = cn_ref[...]

    def pick(x):
        return jnp.sum(jnp.where(head, x, 0.0), axis=1, keepdims=True)

    dec_all = dec_ref[0]
    chunk_id = lax.broadcasted_iota(jnp.int32, dec_all.shape, 0)
    head_lane = lax.broadcasted_iota(jnp.int32, (1, LANES), 1) == h
    head_row = lax.broadcasted_iota(jnp.int32, (SUBLANES, c), 0) == h

    def chunk(ci, n_st):
        r0 = pl.multiple_of(ci * c, c)
        q = qs_ref[pl.ds(r0, c), :]
        k = ks_s_ref[pl.ds(r0, c), :]
        v = v_ref[0, pl.ds(r0, c), :]
        mu = pick(mu_ref[0, pl.ds(r0, c), :])
        g_inter = pick(gi_ref[0, pl.ds(r0, c), :])
        e_negm = pick(en_ref[0, pl.ds(r0, c), :])
        kscale = pick(ks_ref[0, pl.ds(r0, c), :])
        decay = jnp.sum(jnp.where(chunk_id == ci, dec_all, 0.0), axis=0, keepdims=True)
        decay = jnp.sum(jnp.where(head_lane, decay, 0.0), axis=1, keepdims=True)
        r_row = jnp.sum(jnp.where(head_row, rt_ref[0, :, pl.ds(r0, c)], 0.0),
                        axis=0, keepdims=True)
        wts = jnp.where(causal, jnp.exp(jnp.minimum(r_row - mu, 0.0)), 0.0)

        cst = cst_ref[...]
        sc = _dot_nt(q, k) * wts
        num = g_inter * _dot(q, cst) + _dot(sc, v)
        den = (g_inter * jnp.sum(q * n_st, axis=1, keepdims=True)
               + jnp.sum(sc, axis=1, keepdims=True))
        hout = num / jnp.maximum(jnp.abs(den), e_negm)

        yn = hout * lax.rsqrt(jnp.mean(hout * hout, axis=-1, keepdims=True) + EPS) * cn
        og = og_ref[0, pl.ds(r0, c), :]
        z = z_ref[0, pl.ds(r0, c), :]
        o_ref[0, pl.ds(r0, c), :] = (yn * _sigmoid(og) * _silu(z)).astype(o_ref.dtype)

        kw = k * kscale
        cst_ref[...] = decay * cst + _dot_tn(kw, v)
        return decay * n_st + jnp.sum(kw, axis=0, keepdims=True)

    lax.fori_loop(0, s_len // c, chunk, jnp.zeros((1, dk), F32), unroll=4)


def _mlstm(u, gate_terms, conv_w, conv_b, c_norm, *, heads, dk, dv):
    bsz, s_len, _ = u.shape
    nc = s_len // ML_CHUNK
    col_spec = pl.BlockSpec((1, s_len, LANES), lambda b, h: (b, 0, 0))
    qk_blk = lambda off: pl.BlockSpec((1, s_len, dk), lambda b, h, off=off: (b, 0, off + h))
    v_blk = lambda off: pl.BlockSpec((1, s_len, dv), lambda b, h, off=off: (b, 0, off + h))
    cw = lambda off: pl.BlockSpec((CONV_W, dk), lambda b, h, off=off: (0, off + h))
    cb = lambda off: pl.BlockSpec((1, dk), lambda b, h, off=off: (0, off + h))
    nqk = 2 * heads * dk // dv
    return pl.pallas_call(
        _mlstm_kernel,
        grid=(bsz, heads),
        in_specs=[qk_blk(0), qk_blk(heads),
                  v_blk(nqk), v_blk(nqk + heads), v_blk(nqk + 2 * heads),
                  col_spec, col_spec, col_spec, col_spec,
                  pl.BlockSpec((1, nc, LANES), lambda b, h: (b, 0, 0)),
                  pl.BlockSpec((1, SUBLANES, s_len), lambda b, h: (b, 0, 0)),
                  cw(0), cb(0), cw(heads), cb(heads),
                  pl.BlockSpec((1, dv), lambda b, h: (0, h))],
        out_specs=pl.BlockSpec((1, s_len, dv), lambda b, h: (b, 0, h)),
        out_shape=jax.ShapeDtypeStruct((bsz, s_len, heads * dv), BF16),
        scratch_shapes=[pltpu.VMEM((s_len, dk), F32), pltpu.VMEM((s_len, dk), F32),
                        pltpu.VMEM((dk, dv), F32), pltpu.VMEM((s_len + SUBLANES, dk), F32)],
        compiler_params=pltpu.CompilerParams(
            dimension_semantics=("parallel", "parallel"), vmem_limit_bytes=VMEM_LIMIT),
        name="mlstm",
    )(u, u, u, u, u, *gate_terms, conv_w, conv_b.reshape(1, -1), conv_w,
      conv_b.reshape(1, -1), c_norm.reshape(1, -1))


def _out_kernel(h_ref, *rest, n_y, final):
    y_refs = rest[:n_y]
    wo_refs = rest[n_y:2 * n_y]
    p_ref, pw_ref, pn_ref, gw_ref = rest[2 * n_y:2 * n_y + 4]
    rest = rest[2 * n_y + 4:]
    if final:
        fn_ref, o_ref = rest
    else:
        (o_ref,) = rest
    mix = jnp.dot(y_refs[0][...], wo_refs[0][...], preferred_element_type=F32)
    for y_ref, wo_ref in zip(y_refs[1:], wo_refs[1:]):
        mix = mix + jnp.dot(y_ref[...], wo_ref[...], preferred_element_type=F32)
    h1 = h_ref[...] + mix
    pe = _dot(p_ref[...], pw_ref[...])
    pe = pe * lax.rsqrt(jnp.mean(pe * pe, axis=-1, keepdims=True) + EPS) * pn_ref[...]
    h2 = h1 + _sigmoid(_dot(h1, gw_ref[...])) * pe
    if final:
        h2 = h2 * lax.rsqrt(jnp.mean(h2 * h2, axis=-1, keepdims=True) + EPS) * fn_ref[...]
    o_ref[...] = h2


def _out_proj(h, ys, wos, p, ple_w, ple_norm, gate_w, final_norm=None, *, tm=256):
    m, d = h.shape
    final = final_norm is not None
    const = lambda shape: pl.BlockSpec(shape, lambda i: (0, 0), pipeline_mode=pl.Buffered(1))
    in_specs = [pl.BlockSpec((tm, d), lambda i: (i, 0))]
    in_specs += [pl.BlockSpec((tm, y.shape[1]), lambda i: (i, 0)) for y in ys]
    in_specs += [const(w.shape) for w in wos]
    in_specs += [pl.BlockSpec((tm, p.shape[1]), lambda i: (i, 0)),
                 const(ple_w.shape), const((1, d)), const(gate_w.shape)]
    args = [h, *ys, *wos, p, ple_w, ple_norm.reshape(1, d), gate_w]
    if final:
        in_specs.append(const((1, d)))
        args.append(final_norm.reshape(1, d))
    return pl.pallas_call(
        functools.partial(_out_kernel, n_y=len(ys), final=final),
        grid=(m // tm,),
        in_specs=in_specs,
        out_specs=pl.BlockSpec((tm, d), lambda i: (i, 0)),
        out_shape=jax.ShapeDtypeStruct((m, d), F32),
        compiler_params=pltpu.CompilerParams(
            dimension_semantics=("parallel",), vmem_limit_bytes=VMEM_LIMIT),
        name="out_proj_final" if final else "out_proj",
    )(*args)


def kernel(x, p, e_norm, e_w_in, a_lb_logits, a_norm, b_conv_w, b_conv_b, b_w_r, b_b_r, b_w_i, b_b_i, b_lambda, e_w_out, o_norm, o_w_in, c_conv_w, c_conv_b, c_b_i, c_b_f, c_norm, o_w_out, ple_w, ple_norm, ple_gate_w, final_norm):
    bsz, s_len, d = x.shape
    depth = p.shape[0]
    m = bsz * s_len
    a_width = a_norm.shape[1]
    b_width = b_lambda.shape[1]
    a_dk = LANES
    a_heads = a_width // a_dk
    c_heads = c_b_i.shape[1]
    c_dv = c_norm.shape[1] // c_heads
    c_dk = c_conv_w.shape[2] // (2 * c_heads)
    c_main = o_w_in.shape[2] - 2 * c_heads

    h = x.reshape(m, d)
    for i in range(depth):
        j = i // 2
        last = i == depth - 1
        if i % 2 == 0:
            u = _in_proj(h, e_norm[j], e_w_in[j].astype(BF16)).reshape(bsz, s_len, -1)
            ya = _hgrn2(u, a_lb_logits, a_norm[j], layer=j, heads=a_heads, dk=a_dk)
            x_off = (2 * a_heads * a_dk + 2 * a_width) // LANES
            yb = _rglru(u, b_conv_w[j], b_conv_b[j], b_w_r[j].astype(BF16), b_b_r[j],
                        b_w_i[j].astype(BF16), b_b_i[j], b_lambda[j],
                        x_off=x_off, z_off=x_off + b_width // LANES)
            w_out = e_w_out[j].astype(BF16)
            ys = [ya.reshape(m, a_width), yb.reshape(m, b_width)]
            wos = [w_out[:a_width], w_out[a_width:]]
        else:
            w_in = o_w_in[j]
            pad = ((0, 0), (0, LANES - c_heads))
            wg = jnp.concatenate([jnp.pad(w_in[:, c_main:c_main + c_heads], pad),
                                  jnp.pad(w_in[:, c_main + c_heads:], pad)], axis=1).astype(BF16)
            u, gates = _in_proj(h, o_norm[j], w_in[:, :c_main].astype(BF16), wg)
            gate_bias = jnp.concatenate([jnp.pad(c_b_i[j], pad[1]),
                                         jnp.pad(c_b_f[j], pad[1])]).reshape(1, 2 * LANES)
            gate_terms = _mlstm_gates(gates.reshape(bsz, s_len, 2 * LANES), gate_bias)
            yc = _mlstm(u.reshape(bsz, s_len, -1), gate_terms, c_conv_w[j], c_conv_b[j],
                        c_norm[j], heads=c_heads, dk=c_dk, dv=c_dv)
            ys = [yc.reshape(m, -1)]
            wos = [o_w_out[j].astype(BF16)]
        h = _out_proj(h, ys, wos, p[i].reshape(m, -1), ple_w[i].astype(BF16), ple_norm[i],
                      ple_gate_w[i].astype(BF16), final_norm if last else None)
    return h.reshape(bsz, s_len, d)
```

```python
import functools

import jax
import jax.numpy as jnp
from jax import lax
from jax.experimental import pallas as pl
from jax.experimental.pallas import tpu as pltpu

EPS = 1e-6
LRU_C = 8.0
CONV_W = 4
LANES = 128
SUBLANES = 8
VMEM_LIMIT = 56 * 1024 * 1024
BF16 = jnp.bfloat16
F32 = jnp.float32


def _sigmoid(x):
    return jax.nn.sigmoid(x)


def _silu(x):
    return x * jax.nn.sigmoid(x)


def _softplus(x):
    return jnp.maximum(x, 0.0) + jnp.log1p(jnp.exp(-jnp.abs(x)))


def _log_sigmoid(x):
    return -_softplus(-x)


def _shift_rows(x, s, row):
    return jnp.where(row >= s, pltpu.roll(x, s, 0), 0.0)


def _cumsum_rows(x, row):
    s = 1
    while s < x.shape[0]:
        x = x + _shift_rows(x, s, row)
        s *= 2
    return x


def _cummax_rows(x, row):
    s = 1
    while s < x.shape[0]:
        x = jnp.maximum(x, jnp.where(row >= s, pltpu.roll(x, s, 0), -jnp.inf))
        s *= 2
    return x


def _dot(a, b):
    return jnp.dot(a.astype(BF16), b.astype(BF16), preferred_element_type=F32)


def _dot_nt(a, b):
    return lax.dot_general(a.astype(BF16), b.astype(BF16), (((1,), (1,)), ((), ())),
                           preferred_element_type=F32)


def _dot_tn(a, b):
    return lax.dot_general(a.astype(BF16), b.astype(BF16), (((0,), (0,)), ((), ())),
                           preferred_element_type=F32)


NORM_ROWS = 256


def _in_proj_kernel(x_ref, g_ref, w_ref, *rest, with_gates):
    if with_gates:
        wg_ref, o_ref, og_ref, hn_ref = rest
    else:
        o_ref, hn_ref = rest

    @pl.when(pl.program_id(1) == 0)
    def _():
        def body(i, carry):
            r0 = pl.multiple_of(i * NORM_ROWS, NORM_ROWS)
            x = x_ref[pl.ds(r0, NORM_ROWS), :]
            ms = jnp.mean(x * x, axis=-1, keepdims=True)
            hn_ref[pl.ds(r0, NORM_ROWS), :] = (x * lax.rsqrt(ms + EPS) * g_ref[...]).astype(BF16)
            return carry
        lax.fori_loop(0, x_ref.shape[0] // NORM_ROWS, body, 0)
        if with_gates:
            og_ref[...] = jnp.dot(hn_ref[...], wg_ref[...], preferred_element_type=F32)

    o_ref[...] = jnp.dot(hn_ref[...], w_ref[...], preferred_element_type=F32)


def _in_proj(x, g, w, wg=None, *, n=None, tm=1024, tn=512):
    m, d = x.shape
    n = w.shape[1] if n is None else n
    with_gates = wg is not None
    in_specs = [
        pl.BlockSpec((tm, d), lambda i, j: (i, 0)),
        pl.BlockSpec((1, d), lambda i, j: (0, 0)),
        pl.BlockSpec((d, tn), lambda i, j: (0, j)),
    ]
    out_shape = [jax.ShapeDtypeStruct((m, n), F32)]
    out_specs = [pl.BlockSpec((tm, tn), lambda i, j: (i, j))]
    args = [x, g.reshape(1, d), w]
    if with_gates:
        gw = wg.shape[1]
        in_specs.append(pl.BlockSpec((d, gw), lambda i, j: (0, 0)))
        out_shape.append(jax.ShapeDtypeStruct((m, gw), F32))
        out_specs.append(pl.BlockSpec((tm, gw), lambda i, j: (i, 0)))
        args.append(wg)
    res = pl.pallas_call(
        functools.partial(_in_proj_kernel, with_gates=with_gates),
        grid=(m // tm, n // tn),
        in_specs=in_specs,
        out_specs=out_specs,
        out_shape=out_shape,
        scratch_shapes=[pltpu.VMEM((tm, d), BF16)],
        compiler_params=pltpu.CompilerParams(
            dimension_semantics=("parallel", "arbitrary"), vmem_limit_bytes=VMEM_LIMIT),
        name="in_proj_gates" if with_gates else "in_proj",
    )(*args)
    return res if with_gates else res[0]


HG_CHUNK = 128
HG_GROUPS = HG_CHUNK // SUBLANES
HG_INFLIGHT = 4


def _hgrn2_kernel(q_ref, f_ref, i_ref, z_ref, lbl_ref, an_ref, o_ref,
                  st_ref, qs_ref, ks_ref, bs_ref, od_ref, *, layer):
    s_len = q_ref.shape[1]
    dk = q_ref.shape[2]
    c = HG_CHUNK
    ng = HG_GROUPS
    assert c == dk == LANES and i_ref.shape[2] == dk

    lg = lbl_ref[...]
    e = jnp.exp(lg - jnp.max(lg, axis=0, keepdims=True))
    sm = e / jnp.sum(e, axis=0, keepdims=True)
    lb = jnp.sum(sm[:layer + 1], axis=0, keepdims=True)
    an = an_ref[...]

    rowc = lax.broadcasted_iota(jnp.int32, (c, c), 0)
    colc = lax.broadcasted_iota(jnp.int32, (c, c), 1)
    tri = (colc <= rowc).astype(BF16)
    same_group = (colc % ng) == (rowc % ng)
    lane = lax.broadcasted_iota(jnp.int32, (SUBLANES, c), 1)
    diag_off = lane - lax.broadcasted_iota(jnp.int32, (SUBLANES, c), 0)
    zeros8 = jnp.zeros((SUBLANES, dk), F32)
    zeros16 = jnp.zeros((ng, dk), F32)

    st_ref[...] = jnp.zeros_like(st_ref)

    def rows8(x, r):
        return x[r * SUBLANES:(r + 1) * SUBLANES, :]

    def chunk(ci, slot):
        qs, ks, bs, od = qs_ref.at[slot], ks_ref.at[slot], bs_ref.at[slot], od_ref.at[slot]
        r0 = pl.multiple_of(ci * c, c)
        qp = q_ref[0, pl.ds(r0, c), :]
        fp = f_ref[0, pl.ds(r0, c), :]
        v = i_ref[0, pl.ds(r0, c), :]
        q = _silu(qp)
        f = lb + (1.0 - lb) * _sigmoid(fp)
        k = (1.0 - lb) * _sigmoid(-fp)

        lf = jnp.log2(f)
        hi = lf.astype(BF16)
        r1 = lf - hi.astype(F32)
        mid = r1.astype(BF16)
        lo = (r1 - mid.astype(F32)).astype(BF16)
        bb = jnp.dot(tri, jnp.concatenate([hi, mid, lo], axis=1), preferred_element_type=F32)
        yield
        b2 = bb[:, :dk] + bb[:, dk:2 * dk] + bb[:, 2 * dk:]
        qs[...] = q
        ks[...] = k
        bs[...] = b2
        b_last = bs[c - 1:c, :]
        upd = _dot_tn(v, k * jnp.exp2(b_last - b2))

        ps, rights = [], []
        for j in range(ng.bit_length() - 1):
            lhs, rhs, rights_j = [], [], []
            for r in range(ng):
                mid_grp = ((r >> (j + 1)) << (j + 1)) | ((1 << j) - 1)
                beta = bs[mid_grp * SUBLANES + SUBLANES - 1:(mid_grp + 1) * SUBLANES, :]
                if (r >> j) & 1:
                    lhs.append(rows8(q, r) * jnp.exp2(rows8(b2, r) - beta))
                    rhs.append(zeros8)
                    rights_j.append(r)
                else:
                    rhs.append(rows8(k, r) * jnp.exp2(beta - rows8(b2, r)))
            ps.append(_dot_nt(jnp.concatenate(lhs, axis=0), jnp.concatenate(rhs, axis=0)))
            rights.append(rights_j)
        yield

        st = st_ref[...]
        o = _dot_nt(q * jnp.exp2(b2), st)
        st_ref[...] = st * jnp.exp2(b_last) + upd

        def permuted(ref):
            return [ref[pl.ds(pos, ng, stride=SUBLANES), :] for pos in range(SUBLANES)]

        q_p, k_p, b_p = permuted(qs), permuted(ks), permuted(bs)
        lhs_slots, rhs_slots = [], []
        for j in range(SUBLANES.bit_length() - 1):
            for blk in range(SUBLANES >> (j + 1)):
                mid_pos = (blk << (j + 1)) | ((1 << j) - 1)
                lhs, rhs = [], []
                for pos in range(SUBLANES):
                    if pos >> (j + 1) != blk:
                        lhs.append(zeros16)
                        rhs.append(zeros16)
                    elif (pos >> j) & 1:
                        lhs.append(q_p[pos] * jnp.exp2(b_p[pos] - b_p[mid_pos]))
                        rhs.append(zeros16)
                    else:
                        lhs.append(zeros16)
                        rhs.append(k_p[pos] * jnp.exp2(b_p[mid_pos] - b_p[pos]))
                lhs_slots.append(jnp.concatenate(lhs, axis=0).astype(BF16))
                rhs_slots.append(jnp.concatenate(rhs, axis=0).astype(BF16))
        pd = _dot_nt(jnp.concatenate(lhs_slots, axis=1), jnp.concatenate(rhs_slots, axis=1))
        yield

        dg = jnp.sum(q * k, axis=1, keepdims=True)
        a_rows = [jnp.where(diag_off == r * SUBLANES, rows8(dg, r), 0.0) for r in range(ng)]
        for j, (p, rights_j) in enumerate(zip(ps, rights)):
            for i, r in enumerate(rights_j):
                right_start = (((r >> (j + 1)) << (j + 1)) | (1 << j)) * SUBLANES
                a_rows[r] = jnp.where(lane < right_start, rows8(p, i), a_rows[r])
        o = o + _dot(jnp.concatenate(a_rows, axis=0), v)
        v_p = [i_ref[0, pl.ds(r0 + pos, ng, stride=SUBLANES), :] for pos in range(SUBLANES)]
        od[...] = _dot(jnp.where(same_group, pd, 0.0), jnp.concatenate(v_p, axis=0))
        yield

        o = o + jnp.concatenate(
            [od[pl.ds(g, SUBLANES, stride=ng), :] for g in range(ng)], axis=0)
        yn = o * lax.rsqrt(jnp.mean(o * o, axis=-1, keepdims=True) + EPS) * an
        z = z_ref[0, pl.ds(r0, c), :]
        o_ref[0, pl.ds(r0, c), :] = (yn * _silu(z)).astype(o_ref.dtype)

    def chunks(i, carry):
        stages = [chunk(i * HG_INFLIGHT + slot, slot) for slot in range(HG_INFLIGHT)]
        while stages:
            stages = [g for g in stages if next(g, stages) is not stages]
        return carry

    lax.fori_loop(0, s_len // (c * HG_INFLIGHT), chunks, 0)


def _hgrn2(u, lb_logits, a_norm, *, layer, heads, dk):
    bsz, s_len, _ = u.shape
    nl = lb_logits.shape[0]
    blk = lambda off: pl.BlockSpec((1, s_len, dk), lambda b, h, off=off: (b, 0, off + h))
    return pl.pallas_call(
        functools.partial(_hgrn2_kernel, layer=layer),
        grid=(bsz, heads),
        in_specs=[blk(0), blk(heads), blk(2 * heads), blk(3 * heads),
                  pl.BlockSpec((nl, dk), lambda b, h: (0, h)),
                  pl.BlockSpec((1, dk), lambda b, h: (0, h))],
        out_specs=pl.BlockSpec((1, s_len, dk), lambda b, h: (b, 0, h)),
        out_shape=jax.ShapeDtypeStruct((bsz, s_len, heads * dk), BF16),
        scratch_shapes=[pltpu.VMEM((dk, dk), F32)]
        + [pltpu.VMEM((HG_INFLIGHT, HG_CHUNK, dk), F32)] * 4,
        compiler_params=pltpu.CompilerParams(
            dimension_semantics=("parallel", "parallel"), vmem_limit_bytes=VMEM_LIMIT),
        name="hgrn2",
    )(u, u, u, u, lb_logits, a_norm.reshape(1, -1))


def _rglru_kernel(x_ref, z_ref, cw_ref, cb_ref, wr_ref, br_ref, wi_ref, bi_ref, lam_ref,
                  o_ref, a_ref, u_ref, h_ref):
    s_len = x_ref.shape[1]
    w = x_ref.shape[2]
    row = lax.broadcasted_iota(jnp.int32, (s_len, w), 0)
    x = x_ref[0]
    xc = cb_ref[...] + x * cw_ref[CONV_W - 1:CONV_W, :]
    for j in range(1, CONV_W):
        xc = xc + _shift_rows(x, j, row) * cw_ref[CONV_W - 1 - j:CONV_W - j, :]
    r = _sigmoid(_dot(xc, wr_ref[0]) + br_ref[...])
    ig = _sigmoid(_dot(xc, wi_ref[0]) + bi_ref[...])
    log_a = -LRU_C * r * _softplus(-lam_ref[...])
    a = jnp.exp(log_a)
    u = jnp.sqrt(-jnp.tanh(log_a) * (a * a + 1.0)) * (ig * xc)

    pos = row % SUBLANES
    s = 1
    while s < SUBLANES:
        m = pos >= s
        u = u + a * jnp.where(m, pltpu.roll(u, s, 0), 0.0)
        a = a * jnp.where(m, pltpu.roll(a, s, 0), 1.0)
        s *= 2
    a_ref[...] = a
    u_ref[...] = u

    def group(gi, carry):
        r0 = pl.multiple_of(gi * SUBLANES, SUBLANES)
        hg = a_ref[pl.ds(r0, SUBLANES), :] * carry + u_ref[pl.ds(r0, SUBLANES), :]
        h_ref[pl.ds(r0, SUBLANES), :] = hg
        return jnp.broadcast_to(hg[SUBLANES - 1:SUBLANES, :], (SUBLANES, w))

    lax.fori_loop(0, s_len // SUBLANES, group, jnp.zeros((SUBLANES, w), F32), unroll=8)
    o_ref[0] = (h_ref[...] * _silu(z_ref[0])).astype(o_ref.dtype)


def _rglru(u, conv_w, conv_b, w_r, b_r, w_i, b_i, lam, *, x_off, z_off):
    bsz, s_len, _ = u.shape
    nblk, blk, _ = w_r.shape
    vec = lambda rows: pl.BlockSpec((rows, blk), lambda b, n: (0, n))
    mat = pl.BlockSpec((1, blk, blk), lambda b, n: (n, 0, 0))
    return pl.pallas_call(
        _rglru_kernel,
        grid=(bsz, nblk),
        in_specs=[pl.BlockSpec((1, s_len, blk), lambda b, n: (b, 0, x_off + n)),
                  pl.BlockSpec((1, s_len, blk), lambda b, n: (b, 0, z_off + n)),
                  vec(CONV_W), vec(1), mat, vec(1), mat, vec(1), vec(1)],
        out_specs=pl.BlockSpec((1, s_len, blk), lambda b, n: (b, 0, n)),
        out_shape=jax.ShapeDtypeStruct((bsz, s_len, nblk * blk), BF16),
        scratch_shapes=[pltpu.VMEM((s_len, blk), F32)] * 3,
        compiler_params=pltpu.CompilerParams(
            dimension_semantics=("parallel", "parallel"), vmem_limit_bytes=VMEM_LIMIT),
        name="rglru",
    )(u, u, conv_w, conv_b.reshape(1, -1), w_r, b_r.reshape(1, -1), w_i, b_i.reshape(1, -1),
      lam.reshape(1, -1))


ML_CHUNK = 128
ML_UNROLL = 4


def _mlstm_gates_kernel(g_ref, gb_ref, mu_ref, gi_ref, en_ref, ks_ref, dec_ref, rt_ref):
    s_len = g_ref.shape[1]
    c = ML_CHUNK
    t = g_ref[0] + gb_ref[...]
    li = t[:, :LANES]
    lf = _log_sigmoid(t[:, LANES:])
    row = lax.broadcasted_iota(jnp.int32, (s_len, LANES), 0) % c

    def scan(x, op, fill):
        s = 1
        while s < c:
            x = op(x, jnp.where(row >= s, pltpu.roll(x, s, 0), fill))
            s *= 2
        return x

    b = scan(lf, jnp.add, 0.0)
    r = li - b
    rho = scan(r, jnp.maximum, -jnp.inf)
    m = jnp.zeros((1, LANES), F32)
    for ci in range(s_len // c):
        sl = slice(ci * c, (ci + 1) * c)
        b_c, r_c, rho_c = b[sl], r[sl], rho[sl]
        b_last = b_c[c - 1:c, :]
        m_new = b_last + jnp.maximum(m, rho_c[c - 1:c, :])
        mu = jnp.maximum(rho_c, m)
        mu_ref[0, sl, :] = mu
        gi_ref[0, sl, :] = jnp.exp(m - mu)
        en_ref[0, sl, :] = jnp.exp(-(b_c + mu))
        ks_ref[0, sl, :] = jnp.exp(b_last + r_c - m_new)
        dec_ref[0, ci:ci + 1, :] = jnp.exp(b_last + m - m_new)
        rt_ref[0, :, sl] = r_c.T
        m = m_new


def _mlstm_gates(gates, gate_bias):
    bsz, s_len, gw = gates.shape
    nc = s_len // ML_CHUNK
    col = jax.ShapeDtypeStruct((bsz, s_len, LANES), F32)
    col_spec = pl.BlockSpec((1, s_len, LANES), lambda b: (b, 0, 0))
    return pl.pallas_call(
        _mlstm_gates_kernel,
        grid=(bsz,),
        in_specs=[pl.BlockSpec((1, s_len, gw), lambda b: (b, 0, 0)),
                  pl.BlockSpec((1, gw), lambda b: (0, 0))],
        out_specs=[col_spec] * 4 + [pl.BlockSpec((1, nc, LANES), lambda b: (b, 0, 0)),
                                    pl.BlockSpec((1, LANES, s_len), lambda b: (b, 0, 0))],
        out_shape=[col] * 4 + [jax.ShapeDtypeStruct((bsz, nc, LANES), F32),
                               jax.ShapeDtypeStruct((bsz, LANES, s_len), F32)],
        compiler_params=pltpu.CompilerParams(
            dimension_semantics=("parallel",), vmem_limit_bytes=VMEM_LIMIT),
        name="mlstm_gates",
    )(gates, gate_bias)


def _mlstm_kernel(q_ref, k_ref, v_ref, og_ref, z_ref, mu_ref, gi_ref, en_ref, ks_ref, dec_ref,
                  rt_ref, cwq_ref, cbq_ref, cwk_ref, cbk_ref, cn_ref, o_ref, qs_ref, ks_s_ref,
                  cst_ref, pad_ref, nst_ref):
    s_len = q_ref.shape[1]
    dk = q_ref.shape[2]
    c = ML_CHUNK
    assert c == dk == LANES
    h = pl.program_id(1)

    def conv_silu(x_ref, cw_ref, cb_ref):
        pad_ref[0:SUBLANES, :] = jnp.zeros((SUBLANES, dk), F32)
        pad_ref[SUBLANES:, :] = x_ref[0]
        acc = cb_ref[...] + x_ref[0] * cw_ref[CONV_W - 1:CONV_W, :]
        for j in range(1, CONV_W):
            acc = acc + (pad_ref[SUBLANES - j:SUBLANES - j + s_len, :]
                         * cw_ref[CONV_W - 1 - j:CONV_W - j, :])
        return _silu(acc)

    qs_ref[...] = conv_silu(q_ref, cwq_ref, cbq_ref) * (dk ** -0.5)
    ks_s_ref[...] = conv_silu(k_ref, cwk_ref, cbk_ref)
    cst_ref[...] = jnp.zeros_like(cst_ref)
    nst_ref[...] = jnp.zeros_like(nst_ref)

    row = lax.broadcasted_iota(jnp.int32, (c, c), 0)
    col = lax.broadcasted_iota(jnp.int32, (c, c), 1)
    causal = col <= row
    head = col == h
    cn = cn_ref[...]

    def pick(x):
        return jnp.sum(jnp.where(head, x, 0.0), axis=1, keepdims=True)

    dec_all = dec_ref[0]
    chunk_id = lax.broadcasted_iota(jnp.int32, dec_all.shape, 0)
    head_lane = lax.broadcasted_iota(jnp.int32, (1, LANES), 1) == h
    head_row = lax.broadcasted_iota(jnp.int32, (SUBLANES, c), 0) == h

    def chunk(ci, carry):
        r0 = pl.multiple_of(ci * c, c)
        q = qs_ref[pl.ds(r0, c), :]
        k = ks_s_ref[pl.ds(r0, c), :]
        v = v_ref[0, pl.ds(r0, c), :]
        mu = pick(mu_ref[0, pl.ds(r0, c), :])
        g_inter = pick(gi_ref[0, pl.ds(r0, c), :])
        e_negm = pick(en_ref[0, pl.ds(r0, c), :])
        kscale = pick(ks_ref[0, pl.ds(r0, c), :])
        decay = jnp.sum(jnp.where(chunk_id == ci, dec_all, 0.0), axis=0, keepdims=True)
        decay = jnp.sum(jnp.where(head_lane, decay, 0.0), axis=1, keepdims=True)
        r_row = jnp.sum(jnp.where(head_row, rt_ref[0, :, pl.ds(r0, c)], 0.0),
                        axis=0, keepdims=True)
        wts = jnp.where(causal, jnp.exp(jnp.minimum(r_row - mu, 0.0)), 0.0)
        qk = _dot_nt(q, k)
        kw = k * kscale
        upd = _dot_tn(kw, v)

        cst = cst_ref[...]
        n_st = nst_ref[...]
        sc = qk * wts
        inter = _dot(q, cst)
        intra = _dot(sc, v)
        cst_ref[...] = decay * cst + upd
        nst_ref[...] = decay * n_st + jnp.sum(kw, axis=0, keepdims=True)

        num = g_inter * inter + intra
        den = (g_inter * jnp.sum(q * n_st, axis=1, keepdims=True)
               + jnp.sum(sc, axis=1, keepdims=True))
        hout = num / jnp.maximum(jnp.abs(den), e_negm)

        yn = hout * lax.rsqrt(jnp.mean(hout * hout, axis=-1, keepdims=True) + EPS) * cn
        og = og_ref[0, pl.ds(r0, c), :]
        z = z_ref[0, pl.ds(r0, c), :]
        o_ref[0, pl.ds(r0, c), :] = (yn * _sigmoid(og) * _silu(z)).astype(o_ref.dtype)
        return carry

    lax.fori_loop(0, s_len // c, chunk, 0, unroll=ML_UNROLL)


def _mlstm(u, gate_terms, conv_w, conv_b, c_norm, *, heads, dk, dv):
    bsz, s_len, _ = u.shape
    nc = s_len // ML_CHUNK
    col_spec = pl.BlockSpec((1, s_len, LANES), lambda b, h: (b, 0, 0))
    qk_blk = lambda off: pl.BlockSpec((1, s_len, dk), lambda b, h, off=off: (b, 0, off + h))
    v_blk = lambda off: pl.BlockSpec((1, s_len, dv), lambda b, h, off=off: (b, 0, off + h))
    cw = lambda off: pl.BlockSpec((CONV_W, dk), lambda b, h, off=off: (0, off + h))
    cb = lambda off: pl.BlockSpec((1, dk), lambda b, h, off=off: (0, off + h))
    nqk = 2 * heads * dk // dv
    return pl.pallas_call(
        _mlstm_kernel,
        grid=(bsz, heads),
        in_specs=[qk_blk(0), qk_blk(heads),
                  v_blk(nqk), v_blk(nqk + heads), v_blk(nqk + 2 * heads),
                  col_spec, col_spec, col_spec, col_spec,
                  pl.BlockSpec((1, nc, LANES), lambda b, h: (b, 0, 0)),
                  pl.BlockSpec((1, SUBLANES, s_len), lambda b, h: (b, 0, 0)),
                  cw(0), cb(0), cw(heads), cb(heads),
                  pl.BlockSpec((1, dv), lambda b, h: (0, h))],
        out_specs=pl.BlockSpec((1, s_len, dv), lambda b, h: (b, 0, h)),
        out_shape=jax.ShapeDtypeStruct((bsz, s_len, heads * dv), BF16),
        scratch_shapes=[pltpu.VMEM((s_len, dk), F32), pltpu.VMEM((s_len, dk), F32),
                        pltpu.VMEM((dk, dv), F32), pltpu.VMEM((s_len + SUBLANES, dk), F32),
                        pltpu.VMEM((1, dk), F32)],
        compiler_params=pltpu.CompilerParams(
            dimension_semantics=("parallel", "parallel"), vmem_limit_bytes=VMEM_LIMIT),
        name="mlstm",
    )(u, u, u, u, u, *gate_terms, conv_w, conv_b.reshape(1, -1), conv_w,
      conv_b.reshape(1, -1), c_norm.reshape(1, -1))


def _out_kernel(h_ref, *rest, n_y, final):
    y_refs = rest[:n_y]
    wo_refs = rest[n_y:2 * n_y]
    p_ref, pw_ref, pn_ref, gw_ref = rest[2 * n_y:2 * n_y + 4]
    rest = rest[2 * n_y + 4:]
    if final:
        fn_ref, o_ref = rest
    else:
        (o_ref,) = rest
    mix = jnp.dot(y_refs[0][...], wo_refs[0][...], preferred_element_type=F32)
    for y_ref, wo_ref in zip(y_refs[1:], wo_refs[1:]):
        mix = mix + jnp.dot(y_ref[...], wo_ref[...], preferred_element_type=F32)
    h1 = h_ref[...] + mix
    pe = _dot(p_ref[...], pw_ref[...])
    pe = pe * lax.rsqrt(jnp.mean(pe * pe, axis=-1, keepdims=True) + EPS) * pn_ref[...]
    h2 = h1 + _sigmoid(_dot(h1, gw_ref[...])) * pe
    if final:
        h2 = h2 * lax.rsqrt(jnp.mean(h2 * h2, axis=-1, keepdims=True) + EPS) * fn_ref[...]
    o_ref[...] = h2


def _out_proj(h, ys, w_out, p, layer, ple_w, ple_norm, gate_w, final_norm=None, *, tm=256):
    m, d = h.shape
    final = final_norm is not None
    const = lambda shape: pl.BlockSpec(shape, lambda i: (0, 0), pipeline_mode=pl.Buffered(1))
    in_specs = [pl.BlockSpec((tm, d), lambda i: (i, 0))]
    in_specs += [pl.BlockSpec((tm, y.shape[1]), lambda i: (i, 0)) for y in ys]
    assert all(y.shape[1] == ys[0].shape[1] for y in ys)
    in_specs += [pl.BlockSpec((y.shape[1], d), lambda i, r=r: (r, 0), pipeline_mode=pl.Buffered(1))
                 for r, y in enumerate(ys)]
    in_specs += [pl.BlockSpec((None, tm, p.shape[2]), lambda i: (layer, i, 0)),
                 const(ple_w.shape), const((1, d)), const(gate_w.shape)]
    args = [h, *ys, *([w_out] * len(ys)), p, ple_w, ple_norm.reshape(1, d), gate_w]
    if final:
        in_specs.append(const((1, d)))
        args.append(final_norm.reshape(1, d))
    return pl.pallas_call(
        functools.partial(_out_kernel, n_y=len(ys), final=final),
        grid=(m // tm,),
        in_specs=in_specs,
        out_specs=pl.BlockSpec((tm, d), lambda i: (i, 0)),
        out_shape=jax.ShapeDtypeStruct((m, d), F32),
        compiler_params=pltpu.CompilerParams(
            dimension_semantics=("parallel",), vmem_limit_bytes=VMEM_LIMIT),
        name="out_proj_final" if final else "out_proj",
    )(*args)


def kernel(x, p, e_norm, e_w_in, a_lb_logits, a_norm, b_conv_w, b_conv_b, b_w_r, b_b_r, b_w_i, b_b_i, b_lambda, e_w_out, o_norm, o_w_in, c_conv_w, c_conv_b, c_b_i, c_b_f, c_norm, o_w_out, ple_w, ple_norm, ple_gate_w, final_norm):
    bsz, s_len, d = x.shape
    depth = p.shape[0]
    m = bsz * s_len
    a_width = a_norm.shape[1]
    b_width = b_lambda.shape[1]
    a_dk = LANES
    a_heads = a_width // a_dk
    c_heads = c_b_i.shape[1]
    c_dv = c_norm.shape[1] // c_heads
    c_dk = c_conv_w.shape[2] // (2 * c_heads)
    c_main = o_w_in.shape[2] - 2 * c_heads

    h = x.reshape(m, d)
    for i in range(depth):
        j = i // 2
        last = i == depth - 1
        if i % 2 == 0:
            u = _in_proj(h, e_norm[j], e_w_in[j].astype(BF16)).reshape(bsz, s_len, -1)
            ya = _hgrn2(u, a_lb_logits, a_norm[j], layer=j, heads=a_heads, dk=a_dk)
            x_off = (2 * a_heads * a_dk + 2 * a_width) // LANES
            yb = _rglru(u, b_conv_w[j], b_conv_b[j], b_w_r[j].astype(BF16), b_b_r[j],
                        b_w_i[j].astype(BF16), b_b_i[j], b_lambda[j],
                        x_off=x_off, z_off=x_off + b_width // LANES)
            w_out = e_w_out[j].astype(BF16)
            ys = [ya.reshape(m, a_width), yb.reshape(m, b_width)]
        else:
            w_in = o_w_in[j]
            pad = ((0, 0), (0, LANES - c_heads))
            wg = jnp.concatenate([jnp.pad(w_in[:, c_main:c_main + c_heads], pad),
                                  jnp.pad(w_in[:, c_main + c_heads:], pad)], axis=1).astype(BF16)
            u, gates = _in_proj(h, o_norm[j], w_in.astype(BF16), wg, n=c_main)
            gate_bias = jnp.concatenate([jnp.pad(c_b_i[j], pad[1]),
                                         jnp.pad(c_b_f[j], pad[1])]).reshape(1, 2 * LANES)
            gate_terms = _mlstm_gates(gates.reshape(bsz, s_len, 2 * LANES), gate_bias)
            yc = _mlstm(u.reshape(bsz, s_len, -1), gate_terms, c_conv_w[j], c_conv_b[j],
                        c_norm[j], heads=c_heads, dk=c_dk, dv=c_dv)
            ys = [yc.reshape(m, -1)]
            w_out = o_w_out[j].astype(BF16)
        h = _out_proj(h, ys, w_out, p.reshape(depth, m, -1), i, ple_w[i].astype(BF16),
                      ple_norm[i], ple_gate_w[i].astype(BF16), final_norm if last else None)
    return h.reshape(bsz, s_len, d)
```

```python
import functools

import jax
import jax.numpy as jnp
from jax import lax
from jax.experimental import pallas as pl
from jax.experimental.pallas import tpu as pltpu

EPS = 1e-6
LRU_C = 8.0
CONV_W = 4
LANES = 128
SUBLANES = 8
VMEM_LIMIT = 56 * 1024 * 1024
BF16 = jnp.bfloat16
F32 = jnp.float32


def _sigmoid(x):
    return jax.nn.sigmoid(x)


def _silu(x):
    return x * jax.nn.sigmoid(x)


def _softplus(x):
    return jnp.maximum(x, 0.0) + jnp.log1p(jnp.exp(-jnp.abs(x)))


def _log_sigmoid(x):
    return -_softplus(-x)


def _shift_rows(x, s, row):
    return jnp.where(row >= s, pltpu.roll(x, s, 0), 0.0)


def _cumsum_rows(x, row):
    s = 1
    while s < x.shape[0]:
        x = x + _shift_rows(x, s, row)
        s *= 2
    return x


def _cummax_rows(x, row):
    s = 1
    while s < x.shape[0]:
        x = jnp.maximum(x, jnp.where(row >= s, pltpu.roll(x, s, 0), -jnp.inf))
        s *= 2
    return x


def _dot(a, b):
    return jnp.dot(a.astype(BF16), b.astype(BF16), preferred_element_type=F32)


def _dot_nt(a, b):
    return lax.dot_general(a.astype(BF16), b.astype(BF16), (((1,), (1,)), ((), ())),
                           preferred_element_type=F32)


def _dot_tn(a, b):
    return lax.dot_general(a.astype(BF16), b.astype(BF16), (((0,), (0,)), ((), ())),
                           preferred_element_type=F32)


NORM_ROWS = 256


def _in_proj_kernel(x_ref, g_ref, w_ref, *rest, with_gates):
    if with_gates:
        wg_ref, o_ref, og_ref, hn_ref = rest
    else:
        o_ref, hn_ref = rest

    @pl.when(pl.program_id(1) == 0)
    def _():
        def body(i, carry):
            r0 = pl.multiple_of(i * NORM_ROWS, NORM_ROWS)
            x = x_ref[pl.ds(r0, NORM_ROWS), :]
            ms = jnp.mean(x * x, axis=-1, keepdims=True)
            hn_ref[pl.ds(r0, NORM_ROWS), :] = (x * lax.rsqrt(ms + EPS) * g_ref[...]).astype(BF16)
            return carry
        lax.fori_loop(0, x_ref.shape[0] // NORM_ROWS, body, 0)
        if with_gates:
            og_ref[...] = jnp.dot(hn_ref[...], wg_ref[...], preferred_element_type=F32)

    o_ref[...] = jnp.dot(hn_ref[...], w_ref[...], preferred_element_type=F32)


def _in_proj(x, g, w, wg=None, *, n=None, tm=1024, tn=1024):
    m, d = x.shape
    n = w.shape[1] if n is None else n
    with_gates = wg is not None
    in_specs = [
        pl.BlockSpec((tm, d), lambda i, j: (i, 0)),
        pl.BlockSpec((1, d), lambda i, j: (0, 0)),
        pl.BlockSpec((d, tn), lambda i, j: (0, j)),
    ]
    out_shape = [jax.ShapeDtypeStruct((m, n), F32)]
    out_specs = [pl.BlockSpec((tm, tn), lambda i, j: (i, j))]
    args = [x, g.reshape(1, d), w]
    if with_gates:
        gw = wg.shape[1]
        in_specs.append(pl.BlockSpec((d, gw), lambda i, j: (0, 0)))
        out_shape.append(jax.ShapeDtypeStruct((m, gw), F32))
        out_specs.append(pl.BlockSpec((tm, gw), lambda i, j: (i, 0)))
        args.append(wg)
    res = pl.pallas_call(
        functools.partial(_in_proj_kernel, with_gates=with_gates),
        grid=(m // tm, n // tn),
        in_specs=in_specs,
        out_specs=out_specs,
        out_shape=out_shape,
        scratch_shapes=[pltpu.VMEM((tm, d), BF16)],
        compiler_params=pltpu.CompilerParams(
            dimension_semantics=("parallel", "arbitrary"), vmem_limit_bytes=VMEM_LIMIT),
        name="in_proj_gates" if with_gates else "in_proj",
    )(*args)
    return res if with_gates else res[0]


HG_CHUNK = 128
HG_GROUPS = HG_CHUNK // SUBLANES
HG_INFLIGHT = 4


def _hgrn2_kernel(q_ref, f_ref, i_ref, z_ref, lbl_ref, an_ref, o_ref,
                  st_ref, qs_ref, ks_ref, bs_ref, od_ref, *, layer):
    s_len = q_ref.shape[1]
    dk = q_ref.shape[2]
    c = HG_CHUNK
    ng = HG_GROUPS
    assert c == dk == LANES and i_ref.shape[2] == dk

    lg = lbl_ref[...]
    e = jnp.exp(lg - jnp.max(lg, axis=0, keepdims=True))
    sm = e / jnp.sum(e, axis=0, keepdims=True)
    lb = jnp.sum(sm[:layer + 1], axis=0, keepdims=True)
    an = an_ref[...]

    rowc = lax.broadcasted_iota(jnp.int32, (c, c), 0)
    colc = lax.broadcasted_iota(jnp.int32, (c, c), 1)
    tri = (colc <= rowc).astype(BF16)
    same_group = (colc % ng) == (rowc % ng)
    lane = lax.broadcasted_iota(jnp.int32, (SUBLANES, c), 1)
    diag_off = lane - lax.broadcasted_iota(jnp.int32, (SUBLANES, c), 0)
    zeros8 = jnp.zeros((SUBLANES, dk), F32)
    zeros16 = jnp.zeros((ng, dk), F32)

    st_ref[...] = jnp.zeros_like(st_ref)

    def rows8(x, r):
        return x[r * SUBLANES:(r + 1) * SUBLANES, :]

    def chunk(ci, slot):
        qs, ks, bs, od = qs_ref.at[slot], ks_ref.at[slot], bs_ref.at[slot], od_ref.at[slot]
        r0 = pl.multiple_of(ci * c, c)
        qp = q_ref[0, pl.ds(r0, c), :]
        fp = f_ref[0, pl.ds(r0, c), :]
        v = i_ref[0, pl.ds(r0, c), :]
        q = _silu(qp)
        f = lb + (1.0 - lb) * _sigmoid(fp)
        k = (1.0 - lb) * _sigmoid(-fp)

        lf = jnp.log2(f)
        hi = lf.astype(BF16)
        r1 = lf - hi.astype(F32)
        mid = r1.astype(BF16)
        lo = (r1 - mid.astype(F32)).astype(BF16)
        bb = jnp.dot(tri, jnp.concatenate([hi, mid, lo], axis=1), preferred_element_type=F32)
        yield
        b2 = bb[:, :dk] + bb[:, dk:2 * dk] + bb[:, 2 * dk:]
        qs[...] = q
        ks[...] = k
        bs[...] = b2
        b_last = bs[c - 1:c, :]
        upd = _dot_tn(v, k * jnp.exp2(b_last - b2))

        ps, rights = [], []
        for j in range(ng.bit_length() - 1):
            lhs, rhs, rights_j = [], [], []
            for r in range(ng):
                mid_grp = ((r >> (j + 1)) << (j + 1)) | ((1 << j) - 1)
                beta = bs[mid_grp * SUBLANES + SUBLANES - 1:(mid_grp + 1) * SUBLANES, :]
                if (r >> j) & 1:
                    lhs.append(rows8(q, r) * jnp.exp2(rows8(b2, r) - beta))
                    rhs.append(zeros8)
                    rights_j.append(r)
                else:
                    rhs.append(rows8(k, r) * jnp.exp2(beta - rows8(b2, r)))
            ps.append(_dot_nt(jnp.concatenate(lhs, axis=0), jnp.concatenate(rhs, axis=0)))
            rights.append(rights_j)
        yield

        st = st_ref[...]
        o = _dot_nt(q * jnp.exp2(b2), st)
        st_ref[...] = st * jnp.exp2(b_last) + upd

        def permuted(ref):
            return [ref[pl.ds(pos, ng, stride=SUBLANES), :] for pos in range(SUBLANES)]

        q_p, k_p, b_p = permuted(qs), permuted(ks), permuted(bs)
        lhs_slots, rhs_slots = [], []
        for j in range(SUBLANES.bit_length() - 1):
            for blk in range(SUBLANES >> (j + 1)):
                mid_pos = (blk << (j + 1)) | ((1 << j) - 1)
                lhs, rhs = [], []
                for pos in range(SUBLANES):
                    if pos >> (j + 1) != blk:
                        lhs.append(zeros16)
                        rhs.append(zeros16)
                    elif (pos >> j) & 1:
                        lhs.append(q_p[pos] * jnp.exp2(b_p[pos] - b_p[mid_pos]))
                        rhs.append(zeros16)
                    else:
                        lhs.append(zeros16)
                        rhs.append(k_p[pos] * jnp.exp2(b_p[mid_pos] - b_p[pos]))
                lhs_slots.append(jnp.concatenate(lhs, axis=0).astype(BF16))
                rhs_slots.append(jnp.concatenate(rhs, axis=0).astype(BF16))
        pd = _dot_nt(jnp.concatenate(lhs_slots, axis=1), jnp.concatenate(rhs_slots, axis=1))
        yield

        dg = jnp.sum(q * k, axis=1, keepdims=True)
        a_rows = [jnp.where(diag_off == r * SUBLANES, rows8(dg, r), 0.0) for r in range(ng)]
        for j, (p, rights_j) in enumerate(zip(ps, rights)):
            for i, r in enumerate(rights_j):
                right_start = (((r >> (j + 1)) << (j + 1)) | (1 << j)) * SUBLANES
                a_rows[r] = jnp.where(lane < right_start, rows8(p, i), a_rows[r])
        o = o + _dot(jnp.concatenate(a_rows, axis=0), v)
        v_p = [i_ref[0, pl.ds(r0 + pos, ng, stride=SUBLANES), :] for pos in range(SUBLANES)]
        od[...] = _dot(jnp.where(same_group, pd, 0.0), jnp.concatenate(v_p, axis=0))
        yield

        o = o + jnp.concatenate(
            [od[pl.ds(g, SUBLANES, stride=ng), :] for g in range(ng)], axis=0)
        yn = o * lax.rsqrt(jnp.mean(o * o, axis=-1, keepdims=True) + EPS) * an
        z = z_ref[0, pl.ds(r0, c), :]
        o_ref[0, pl.ds(r0, c), :] = (yn * _silu(z)).astype(o_ref.dtype)

    def chunks(i, carry):
        stages = [chunk(i * HG_INFLIGHT + slot, slot) for slot in range(HG_INFLIGHT)]
        while stages:
            stages = [g for g in stages if next(g, stages) is not stages]
        return carry

    lax.fori_loop(0, s_len // (c * HG_INFLIGHT), chunks, 0)


def _hgrn2(u, lb_logits, a_norm, *, layer, heads, dk):
    bsz, s_len, _ = u.shape
    nl = lb_logits.shape[0]
    blk = lambda off: pl.BlockSpec((1, s_len, dk), lambda b, h, off=off: (b, 0, off + h))
    return pl.pallas_call(
        functools.partial(_hgrn2_kernel, layer=layer),
        grid=(bsz, heads),
        in_specs=[blk(0), blk(heads), blk(2 * heads), blk(3 * heads),
                  pl.BlockSpec((nl, dk), lambda b, h: (0, h)),
                  pl.BlockSpec((1, dk), lambda b, h: (0, h))],
        out_specs=pl.BlockSpec((1, s_len, dk), lambda b, h: (b, 0, h)),
        out_shape=jax.ShapeDtypeStruct((bsz, s_len, heads * dk), BF16),
        scratch_shapes=[pltpu.VMEM((dk, dk), F32)]
        + [pltpu.VMEM((HG_INFLIGHT, HG_CHUNK, dk), F32)] * 4,
        compiler_params=pltpu.CompilerParams(
            dimension_semantics=("parallel", "parallel"), vmem_limit_bytes=VMEM_LIMIT),
        name="hgrn2",
    )(u, u, u, u, lb_logits, a_norm.reshape(1, -1))


def _rglru_kernel(x_ref, z_ref, cw_ref, cb_ref, wr_ref, br_ref, wi_ref, bi_ref, lam_ref,
                  o_ref, pad_ref, a_ref, u_ref, h_ref, ca_ref, cu_ref):
    s_len = x_ref.shape[1]
    w = x_ref.shape[2]
    ng = s_len // SUBLANES
    x = x_ref[0]
    pad_ref[0:SUBLANES, :] = jnp.zeros((SUBLANES, w), F32)
    pad_ref[SUBLANES:, :] = x
    xc = cb_ref[...] + x * cw_ref[CONV_W - 1:CONV_W, :]
    for j in range(1, CONV_W):
        xc = xc + pad_ref[SUBLANES - j:SUBLANES - j + s_len, :] * cw_ref[CONV_W - 1 - j:CONV_W - j, :]
    r = _sigmoid(_dot(xc, wr_ref[0]) + br_ref[...])
    ig = _sigmoid(_dot(xc, wi_ref[0]) + bi_ref[...])
    log_a = -LRU_C * r * _softplus(-lam_ref[...])
    a = jnp.exp(log_a)
    a_ref[...] = a
    u_ref[...] = jnp.sqrt(-jnp.tanh(log_a) * (a * a + 1.0)) * (ig * xc)

    ca = cu = None
    for pos in range(SUBLANES):
        ap = a_ref[pl.ds(pos, ng, stride=SUBLANES), :]
        up = u_ref[pl.ds(pos, ng, stride=SUBLANES), :]
        ca, cu = (ap, up) if pos == 0 else (ap * ca, ap * cu + up)
        ca_ref[pos] = ca
        cu_ref[pos] = cu
    grp = lax.broadcasted_iota(jnp.int32, (ng, w), 0)
    ta, tu = ca, cu
    s = 1
    while s < ng:
        m = grp >= s
        tu = tu + ta * jnp.where(m, pltpu.roll(tu, s, 0), 0.0)
        ta = ta * jnp.where(m, pltpu.roll(ta, s, 0), 1.0)
        s *= 2
    h_in = jnp.where(grp >= 1, pltpu.roll(tu, 1, 0), 0.0)
    for pos in range(SUBLANES):
        h_ref[pl.ds(pos, ng, stride=SUBLANES), :] = ca_ref[pos] * h_in + cu_ref[pos]
    o_ref[0] = (h_ref[...] * _silu(z_ref[0])).astype(o_ref.dtype)


def _rglru(u, conv_w, conv_b, w_r, b_r, w_i, b_i, lam, *, x_off, z_off):
    bsz, s_len, _ = u.shape
    nblk, blk, _ = w_r.shape
    vec = lambda rows: pl.BlockSpec((rows, blk), lambda b, n: (0, n))
    mat = pl.BlockSpec((1, blk, blk), lambda b, n: (n, 0, 0))
    return pl.pallas_call(
        _rglru_kernel,
        grid=(bsz, nblk),
        in_specs=[pl.BlockSpec((1, s_len, blk), lambda b, n: (b, 0, x_off + n)),
                  pl.BlockSpec((1, s_len, blk), lambda b, n: (b, 0, z_off + n)),
                  vec(CONV_W), vec(1), mat, vec(1), mat, vec(1), vec(1)],
        out_specs=pl.BlockSpec((1, s_len, blk), lambda b, n: (b, 0, n)),
        out_shape=jax.ShapeDtypeStruct((bsz, s_len, nblk * blk), BF16),
        scratch_shapes=[pltpu.VMEM((s_len + SUBLANES, blk), F32)]
        + [pltpu.VMEM((s_len, blk), F32)] * 3
        + [pltpu.VMEM((SUBLANES, s_len // SUBLANES, blk), F32)] * 2,
        compiler_params=pltpu.CompilerParams(
            dimension_semantics=("parallel", "parallel"), vmem_limit_bytes=VMEM_LIMIT),
        name="rglru",
    )(u, u, conv_w, conv_b.reshape(1, -1), w_r, b_r.reshape(1, -1), w_i, b_i.reshape(1, -1),
      lam.reshape(1, -1))


ML_CHUNK = 128
ML_UNROLL = 4


def _mlstm_gates_kernel(g_ref, gb_ref, mu_ref, gi_ref, en_ref, ks_ref, dec_ref, rt_ref):
    s_len = g_ref.shape[1]
    c = ML_CHUNK
    t = g_ref[0] + gb_ref[...]
    li = t[:, :LANES]
    lf = _log_sigmoid(t[:, LANES:])
    row = lax.broadcasted_iota(jnp.int32, (s_len, LANES), 0) % c

    def scan(x, op, fill):
        s = 1
        while s < c:
            x = op(x, jnp.where(row >= s, pltpu.roll(x, s, 0), fill))
            s *= 2
        return x

    b = scan(lf, jnp.add, 0.0)
    r = li - b
    rho = scan(r, jnp.maximum, -jnp.inf)
    m = jnp.zeros((1, LANES), F32)
    for ci in range(s_len // c):
        sl = slice(ci * c, (ci + 1) * c)
        b_c, r_c, rho_c = b[sl], r[sl], rho[sl]
        b_last = b_c[c - 1:c, :]
        m_new = b_last + jnp.maximum(m, rho_c[c - 1:c, :])
        mu = jnp.maximum(rho_c, m)
        mu_ref[0, sl, :] = mu
        gi_ref[0, sl, :] = jnp.exp(m - mu)
        en_ref[0, sl, :] = jnp.exp(-(b_c + mu))
        ks_ref[0, sl, :] = jnp.exp(b_last + r_c - m_new)
        dec_ref[0, ci:ci + 1, :] = jnp.exp(b_last + m - m_new)
        rt_ref[0, :, sl] = r_c.T
        m = m_new


def _mlstm_gates(gates, gate_bias):
    bsz, s_len, gw = gates.shape
    nc = s_len // ML_CHUNK
    col = jax.ShapeDtypeStruct((bsz, s_len, LANES), F32)
    col_spec = pl.BlockSpec((1, s_len, LANES), lambda b: (b, 0, 0))
    return pl.pallas_call(
        _mlstm_gates_kernel,
        grid=(bsz,),
        in_specs=[pl.BlockSpec((1, s_len, gw), lambda b: (b, 0, 0)),
                  pl.BlockSpec((1, gw), lambda b: (0, 0))],
        out_specs=[col_spec] * 4 + [pl.BlockSpec((1, nc, LANES), lambda b: (b, 0, 0)),
                                    pl.BlockSpec((1, LANES, s_len), lambda b: (b, 0, 0))],
        out_shape=[col] * 4 + [jax.ShapeDtypeStruct((bsz, nc, LANES), F32),
                               jax.ShapeDtypeStruct((bsz, LANES, s_len), F32)],
        compiler_params=pltpu.CompilerParams(
            dimension_semantics=("parallel",), vmem_limit_bytes=VMEM_LIMIT),
        name="mlstm_gates",
    )(gates, gate_bias)


def _mlstm_kernel(q_ref, k_ref, v_ref, og_ref, z_ref, mu_ref, gi_ref, en_ref, ks_ref, dec_ref,
                  rt_ref, cwq_ref, cbq_ref, cwk_ref, cbk_ref, cn_ref, o_ref, qs_ref, ks_s_ref,
                  cst_ref, pad_ref, nst_ref):
    s_len = q_ref.shape[1]
    dk = q_ref.shape[2]
    c = ML_CHUNK
    assert c == dk == LANES
    h = pl.program_id(1)

    def conv_silu(x_ref, cw_ref, cb_ref):
        pad_ref[0:SUBLANES, :] = jnp.zeros((SUBLANES, dk), F32)
        pad_ref[SUBLANES:, :] = x_ref[0]
        acc = cb_ref[...] + x_ref[0] * cw_ref[CONV_W - 1:CONV_W, :]
        for j in range(1, CONV_W):
            acc = acc + (pad_ref[SUBLANES - j:SUBLANES - j + s_len, :]
                         * cw_ref[CONV_W - 1 - j:CONV_W - j, :])
        return _silu(acc)

    qs_ref[...] = conv_silu(q_ref, cwq_ref, cbq_ref) * (dk ** -0.5)
    ks_s_ref[...] = conv_silu(k_ref, cwk_ref, cbk_ref)
    cst_ref[...] = jnp.zeros_like(cst_ref)
    nst_ref[...] = jnp.zeros_like(nst_ref)

    row = lax.broadcasted_iota(jnp.int32, (c, c), 0)
    col = lax.broadcasted_iota(jnp.int32, (c, c), 1)
    causal = col <= row
    head = col == h
    cn = cn_ref[...]

    def pick(x):
        return jnp.sum(jnp.where(head, x, 0.0), axis=1, keepdims=True)

    dec_all = dec_ref[0]
    chunk_id = lax.broadcasted_iota(jnp.int32, dec_all.shape, 0)
    head_lane = lax.broadcasted_iota(jnp.int32, (1, LANES), 1) == h
    head_row = lax.broadcasted_iota(jnp.int32, (SUBLANES, c), 0) == h

    def chunk(ci, carry):
        r0 = pl.multiple_of(ci * c, c)
        q = qs_ref[pl.ds(r0, c), :]
        k = ks_s_ref[pl.ds(r0, c), :]
        v = v_ref[0, pl.ds(r0, c), :]
        mu = pick(mu_ref[0, pl.ds(r0, c), :])
        g_inter = pick(gi_ref[0, pl.ds(r0, c), :])
        e_negm = pick(en_ref[0, pl.ds(r0, c), :])
        kscale = pick(ks_ref[0, pl.ds(r0, c), :])
        decay = jnp.sum(jnp.where(chunk_id == ci, dec_all, 0.0), axis=0, keepdims=True)
        decay = jnp.sum(jnp.where(head_lane, decay, 0.0), axis=1, keepdims=True)
        r_row = jnp.sum(jnp.where(head_row, rt_ref[0, :, pl.ds(r0, c)], 0.0),
                        axis=0, keepdims=True)
        wts = jnp.where(causal, jnp.exp(jnp.minimum(r_row - mu, 0.0)), 0.0)
        qk = _dot_nt(q, k)
        kw = k * kscale
        upd = _dot_tn(kw, v)

        cst = cst_ref[...]
        n_st = nst_ref[...]
        sc = qk * wts
        inter = _dot(q, cst)
        intra = _dot(sc, v)
        cst_ref[...] = decay * cst + upd
        nst_ref[...] = decay * n_st + jnp.sum(kw, axis=0, keepdims=True)

        num = g_inter * inter + intra
        den = (g_inter * jnp.sum(q * n_st, axis=1, keepdims=True)
               + jnp.sum(sc, axis=1, keepdims=True))
        hout = num / jnp.maximum(jnp.abs(den), e_negm)

        yn = hout * lax.rsqrt(jnp.mean(hout * hout, axis=-1, keepdims=True) + EPS) * cn
        og = og_ref[0, pl.ds(r0, c), :]
        z = z_ref[0, pl.ds(r0, c), :]
        o_ref[0, pl.ds(r0, c), :] = (yn * _sigmoid(og) * _silu(z)).astype(o_ref.dtype)
        return carry

    lax.fori_loop(0, s_len // c, chunk, 0, unroll=ML_UNROLL)


def _mlstm(u, gate_terms, conv_w, conv_b, c_norm, *, heads, dk, dv):
    bsz, s_len, _ = u.shape
    nc = s_len // ML_CHUNK
    col_spec = pl.BlockSpec((1, s_len, LANES), lambda b, h: (b, 0, 0))
    qk_blk = lambda off: pl.BlockSpec((1, s_len, dk), lambda b, h, off=off: (b, 0, off + h))
    v_blk = lambda off: pl.BlockSpec((1, s_len, dv), lambda b, h, off=off: (b, 0, off + h))
    cw = lambda off: pl.BlockSpec((CONV_W, dk), lambda b, h, off=off: (0, off + h))
    cb = lambda off: pl.BlockSpec((1, dk), lambda b, h, off=off: (0, off + h))
    nqk = 2 * heads * dk // dv
    return pl.pallas_call(
        _mlstm_kernel,
        grid=(bsz, heads),
        in_specs=[qk_blk(0), qk_blk(heads),
                  v_blk(nqk), v_blk(nqk + heads), v_blk(nqk + 2 * heads),
                  col_spec, col_spec, col_spec, col_spec,
                  pl.BlockSpec((1, nc, LANES), lambda b, h: (b, 0, 0)),
                  pl.BlockSpec((1, SUBLANES, s_len), lambda b, h: (b, 0, 0)),
                  cw(0), cb(0), cw(heads), cb(heads),
                  pl.BlockSpec((1, dv), lambda b, h: (0, h))],
        out_specs=pl.BlockSpec((1, s_len, dv), lambda b, h: (b, 0, h)),
        out_shape=jax.ShapeDtypeStruct((bsz, s_len, heads * dv), BF16),
        scratch_shapes=[pltpu.VMEM((s_len, dk), F32), pltpu.VMEM((s_len, dk), F32),
                        pltpu.VMEM((dk, dv), F32), pltpu.VMEM((s_len + SUBLANES, dk), F32),
                        pltpu.VMEM((1, dk), F32)],
        compiler_params=pltpu.CompilerParams(
            dimension_semantics=("parallel", "parallel"), vmem_limit_bytes=VMEM_LIMIT),
        name="mlstm",
    )(u, u, u, u, u, *gate_terms, conv_w, conv_b.reshape(1, -1), conv_w,
      conv_b.reshape(1, -1), c_norm.reshape(1, -1))


def _out_kernel(h_ref, *rest, n_y, final):
    y_refs = rest[:n_y]
    wo_refs = rest[n_y:2 * n_y]
    p_ref, pw_ref, pn_ref, gw_ref = rest[2 * n_y:2 * n_y + 4]
    rest = rest[2 * n_y + 4:]
    if final:
        fn_ref, o_ref = rest
    else:
        (o_ref,) = rest
    mix = jnp.dot(y_refs[0][...], wo_refs[0][...], preferred_element_type=F32)
    for y_ref, wo_ref in zip(y_refs[1:], wo_refs[1:]):
        mix = mix + jnp.dot(y_ref[...], wo_ref[...], preferred_element_type=F32)
    h1 = h_ref[...] + mix
    pe = _dot(p_ref[...], pw_ref[...])
    pe = pe * lax.rsqrt(jnp.mean(pe * pe, axis=-1, keepdims=True) + EPS) * pn_ref[...]
    h2 = h1 + _sigmoid(_dot(h1, gw_ref[...])) * pe
    if final:
        h2 = h2 * lax.rsqrt(jnp.mean(h2 * h2, axis=-1, keepdims=True) + EPS) * fn_ref[...]
    o_ref[...] = h2


def _out_proj(h, ys, w_out, p, layer, ple_w, ple_norm, gate_w, final_norm=None, *, tm=256):
    m, d = h.shape
    final = final_norm is not None
    const = lambda shape: pl.BlockSpec(shape, lambda i: (0, 0), pipeline_mode=pl.Buffered(1))
    in_specs = [pl.BlockSpec((tm, d), lambda i: (i, 0))]
    in_specs += [pl.BlockSpec((tm, y.shape[1]), lambda i: (i, 0)) for y in ys]
    assert all(y.shape[1] == ys[0].shape[1] for y in ys)
    in_specs += [pl.BlockSpec((y.shape[1], d), lambda i, r=r: (r, 0), pipeline_mode=pl.Buffered(1))
                 for r, y in enumerate(ys)]
    in_specs += [pl.BlockSpec((None, tm, p.shape[2]), lambda i: (layer, i, 0)),
                 const(ple_w.shape), const((1, d)), const(gate_w.shape)]
    args = [h, *ys, *([w_out] * len(ys)), p, ple_w, ple_norm.reshape(1, d), gate_w]
    if final:
        in_specs.append(const((1, d)))
        args.append(final_norm.reshape(1, d))
    return pl.pallas_call(
        functools.partial(_out_kernel, n_y=len(ys), final=final),
        grid=(m // tm,),
        in_specs=in_specs,
        out_specs=pl.BlockSpec((tm, d), lambda i: (i, 0)),
        out_shape=jax.ShapeDtypeStruct((m, d), F32),
        compiler_params=pltpu.CompilerParams(
            dimension_semantics=("parallel",), vmem_limit_bytes=VMEM_LIMIT),
        name="out_proj_final" if final else "out_proj",
    )(*args)


def kernel(x, p, e_norm, e_w_in, a_lb_logits, a_norm, b_conv_w, b_conv_b, b_w_r, b_b_r, b_w_i, b_b_i, b_lambda, e_w_out, o_norm, o_w_in, c_conv_w, c_conv_b, c_b_i, c_b_f, c_norm, o_w_out, ple_w, ple_norm, ple_gate_w, final_norm):
    bsz, s_len, d = x.shape
    depth = p.shape[0]
    m = bsz * s_len
    a_width = a_norm.shape[1]
    b_width = b_lambda.shape[1]
    a_dk = LANES
    a_heads = a_width // a_dk
    c_heads = c_b_i.shape[1]
    c_dv = c_norm.shape[1] // c_heads
    c_dk = c_conv_w.shape[2] // (2 * c_heads)
    c_main = o_w_in.shape[2] - 2 * c_heads

    h = x.reshape(m, d)
    for i in range(depth):
        j = i // 2
        last = i == depth - 1
        if i % 2 == 0:
            u = _in_proj(h, e_norm[j], e_w_in[j].astype(BF16)).reshape(bsz, s_len, -1)
            ya = _hgrn2(u, a_lb_logits, a_norm[j], layer=j, heads=a_heads, dk=a_dk)
            x_off = (2 * a_heads * a_dk + 2 * a_width) // LANES
            yb = _rglru(u, b_conv_w[j], b_conv_b[j], b_w_r[j].astype(BF16), b_b_r[j],
                        b_w_i[j].astype(BF16), b_b_i[j], b_lambda[j],
                        x_off=x_off, z_off=x_off + b_width // LANES)
            w_out = e_w_out[j].astype(BF16)
            ys = [ya.reshape(m, a_width), yb.reshape(m, b_width)]
        else:
            w_in = o_w_in[j]
            pad = ((0, 0), (0, LANES - c_heads))
            wg = jnp.concatenate([jnp.pad(w_in[:, c_main:c_main + c_heads], pad),
                                  jnp.pad(w_in[:, c_main + c_heads:], pad)], axis=1).astype(BF16)
            u, gates = _in_proj(h, o_norm[j], w_in.astype(BF16), wg, n=c_main)
            gate_bias = jnp.concatenate([jnp.pad(c_b_i[j], pad[1]),
                                         jnp.pad(c_b_f[j], pad[1])]).reshape(1, 2 * LANES)
            gate_terms = _mlstm_gates(gates.reshape(bsz, s_len, 2 * LANES), gate_bias)
            yc = _mlstm(u.reshape(bsz, s_len, -1), gate_terms, c_conv_w[j], c_conv_b[j],
                        c_norm[j], heads=c_heads, dk=c_dk, dv=c_dv)
            ys = [yc.reshape(m, -1)]
            w_out = o_w_out[j].astype(BF16)
        h = _out_proj(h, ys, w_out, p.reshape(depth, m, -1), i, ple_w[i].astype(BF16),
                      ple_norm[i], ple_gate_w[i].astype(BF16), final_norm if last else None)
    return h.reshape(bsz, s_len, d)
```

```python
import functools

import jax
import jax.numpy as jnp
from jax import lax
from jax.experimental import pallas as pl
from jax.experimental.pallas import tpu as pltpu

EPS = 1e-6
LRU_C = 8.0
CONV_W = 4
LANES = 128
SUBLANES = 8
VMEM_LIMIT = 56 * 1024 * 1024
BF16 = jnp.bfloat16
F32 = jnp.float32


def _sigmoid(x):
    return jax.nn.sigmoid(x)


def _silu(x):
    return x * jax.nn.sigmoid(x)


def _softplus(x):
    return jnp.maximum(x, 0.0) + jnp.log1p(jnp.exp(-jnp.abs(x)))


def _log_sigmoid(x):
    return -_softplus(-x)


def _shift_rows(x, s, row):
    return jnp.where(row >= s, pltpu.roll(x, s, 0), 0.0)


def _cumsum_rows(x, row):
    s = 1
    while s < x.shape[0]:
        x = x + _shift_rows(x, s, row)
        s *= 2
    return x


def _cummax_rows(x, row):
    s = 1
    while s < x.shape[0]:
        x = jnp.maximum(x, jnp.where(row >= s, pltpu.roll(x, s, 0), -jnp.inf))
        s *= 2
    return x


def _dot(a, b):
    return jnp.dot(a.astype(BF16), b.astype(BF16), preferred_element_type=F32)


def _dot_nt(a, b):
    return lax.dot_general(a.astype(BF16), b.astype(BF16), (((1,), (1,)), ((), ())),
                           preferred_element_type=F32)


def _dot_tn(a, b):
    return lax.dot_general(a.astype(BF16), b.astype(BF16), (((0,), (0,)), ((), ())),
                           preferred_element_type=F32)


NORM_ROWS = 256


def _in_proj_kernel(x_ref, g_ref, w_ref, *rest, with_gates):
    if with_gates:
        wg_ref, o_ref, og_ref, hn_ref = rest
    else:
        o_ref, hn_ref = rest

    @pl.when(pl.program_id(1) == 0)
    def _():
        def body(i, carry):
            r0 = pl.multiple_of(i * NORM_ROWS, NORM_ROWS)
            x = x_ref[pl.ds(r0, NORM_ROWS), :]
            ms = jnp.mean(x * x, axis=-1, keepdims=True)
            hn_ref[pl.ds(r0, NORM_ROWS), :] = (x * lax.rsqrt(ms + EPS) * g_ref[...]).astype(BF16)
            return carry
        lax.fori_loop(0, x_ref.shape[0] // NORM_ROWS, body, 0)
        if with_gates:
            og_ref[...] = jnp.dot(hn_ref[...], wg_ref[...], preferred_element_type=F32)

    o_ref[...] = jnp.dot(hn_ref[...], w_ref[...], preferred_element_type=F32)


CAST_ROWS = 256


def _cast_kernel(w_ref, o_ref):
    o_ref[...] = w_ref[:, :o_ref.shape[1]].astype(o_ref.dtype)


def _cast_bf16(w, n):
    k, cols = w.shape
    return pl.pallas_call(
        _cast_kernel,
        grid=(k // CAST_ROWS,),
        in_specs=[pl.BlockSpec((CAST_ROWS, cols), lambda i: (i, 0))],
        out_specs=pl.BlockSpec((CAST_ROWS, n), lambda i: (i, 0)),
        out_shape=jax.ShapeDtypeStruct((k, n), BF16),
        compiler_params=pltpu.CompilerParams(
            dimension_semantics=("parallel",), vmem_limit_bytes=VMEM_LIMIT),
        name="cast_bf16",
    )(w)


IN_PROJ_ROWS = 1024
IN_PROJ_VMEM = 46 * 1024 * 1024


def _in_proj_cols(n, tm, d, gw):
    for tn in range(n, 0, -512):
        need = 2 * tm * d * 4 + tm * d * 2 + 2 * d * tn * 2 + 2 * tm * tn * 4 + 2 * (d * gw * 2 + tm * gw * 4)
        if n % tn == 0 and need <= IN_PROJ_VMEM:
            return tn
    raise ValueError("no column tile fits")


def _in_proj(x, g, w, wg=None, *, n=None, tm=IN_PROJ_ROWS):
    m, d = x.shape
    n = w.shape[1] if n is None else n
    with_gates = wg is not None
    tn = _in_proj_cols(n, tm, d, wg.shape[1] if with_gates else 0)
    in_specs = [
        pl.BlockSpec((tm, d), lambda i, j: (i, 0)),
        pl.BlockSpec((1, d), lambda i, j: (0, 0)),
        pl.BlockSpec((d, tn), lambda i, j: (0, j)),
    ]
    out_shape = [jax.ShapeDtypeStruct((m, n), F32)]
    out_specs = [pl.BlockSpec((tm, tn), lambda i, j: (i, j))]
    args = [x, g.reshape(1, d), w]
    if with_gates:
        gw = wg.shape[1]
        in_specs.append(pl.BlockSpec((d, gw), lambda i, j: (0, 0)))
        out_shape.append(jax.ShapeDtypeStruct((m, gw), F32))
        out_specs.append(pl.BlockSpec((tm, gw), lambda i, j: (i, 0)))
        args.append(wg)
    res = pl.pallas_call(
        functools.partial(_in_proj_kernel, with_gates=with_gates),
        grid=(m // tm, n // tn),
        in_specs=in_specs,
        out_specs=out_specs,
        out_shape=out_shape,
        scratch_shapes=[pltpu.VMEM((tm, d), BF16)],
        compiler_params=pltpu.CompilerParams(
            dimension_semantics=("parallel", "arbitrary"), vmem_limit_bytes=VMEM_LIMIT),
        name="in_proj_gates" if with_gates else "in_proj",
    )(*args)
    return res if with_gates else res[0]


HG_CHUNK = 128
HG_GROUPS = HG_CHUNK // SUBLANES
HG_INFLIGHT = 4


def _hgrn2_kernel(q_ref, f_ref, i_ref, z_ref, lbl_ref, an_ref, o_ref,
                  st_ref, qs_ref, ks_ref, bs_ref, od_ref, tri_ref, *, layer):
    s_len = q_ref.shape[1]
    dk = q_ref.shape[2]
    c = HG_CHUNK
    ng = HG_GROUPS
    assert c == dk == LANES and i_ref.shape[2] == dk

    lg = lbl_ref[...]
    e = jnp.exp(lg - jnp.max(lg, axis=0, keepdims=True))
    sm = e / jnp.sum(e, axis=0, keepdims=True)
    lb = jnp.sum(sm[:layer + 1], axis=0, keepdims=True)
    an = an_ref[...]

    rowc = lax.broadcasted_iota(jnp.int32, (c, c), 0)
    colc = lax.broadcasted_iota(jnp.int32, (c, c), 1)
    tri_ref[...] = (colc <= rowc).astype(BF16)
    same_group = (colc % ng) == (rowc % ng)
    lane = lax.broadcasted_iota(jnp.int32, (SUBLANES, c), 1)
    diag_off = lane - lax.broadcasted_iota(jnp.int32, (SUBLANES, c), 0)
    zeros8 = jnp.zeros((SUBLANES, dk), F32)
    zeros16 = jnp.zeros((ng, dk), F32)

    st_ref[...] = jnp.zeros_like(st_ref)

    def rows8(x, r):
        return x[r * SUBLANES:(r + 1) * SUBLANES, :]

    def chunk(ci, slot):
        qs, ks, bs, od = qs_ref.at[slot], ks_ref.at[slot], bs_ref.at[slot], od_ref.at[slot]
        r0 = pl.multiple_of(ci * c, c)
        qp = q_ref[0, pl.ds(r0, c), :]
        fp = f_ref[0, pl.ds(r0, c), :]
        v = i_ref[0, pl.ds(r0, c), :]
        q = _silu(qp)
        gate = (1.0 - lb) * _sigmoid(fp)
        f = lb + gate
        k = (1.0 - lb) - gate

        lf = jnp.log2(f)
        hi = lf.astype(BF16)
        r1 = lf - hi.astype(F32)
        mid = r1.astype(BF16)
        lo = (r1 - mid.astype(F32)).astype(BF16)
        bb = jnp.dot(tri_ref[...], jnp.concatenate([hi, mid, lo], axis=1),
                     preferred_element_type=F32)
        yield
        b2 = bb[:, :dk] + bb[:, dk:2 * dk] + bb[:, 2 * dk:]
        qs[...] = q
        ks[...] = k
        bs[...] = b2
        b_last = bs[c - 1:c, :]
        upd = _dot_tn(v, k * jnp.exp2(b_last - b2))

        ps, rights = [], []
        for j in range(ng.bit_length() - 1):
            lhs, rhs, rights_j = [], [], []
            for r in range(ng):
                mid_grp = ((r >> (j + 1)) << (j + 1)) | ((1 << j) - 1)
                beta = bs[mid_grp * SUBLANES + SUBLANES - 1:(mid_grp + 1) * SUBLANES, :]
                if (r >> j) & 1:
                    lhs.append(rows8(q, r) * jnp.exp2(rows8(b2, r) - beta))
                    rhs.append(zeros8)
                    rights_j.append(r)
                else:
                    rhs.append(rows8(k, r) * jnp.exp2(beta - rows8(b2, r)))
            ps.append(_dot_nt(jnp.concatenate(lhs, axis=0), jnp.concatenate(rhs, axis=0)))
            rights.append(rights_j)
        yield

        st = st_ref[...]
        o = _dot_nt(q * jnp.exp2(b2), st)
        st_ref[...] = st * jnp.exp2(b_last) + upd

        def permuted(ref):
            return [ref[pl.ds(pos, ng, stride=SUBLANES), :] for pos in range(SUBLANES)]

        q_p, k_p, b_p = permuted(qs), permuted(ks), permuted(bs)
        lhs_slots, rhs_slots = [], []
        for j in range(SUBLANES.bit_length() - 1):
            for blk in range(SUBLANES >> (j + 1)):
                mid_pos = (blk << (j + 1)) | ((1 << j) - 1)
                lhs, rhs = [], []
                for pos in range(SUBLANES):
                    if pos >> (j + 1) != blk:
                        lhs.append(zeros16)
                        rhs.append(zeros16)
                    elif (pos >> j) & 1:
                        lhs.append(q_p[pos] * jnp.exp2(b_p[pos] - b_p[mid_pos]))
                        rhs.append(zeros16)
                    else:
                        lhs.append(zeros16)
                        rhs.append(k_p[pos] * jnp.exp2(b_p[mid_pos] - b_p[pos]))
                lhs_slots.append(jnp.concatenate(lhs, axis=0).astype(BF16))
                rhs_slots.append(jnp.concatenate(rhs, axis=0).astype(BF16))
        pd = _dot_nt(jnp.concatenate(lhs_slots, axis=1), jnp.concatenate(rhs_slots, axis=1))
        yield

        dg = jnp.sum(q * k, axis=1, keepdims=True)
        a_rows = [jnp.where(diag_off == r * SUBLANES, rows8(dg, r), 0.0) for r in range(ng)]
        for j, (p, rights_j) in enumerate(zip(ps, rights)):
            for i, r in enumerate(rights_j):
                right_start = (((r >> (j + 1)) << (j + 1)) | (1 << j)) * SUBLANES
                a_rows[r] = jnp.where(lane < right_start, rows8(p, i), a_rows[r])
        o = o + _dot(jnp.concatenate(a_rows, axis=0), v)
        v_p = [i_ref[0, pl.ds(r0 + pos, ng, stride=SUBLANES), :] for pos in range(SUBLANES)]
        od[...] = _dot(jnp.where(same_group, pd, 0.0), jnp.concatenate(v_p, axis=0))
        yield

        o = o + jnp.concatenate(
            [od[pl.ds(g, SUBLANES, stride=ng), :] for g in range(ng)], axis=0)
        yn = o * lax.rsqrt(jnp.mean(o * o, axis=-1, keepdims=True) + EPS) * an
        z = z_ref[0, pl.ds(r0, c), :]
        o_ref[0, pl.ds(r0, c), :] = (yn * _silu(z)).astype(o_ref.dtype)

    def chunks(i, carry):
        stages = [chunk(i * HG_INFLIGHT + slot, slot) for slot in range(HG_INFLIGHT)]
        while stages:
            stages = [g for g in stages if next(g, stages) is not stages]
        return carry

    lax.fori_loop(0, s_len // (c * HG_INFLIGHT), chunks, 0, unroll=2)


def _hgrn2(u, lb_logits, a_norm, *, layer, heads, dk):
    bsz, s_len, _ = u.shape
    nl = lb_logits.shape[0]
    blk = lambda off: pl.BlockSpec((1, s_len, dk), lambda b, h, off=off: (b, 0, off + h))
    return pl.pallas_call(
        functools.partial(_hgrn2_kernel, layer=layer),
        grid=(bsz, heads),
        in_specs=[blk(0), blk(heads), blk(2 * heads), blk(3 * heads),
                  pl.BlockSpec((nl, dk), lambda b, h: (0, h)),
                  pl.BlockSpec((1, dk), lambda b, h: (0, h))],
        out_specs=pl.BlockSpec((1, s_len, dk), lambda b, h: (b, 0, h)),
        out_shape=jax.ShapeDtypeStruct((bsz, s_len, heads * dk), BF16),
        scratch_shapes=[pltpu.VMEM((dk, dk), F32)]
        + [pltpu.VMEM((HG_INFLIGHT, HG_CHUNK, dk), F32)] * 4
        + [pltpu.VMEM((HG_CHUNK, HG_CHUNK), BF16)],
        compiler_params=pltpu.CompilerParams(
            dimension_semantics=("parallel", "parallel"), vmem_limit_bytes=VMEM_LIMIT),
        name="hgrn2",
    )(u, u, u, u, lb_logits, a_norm.reshape(1, -1))


def _rglru_kernel(x_ref, z_ref, cw_ref, cb_ref, wr_ref, br_ref, wi_ref, bi_ref, lam_ref,
                  o_ref, pad_ref, a_ref, u_ref, h_ref, ca_ref, cu_ref):
    s_len = x_ref.shape[1]
    w = x_ref.shape[2]
    ng = s_len // SUBLANES
    x = x_ref[0]
    pad_ref[0:SUBLANES, :] = jnp.zeros((SUBLANES, w), F32)
    pad_ref[SUBLANES:, :] = x
    xc = cb_ref[...] + x * cw_ref[CONV_W - 1:CONV_W, :]
    for j in range(1, CONV_W):
        xc = xc + pad_ref[SUBLANES - j:SUBLANES - j + s_len, :] * cw_ref[CONV_W - 1 - j:CONV_W - j, :]
    r = _sigmoid(_dot(xc, wr_ref[0]) + br_ref[...])
    ig = _sigmoid(_dot(xc, wi_ref[0]) + bi_ref[...])
    log_a = -LRU_C * r * _softplus(-lam_ref[...])
    a = jnp.exp(log_a)
    a_ref[...] = a
    u_ref[...] = jnp.sqrt(-jnp.tanh(log_a) * (a * a + 1.0)) * (ig * xc)

    ca = cu = None
    for pos in range(SUBLANES):
        ap = a_ref[pl.ds(pos, ng, stride=SUBLANES), :]
        up = u_ref[pl.ds(pos, ng, stride=SUBLANES), :]
        ca, cu = (ap, up) if pos == 0 else (ap * ca, ap * cu + up)
        ca_ref[pos] = ca
        cu_ref[pos] = cu
    grp = lax.broadcasted_iota(jnp.int32, (ng, w), 0)
    ta, tu = ca, cu
    s = 1
    while s < ng:
        m = grp >= s
        tu = tu + ta * jnp.where(m, pltpu.roll(tu, s, 0), 0.0)
        ta = ta * jnp.where(m, pltpu.roll(ta, s, 0), 1.0)
        s *= 2
    h_in = jnp.where(grp >= 1, pltpu.roll(tu, 1, 0), 0.0)
    for pos in range(SUBLANES):
        h_ref[pl.ds(pos, ng, stride=SUBLANES), :] = ca_ref[pos] * h_in + cu_ref[pos]
    o_ref[0] = (h_ref[...] * _silu(z_ref[0])).astype(o_ref.dtype)


def _rglru(u, conv_w, conv_b, w_r, b_r, w_i, b_i, lam, *, x_off, z_off):
    bsz, s_len, _ = u.shape
    nblk, blk, _ = w_r.shape
    vec = lambda rows: pl.BlockSpec((rows, blk), lambda b, n: (0, n))
    mat = pl.BlockSpec((1, blk, blk), lambda b, n: (n, 0, 0))
    return pl.pallas_call(
        _rglru_kernel,
        grid=(bsz, nblk),
        in_specs=[pl.BlockSpec((1, s_len, blk), lambda b, n: (b, 0, x_off + n)),
                  pl.BlockSpec((1, s_len, blk), lambda b, n: (b, 0, z_off + n)),
                  vec(CONV_W), vec(1), mat, vec(1), mat, vec(1), vec(1)],
        out_specs=pl.BlockSpec((1, s_len, blk), lambda b, n: (b, 0, n)),
        out_shape=jax.ShapeDtypeStruct((bsz, s_len, nblk * blk), BF16),
        scratch_shapes=[pltpu.VMEM((s_len + SUBLANES, blk), F32)]
        + [pltpu.VMEM((s_len, blk), F32)] * 3
        + [pltpu.VMEM((SUBLANES, s_len // SUBLANES, blk), F32)] * 2,
        compiler_params=pltpu.CompilerParams(
            dimension_semantics=("parallel", "parallel"), vmem_limit_bytes=VMEM_LIMIT),
        name="rglru",
    )(u, u, conv_w, conv_b.reshape(1, -1), w_r, b_r.reshape(1, -1), w_i, b_i.reshape(1, -1),
      lam.reshape(1, -1))


ML_CHUNK = 128
ML_UNROLL = 4


def _mlstm_gates_kernel(g_ref, gb_ref, mu_ref, gi_ref, en_ref, ks_ref, dec_ref, rt_ref):
    s_len = g_ref.shape[1]
    c = ML_CHUNK
    t = g_ref[0] + gb_ref[...]
    li = t[:, :LANES]
    lf = _log_sigmoid(t[:, LANES:])
    row = lax.broadcasted_iota(jnp.int32, (s_len, LANES), 0) % c

    def scan(x, op, fill):
        s = 1
        while s < c:
            x = op(x, jnp.where(row >= s, pltpu.roll(x, s, 0), fill))
            s *= 2
        return x

    b = scan(lf, jnp.add, 0.0)
    r = li - b
    rho = scan(r, jnp.maximum, -jnp.inf)
    m = jnp.zeros((1, LANES), F32)
    for ci in range(s_len // c):
        sl = slice(ci * c, (ci + 1) * c)
        b_c, r_c, rho_c = b[sl], r[sl], rho[sl]
        b_last = b_c[c - 1:c, :]
        m_new = b_last + jnp.maximum(m, rho_c[c - 1:c, :])
        mu = jnp.maximum(rho_c, m)
        mu_ref[0, sl, :] = mu
        gi_ref[0, sl, :] = jnp.exp(m - mu)
        en_ref[0, sl, :] = jnp.exp(-(b_c + mu))
        ks_ref[0, sl, :] = jnp.exp(b_last + r_c - m_new)
        dec_ref[0, ci:ci + 1, :] = jnp.exp(b_last + m - m_new)
        rt_ref[0, :, sl] = r_c.T
        m = m_new


def _mlstm_gates(gates, gate_bias):
    bsz, s_len, gw = gates.shape
    nc = s_len // ML_CHUNK
    col = jax.ShapeDtypeStruct((bsz, s_len, LANES), F32)
    col_spec = pl.BlockSpec((1, s_len, LANES), lambda b: (b, 0, 0))
    return pl.pallas_call(
        _mlstm_gates_kernel,
        grid=(bsz,),
        in_specs=[pl.BlockSpec((1, s_len, gw), lambda b: (b, 0, 0)),
                  pl.BlockSpec((1, gw), lambda b: (0, 0))],
        out_specs=[col_spec] * 4 + [pl.BlockSpec((1, nc, LANES), lambda b: (b, 0, 0)),
                                    pl.BlockSpec((1, LANES, s_len), lambda b: (b, 0, 0))],
        out_shape=[col] * 4 + [jax.ShapeDtypeStruct((bsz, nc, LANES), F32),
                               jax.ShapeDtypeStruct((bsz, LANES, s_len), F32)],
        compiler_params=pltpu.CompilerParams(
            dimension_semantics=("parallel",), vmem_limit_bytes=VMEM_LIMIT),
        name="mlstm_gates",
    )(gates, gate_bias)


def _mlstm_kernel(q_ref, k_ref, v_ref, og_ref, z_ref, mu_ref, gi_ref, en_ref, ks_ref, dec_ref,
                  rt_ref, cwq_ref, cbq_ref, cwk_ref, cbk_ref, cn_ref, o_ref, qs_ref, ks_s_ref,
                  cst_ref, pad_ref, nst_ref):
    s_len = q_ref.shape[1]
    dk = q_ref.shape[2]
    c = ML_CHUNK
    assert c == dk == LANES
    h = pl.program_id(1)

    def conv_silu(x_ref, cw_ref, cb_ref):
        pad_ref[0:SUBLANES, :] = jnp.zeros((SUBLANES, dk), F32)
        pad_ref[SUBLANES:, :] = x_ref[0]
        acc = cb_ref[...] + x_ref[0] * cw_ref[CONV_W - 1:CONV_W, :]
        for j in range(1, CONV_W):
            acc = acc + (pad_ref[SUBLANES - j:SUBLANES - j + s_len, :]
                         * cw_ref[CONV_W - 1 - j:CONV_W - j, :])
        return _silu(acc)

    qs_ref[...] = conv_silu(q_ref, cwq_ref, cbq_ref) * (dk ** -0.5)
    ks_s_ref[...] = conv_silu(k_ref, cwk_ref, cbk_ref)
    cst_ref[...] = jnp.zeros_like(cst_ref)
    nst_ref[...] = jnp.zeros_like(nst_ref)

    row = lax.broadcasted_iota(jnp.int32, (c, c), 0)
    col = lax.broadcasted_iota(jnp.int32, (c, c), 1)
    causal = col <= row
    head = col == h
    cn = cn_ref[...]

    def pick(x):
        return jnp.sum(jnp.where(head, x, 0.0), axis=1, keepdims=True)

    dec_all = dec_ref[0]
    chunk_id = lax.broadcasted_iota(jnp.int32, dec_all.shape, 0)
    head_lane = lax.broadcasted_iota(jnp.int32, (1, LANES), 1) == h
    head_row = lax.broadcasted_iota(jnp.int32, (SUBLANES, c), 0) == h

    def chunk(ci, carry):
        r0 = pl.multiple_of(ci * c, c)
        q = qs_ref[pl.ds(r0, c), :]
        k = ks_s_ref[pl.ds(r0, c), :]
        v = v_ref[0, pl.ds(r0, c), :]
        mu = pick(mu_ref[0, pl.ds(r0, c), :])
        g_inter = pick(gi_ref[0, pl.ds(r0, c), :])
        e_negm = pick(en_ref[0, pl.ds(r0, c), :])
        kscale = pick(ks_ref[0, pl.ds(r0, c), :])
        decay = jnp.sum(jnp.where(chunk_id == ci, dec_all, 0.0), axis=0, keepdims=True)
        decay = jnp.sum(jnp.where(head_lane, decay, 0.0), axis=1, keepdims=True)
        r_row = jnp.sum(jnp.where(head_row, rt_ref[0, :, pl.ds(r0, c)], 0.0),
                        axis=0, keepdims=True)
        wts = jnp.where(causal, jnp.exp(jnp.minimum(r_row - mu, 0.0)), 0.0)
        qk = _dot_nt(q, k)
        kw = k * kscale
        upd = _dot_tn(kw, v)

        cst = cst_ref[...]
        n_st = nst_ref[...]
        sc = qk * wts
        inter = _dot(q, cst)
        intra = _dot(sc, v)
        cst_ref[...] = decay * cst + upd
        nst_ref[...] = decay * n_st + jnp.sum(kw, axis=0, keepdims=True)

        num = g_inter * inter + intra
        den = (g_inter * jnp.sum(q * n_st, axis=1, keepdims=True)
               + jnp.sum(sc, axis=1, keepdims=True))
        hout = num / jnp.maximum(jnp.abs(den), e_negm)

        yn = hout * lax.rsqrt(jnp.mean(hout * hout, axis=-1, keepdims=True) + EPS) * cn
        og = og_ref[0, pl.ds(r0, c), :]
        z = z_ref[0, pl.ds(r0, c), :]
        o_ref[0, pl.ds(r0, c), :] = (yn * _sigmoid(og) * _silu(z)).astype(o_ref.dtype)
        return carry

    lax.fori_loop(0, s_len // c, chunk, 0, unroll=ML_UNROLL)


def _mlstm(u, gate_terms, conv_w, conv_b, c_norm, *, heads, dk, dv):
    bsz, s_len, _ = u.shape
    nc = s_len // ML_CHUNK
    col_spec = pl.BlockSpec((1, s_len, LANES), lambda b, h: (b, 0, 0))
    qk_blk = lambda off: pl.BlockSpec((1, s_len, dk), lambda b, h, off=off: (b, 0, off + h))
    v_blk = lambda off: pl.BlockSpec((1, s_len, dv), lambda b, h, off=off: (b, 0, off + h))
    cw = lambda off: pl.BlockSpec((CONV_W, dk), lambda b, h, off=off: (0, off + h))
    cb = lambda off: pl.BlockSpec((1, dk), lambda b, h, off=off: (0, off + h))
    nqk = 2 * heads * dk // dv
    return pl.pallas_call(
        _mlstm_kernel,
        grid=(bsz, heads),
        in_specs=[qk_blk(0), qk_blk(heads),
                  v_blk(nqk), v_blk(nqk + heads), v_blk(nqk + 2 * heads),
                  col_spec, col_spec, col_spec, col_spec,
                  pl.BlockSpec((1, nc, LANES), lambda b, h: (b, 0, 0)),
                  pl.BlockSpec((1, SUBLANES, s_len), lambda b, h: (b, 0, 0)),
                  cw(0), cb(0), cw(heads), cb(heads),
                  pl.BlockSpec((1, dv), lambda b, h: (0, h))],
        out_specs=pl.BlockSpec((1, s_len, dv), lambda b, h: (b, 0, h)),
        out_shape=jax.ShapeDtypeStruct((bsz, s_len, heads * dv), BF16),
        scratch_shapes=[pltpu.VMEM((s_len, dk), F32), pltpu.VMEM((s_len, dk), F32),
                        pltpu.VMEM((dk, dv), F32), pltpu.VMEM((s_len + SUBLANES, dk), F32),
                        pltpu.VMEM((1, dk), F32)],
        compiler_params=pltpu.CompilerParams(
            dimension_semantics=("parallel", "parallel"), vmem_limit_bytes=VMEM_LIMIT),
        name="mlstm",
    )(u, u, u, u, u, *gate_terms, conv_w, conv_b.reshape(1, -1), conv_w,
      conv_b.reshape(1, -1), c_norm.reshape(1, -1))


def _out_kernel(h_ref, *rest, n_y, final):
    y_refs = rest[:n_y]
    wo_refs = rest[n_y:2 * n_y]
    p_ref, pw_ref, pn_ref, gw_ref = rest[2 * n_y:2 * n_y + 4]
    rest = rest[2 * n_y + 4:]
    if final:
        fn_ref, o_ref = rest
    else:
        (o_ref,) = rest
    mix = jnp.dot(y_refs[0][...], wo_refs[0][...], preferred_element_type=F32)
    for y_ref, wo_ref in zip(y_refs[1:], wo_refs[1:]):
        mix = mix + jnp.dot(y_ref[...], wo_ref[...], preferred_element_type=F32)
    h1 = h_ref[...] + mix
    pe = _dot(p_ref[...], pw_ref[...])
    pe = pe * lax.rsqrt(jnp.mean(pe * pe, axis=-1, keepdims=True) + EPS) * pn_ref[...]
    h2 = h1 + _sigmoid(_dot(h1, gw_ref[...])) * pe
    if final:
        h2 = h2 * lax.rsqrt(jnp.mean(h2 * h2, axis=-1, keepdims=True) + EPS) * fn_ref[...]
    o_ref[...] = h2


def _out_proj(h, ys, w_out, p, layer, ple_w, ple_norm, gate_w, final_norm=None, *, tm=256):
    m, d = h.shape
    final = final_norm is not None
    const = lambda shape: pl.BlockSpec(shape, lambda i: (0, 0), pipeline_mode=pl.Buffered(1))
    in_specs = [pl.BlockSpec((tm, d), lambda i: (i, 0))]
    in_specs += [pl.BlockSpec((tm, y.shape[1]), lambda i: (i, 0)) for y in ys]
    assert all(y.shape[1] == ys[0].shape[1] for y in ys)
    in_specs += [pl.BlockSpec((y.shape[1], d), lambda i, r=r: (r, 0), pipeline_mode=pl.Buffered(1))
                 for r, y in enumerate(ys)]
    in_specs += [pl.BlockSpec((None, tm, p.shape[2]), lambda i: (layer, i, 0)),
                 const(ple_w.shape), const((1, d)), const(gate_w.shape)]
    args = [h, *ys, *([w_out] * len(ys)), p, ple_w, ple_norm.reshape(1, d), gate_w]
    if final:
        in_specs.append(const((1, d)))
        args.append(final_norm.reshape(1, d))
    return pl.pallas_call(
        functools.partial(_out_kernel, n_y=len(ys), final=final),
        grid=(m // tm,),
        in_specs=in_specs,
        out_specs=pl.BlockSpec((tm, d), lambda i: (i, 0)),
        out_shape=jax.ShapeDtypeStruct((m, d), F32),
        compiler_params=pltpu.CompilerParams(
            dimension_semantics=("parallel",), vmem_limit_bytes=VMEM_LIMIT),
        name="out_proj_final" if final else "out_proj",
    )(*args)


def kernel(x, p, e_norm, e_w_in, a_lb_logits, a_norm, b_conv_w, b_conv_b, b_w_r, b_b_r, b_w_i, b_b_i, b_lambda, e_w_out, o_norm, o_w_in, c_conv_w, c_conv_b, c_b_i, c_b_f, c_norm, o_w_out, ple_w, ple_norm, ple_gate_w, final_norm):
    bsz, s_len, d = x.shape
    depth = p.shape[0]
    m = bsz * s_len
    a_width = a_norm.shape[1]
    b_width = b_lambda.shape[1]
    a_dk = LANES
    a_heads = a_width // a_dk
    c_heads = c_b_i.shape[1]
    c_dv = c_norm.shape[1] // c_heads
    c_dk = c_conv_w.shape[2] // (2 * c_heads)
    c_main = o_w_in.shape[2] - 2 * c_heads

    h = x.reshape(m, d)
    for i in range(depth):
        j = i // 2
        last = i == depth - 1
        if i % 2 == 0:
            u = _in_proj(h, e_norm[j], e_w_in[j].astype(BF16)).reshape(bsz, s_len, -1)
            ya = _hgrn2(u, a_lb_logits, a_norm[j], layer=j, heads=a_heads, dk=a_dk)
            x_off = (2 * a_heads * a_dk + 2 * a_width) // LANES
            yb = _rglru(u, b_conv_w[j], b_conv_b[j], b_w_r[j].astype(BF16), b_b_r[j],
                        b_w_i[j].astype(BF16), b_b_i[j], b_lambda[j],
                        x_off=x_off, z_off=x_off + b_width // LANES)
            w_out = e_w_out[j].astype(BF16)
            ys = [ya.reshape(m, a_width), yb.reshape(m, b_width)]
        else:
            w_in = o_w_in[j]
            pad = ((0, 0), (0, LANES - c_heads))
            wg = jnp.concatenate([jnp.pad(w_in[:, c_main:c_main + c_heads], pad),
                                  jnp.pad(w_in[:, c_main + c_heads:], pad)], axis=1).astype(BF16)
            u, gates = _in_proj(h, o_norm[j], _cast_bf16(w_in, c_main), wg)
            gate_bias = jnp.concatenate([jnp.pad(c_b_i[j], pad[1]),
                                         jnp.pad(c_b_f[j], pad[1])]).reshape(1, 2 * LANES)
            gate_terms = _mlstm_gates(gates.reshape(bsz, s_len, 2 * LANES), gate_bias)
            yc = _mlstm(u.reshape(bsz, s_len, -1), gate_terms, c_conv_w[j], c_conv_b[j],
                        c_norm[j], heads=c_heads, dk=c_dk, dv=c_dv)
            ys = [yc.reshape(m, -1)]
            w_out = o_w_out[j].astype(BF16)
        h = _out_proj(h, ys, w_out, p.reshape(depth, m, -1), i, ple_w[i].astype(BF16),
                      ple_norm[i], ple_gate_w[i].astype(BF16), final_norm if last else None)
    return h.reshape(bsz, s_len, d)
```

```python
import functools

import jax
import jax.numpy as jnp
from jax import lax
from jax.experimental import pallas as pl
from jax.experimental.pallas import tpu as pltpu

EPS = 1e-6
LRU_C = 8.0
CONV_W = 4
LANES = 128
SUBLANES = 8
VMEM_LIMIT = 56 * 1024 * 1024
BF16 = jnp.bfloat16
F32 = jnp.float32


def _sigmoid(x):
    return jax.nn.sigmoid(x)


def _silu(x):
    return x * jax.nn.sigmoid(x)


def _softplus(x):
    return jnp.maximum(x, 0.0) + jnp.log1p(jnp.exp(-jnp.abs(x)))


def _log_sigmoid(x):
    return -_softplus(-x)


def _shift_rows(x, s, row):
    return jnp.where(row >= s, pltpu.roll(x, s, 0), 0.0)


def _cumsum_rows(x, row):
    s = 1
    while s < x.shape[0]:
        x = x + _shift_rows(x, s, row)
        s *= 2
    return x


def _cummax_rows(x, row):
    s = 1
    while s < x.shape[0]:
        x = jnp.maximum(x, jnp.where(row >= s, pltpu.roll(x, s, 0), -jnp.inf))
        s *= 2
    return x


def _causal_conv(x_ref, cw_ref, cb_ref):
    s_len = x_ref.shape[1]
    taps = [cw_ref[k:k + 1, :] for k in range(CONV_W)]
    acc = cb_ref[...] + x_ref[0, SUBLANES:, :] * taps[CONV_W - 1]
    for j in range(1, CONV_W):
        acc = acc + x_ref[0, SUBLANES - j:s_len - j, :] * taps[CONV_W - 1 - j]
    x0 = x_ref[0, 0:SUBLANES, :]
    row = lax.broadcasted_iota(jnp.int32, x0.shape, 0)
    acc0 = cb_ref[...] + x0 * taps[CONV_W - 1]
    for j in range(1, CONV_W):
        acc0 = acc0 + _shift_rows(x0, j, row) * taps[CONV_W - 1 - j]
    return jnp.concatenate([acc0, acc], axis=0)


def _dot(a, b):
    return jnp.dot(a.astype(BF16), b.astype(BF16), preferred_element_type=F32)


def _dot_nt(a, b):
    return lax.dot_general(a.astype(BF16), b.astype(BF16), (((1,), (1,)), ((), ())),
                           preferred_element_type=F32)


def _dot_tn(a, b):
    return lax.dot_general(a.astype(BF16), b.astype(BF16), (((0,), (0,)), ((), ())),
                           preferred_element_type=F32)


NORM_ROWS = 256


def _in_proj_kernel(x_ref, g_ref, w_ref, *rest, with_gates):
    if with_gates:
        wg_ref, o_ref, og_ref, hn_ref = rest
    else:
        o_ref, hn_ref = rest

    @pl.when(pl.program_id(1) == 0)
    def _():
        def body(i, carry):
            r0 = pl.multiple_of(i * NORM_ROWS, NORM_ROWS)
            x = x_ref[pl.ds(r0, NORM_ROWS), :]
            ms = jnp.mean(x * x, axis=-1, keepdims=True)
            hn_ref[pl.ds(r0, NORM_ROWS), :] = (x * lax.rsqrt(ms + EPS) * g_ref[...]).astype(BF16)
            return carry
        lax.fori_loop(0, x_ref.shape[0] // NORM_ROWS, body, 0)
        if with_gates:
            og_ref[...] = jnp.dot(hn_ref[...], wg_ref[...], preferred_element_type=F32)

    o_ref[...] = jnp.dot(hn_ref[...], w_ref[...], preferred_element_type=F32)


IN_PROJ_ROWS = 1024
IN_PROJ_VMEM = 46 * 1024 * 1024


def _in_proj_cols(n, tm, d, gw):
    for tn in range(n, 0, -512):
        need = 2 * tm * d * 4 + tm * d * 2 + 2 * d * tn * 2 + 2 * tm * tn * 4 + 2 * (d * gw * 2 + tm * gw * 4)
        if n % tn == 0 and need <= IN_PROJ_VMEM:
            return tn
    raise ValueError("no column tile fits")


def _in_proj(x, g, w, wg=None, *, n=None, tm=IN_PROJ_ROWS):
    m, d = x.shape
    n = w.shape[1] if n is None else n
    with_gates = wg is not None
    tn = _in_proj_cols(n, tm, d, wg.shape[1] if with_gates else 0)
    in_specs = [
        pl.BlockSpec((tm, d), lambda i, j: (i, 0)),
        pl.BlockSpec((1, d), lambda i, j: (0, 0)),
        pl.BlockSpec((d, tn), lambda i, j: (0, j)),
    ]
    out_shape = [jax.ShapeDtypeStruct((m, n), F32)]
    out_specs = [pl.BlockSpec((tm, tn), lambda i, j: (i, j))]
    args = [x, g.reshape(1, d), w]
    if with_gates:
        gw = wg.shape[1]
        in_specs.append(pl.BlockSpec((d, gw), lambda i, j: (0, 0)))
        out_shape.append(jax.ShapeDtypeStruct((m, gw), F32))
        out_specs.append(pl.BlockSpec((tm, gw), lambda i, j: (i, 0)))
        args.append(wg)
    res = pl.pallas_call(
        functools.partial(_in_proj_kernel, with_gates=with_gates),
        grid=(m // tm, n // tn),
        in_specs=in_specs,
        out_specs=out_specs,
        out_shape=out_shape,
        scratch_shapes=[pltpu.VMEM((tm, d), BF16)],
        compiler_params=pltpu.CompilerParams(
            dimension_semantics=("parallel", "arbitrary"), vmem_limit_bytes=VMEM_LIMIT),
        name="in_proj_gates" if with_gates else "in_proj",
    )(*args)
    return res if with_gates else res[0]


HG_CHUNK = 128
HG_GROUPS = HG_CHUNK // SUBLANES
HG_INFLIGHT = 4


def _hgrn2_kernel(q_ref, f_ref, i_ref, z_ref, lbl_ref, an_ref, o_ref,
                  st_ref, qs_ref, ks_ref, bs_ref, od_ref, tri_ref, *, layer):
    s_len = q_ref.shape[1]
    dk = q_ref.shape[2]
    c = HG_CHUNK
    ng = HG_GROUPS
    assert c == dk == LANES and i_ref.shape[2] == dk

    lg = lbl_ref[...]
    e = jnp.exp(lg - jnp.max(lg, axis=0, keepdims=True))
    sm = e / jnp.sum(e, axis=0, keepdims=True)
    lb = jnp.sum(sm[:layer + 1], axis=0, keepdims=True)
    an = an_ref[...]

    rowc = lax.broadcasted_iota(jnp.int32, (c, c), 0)
    colc = lax.broadcasted_iota(jnp.int32, (c, c), 1)
    tri_ref[...] = (colc <= rowc).astype(BF16)
    same_group = (colc % ng) == (rowc % ng)
    lane = lax.broadcasted_iota(jnp.int32, (SUBLANES, c), 1)
    diag_off = lane - lax.broadcasted_iota(jnp.int32, (SUBLANES, c), 0)
    zeros8 = jnp.zeros((SUBLANES, dk), F32)
    zeros16 = jnp.zeros((ng, dk), F32)

    st_ref[...] = jnp.zeros_like(st_ref)

    def rows8(x, r):
        return x[r * SUBLANES:(r + 1) * SUBLANES, :]

    def chunk(ci, slot):
        qs, ks, bs, od = qs_ref.at[slot], ks_ref.at[slot], bs_ref.at[slot], od_ref.at[slot]
        r0 = pl.multiple_of(ci * c, c)
        qp = q_ref[0, pl.ds(r0, c), :]
        fp = f_ref[0, pl.ds(r0, c), :]
        v = i_ref[0, pl.ds(r0, c), :]
        q = _silu(qp)
        gate = (1.0 - lb) * _sigmoid(fp)
        f = lb + gate
        k = (1.0 - lb) - gate

        lf = jnp.log2(f)
        hi = lf.astype(BF16)
        r1 = lf - hi.astype(F32)
        mid = r1.astype(BF16)
        lo = (r1 - mid.astype(F32)).astype(BF16)
        bb = jnp.dot(tri_ref[...], jnp.concatenate([hi, mid, lo], axis=1),
                     preferred_element_type=F32)
        yield
        b2 = bb[:, :dk] + bb[:, dk:2 * dk] + bb[:, 2 * dk:]
        qs[...] = q
        ks[...] = k
        bs[...] = b2
        b_last = bs[c - 1:c, :]
        upd = _dot_tn(v, k * jnp.exp2(b_last - b2))

        ps, rights = [], []
        for j in range(ng.bit_length() - 1):
            lhs, rhs, rights_j = [], [], []
            for r in range(ng):
                mid_grp = ((r >> (j + 1)) << (j + 1)) | ((1 << j) - 1)
                beta = bs[mid_grp * SUBLANES + SUBLANES - 1:(mid_grp + 1) * SUBLANES, :]
                if (r >> j) & 1:
                    lhs.append(rows8(q, r) * jnp.exp2(rows8(b2, r) - beta))
                    rhs.append(zeros8)
                    rights_j.append(r)
                else:
                    rhs.append(rows8(k, r) * jnp.exp2(beta - rows8(b2, r)))
            ps.append(_dot_nt(jnp.concatenate(lhs, axis=0), jnp.concatenate(rhs, axis=0)))
            rights.append(rights_j)
        yield

        st = st_ref[...]
        o = _dot_nt(q * jnp.exp2(b2), st)
        st_ref[...] = st * jnp.exp2(b_last) + upd

        def permuted(ref):
            return [ref[pl.ds(pos, ng, stride=SUBLANES), :] for pos in range(SUBLANES)]

        q_p, k_p, b_p = permuted(qs), permuted(ks), permuted(bs)
        lhs_slots, rhs_slots = [], []
        for j in range(SUBLANES.bit_length() - 1):
            for blk in range(SUBLANES >> (j + 1)):
                mid_pos = (blk << (j + 1)) | ((1 << j) - 1)
                lhs, rhs = [], []
                for pos in range(SUBLANES):
                    if pos >> (j + 1) != blk:
                        lhs.append(zeros16)
                        rhs.append(zeros16)
                    elif (pos >> j) & 1:
                        lhs.append(q_p[pos] * jnp.exp2(b_p[pos] - b_p[mid_pos]))
                        rhs.append(zeros16)
                    else:
                        lhs.append(zeros16)
                        rhs.append(k_p[pos] * jnp.exp2(b_p[mid_pos] - b_p[pos]))
                lhs_slots.append(jnp.concatenate(lhs, axis=0).astype(BF16))
                rhs_slots.append(jnp.concatenate(rhs, axis=0).astype(BF16))
        pd = _dot_nt(jnp.concatenate(lhs_slots, axis=1), jnp.concatenate(rhs_slots, axis=1))
        yield

        dg = jnp.sum(q * k, axis=1, keepdims=True)
        a_rows = [jnp.where(diag_off == r * SUBLANES, rows8(dg, r), 0.0) for r in range(ng)]
        for j, (p, rights_j) in enumerate(zip(ps, rights)):
            for i, r in enumerate(rights_j):
                right_start = (((r >> (j + 1)) << (j + 1)) | (1 << j)) * SUBLANES
                a_rows[r] = jnp.where(lane < right_start, rows8(p, i), a_rows[r])
        o = o + _dot(jnp.concatenate(a_rows, axis=0), v)
        v_p = [i_ref[0, pl.ds(r0 + pos, ng, stride=SUBLANES), :] for pos in range(SUBLANES)]
        od[...] = _dot(jnp.where(same_group, pd, 0.0), jnp.concatenate(v_p, axis=0))
        yield

        o = o + jnp.concatenate(
            [od[pl.ds(g, SUBLANES, stride=ng), :] for g in range(ng)], axis=0)
        yn = o * lax.rsqrt(jnp.mean(o * o, axis=-1, keepdims=True) + EPS) * an
        z = z_ref[0, pl.ds(r0, c), :]
        o_ref[0, pl.ds(r0, c), :] = (yn * _silu(z)).astype(o_ref.dtype)

    def chunks(i, carry):
        stages = [chunk(i * HG_INFLIGHT + slot, slot) for slot in range(HG_INFLIGHT)]
        while stages:
            stages = [g for g in stages if next(g, stages) is not stages]
        return carry

    lax.fori_loop(0, s_len // (c * HG_INFLIGHT), chunks, 0, unroll=2)


def _hgrn2(u, lb_logits, a_norm, *, layer, heads, dk):
    bsz, s_len, _ = u.shape
    nl = lb_logits.shape[0]
    blk = lambda off: pl.BlockSpec((1, s_len, dk), lambda b, h, off=off: (b, 0, off + h))
    return pl.pallas_call(
        functools.partial(_hgrn2_kernel, layer=layer),
        grid=(bsz, heads),
        in_specs=[blk(0), blk(heads), blk(2 * heads), blk(3 * heads),
                  pl.BlockSpec((nl, dk), lambda b, h: (0, h)),
                  pl.BlockSpec((1, dk), lambda b, h: (0, h))],
        out_specs=pl.BlockSpec((1, s_len, dk), lambda b, h: (b, 0, h)),
        out_shape=jax.ShapeDtypeStruct((bsz, s_len, heads * dk), BF16),
        scratch_shapes=[pltpu.VMEM((dk, dk), F32)]
        + [pltpu.VMEM((HG_INFLIGHT, HG_CHUNK, dk), F32)] * 4
        + [pltpu.VMEM((HG_CHUNK, HG_CHUNK), BF16)],
        compiler_params=pltpu.CompilerParams(
            dimension_semantics=("parallel", "parallel"), vmem_limit_bytes=VMEM_LIMIT),
        name="hgrn2",
    )(u, u, u, u, lb_logits, a_norm.reshape(1, -1))


def _rglru_kernel(x_ref, z_ref, cw_ref, cb_ref, wr_ref, br_ref, wi_ref, bi_ref, lam_ref,
                  o_ref, a_ref, u_ref, h_ref, ca_ref, cu_ref):
    s_len = x_ref.shape[1]
    w = x_ref.shape[2]
    ng = s_len // SUBLANES
    xc = _causal_conv(x_ref, cw_ref, cb_ref)
    r = _sigmoid(_dot(xc, wr_ref[0]) + br_ref[...])
    ig = _sigmoid(_dot(xc, wi_ref[0]) + bi_ref[...])
    log_a = -LRU_C * r * _softplus(-lam_ref[...])
    a = jnp.exp(log_a)
    a_ref[...] = a
    u_ref[...] = jnp.sqrt(-jnp.tanh(log_a) * (a * a + 1.0)) * (ig * xc)

    ca = cu = None
    for pos in range(SUBLANES):
        ap = a_ref[pl.ds(pos, ng, stride=SUBLANES), :]
        up = u_ref[pl.ds(pos, ng, stride=SUBLANES), :]
        ca, cu = (ap, up) if pos == 0 else (ap * ca, ap * cu + up)
        ca_ref[pos] = ca
        cu_ref[pos] = cu
    grp = lax.broadcasted_iota(jnp.int32, (ng, w), 0)
    ta, tu = ca, cu
    s = 1
    while s < ng:
        m = grp >= s
        tu = tu + ta * jnp.where(m, pltpu.roll(tu, s, 0), 0.0)
        ta = ta * jnp.where(m, pltpu.roll(ta, s, 0), 1.0)
        s *= 2
    h_in = jnp.where(grp >= 1, pltpu.roll(tu, 1, 0), 0.0)
    for pos in range(SUBLANES):
        h_ref[pl.ds(pos, ng, stride=SUBLANES), :] = ca_ref[pos] * h_in + cu_ref[pos]
    o_ref[0] = (h_ref[...] * _silu(z_ref[0])).astype(o_ref.dtype)


def _rglru(u, conv_w, conv_b, w_r, b_r, w_i, b_i, lam, *, x_off, z_off):
    bsz, s_len, _ = u.shape
    nblk, blk, _ = w_r.shape
    vec = lambda rows: pl.BlockSpec((rows, blk), lambda b, n: (0, n))
    mat = pl.BlockSpec((1, blk, blk), lambda b, n: (n, 0, 0))
    return pl.pallas_call(
        _rglru_kernel,
        grid=(bsz, nblk),
        in_specs=[pl.BlockSpec((1, s_len, blk), lambda b, n: (b, 0, x_off + n)),
                  pl.BlockSpec((1, s_len, blk), lambda b, n: (b, 0, z_off + n)),
                  vec(CONV_W), vec(1), mat, vec(1), mat, vec(1), vec(1)],
        out_specs=pl.BlockSpec((1, s_len, blk), lambda b, n: (b, 0, n)),
        out_shape=jax.ShapeDtypeStruct((bsz, s_len, nblk * blk), BF16),
        scratch_shapes=[pltpu.VMEM((s_len, blk), F32)] * 3
        + [pltpu.VMEM((SUBLANES, s_len // SUBLANES, blk), F32)] * 2,
        compiler_params=pltpu.CompilerParams(
            dimension_semantics=("parallel", "parallel"), vmem_limit_bytes=VMEM_LIMIT),
        name="rglru",
    )(u, u, conv_w, conv_b.reshape(1, -1), w_r, b_r.reshape(1, -1), w_i, b_i.reshape(1, -1),
      lam.reshape(1, -1))


ML_CHUNK = 128
ML_UNROLL = 8


def _mlstm_gates_kernel(g_ref, gb_ref, mu_ref, gi_ref, en_ref, ks_ref, dec_ref, rt_ref):
    s_len = g_ref.shape[1]
    c = ML_CHUNK
    t = g_ref[0] + gb_ref[...]
    li = t[:, :LANES]
    lf = _log_sigmoid(t[:, LANES:])
    row = lax.broadcasted_iota(jnp.int32, (s_len, LANES), 0) % c

    def scan(x, op, fill):
        s = 1
        while s < c:
            x = op(x, jnp.where(row >= s, pltpu.roll(x, s, 0), fill))
            s *= 2
        return x

    b = scan(lf, jnp.add, 0.0)
    r = li - b
    rho = scan(r, jnp.maximum, -jnp.inf)
    m = jnp.zeros((1, LANES), F32)
    for ci in range(s_len // c):
        sl = slice(ci * c, (ci + 1) * c)
        b_c, r_c, rho_c = b[sl], r[sl], rho[sl]
        b_last = b_c[c - 1:c, :]
        m_new = b_last + jnp.maximum(m, rho_c[c - 1:c, :])
        mu = jnp.maximum(rho_c, m)
        mu_ref[0, sl, :] = mu
        gi_ref[0, sl, :] = jnp.exp(m - mu)
        en_ref[0, sl, :] = jnp.exp(-(b_c + mu))
        ks_ref[0, sl, :] = jnp.exp(b_last + r_c - m_new)
        dec_ref[0, ci:ci + 1, :] = jnp.exp(b_last + m - m_new)
        rt_ref[0, :, sl] = r_c.T
        m = m_new


def _mlstm_gates(gates, gate_bias):
    bsz, s_len, gw = gates.shape
    nc = s_len // ML_CHUNK
    col = jax.ShapeDtypeStruct((bsz, s_len, LANES), F32)
    col_spec = pl.BlockSpec((1, s_len, LANES), lambda b: (b, 0, 0))
    return pl.pallas_call(
        _mlstm_gates_kernel,
        grid=(bsz,),
        in_specs=[pl.BlockSpec((1, s_len, gw), lambda b: (b, 0, 0)),
                  pl.BlockSpec((1, gw), lambda b: (0, 0))],
        out_specs=[col_spec] * 4 + [pl.BlockSpec((1, nc, LANES), lambda b: (b, 0, 0)),
                                    pl.BlockSpec((1, LANES, s_len), lambda b: (b, 0, 0))],
        out_shape=[col] * 4 + [jax.ShapeDtypeStruct((bsz, nc, LANES), F32),
                               jax.ShapeDtypeStruct((bsz, LANES, s_len), F32)],
        compiler_params=pltpu.CompilerParams(
            dimension_semantics=("parallel",), vmem_limit_bytes=VMEM_LIMIT),
        name="mlstm_gates",
    )(gates, gate_bias)


def _mlstm_kernel(q_ref, k_ref, v_ref, og_ref, z_ref, mu_ref, gi_ref, en_ref, ks_ref, dec_ref,
                  rt_ref, cwq_ref, cbq_ref, cwk_ref, cbk_ref, cn_ref, o_ref, qs_ref, ks_s_ref,
                  cst_ref, nst_ref):
    s_len = q_ref.shape[1]
    dk = q_ref.shape[2]
    c = ML_CHUNK
    assert c == dk == LANES
    h = pl.program_id(1)

    qs_ref[...] = _silu(_causal_conv(q_ref, cwq_ref, cbq_ref)) * (dk ** -0.5)
    ks_s_ref[...] = _silu(_causal_conv(k_ref, cwk_ref, cbk_ref))
    cst_ref[...] = jnp.zeros_like(cst_ref)
    nst_ref[...] = jnp.zeros_like(nst_ref)

    row = lax.broadcasted_iota(jnp.int32, (c, c), 0)
    col = lax.broadcasted_iota(jnp.int32, (c, c), 1)
    causal = col <= row
    head = col == h
    cn = cn_ref[...]

    def pick(x):
        return jnp.sum(jnp.where(head, x, 0.0), axis=1, keepdims=True)

    dec_all = dec_ref[0]
    chunk_id = lax.broadcasted_iota(jnp.int32, dec_all.shape, 0)
    head_lane = lax.broadcasted_iota(jnp.int32, (1, LANES), 1) == h
    head_row = lax.broadcasted_iota(jnp.int32, (SUBLANES, c), 0) == h

    def chunk(ci, carry):
        r0 = pl.multiple_of(ci * c, c)
        q = qs_ref[pl.ds(r0, c), :]
        k = ks_s_ref[pl.ds(r0, c), :]
        v = v_ref[0, pl.ds(r0, c), :]
        mu = pick(mu_ref[0, pl.ds(r0, c), :])
        g_inter = pick(gi_ref[0, pl.ds(r0, c), :])
        e_negm = pick(en_ref[0, pl.ds(r0, c), :])
        kscale = pick(ks_ref[0, pl.ds(r0, c), :])
        decay = jnp.sum(jnp.where(chunk_id == ci, dec_all, 0.0), axis=0, keepdims=True)
        decay = jnp.sum(jnp.where(head_lane, decay, 0.0), axis=1, keepdims=True)
        r_row = jnp.sum(jnp.where(head_row, rt_ref[0, :, pl.ds(r0, c)], 0.0),
                        axis=0, keepdims=True)
        wts = jnp.where(causal, jnp.exp(jnp.minimum(r_row - mu, 0.0)), 0.0)
        qk = _dot_nt(q, k)
        kw = k * kscale
        upd = _dot_tn(kw, v)

        cst = cst_ref[...]
        n_st = nst_ref[...]
        sc = qk * wts
        inter = _dot(q, cst)
        intra = _dot(sc, v)
        cst_ref[...] = decay * cst + upd
        nst_ref[...] = decay * n_st + jnp.sum(kw, axis=0, keepdims=True)

        num = g_inter * inter + intra
        den = (g_inter * jnp.sum(q * n_st, axis=1, keepdims=True)
               + jnp.sum(sc, axis=1, keepdims=True))
        hout = num / jnp.maximum(jnp.abs(den), e_negm)

        yn = hout * lax.rsqrt(jnp.mean(hout * hout, axis=-1, keepdims=True) + EPS) * cn
        og = og_ref[0, pl.ds(r0, c), :]
        z = z_ref[0, pl.ds(r0, c), :]
        o_ref[0, pl.ds(r0, c), :] = (yn * _sigmoid(og) * _silu(z)).astype(o_ref.dtype)
        return carry

    lax.fori_loop(0, s_len // c, chunk, 0, unroll=ML_UNROLL)


def _mlstm(u, gate_terms, conv_w, conv_b, c_norm, *, heads, dk, dv):
    bsz, s_len, _ = u.shape
    nc = s_len // ML_CHUNK
    col_spec = pl.BlockSpec((1, s_len, LANES), lambda b, h: (b, 0, 0))
    qk_blk = lambda off: pl.BlockSpec((1, s_len, dk), lambda b, h, off=off: (b, 0, off + h))
    v_blk = lambda off: pl.BlockSpec((1, s_len, dv), lambda b, h, off=off: (b, 0, off + h))
    cw = lambda off: pl.BlockSpec((CONV_W, dk), lambda b, h, off=off: (0, off + h))
    cb = lambda off: pl.BlockSpec((1, dk), lambda b, h, off=off: (0, off + h))
    nqk = 2 * heads * dk // dv
    return pl.pallas_call(
        _mlstm_kernel,
        grid=(bsz, heads),
        in_specs=[qk_blk(0), qk_blk(heads),
                  v_blk(nqk), v_blk(nqk + heads), v_blk(nqk + 2 * heads),
                  col_spec, col_spec, col_spec, col_spec,
                  pl.BlockSpec((1, nc, LANES), lambda b, h: (b, 0, 0)),
                  pl.BlockSpec((1, SUBLANES, s_len), lambda b, h: (b, 0, 0)),
                  cw(0), cb(0), cw(heads), cb(heads),
                  pl.BlockSpec((1, dv), lambda b, h: (0, h))],
        out_specs=pl.BlockSpec((1, s_len, dv), lambda b, h: (b, 0, h)),
        out_shape=jax.ShapeDtypeStruct((bsz, s_len, heads * dv), BF16),
        scratch_shapes=[pltpu.VMEM((s_len, dk), F32), pltpu.VMEM((s_len, dk), F32),
                        pltpu.VMEM((dk, dv), F32), pltpu.VMEM((1, dk), F32)],
        compiler_params=pltpu.CompilerParams(
            dimension_semantics=("parallel", "parallel"), vmem_limit_bytes=VMEM_LIMIT),
        name="mlstm",
    )(u, u, u, u, u, *gate_terms, conv_w, conv_b.reshape(1, -1), conv_w,
      conv_b.reshape(1, -1), c_norm.reshape(1, -1))


def _out_kernel(h_ref, *rest, n_y, final):
    y_refs = rest[:n_y]
    wo_refs = rest[n_y:2 * n_y]
    p_ref, pw_ref, pn_ref, gw_ref = rest[2 * n_y:2 * n_y + 4]
    rest = rest[2 * n_y + 4:]
    if final:
        fn_ref, o_ref = rest
    else:
        (o_ref,) = rest
    mix = jnp.dot(y_refs[0][...], wo_refs[0][...], preferred_element_type=F32)
    for y_ref, wo_ref in zip(y_refs[1:], wo_refs[1:]):
        mix = mix + jnp.dot(y_ref[...], wo_ref[...], preferred_element_type=F32)
    h1 = h_ref[...] + mix
    pe = _dot(p_ref[...], pw_ref[...])
    pe = pe * lax.rsqrt(jnp.mean(pe * pe, axis=-1, keepdims=True) + EPS) * pn_ref[...]
    h2 = h1 + _sigmoid(_dot(h1, gw_ref[...])) * pe
    if final:
        h2 = h2 * lax.rsqrt(jnp.mean(h2 * h2, axis=-1, keepdims=True) + EPS) * fn_ref[...]
    o_ref[...] = h2


def _out_proj(h, ys, w_out, p, layer, ple_w, ple_norm, gate_w, final_norm=None, *, tm=256):
    m, d = h.shape
    final = final_norm is not None
    const = lambda shape: pl.BlockSpec(shape, lambda i: (0, 0), pipeline_mode=pl.Buffered(1))
    in_specs = [pl.BlockSpec((tm, d), lambda i: (i, 0))]
    in_specs += [pl.BlockSpec((tm, y.shape[1]), lambda i: (i, 0)) for y in ys]
    assert all(y.shape[1] == ys[0].shape[1] for y in ys)
    in_specs += [pl.BlockSpec((y.shape[1], d), lambda i, r=r: (r, 0), pipeline_mode=pl.Buffered(1))
                 for r, y in enumerate(ys)]
    in_specs += [pl.BlockSpec((None, tm, p.shape[2]), lambda i: (layer, i, 0)),
                 const(ple_w.shape), const((1, d)), const(gate_w.shape)]
    args = [h, *ys, *([w_out] * len(ys)), p, ple_w, ple_norm.reshape(1, d), gate_w]
    if final:
        in_specs.append(const((1, d)))
        args.append(final_norm.reshape(1, d))
    return pl.pallas_call(
        functools.partial(_out_kernel, n_y=len(ys), final=final),
        grid=(m // tm,),
        in_specs=in_specs,
        out_specs=pl.BlockSpec((tm, d), lambda i: (i, 0)),
        out_shape=jax.ShapeDtypeStruct((m, d), F32),
        compiler_params=pltpu.CompilerParams(
            dimension_semantics=("parallel",), vmem_limit_bytes=VMEM_LIMIT),
        name="out_proj_final" if final else "out_proj",
    )(*args)


def kernel(x, p, e_norm, e_w_in, a_lb_logits, a_norm, b_conv_w, b_conv_b, b_w_r, b_b_r, b_w_i, b_b_i, b_lambda, e_w_out, o_norm, o_w_in, c_conv_w, c_conv_b, c_b_i, c_b_f, c_norm, o_w_out, ple_w, ple_norm, ple_gate_w, final_norm):
    bsz, s_len, d = x.shape
    depth = p.shape[0]
    m = bsz * s_len
    a_width = a_norm.shape[1]
    b_width = b_lambda.shape[1]
    a_dk = LANES
    a_heads = a_width // a_dk
    c_heads = c_b_i.shape[1]
    c_dv = c_norm.shape[1] // c_heads
    c_dk = c_conv_w.shape[2] // (2 * c_heads)
    c_main = o_w_in.shape[2] - 2 * c_heads

    h = x.reshape(m, d)
    for i in range(depth):
        j = i // 2
        last = i == depth - 1
        if i % 2 == 0:
            u = _in_proj(h, e_norm[j], e_w_in[j].astype(BF16)).reshape(bsz, s_len, -1)
            ya = _hgrn2(u, a_lb_logits, a_norm[j], layer=j, heads=a_heads, dk=a_dk)
            x_off = (2 * a_heads * a_dk + 2 * a_width) // LANES
            yb = _rglru(u, b_conv_w[j], b_conv_b[j], b_w_r[j].astype(BF16), b_b_r[j],
                        b_w_i[j].astype(BF16), b_b_i[j], b_lambda[j],
                        x_off=x_off, z_off=x_off + b_width // LANES)
            w_out = e_w_out[j].astype(BF16)
            ys = [ya.reshape(m, a_width), yb.reshape(m, b_width)]
        else:
            w_in = o_w_in[j]
            pad = ((0, 0), (0, LANES - c_heads))
            wg = jnp.concatenate([jnp.pad(w_in[:, c_main:c_main + c_heads], pad),
                                  jnp.pad(w_in[:, c_main + c_heads:], pad)], axis=1).astype(BF16)
            u, gates = _in_proj(h, o_norm[j], w_in.astype(BF16), wg, n=c_main)
            gate_bias = jnp.concatenate([jnp.pad(c_b_i[j], pad[1]),
                                         jnp.pad(c_b_f[j], pad[1])]).reshape(1, 2 * LANES)
            gate_terms = _mlstm_gates(gates.reshape(bsz, s_len, 2 * LANES), gate_bias)
            yc = _mlstm(u.reshape(bsz, s_len, -1), gate_terms, c_conv_w[j], c_conv_b[j],
                        c_norm[j], heads=c_heads, dk=c_dk, dv=c_dv)
            ys = [yc.reshape(m, -1)]
            w_out = o_w_out[j].astype(BF16)
        h = _out_proj(h, ys, w_out, p.reshape(depth, m, -1), i, ple_w[i].astype(BF16),
                      ple_norm[i], ple_gate_w[i].astype(BF16), final_norm if last else None)
    return h.reshape(bsz, s_len, d)
```

```python
import functools

import jax
import jax.numpy as jnp
from jax import lax
from jax.experimental import pallas as pl
from jax.experimental.pallas import tpu as pltpu

EPS = 1e-6
LRU_C = 8.0
CONV_W = 4
LANES = 128
SUBLANES = 8
VMEM_LIMIT = 56 * 1024 * 1024
BF16 = jnp.bfloat16
F32 = jnp.float32


def _sigmoid(x):
    return jax.nn.sigmoid(x)


def _silu(x):
    return x * jax.nn.sigmoid(x)


def _softplus(x):
    return jnp.maximum(x, 0.0) + jnp.log1p(jnp.exp(-jnp.abs(x)))


def _log_sigmoid(x):
    return -_softplus(-x)


def _shift_rows(x, s, row):
    return jnp.where(row >= s, pltpu.roll(x, s, 0), 0.0)


def _cumsum_rows(x, row):
    s = 1
    while s < x.shape[0]:
        x = x + _shift_rows(x, s, row)
        s *= 2
    return x


def _cummax_rows(x, row):
    s = 1
    while s < x.shape[0]:
        x = jnp.maximum(x, jnp.where(row >= s, pltpu.roll(x, s, 0), -jnp.inf))
        s *= 2
    return x


def _causal_conv(x_ref, cw_ref, cb_ref):
    s_len = x_ref.shape[1]
    taps = [cw_ref[k:k + 1, :] for k in range(CONV_W)]
    acc = cb_ref[...] + x_ref[0, SUBLANES:, :] * taps[CONV_W - 1]
    for j in range(1, CONV_W):
        acc = acc + x_ref[0, SUBLANES - j:s_len - j, :] * taps[CONV_W - 1 - j]
    x0 = x_ref[0, 0:SUBLANES, :]
    row = lax.broadcasted_iota(jnp.int32, x0.shape, 0)
    acc0 = cb_ref[...] + x0 * taps[CONV_W - 1]
    for j in range(1, CONV_W):
        acc0 = acc0 + _shift_rows(x0, j, row) * taps[CONV_W - 1 - j]
    return jnp.concatenate([acc0, acc], axis=0)


def _dot(a, b):
    return jnp.dot(a.astype(BF16), b.astype(BF16), preferred_element_type=F32)


def _dot_nt(a, b):
    return lax.dot_general(a.astype(BF16), b.astype(BF16), (((1,), (1,)), ((), ())),
                           preferred_element_type=F32)


def _dot_tn(a, b):
    return lax.dot_general(a.astype(BF16), b.astype(BF16), (((0,), (0,)), ((), ())),
                           preferred_element_type=F32)


NORM_ROWS = 256


def _in_proj_kernel(*refs, with_norm, with_gates):
    refs = list(refs)
    x_ref = refs.pop(0)
    g_ref = refs.pop(0) if with_norm else None
    w_ref = refs.pop(0)
    wg_ref = refs.pop(0) if with_gates else None
    o_ref = refs.pop(0)
    og_ref = refs.pop(0) if with_gates else None
    hn_ref = refs.pop(0) if with_norm else x_ref

    @pl.when(pl.program_id(1) == 0)
    def _():
        if with_norm:
            def body(i, carry):
                r0 = pl.multiple_of(i * NORM_ROWS, NORM_ROWS)
                x = x_ref[pl.ds(r0, NORM_ROWS), :]
                ms = jnp.mean(x * x, axis=-1, keepdims=True)
                hn_ref[pl.ds(r0, NORM_ROWS), :] = (x * lax.rsqrt(ms + EPS) * g_ref[...]).astype(BF16)
                return carry
            lax.fori_loop(0, x_ref.shape[0] // NORM_ROWS, body, 0)
        if with_gates:
            og_ref[...] = jnp.dot(hn_ref[...], wg_ref[...], preferred_element_type=F32)

    o_ref[...] = jnp.dot(hn_ref[...], w_ref[...], preferred_element_type=F32)


IN_PROJ_ROWS = 1024
IN_PROJ_VMEM = 46 * 1024 * 1024


def _in_proj_cols(n, tm, d, gw, x_bytes, with_norm):
    for tn in range(n, 0, -512):
        need = (2 * tm * d * x_bytes + (tm * d * 2 if with_norm else 0) + 2 * d * tn * 2
                + 2 * tm * tn * 4 + 2 * (d * gw * 2 + tm * gw * 4))
        if n % tn == 0 and need <= IN_PROJ_VMEM:
            return tn
    raise ValueError("no column tile fits")


def _in_proj(x, g, w, wg=None, *, n=None, tm=IN_PROJ_ROWS):
    m, d = x.shape
    n = w.shape[1] if n is None else n
    with_gates = wg is not None
    with_norm = g is not None
    gw = wg.shape[1] if with_gates else 0
    tn = _in_proj_cols(n, tm, d, gw, x.dtype.itemsize, with_norm)
    in_specs = [pl.BlockSpec((tm, d), lambda i, j: (i, 0))]
    args = [x]
    if with_norm:
        in_specs.append(pl.BlockSpec((1, d), lambda i, j: (0, 0)))
        args.append(g.reshape(1, d))
    in_specs.append(pl.BlockSpec((d, tn), lambda i, j: (0, j)))
    args.append(w)
    out_shape = [jax.ShapeDtypeStruct((m, n), F32)]
    out_specs = [pl.BlockSpec((tm, tn), lambda i, j: (i, j))]
    if with_gates:
        in_specs.append(pl.BlockSpec((d, gw), lambda i, j: (0, 0)))
        out_shape.append(jax.ShapeDtypeStruct((m, gw), F32))
        out_specs.append(pl.BlockSpec((tm, gw), lambda i, j: (i, 0)))
        args.append(wg)
    res = pl.pallas_call(
        functools.partial(_in_proj_kernel, with_norm=with_norm, with_gates=with_gates),
        grid=(m // tm, n // tn),
        in_specs=in_specs,
        out_specs=out_specs,
        out_shape=out_shape,
        scratch_shapes=[pltpu.VMEM((tm, d), BF16)] if with_norm else [],
        compiler_params=pltpu.CompilerParams(
            dimension_semantics=("parallel", "arbitrary"), vmem_limit_bytes=VMEM_LIMIT),
        name="in_proj_gates" if with_gates else "in_proj",
    )(*args)
    return res if with_gates else res[0]


HG_CHUNK = 128
HG_GROUPS = HG_CHUNK // SUBLANES
HG_INFLIGHT = 4


def _hgrn2_kernel(q_ref, f_ref, i_ref, z_ref, lbl_ref, an_ref, o_ref,
                  st_ref, qs_ref, ks_ref, bs_ref, od_ref, tri_ref, *, layer):
    s_len = q_ref.shape[1]
    dk = q_ref.shape[2]
    c = HG_CHUNK
    ng = HG_GROUPS
    assert c == dk == LANES and i_ref.shape[2] == dk

    lg = lbl_ref[...]
    e = jnp.exp(lg - jnp.max(lg, axis=0, keepdims=True))
    sm = e / jnp.sum(e, axis=0, keepdims=True)
    lb = jnp.sum(sm[:layer + 1], axis=0, keepdims=True)
    an = an_ref[...]

    rowc = lax.broadcasted_iota(jnp.int32, (c, c), 0)
    colc = lax.broadcasted_iota(jnp.int32, (c, c), 1)
    tri_ref[...] = (colc <= rowc).astype(BF16)
    same_group = (colc % ng) == (rowc % ng)
    lane = lax.broadcasted_iota(jnp.int32, (SUBLANES, c), 1)
    diag_off = lane - lax.broadcasted_iota(jnp.int32, (SUBLANES, c), 0)
    zeros8 = jnp.zeros((SUBLANES, dk), F32)
    zeros16 = jnp.zeros((ng, dk), F32)

    st_ref[...] = jnp.zeros_like(st_ref)

    def rows8(x, r):
        return x[r * SUBLANES:(r + 1) * SUBLANES, :]

    def chunk(ci, slot):
        qs, ks, bs, od = qs_ref.at[slot], ks_ref.at[slot], bs_ref.at[slot], od_ref.at[slot]
        r0 = pl.multiple_of(ci * c, c)
        qp = q_ref[0, pl.ds(r0, c), :]
        fp = f_ref[0, pl.ds(r0, c), :]
        v = i_ref[0, pl.ds(r0, c), :]
        q = _silu(qp)
        gate = (1.0 - lb) * _sigmoid(fp)
        f = lb + gate
        k = (1.0 - lb) - gate

        lf = jnp.log2(f)
        hi = lf.astype(BF16)
        r1 = lf - hi.astype(F32)
        mid = r1.astype(BF16)
        lo = (r1 - mid.astype(F32)).astype(BF16)
        bb = jnp.dot(tri_ref[...], jnp.concatenate([hi, mid, lo], axis=1),
                     preferred_element_type=F32)
        yield
        b2 = bb[:, :dk] + bb[:, dk:2 * dk] + bb[:, 2 * dk:]
        qs[...] = q
        ks[...] = k
        bs[...] = b2
        b_last = bs[c - 1:c, :]
        upd = _dot_tn(v, k * jnp.exp2(b_last - b2))

        ps, rights = [], []
        for j in range(ng.bit_length() - 1):
            lhs, rhs, rights_j = [], [], []
            for r in range(ng):
                mid_grp = ((r >> (j + 1)) << (j + 1)) | ((1 << j) - 1)
                beta = bs[mid_grp * SUBLANES + SUBLANES - 1:(mid_grp + 1) * SUBLANES, :]
                if (r >> j) & 1:
                    lhs.append(rows8(q, r) * jnp.exp2(rows8(b2, r) - beta))
                    rhs.append(zeros8)
                    rights_j.append(r)
                else:
                    rhs.append(rows8(k, r) * jnp.exp2(beta - rows8(b2, r)))
            ps.append(_dot_nt(jnp.concatenate(lhs, axis=0), jnp.concatenate(rhs, axis=0)))
            rights.append(rights_j)
        yield

        st = st_ref[...]
        o = _dot_nt(q * jnp.exp2(b2), st)
        st_ref[...] = st * jnp.exp2(b_last) + upd

        def permuted(ref):
            return [ref[pl.ds(pos, ng, stride=SUBLANES), :] for pos in range(SUBLANES)]

        q_p, k_p, b_p = permuted(qs), permuted(ks), permuted(bs)
        lhs_slots, rhs_slots = [], []
        for j in range(SUBLANES.bit_length() - 1):
            for blk in range(SUBLANES >> (j + 1)):
                mid_pos = (blk << (j + 1)) | ((1 << j) - 1)
                lhs, rhs = [], []
                for pos in range(SUBLANES):
                    if pos >> (j + 1) != blk:
                        lhs.append(zeros16)
                        rhs.append(zeros16)
                    elif (pos >> j) & 1:
                        lhs.append(q_p[pos] * jnp.exp2(b_p[pos] - b_p[mid_pos]))
                        rhs.append(zeros16)
                    else:
                        lhs.append(zeros16)
                        rhs.append(k_p[pos] * jnp.exp2(b_p[mid_pos] - b_p[pos]))
                lhs_slots.append(jnp.concatenate(lhs, axis=0).astype(BF16))
                rhs_slots.append(jnp.concatenate(rhs, axis=0).astype(BF16))
        pd = _dot_nt(jnp.concatenate(lhs_slots, axis=1), jnp.concatenate(rhs_slots, axis=1))
        yield

        dg = jnp.sum(q * k, axis=1, keepdims=True)
        a_rows = [jnp.where(diag_off == r * SUBLANES, rows8(dg, r), 0.0) for r in range(ng)]
        for j, (p, rights_j) in enumerate(zip(ps, rights)):
            for i, r in enumerate(rights_j):
                right_start = (((r >> (j + 1)) << (j + 1)) | (1 << j)) * SUBLANES
                a_rows[r] = jnp.where(lane < right_start, rows8(p, i), a_rows[r])
        o = o + _dot(jnp.concatenate(a_rows, axis=0), v)
        v_p = [i_ref[0, pl.ds(r0 + pos, ng, stride=SUBLANES), :] for pos in range(SUBLANES)]
        od[...] = _dot(jnp.where(same_group, pd, 0.0), jnp.concatenate(v_p, axis=0))
        yield

        o = o + jnp.concatenate(
            [od[pl.ds(g, SUBLANES, stride=ng), :] for g in range(ng)], axis=0)
        yn = o * lax.rsqrt(jnp.mean(o * o, axis=-1, keepdims=True) + EPS) * an
        z = z_ref[0, pl.ds(r0, c), :]
        o_ref[0, pl.ds(r0, c), :] = (yn * _silu(z)).astype(o_ref.dtype)

    def chunks(i, carry):
        stages = [chunk(i * HG_INFLIGHT + slot, slot) for slot in range(HG_INFLIGHT)]
        while stages:
            stages = [g for g in stages if next(g, stages) is not stages]
        return carry

    lax.fori_loop(0, s_len // (c * HG_INFLIGHT), chunks, 0, unroll=2)


def _hgrn2(u, lb_logits, a_norm, *, layer, heads, dk):
    bsz, s_len, _ = u.shape
    nl = lb_logits.shape[0]
    blk = lambda off: pl.BlockSpec((1, s_len, dk), lambda b, h, off=off: (b, 0, off + h))
    return pl.pallas_call(
        functools.partial(_hgrn2_kernel, layer=layer),
        grid=(bsz, heads),
        in_specs=[blk(0), blk(heads), blk(2 * heads), blk(3 * heads),
                  pl.BlockSpec((nl, dk), lambda b, h: (0, h)),
                  pl.BlockSpec((1, dk), lambda b, h: (0, h))],
        out_specs=pl.BlockSpec((1, s_len, dk), lambda b, h: (b, 0, h)),
        out_shape=jax.ShapeDtypeStruct((bsz, s_len, heads * dk), BF16),
        scratch_shapes=[pltpu.VMEM((dk, dk), F32)]
        + [pltpu.VMEM((HG_INFLIGHT, HG_CHUNK, dk), F32)] * 4
        + [pltpu.VMEM((HG_CHUNK, HG_CHUNK), BF16)],
        compiler_params=pltpu.CompilerParams(
            dimension_semantics=("parallel", "parallel"), vmem_limit_bytes=VMEM_LIMIT),
        name="hgrn2",
    )(u, u, u, u, lb_logits, a_norm.reshape(1, -1))


def _rglru_kernel(x_ref, z_ref, cw_ref, cb_ref, wr_ref, br_ref, wi_ref, bi_ref, lam_ref,
                  o_ref, a_ref, u_ref, h_ref, ca_ref, cu_ref):
    s_len = x_ref.shape[1]
    w = x_ref.shape[2]
    ng = s_len // SUBLANES
    xc = _causal_conv(x_ref, cw_ref, cb_ref)
    r = _sigmoid(_dot(xc, wr_ref[0]) + br_ref[...])
    ig = _sigmoid(_dot(xc, wi_ref[0]) + bi_ref[...])
    log_a = -LRU_C * r * _softplus(-lam_ref[...])
    a = jnp.exp(log_a)
    a_ref[...] = a
    u_ref[...] = jnp.sqrt(-jnp.tanh(log_a) * (a * a + 1.0)) * (ig * xc)

    ca = cu = None
    for pos in range(SUBLANES):
        ap = a_ref[pl.ds(pos, ng, stride=SUBLANES), :]
        up = u_ref[pl.ds(pos, ng, stride=SUBLANES), :]
        ca, cu = (ap, up) if pos == 0 else (ap * ca, ap * cu + up)
        ca_ref[pos] = ca
        cu_ref[pos] = cu
    grp = lax.broadcasted_iota(jnp.int32, (ng, w), 0)
    ta, tu = ca, cu
    s = 1
    while s < ng:
        m = grp >= s
        tu = tu + ta * jnp.where(m, pltpu.roll(tu, s, 0), 0.0)
        ta = ta * jnp.where(m, pltpu.roll(ta, s, 0), 1.0)
        s *= 2
    h_in = jnp.where(grp >= 1, pltpu.roll(tu, 1, 0), 0.0)
    for pos in range(SUBLANES):
        h_ref[pl.ds(pos, ng, stride=SUBLANES), :] = ca_ref[pos] * h_in + cu_ref[pos]
    o_ref[0] = (h_ref[...] * _silu(z_ref[0])).astype(o_ref.dtype)


def _rglru(u, conv_w, conv_b, w_r, b_r, w_i, b_i, lam, *, x_off, z_off):
    bsz, s_len, _ = u.shape
    nblk, blk, _ = w_r.shape
    vec = lambda rows: pl.BlockSpec((rows, blk), lambda b, n: (0, n))
    mat = pl.BlockSpec((1, blk, blk), lambda b, n: (n, 0, 0))
    return pl.pallas_call(
        _rglru_kernel,
        grid=(bsz, nblk),
        in_specs=[pl.BlockSpec((1, s_len, blk), lambda b, n: (b, 0, x_off + n)),
                  pl.BlockSpec((1, s_len, blk), lambda b, n: (b, 0, z_off + n)),
                  vec(CONV_W), vec(1), mat, vec(1), mat, vec(1), vec(1)],
        out_specs=pl.BlockSpec((1, s_len, blk), lambda b, n: (b, 0, n)),
        out_shape=jax.ShapeDtypeStruct((bsz, s_len, nblk * blk), BF16),
        scratch_shapes=[pltpu.VMEM((s_len, blk), F32)] * 3
        + [pltpu.VMEM((SUBLANES, s_len // SUBLANES, blk), F32)] * 2,
        compiler_params=pltpu.CompilerParams(
            dimension_semantics=("parallel", "parallel"), vmem_limit_bytes=VMEM_LIMIT),
        name="rglru",
    )(u, u, conv_w, conv_b.reshape(1, -1), w_r, b_r.reshape(1, -1), w_i, b_i.reshape(1, -1),
      lam.reshape(1, -1))


ML_CHUNK = 128
ML_UNROLL = 8


def _mlstm_gates_kernel(g_ref, gb_ref, mu_ref, gi_ref, en_ref, ks_ref, dec_ref, rt_ref):
    s_len = g_ref.shape[1]
    c = ML_CHUNK
    t = g_ref[0] + gb_ref[...]
    li = t[:, :LANES]
    lf = _log_sigmoid(t[:, LANES:])
    row = lax.broadcasted_iota(jnp.int32, (s_len, LANES), 0) % c

    def scan(x, op, fill):
        s = 1
        while s < c:
            x = op(x, jnp.where(row >= s, pltpu.roll(x, s, 0), fill))
            s *= 2
        return x

    b = scan(lf, jnp.add, 0.0)
    r = li - b
    rho = scan(r, jnp.maximum, -jnp.inf)
    m = jnp.zeros((1, LANES), F32)
    for ci in range(s_len // c):
        sl = slice(ci * c, (ci + 1) * c)
        b_c, r_c, rho_c = b[sl], r[sl], rho[sl]
        b_last = b_c[c - 1:c, :]
        m_new = b_last + jnp.maximum(m, rho_c[c - 1:c, :])
        mu = jnp.maximum(rho_c, m)
        mu_ref[0, sl, :] = mu
        gi_ref[0, sl, :] = jnp.exp(m - mu)
        en_ref[0, sl, :] = jnp.exp(-(b_c + mu))
        ks_ref[0, sl, :] = jnp.exp(b_last + r_c - m_new)
        dec_ref[0, ci:ci + 1, :] = jnp.exp(b_last + m - m_new)
        rt_ref[0, :, sl] = r_c.T
        m = m_new


def _mlstm_gates(gates, gate_bias):
    bsz, s_len, gw = gates.shape
    nc = s_len // ML_CHUNK
    col = jax.ShapeDtypeStruct((bsz, s_len, LANES), F32)
    col_spec = pl.BlockSpec((1, s_len, LANES), lambda b: (b, 0, 0))
    return pl.pallas_call(
        _mlstm_gates_kernel,
        grid=(bsz,),
        in_specs=[pl.BlockSpec((1, s_len, gw), lambda b: (b, 0, 0)),
                  pl.BlockSpec((1, gw), lambda b: (0, 0))],
        out_specs=[col_spec] * 4 + [pl.BlockSpec((1, nc, LANES), lambda b: (b, 0, 0)),
                                    pl.BlockSpec((1, LANES, s_len), lambda b: (b, 0, 0))],
        out_shape=[col] * 4 + [jax.ShapeDtypeStruct((bsz, nc, LANES), F32),
                               jax.ShapeDtypeStruct((bsz, LANES, s_len), F32)],
        compiler_params=pltpu.CompilerParams(
            dimension_semantics=("parallel",), vmem_limit_bytes=VMEM_LIMIT),
        name="mlstm_gates",
    )(gates, gate_bias)


def _mlstm_kernel(q_ref, k_ref, v_ref, og_ref, z_ref, mu_ref, gi_ref, en_ref, ks_ref, dec_ref,
                  rt_ref, cwq_ref, cbq_ref, cwk_ref, cbk_ref, cn_ref, o_ref, qs_ref, ks_s_ref,
                  cst_ref, nst_ref):
    s_len = q_ref.shape[1]
    dk = q_ref.shape[2]
    c = ML_CHUNK
    assert c == dk == LANES
    h = pl.program_id(1)

    qs_ref[...] = _silu(_causal_conv(q_ref, cwq_ref, cbq_ref)) * (dk ** -0.5)
    ks_s_ref[...] = _silu(_causal_conv(k_ref, cwk_ref, cbk_ref))
    cst_ref[...] = jnp.zeros_like(cst_ref)
    nst_ref[...] = jnp.zeros_like(nst_ref)

    row = lax.broadcasted_iota(jnp.int32, (c, c), 0)
    col = lax.broadcasted_iota(jnp.int32, (c, c), 1)
    causal = col <= row
    head = col == h
    cn = cn_ref[...]

    def pick(x):
        return jnp.sum(jnp.where(head, x, 0.0), axis=1, keepdims=True)

    dec_all = dec_ref[0]
    chunk_id = lax.broadcasted_iota(jnp.int32, dec_all.shape, 0)
    head_lane = lax.broadcasted_iota(jnp.int32, (1, LANES), 1) == h
    head_row = lax.broadcasted_iota(jnp.int32, (SUBLANES, c), 0) == h

    def chunk(ci, carry):
        r0 = pl.multiple_of(ci * c, c)
        q = qs_ref[pl.ds(r0, c), :]
        k = ks_s_ref[pl.ds(r0, c), :]
        v = v_ref[0, pl.ds(r0, c), :]
        mu = pick(mu_ref[0, pl.ds(r0, c), :])
        g_inter = pick(gi_ref[0, pl.ds(r0, c), :])
        e_negm = pick(en_ref[0, pl.ds(r0, c), :])
        kscale = pick(ks_ref[0, pl.ds(r0, c), :])
        decay = jnp.sum(jnp.where(chunk_id == ci, dec_all, 0.0), axis=0, keepdims=True)
        decay = jnp.sum(jnp.where(head_lane, decay, 0.0), axis=1, keepdims=True)
        r_row = jnp.sum(jnp.where(head_row, rt_ref[0, :, pl.ds(r0, c)], 0.0),
                        axis=0, keepdims=True)
        wts = jnp.where(causal, jnp.exp(jnp.minimum(r_row - mu, 0.0)), 0.0)
        qk = _dot_nt(q, k)
        kw = k * kscale
        upd = _dot_tn(kw, v)

        cst = cst_ref[...]
        n_st = nst_ref[...]
        sc = qk * wts
        inter = _dot(q, cst)
        intra = _dot(sc, v)
        cst_ref[...] = decay * cst + upd
        nst_ref[...] = decay * n_st + jnp.sum(kw, axis=0, keepdims=True)

        num = g_inter * inter + intra
        den = (g_inter * jnp.sum(q * n_st, axis=1, keepdims=True)
               + jnp.sum(sc, axis=1, keepdims=True))
        hout = num / jnp.maximum(jnp.abs(den), e_negm)

        yn = hout * lax.rsqrt(jnp.mean(hout * hout, axis=-1, keepdims=True) + EPS) * cn
        og = og_ref[0, pl.ds(r0, c), :]
        z = z_ref[0, pl.ds(r0, c), :]
        o_ref[0, pl.ds(r0, c), :] = (yn * _sigmoid(og) * _silu(z)).astype(o_ref.dtype)
        return carry

    lax.fori_loop(0, s_len // c, chunk, 0, unroll=ML_UNROLL)


def _mlstm(u, gate_terms, conv_w, conv_b, c_norm, *, heads, dk, dv):
    bsz, s_len, _ = u.shape
    nc = s_len // ML_CHUNK
    col_spec = pl.BlockSpec((1, s_len, LANES), lambda b, h: (b, 0, 0))
    qk_blk = lambda off: pl.BlockSpec((1, s_len, dk), lambda b, h, off=off: (b, 0, off + h))
    v_blk = lambda off: pl.BlockSpec((1, s_len, dv), lambda b, h, off=off: (b, 0, off + h))
    cw = lambda off: pl.BlockSpec((CONV_W, dk), lambda b, h, off=off: (0, off + h))
    cb = lambda off: pl.BlockSpec((1, dk), lambda b, h, off=off: (0, off + h))
    nqk = 2 * heads * dk // dv
    return pl.pallas_call(
        _mlstm_kernel,
        grid=(bsz, heads),
        in_specs=[qk_blk(0), qk_blk(heads),
                  v_blk(nqk), v_blk(nqk + heads), v_blk(nqk + 2 * heads),
                  col_spec, col_spec, col_spec, col_spec,
                  pl.BlockSpec((1, nc, LANES), lambda b, h: (b, 0, 0)),
                  pl.BlockSpec((1, SUBLANES, s_len), lambda b, h: (b, 0, 0)),
                  cw(0), cb(0), cw(heads), cb(heads),
                  pl.BlockSpec((1, dv), lambda b, h: (0, h))],
        out_specs=pl.BlockSpec((1, s_len, dv), lambda b, h: (b, 0, h)),
        out_shape=jax.ShapeDtypeStruct((bsz, s_len, heads * dv), BF16),
        scratch_shapes=[pltpu.VMEM((s_len, dk), F32), pltpu.VMEM((s_len, dk), F32),
                        pltpu.VMEM((dk, dv), F32), pltpu.VMEM((1, dk), F32)],
        compiler_params=pltpu.CompilerParams(
            dimension_semantics=("parallel", "parallel"), vmem_limit_bytes=VMEM_LIMIT),
        name="mlstm",
    )(u, u, u, u, u, *gate_terms, conv_w, conv_b.reshape(1, -1), conv_w,
      conv_b.reshape(1, -1), c_norm.reshape(1, -1))


def _out_kernel(h_ref, *rest, n_y, final):
    y_refs = rest[:n_y]
    wo_refs = rest[n_y:2 * n_y]
    p_ref, pw_ref, pn_ref, gw_ref, norm_ref = rest[2 * n_y:2 * n_y + 5]
    out_refs = rest[2 * n_y + 5:]
    mix = jnp.dot(y_refs[0][...], wo_refs[0][...], preferred_element_type=F32)
    for y_ref, wo_ref in zip(y_refs[1:], wo_refs[1:]):
        mix = mix + jnp.dot(y_ref[...], wo_ref[...], preferred_element_type=F32)
    h1 = h_ref[...] + mix
    pe = _dot(p_ref[...], pw_ref[...])
    pe = pe * lax.rsqrt(jnp.mean(pe * pe, axis=-1, keepdims=True) + EPS) * pn_ref[...]
    h2 = h1 + _sigmoid(_dot(h1, gw_ref[...])) * pe
    hn = h2 * lax.rsqrt(jnp.mean(h2 * h2, axis=-1, keepdims=True) + EPS) * norm_ref[...]
    if final:
        (o_ref,) = out_refs
        o_ref[...] = hn
    else:
        o_ref, hn_ref = out_refs
        o_ref[...] = h2
        hn_ref[...] = hn.astype(hn_ref.dtype)


def _out_proj(h, ys, w_out, p, layer, ple_w, ple_norm, gate_w, norm, final, *, tm=512):
    m, d = h.shape
    const = lambda shape: pl.BlockSpec(shape, lambda i: (0, 0), pipeline_mode=pl.Buffered(1))
    in_specs = [pl.BlockSpec((tm, d), lambda i: (i, 0))]
    in_specs += [pl.BlockSpec((tm, y.shape[1]), lambda i: (i, 0)) for y in ys]
    assert all(y.shape[1] == ys[0].shape[1] for y in ys)
    in_specs += [pl.BlockSpec((y.shape[1], d), lambda i, r=r: (r, 0), pipeline_mode=pl.Buffered(1))
                 for r, y in enumerate(ys)]
    in_specs += [pl.BlockSpec((None, tm, p.shape[2]), lambda i: (layer, i, 0)),
                 const(ple_w.shape), const((1, d)), const(gate_w.shape), const((1, d))]
    args = [h, *ys, *([w_out] * len(ys)), p, ple_w, ple_norm.reshape(1, d), gate_w,
            norm.reshape(1, d)]
    row_spec = pl.BlockSpec((tm, d), lambda i: (i, 0))
    return pl.pallas_call(
        functools.partial(_out_kernel, n_y=len(ys), final=final),
        grid=(m // tm,),
        in_specs=in_specs,
        out_specs=row_spec if final else [row_spec, row_spec],
        out_shape=(jax.ShapeDtypeStruct((m, d), F32) if final else
                   [jax.ShapeDtypeStruct((m, d), F32), jax.ShapeDtypeStruct((m, d), BF16)]),
        compiler_params=pltpu.CompilerParams(
            dimension_semantics=("parallel",), vmem_limit_bytes=VMEM_LIMIT),
        name="out_proj_final" if final else "out_proj",
    )(*args)


def kernel(x, p, e_norm, e_w_in, a_lb_logits, a_norm, b_conv_w, b_conv_b, b_w_r, b_b_r, b_w_i, b_b_i, b_lambda, e_w_out, o_norm, o_w_in, c_conv_w, c_conv_b, c_b_i, c_b_f, c_norm, o_w_out, ple_w, ple_norm, ple_gate_w, final_norm):
    bsz, s_len, d = x.shape
    depth = p.shape[0]
    m = bsz * s_len
    a_width = a_norm.shape[1]
    b_width = b_lambda.shape[1]
    a_dk = LANES
    a_heads = a_width // a_dk
    c_heads = c_b_i.shape[1]
    c_dv = c_norm.shape[1] // c_heads
    c_dk = c_conv_w.shape[2] // (2 * c_heads)
    c_main = o_w_in.shape[2] - 2 * c_heads

    h = x.reshape(m, d)
    hn = None
    layer_norm = lambda i: e_norm[i // 2] if i % 2 == 0 else o_norm[i // 2]
    for i in range(depth):
        j = i // 2
        last = i == depth - 1
        xin, g = (h, layer_norm(i)) if hn is None else (hn, None)
        if i % 2 == 0:
            u = _in_proj(xin, g, e_w_in[j].astype(BF16)).reshape(bsz, s_len, -1)
            ya = _hgrn2(u, a_lb_logits, a_norm[j], layer=j, heads=a_heads, dk=a_dk)
            x_off = (2 * a_heads * a_dk + 2 * a_width) // LANES
            yb = _rglru(u, b_conv_w[j], b_conv_b[j], b_w_r[j].astype(BF16), b_b_r[j],
                        b_w_i[j].astype(BF16), b_b_i[j], b_lambda[j],
                        x_off=x_off, z_off=x_off + b_width // LANES)
            w_out = e_w_out[j].astype(BF16)
            ys = [ya.reshape(m, a_width), yb.reshape(m, b_width)]
        else:
            w_in = o_w_in[j]
            pad = ((0, 0), (0, LANES - c_heads))
            wg = jnp.concatenate([jnp.pad(w_in[:, c_main:c_main + c_heads], pad),
                                  jnp.pad(w_in[:, c_main + c_heads:], pad)], axis=1).astype(BF16)
            u, gates = _in_proj(xin, g, w_in.astype(BF16), wg, n=c_main)
            gate_bias = jnp.concatenate([jnp.pad(c_b_i[j], pad[1]),
                                         jnp.pad(c_b_f[j], pad[1])]).reshape(1, 2 * LANES)
            gate_terms = _mlstm_gates(gates.reshape(bsz, s_len, 2 * LANES), gate_bias)
            yc = _mlstm(u.reshape(bsz, s_len, -1), gate_terms, c_conv_w[j], c_conv_b[j],
                        c_norm[j], heads=c_heads, dk=c_dk, dv=c_dv)
            ys = [yc.reshape(m, -1)]
            w_out = o_w_out[j].astype(BF16)
        res = _out_proj(h, ys, w_out, p.reshape(depth, m, -1), i, ple_w[i].astype(BF16),
                        ple_norm[i], ple_gate_w[i].astype(BF16),
                        final_norm if last else layer_norm(i + 1), last)
        h, hn = (res, None) if last else res
    return h.reshape(bsz, s_len, d)
```

```python
import functools

import jax
import jax.numpy as jnp
from jax import lax
from jax.experimental import pallas as pl
from jax.experimental.pallas import tpu as pltpu

EPS = 1e-6
LRU_C = 8.0
CONV_W = 4
LANES = 128
SUBLANES = 8
VMEM_LIMIT = 56 * 1024 * 1024
BF16 = jnp.bfloat16
F32 = jnp.float32


def _sigmoid(x):
    return jax.nn.sigmoid(x)


def _silu(x):
    return x * jax.nn.sigmoid(x)


def _softplus(x):
    return jnp.maximum(x, 0.0) + jnp.log1p(jnp.exp(-jnp.abs(x)))


def _log_sigmoid(x):
    return -_softplus(-x)


def _shift_rows(x, s, row):
    return jnp.where(row >= s, pltpu.roll(x, s, 0), 0.0)


def _cumsum_rows(x, row):
    s = 1
    while s < x.shape[0]:
        x = x + _shift_rows(x, s, row)
        s *= 2
    return x


def _cummax_rows(x, row):
    s = 1
    while s < x.shape[0]:
        x = jnp.maximum(x, jnp.where(row >= s, pltpu.roll(x, s, 0), -jnp.inf))
        s *= 2
    return x


def _causal_conv(x_ref, cw_ref, cb_ref):
    s_len = x_ref.shape[1]
    taps = [cw_ref[k:k + 1, :] for k in range(CONV_W)]
    acc = cb_ref[...] + x_ref[0, SUBLANES:, :] * taps[CONV_W - 1]
    for j in range(1, CONV_W):
        acc = acc + x_ref[0, SUBLANES - j:s_len - j, :] * taps[CONV_W - 1 - j]
    x0 = x_ref[0, 0:SUBLANES, :]
    row = lax.broadcasted_iota(jnp.int32, x0.shape, 0)
    acc0 = cb_ref[...] + x0 * taps[CONV_W - 1]
    for j in range(1, CONV_W):
        acc0 = acc0 + _shift_rows(x0, j, row) * taps[CONV_W - 1 - j]
    return jnp.concatenate([acc0, acc], axis=0)


def _dot(a, b):
    return jnp.dot(a.astype(BF16), b.astype(BF16), preferred_element_type=F32)


def _dot_nt(a, b):
    return lax.dot_general(a.astype(BF16), b.astype(BF16), (((1,), (1,)), ((), ())),
                           preferred_element_type=F32)


def _dot_tn(a, b):
    return lax.dot_general(a.astype(BF16), b.astype(BF16), (((0,), (0,)), ((), ())),
                           preferred_element_type=F32)


NORM_ROWS = 256


def _in_proj_kernel(*refs, with_norm, with_gates, transposed):
    refs = list(refs)
    x_ref = refs.pop(0)
    g_ref = refs.pop(0) if with_norm else None
    w_ref = refs.pop(0)
    wg_ref = refs.pop(0) if with_gates else None
    o_ref = refs.pop(0)
    og_ref = refs.pop(0) if with_gates else None
    hn_ref = refs.pop(0) if with_norm else x_ref
    mm = _dot_nt if transposed else functools.partial(jnp.dot, preferred_element_type=F32)

    @pl.when(pl.program_id(1) == 0)
    def _():
        if with_norm:
            def body(i, carry):
                r0 = pl.multiple_of(i * NORM_ROWS, NORM_ROWS)
                x = x_ref[pl.ds(r0, NORM_ROWS), :]
                ms = jnp.mean(x * x, axis=-1, keepdims=True)
                hn_ref[pl.ds(r0, NORM_ROWS), :] = (x * lax.rsqrt(ms + EPS) * g_ref[...]).astype(BF16)
                return carry
            lax.fori_loop(0, x_ref.shape[0] // NORM_ROWS, body, 0)
        if with_gates:
            og_ref[...] = mm(hn_ref[...], wg_ref[...])

    o_ref[...] = mm(hn_ref[...], w_ref[...])


IN_PROJ_ROWS = 1024
IN_PROJ_VMEM = 46 * 1024 * 1024


def _in_proj_cols(n, tm, d, gw, x_bytes, with_norm):
    for tn in range(n, 0, -512):
        need = (2 * tm * d * x_bytes + (tm * d * 2 if with_norm else 0) + 2 * d * tn * 2
                + 2 * tm * tn * 4 + 2 * (d * gw * 2 + tm * gw * 4))
        if n % tn == 0 and need <= IN_PROJ_VMEM:
            return tn
    raise ValueError("no column tile fits")


def _in_proj(x, g, w, wg=None, *, n=None, transposed=False, tm=IN_PROJ_ROWS):
    m, d = x.shape
    n = w.shape[0 if transposed else 1] if n is None else n
    with_gates = wg is not None
    with_norm = g is not None
    gw = wg.shape[0 if transposed else 1] if with_gates else 0
    w_spec = (lambda cols, idx: pl.BlockSpec((cols, d), lambda i, j: (idx(j), 0)) if transposed
              else pl.BlockSpec((d, cols), lambda i, j: (0, idx(j))))
    tn = _in_proj_cols(n, tm, d, gw, x.dtype.itemsize, with_norm)
    in_specs = [pl.BlockSpec((tm, d), lambda i, j: (i, 0))]
    args = [x]
    if with_norm:
        in_specs.append(pl.BlockSpec((1, d), lambda i, j: (0, 0)))
        args.append(g.reshape(1, d))
    in_specs.append(w_spec(tn, lambda j: j))
    args.append(w)
    out_shape = [jax.ShapeDtypeStruct((m, n), F32)]
    out_specs = [pl.BlockSpec((tm, tn), lambda i, j: (i, j))]
    if with_gates:
        in_specs.append(w_spec(gw, lambda j: 0))
        out_shape.append(jax.ShapeDtypeStruct((m, gw), F32))
        out_specs.append(pl.BlockSpec((tm, gw), lambda i, j: (i, 0)))
        args.append(wg)
    res = pl.pallas_call(
        functools.partial(_in_proj_kernel, with_norm=with_norm, with_gates=with_gates,
                          transposed=transposed),
        grid=(m // tm, n // tn),
        in_specs=in_specs,
        out_specs=out_specs,
        out_shape=out_shape,
        scratch_shapes=[pltpu.VMEM((tm, d), BF16)] if with_norm else [],
        compiler_params=pltpu.CompilerParams(
            dimension_semantics=("parallel", "arbitrary"), vmem_limit_bytes=VMEM_LIMIT),
        name="in_proj_gates" if with_gates else "in_proj",
    )(*args)
    return res if with_gates else res[0]


HG_CHUNK = 128
HG_GROUPS = HG_CHUNK // SUBLANES
HG_INFLIGHT = 4


def _hgrn2_kernel(q_ref, f_ref, i_ref, z_ref, lbl_ref, an_ref, o_ref,
                  st_ref, qs_ref, ks_ref, bs_ref, od_ref, tri_ref, *, layer):
    s_len = q_ref.shape[1]
    dk = q_ref.shape[2]
    c = HG_CHUNK
    ng = HG_GROUPS
    assert c == dk == LANES and i_ref.shape[2] == dk

    lg = lbl_ref[...]
    e = jnp.exp(lg - jnp.max(lg, axis=0, keepdims=True))
    sm = e / jnp.sum(e, axis=0, keepdims=True)
    lb = jnp.sum(sm[:layer + 1], axis=0, keepdims=True)
    an = an_ref[...]

    rowc = lax.broadcasted_iota(jnp.int32, (c, c), 0)
    colc = lax.broadcasted_iota(jnp.int32, (c, c), 1)
    tri_ref[...] = (colc <= rowc).astype(BF16)
    same_group = (colc % ng) == (rowc % ng)
    lane = lax.broadcasted_iota(jnp.int32, (SUBLANES, c), 1)
    diag_off = lane - lax.broadcasted_iota(jnp.int32, (SUBLANES, c), 0)
    zeros8 = jnp.zeros((SUBLANES, dk), F32)
    zeros16 = jnp.zeros((ng, dk), F32)

    st_ref[...] = jnp.zeros_like(st_ref)

    def rows8(x, r):
        return x[r * SUBLANES:(r + 1) * SUBLANES, :]

    def chunk(ci, slot):
        qs, ks, bs, od = qs_ref.at[slot], ks_ref.at[slot], bs_ref.at[slot], od_ref.at[slot]
        r0 = pl.multiple_of(ci * c, c)
        qp = q_ref[0, pl.ds(r0, c), :]
        fp = f_ref[0, pl.ds(r0, c), :]
        v = i_ref[0, pl.ds(r0, c), :]
        q = _silu(qp)
        gate = (1.0 - lb) * _sigmoid(fp)
        f = lb + gate
        k = (1.0 - lb) - gate

        lf = jnp.log2(f)
        hi = lf.astype(BF16)
        r1 = lf - hi.astype(F32)
        mid = r1.astype(BF16)
        lo = (r1 - mid.astype(F32)).astype(BF16)
        bb = jnp.dot(tri_ref[...], jnp.concatenate([hi, mid, lo], axis=1),
                     preferred_element_type=F32)
        yield
        b2 = bb[:, :dk] + bb[:, dk:2 * dk] + bb[:, 2 * dk:]
        qs[...] = q
        ks[...] = k
        bs[...] = b2
        b_last = bs[c - 1:c, :]
        upd = _dot_tn(v, k * jnp.exp2(b_last - b2))

        ps, rights = [], []
        for j in range(ng.bit_length() - 1):
            lhs, rhs, rights_j = [], [], []
            for r in range(ng):
                mid_grp = ((r >> (j + 1)) << (j + 1)) | ((1 << j) - 1)
                beta = bs[mid_grp * SUBLANES + SUBLANES - 1:(mid_grp + 1) * SUBLANES, :]
                if (r >> j) & 1:
                    lhs.append(rows8(q, r) * jnp.exp2(rows8(b2, r) - beta))
                    rhs.append(zeros8)
                    rights_j.append(r)
                else:
                    rhs.append(rows8(k, r) * jnp.exp2(beta - rows8(b2, r)))
            ps.append(_dot_nt(jnp.concatenate(lhs, axis=0), jnp.concatenate(rhs, axis=0)))
            rights.append(rights_j)
        yield

        st = st_ref[...]
        o = _dot_nt(q * jnp.exp2(b2), st)
        st_ref[...] = st * jnp.exp2(b_last) + upd

        def permuted(ref):
            return [ref[pl.ds(pos, ng, stride=SUBLANES), :] for pos in range(SUBLANES)]

        q_p, k_p, b_p = permuted(qs), permuted(ks), permuted(bs)
        lhs_slots, rhs_slots = [], []
        for j in range(SUBLANES.bit_length() - 1):
            for blk in range(SUBLANES >> (j + 1)):
                mid_pos = (blk << (j + 1)) | ((1 << j) - 1)
                lhs, rhs = [], []
                for pos in range(SUBLANES):
                    if pos >> (j + 1) != blk:
                        lhs.append(zeros16)
                        rhs.append(zeros16)
                    elif (pos >> j) & 1:
                        lhs.append(q_p[pos] * jnp.exp2(b_p[pos] - b_p[mid_pos]))
                        rhs.append(zeros16)
                    else:
                        lhs.append(zeros16)
                        rhs.append(k_p[pos] * jnp.exp2(b_p[mid_pos] - b_p[pos]))
                lhs_slots.append(jnp.concatenate(lhs, axis=0).astype(BF16))
                rhs_slots.append(jnp.concatenate(rhs, axis=0).astype(BF16))
        pd = _dot_nt(jnp.concatenate(lhs_slots, axis=1), jnp.concatenate(rhs_slots, axis=1))
        yield

        dg = jnp.sum(q * k, axis=1, keepdims=True)
        a_rows = [jnp.where(diag_off == r * SUBLANES, rows8(dg, r), 0.0) for r in range(ng)]
        for j, (p, rights_j) in enumerate(zip(ps, rights)):
            for i, r in enumerate(rights_j):
                right_start = (((r >> (j + 1)) << (j + 1)) | (1 << j)) * SUBLANES
                a_rows[r] = jnp.where(lane < right_start, rows8(p, i), a_rows[r])
        o = o + _dot(jnp.concatenate(a_rows, axis=0), v)
        v_p = [i_ref[0, pl.ds(r0 + pos, ng, stride=SUBLANES), :] for pos in range(SUBLANES)]
        od[...] = _dot(jnp.where(same_group, pd, 0.0), jnp.concatenate(v_p, axis=0))
        yield

        o = o + jnp.concatenate(
            [od[pl.ds(g, SUBLANES, stride=ng), :] for g in range(ng)], axis=0)
        yn = o * lax.rsqrt(jnp.mean(o * o, axis=-1, keepdims=True) + EPS) * an
        z = z_ref[0, pl.ds(r0, c), :]
        o_ref[0, pl.ds(r0, c), :] = (yn * _silu(z)).astype(o_ref.dtype)

    def chunks(i, carry):
        stages = [chunk(i * HG_INFLIGHT + slot, slot) for slot in range(HG_INFLIGHT)]
        while stages:
            stages = [g for g in stages if next(g, stages) is not stages]
        return carry

    lax.fori_loop(0, s_len // (c * HG_INFLIGHT), chunks, 0, unroll=2)


def _hgrn2(u, lb_logits, a_norm, *, layer, heads, dk):
    bsz, s_len, _ = u.shape
    nl = lb_logits.shape[0]
    blk = lambda off: pl.BlockSpec((1, s_len, dk), lambda b, h, off=off: (b, 0, off + h))
    return pl.pallas_call(
        functools.partial(_hgrn2_kernel, layer=layer),
        grid=(bsz, heads),
        in_specs=[blk(0), blk(heads), blk(2 * heads), blk(3 * heads),
                  pl.BlockSpec((nl, dk), lambda b, h: (0, h)),
                  pl.BlockSpec((1, dk), lambda b, h: (0, h))],
        out_specs=pl.BlockSpec((1, s_len, dk), lambda b, h: (b, 0, h)),
        out_shape=jax.ShapeDtypeStruct((bsz, s_len, heads * dk), BF16),
        scratch_shapes=[pltpu.VMEM((dk, dk), F32)]
        + [pltpu.VMEM((HG_INFLIGHT, HG_CHUNK, dk), F32)] * 4
        + [pltpu.VMEM((HG_CHUNK, HG_CHUNK), BF16)],
        compiler_params=pltpu.CompilerParams(
            dimension_semantics=("parallel", "parallel"), vmem_limit_bytes=VMEM_LIMIT),
        name="hgrn2",
    )(u, u, u, u, lb_logits, a_norm.reshape(1, -1))


def _rglru_kernel(x_ref, z_ref, cw_ref, cb_ref, wr_ref, br_ref, wi_ref, bi_ref, lam_ref,
                  o_ref, a_ref, u_ref, h_ref, ca_ref, cu_ref):
    s_len = x_ref.shape[1]
    w = x_ref.shape[2]
    ng = s_len // SUBLANES
    xc = _causal_conv(x_ref, cw_ref, cb_ref)
    r = _sigmoid(_dot(xc, wr_ref[0]) + br_ref[...])
    ig = _sigmoid(_dot(xc, wi_ref[0]) + bi_ref[...])
    log_a = -LRU_C * r * _softplus(-lam_ref[...])
    a = jnp.exp(log_a)
    a_ref[...] = a
    u_ref[...] = jnp.sqrt(-jnp.tanh(log_a) * (a * a + 1.0)) * (ig * xc)

    ca = cu = None
    for pos in range(SUBLANES):
        ap = a_ref[pl.ds(pos, ng, stride=SUBLANES), :]
        up = u_ref[pl.ds(pos, ng, stride=SUBLANES), :]
        ca, cu = (ap, up) if pos == 0 else (ap * ca, ap * cu + up)
        ca_ref[pos] = ca
        cu_ref[pos] = cu
    grp = lax.broadcasted_iota(jnp.int32, (ng, w), 0)
    ta, tu = ca, cu
    s = 1
    while s < ng:
        m = grp >= s
        tu = tu + ta * jnp.where(m, pltpu.roll(tu, s, 0), 0.0)
        ta = ta * jnp.where(m, pltpu.roll(ta, s, 0), 1.0)
        s *= 2
    h_in = jnp.where(grp >= 1, pltpu.roll(tu, 1, 0), 0.0)
    for pos in range(SUBLANES):
        h_ref[pl.ds(pos, ng, stride=SUBLANES), :] = ca_ref[pos] * h_in + cu_ref[pos]
    o_ref[0] = (h_ref[...] * _silu(z_ref[0])).astype(o_ref.dtype)


def _rglru(u, conv_w, conv_b, w_r, b_r, w_i, b_i, lam, *, x_off, z_off):
    bsz, s_len, _ = u.shape
    nblk, blk, _ = w_r.shape
    vec = lambda rows: pl.BlockSpec((rows, blk), lambda b, n: (0, n))
    mat = pl.BlockSpec((1, blk, blk), lambda b, n: (n, 0, 0))
    return pl.pallas_call(
        _rglru_kernel,
        grid=(bsz, nblk),
        in_specs=[pl.BlockSpec((1, s_len, blk), lambda b, n: (b, 0, x_off + n)),
                  pl.BlockSpec((1, s_len, blk), lambda b, n: (b, 0, z_off + n)),
                  vec(CONV_W), vec(1), mat, vec(1), mat, vec(1), vec(1)],
        out_specs=pl.BlockSpec((1, s_len, blk), lambda b, n: (b, 0, n)),
        out_shape=jax.ShapeDtypeStruct((bsz, s_len, nblk * blk), BF16),
        scratch_shapes=[pltpu.VMEM((s_len, blk), F32)] * 3
        + [pltpu.VMEM((SUBLANES, s_len // SUBLANES, blk), F32)] * 2,
        compiler_params=pltpu.CompilerParams(
            dimension_semantics=("parallel", "parallel"), vmem_limit_bytes=VMEM_LIMIT),
        name="rglru",
    )(u, u, conv_w, conv_b.reshape(1, -1), w_r, b_r.reshape(1, -1), w_i, b_i.reshape(1, -1),
      lam.reshape(1, -1))


ML_CHUNK = 128
ML_UNROLL = 8


def _mlstm_gates_kernel(g_ref, gb_ref, mu_ref, gi_ref, en_ref, ks_ref, dec_ref, rt_ref):
    s_len = g_ref.shape[1]
    c = ML_CHUNK
    t = g_ref[0] + gb_ref[...]
    li = t[:, :LANES]
    lf = _log_sigmoid(t[:, LANES:])
    row = lax.broadcasted_iota(jnp.int32, (s_len, LANES), 0) % c

    def scan(x, op, fill):
        s = 1
        while s < c:
            x = op(x, jnp.where(row >= s, pltpu.roll(x, s, 0), fill))
            s *= 2
        return x

    b = scan(lf, jnp.add, 0.0)
    r = li - b
    rho = scan(r, jnp.maximum, -jnp.inf)
    m = jnp.zeros((1, LANES), F32)
    for ci in range(s_len // c):
        sl = slice(ci * c, (ci + 1) * c)
        b_c, r_c, rho_c = b[sl], r[sl], rho[sl]
        b_last = b_c[c - 1:c, :]
        m_new = b_last + jnp.maximum(m, rho_c[c - 1:c, :])
        mu = jnp.maximum(rho_c, m)
        mu_ref[0, sl, :] = mu
        gi_ref[0, sl, :] = jnp.exp(m - mu)
        en_ref[0, sl, :] = jnp.exp(-(b_c + mu))
        ks_ref[0, sl, :] = jnp.exp(b_last + r_c - m_new)
        dec_ref[0, ci:ci + 1, :] = jnp.exp(b_last + m - m_new)
        rt_ref[0, :, sl] = r_c.T
        m = m_new


def _mlstm_gates(gates, gate_bias):
    bsz, s_len, gw = gates.shape
    nc = s_len // ML_CHUNK
    col = jax.ShapeDtypeStruct((bsz, s_len, LANES), F32)
    col_spec = pl.BlockSpec((1, s_len, LANES), lambda b: (b, 0, 0))
    return pl.pallas_call(
        _mlstm_gates_kernel,
        grid=(bsz,),
        in_specs=[pl.BlockSpec((1, s_len, gw), lambda b: (b, 0, 0)),
                  pl.BlockSpec((1, gw), lambda b: (0, 0))],
        out_specs=[col_spec] * 4 + [pl.BlockSpec((1, nc, LANES), lambda b: (b, 0, 0)),
                                    pl.BlockSpec((1, LANES, s_len), lambda b: (b, 0, 0))],
        out_shape=[col] * 4 + [jax.ShapeDtypeStruct((bsz, nc, LANES), F32),
                               jax.ShapeDtypeStruct((bsz, LANES, s_len), F32)],
        compiler_params=pltpu.CompilerParams(
            dimension_semantics=("parallel",), vmem_limit_bytes=VMEM_LIMIT),
        name="mlstm_gates",
    )(gates, gate_bias)


def _mlstm_kernel(q_ref, k_ref, v_ref, og_ref, z_ref, mu_ref, gi_ref, en_ref, ks_ref, dec_ref,
                  rt_ref, cwq_ref, cbq_ref, cwk_ref, cbk_ref, cn_ref, o_ref, qs_ref, ks_s_ref,
                  cst_ref, nst_ref):
    s_len = q_ref.shape[1]
    dk = q_ref.shape[2]
    c = ML_CHUNK
    assert c == dk == LANES
    h = pl.program_id(1)

    qs_ref[...] = _silu(_causal_conv(q_ref, cwq_ref, cbq_ref)) * (dk ** -0.5)
    ks_s_ref[...] = _silu(_causal_conv(k_ref, cwk_ref, cbk_ref))
    cst_ref[...] = jnp.zeros_like(cst_ref)
    nst_ref[...] = jnp.zeros_like(nst_ref)

    row = lax.broadcasted_iota(jnp.int32, (c, c), 0)
    col = lax.broadcasted_iota(jnp.int32, (c, c), 1)
    causal = col <= row
    head = col == h
    cn = cn_ref[...]

    def pick(x):
        return jnp.sum(jnp.where(head, x, 0.0), axis=1, keepdims=True)

    dec_all = dec_ref[0]
    chunk_id = lax.broadcasted_iota(jnp.int32, dec_all.shape, 0)
    head_lane = lax.broadcasted_iota(jnp.int32, (1, LANES), 1) == h
    head_row = lax.broadcasted_iota(jnp.int32, (SUBLANES, c), 0) == h

    def chunk(ci, carry):
        r0 = pl.multiple_of(ci * c, c)
        q = qs_ref[pl.ds(r0, c), :]
        k = ks_s_ref[pl.ds(r0, c), :]
        v = v_ref[0, pl.ds(r0, c), :]
        mu = pick(mu_ref[0, pl.ds(r0, c), :])
        g_inter = pick(gi_ref[0, pl.ds(r0, c), :])
        e_negm = pick(en_ref[0, pl.ds(r0, c), :])
        kscale = pick(ks_ref[0, pl.ds(r0, c), :])
        decay = jnp.sum(jnp.where(chunk_id == ci, dec_all, 0.0), axis=0, keepdims=True)
        decay = jnp.sum(jnp.where(head_lane, decay, 0.0), axis=1, keepdims=True)
        r_row = jnp.sum(jnp.where(head_row, rt_ref[0, :, pl.ds(r0, c)], 0.0),
                        axis=0, keepdims=True)
        wts = jnp.where(causal, jnp.exp(jnp.minimum(r_row - mu, 0.0)), 0.0)
        qk = _dot_nt(q, k)
        kw = k * kscale
        upd = _dot_tn(kw, v)

        cst = cst_ref[...]
        n_st = nst_ref[...]
        sc = qk * wts
        inter = _dot(q, cst)
        intra = _dot(sc, v)
        cst_ref[...] = decay * cst + upd
        nst_ref[...] = decay * n_st + jnp.sum(kw, axis=0, keepdims=True)

        num = g_inter * inter + intra
        den = (g_inter * jnp.sum(q * n_st, axis=1, keepdims=True)
               + jnp.sum(sc, axis=1, keepdims=True))
        hout = num / jnp.maximum(jnp.abs(den), e_negm)

        yn = hout * lax.rsqrt(jnp.mean(hout * hout, axis=-1, keepdims=True) + EPS) * cn
        og = og_ref[0, pl.ds(r0, c), :]
        z = z_ref[0, pl.ds(r0, c), :]
        o_ref[0, pl.ds(r0, c), :] = (yn * _sigmoid(og) * _silu(z)).astype(o_ref.dtype)
        return carry

    lax.fori_loop(0, s_len // c, chunk, 0, unroll=ML_UNROLL)


def _mlstm(u, gate_terms, conv_w, conv_b, c_norm, *, heads, dk, dv):
    bsz, s_len, _ = u.shape
    nc = s_len // ML_CHUNK
    col_spec = pl.BlockSpec((1, s_len, LANES), lambda b, h: (b, 0, 0))
    qk_blk = lambda off: pl.BlockSpec((1, s_len, dk), lambda b, h, off=off: (b, 0, off + h))
    v_blk = lambda off: pl.BlockSpec((1, s_len, dv), lambda b, h, off=off: (b, 0, off + h))
    cw = lambda off: pl.BlockSpec((CONV_W, dk), lambda b, h, off=off: (0, off + h))
    cb = lambda off: pl.BlockSpec((1, dk), lambda b, h, off=off: (0, off + h))
    nqk = 2 * heads * dk // dv
    return pl.pallas_call(
        _mlstm_kernel,
        grid=(bsz, heads),
        in_specs=[qk_blk(0), qk_blk(heads),
                  v_blk(nqk), v_blk(nqk + heads), v_blk(nqk + 2 * heads),
                  col_spec, col_spec, col_spec, col_spec,
                  pl.BlockSpec((1, nc, LANES), lambda b, h: (b, 0, 0)),
                  pl.BlockSpec((1, SUBLANES, s_len), lambda b, h: (b, 0, 0)),
                  cw(0), cb(0), cw(heads), cb(heads),
                  pl.BlockSpec((1, dv), lambda b, h: (0, h))],
        out_specs=pl.BlockSpec((1, s_len, dv), lambda b, h: (b, 0, h)),
        out_shape=jax.ShapeDtypeStruct((bsz, s_len, heads * dv), BF16),
        scratch_shapes=[pltpu.VMEM((s_len, dk), F32), pltpu.VMEM((s_len, dk), F32),
                        pltpu.VMEM((dk, dv), F32), pltpu.VMEM((1, dk), F32)],
        compiler_params=pltpu.CompilerParams(
            dimension_semantics=("parallel", "parallel"), vmem_limit_bytes=VMEM_LIMIT),
        name="mlstm",
    )(u, u, u, u, u, *gate_terms, conv_w, conv_b.reshape(1, -1), conv_w,
      conv_b.reshape(1, -1), c_norm.reshape(1, -1))


OUT_SUB_ROWS = 256


def _out_kernel(h_ref, *rest, n_y, final):
    y_refs = rest[:n_y]
    wo_refs = rest[n_y:2 * n_y]
    p_ref, pw_ref, pn_ref, gw_ref, norm_ref = rest[2 * n_y:2 * n_y + 5]
    out_refs = rest[2 * n_y + 5:]

    def rows_block(r0):
        rows = slice(r0, r0 + OUT_SUB_ROWS)
        mix = jnp.dot(y_refs[0][rows, :], wo_refs[0][...], preferred_element_type=F32)
        for y_ref, wo_ref in zip(y_refs[1:], wo_refs[1:]):
            mix = mix + jnp.dot(y_ref[rows, :], wo_ref[...], preferred_element_type=F32)
        yield
        h1 = h_ref[rows, :] + mix
        pe = _dot(p_ref[rows, :], pw_ref[...])
        gate = _dot(h1, gw_ref[...])
        yield
        pe = pe * lax.rsqrt(jnp.mean(pe * pe, axis=-1, keepdims=True) + EPS) * pn_ref[...]
        h2 = h1 + _sigmoid(gate) * pe
        hn = h2 * lax.rsqrt(jnp.mean(h2 * h2, axis=-1, keepdims=True) + EPS) * norm_ref[...]
        if final:
            out_refs[0][rows, :] = hn
        else:
            out_refs[0][rows, :] = h2
            out_refs[1][rows, :] = hn.astype(out_refs[1].dtype)

    stages = [rows_block(r0) for r0 in range(0, h_ref.shape[0], OUT_SUB_ROWS)]
    while stages:
        stages = [g for g in stages if next(g, stages) is not stages]


def _out_proj(h, ys, w_out, p, layer, ple_w, ple_norm, gate_w, norm, final, *, tm=512):
    m, d = h.shape
    const = lambda shape: pl.BlockSpec(shape, lambda i: (0, 0), pipeline_mode=pl.Buffered(1))
    in_specs = [pl.BlockSpec((tm, d), lambda i: (i, 0))]
    in_specs += [pl.BlockSpec((tm, y.shape[1]), lambda i: (i, 0)) for y in ys]
    assert all(y.shape[1] == ys[0].shape[1] for y in ys)
    in_specs += [pl.BlockSpec((y.shape[1], d), lambda i, r=r: (r, 0), pipeline_mode=pl.Buffered(1))
                 for r, y in enumerate(ys)]
    in_specs += [pl.BlockSpec((None, tm, p.shape[2]), lambda i: (layer, i, 0)),
                 const(ple_w.shape), const((1, d)), const(gate_w.shape), const((1, d))]
    args = [h, *ys, *([w_out] * len(ys)), p, ple_w, ple_norm.reshape(1, d), gate_w,
            norm.reshape(1, d)]
    row_spec = pl.BlockSpec((tm, d), lambda i: (i, 0))
    return pl.pallas_call(
        functools.partial(_out_kernel, n_y=len(ys), final=final),
        grid=(m // tm,),
        in_specs=in_specs,
        out_specs=row_spec if final else [row_spec, row_spec],
        out_shape=(jax.ShapeDtypeStruct((m, d), F32) if final else
                   [jax.ShapeDtypeStruct((m, d), F32), jax.ShapeDtypeStruct((m, d), BF16)]),
        compiler_params=pltpu.CompilerParams(
            dimension_semantics=("parallel",), vmem_limit_bytes=VMEM_LIMIT),
        name="out_proj_final" if final else "out_proj",
    )(*args)


def kernel(x, p, e_norm, e_w_in, a_lb_logits, a_norm, b_conv_w, b_conv_b, b_w_r, b_b_r, b_w_i, b_b_i, b_lambda, e_w_out, o_norm, o_w_in, c_conv_w, c_conv_b, c_b_i, c_b_f, c_norm, o_w_out, ple_w, ple_norm, ple_gate_w, final_norm):
    bsz, s_len, d = x.shape
    depth = p.shape[0]
    m = bsz * s_len
    a_width = a_norm.shape[1]
    b_width = b_lambda.shape[1]
    a_dk = LANES
    a_heads = a_width // a_dk
    c_heads = c_b_i.shape[1]
    c_dv = c_norm.shape[1] // c_heads
    c_dk = c_conv_w.shape[2] // (2 * c_heads)
    c_main = o_w_in.shape[2] - 2 * c_heads

    h = x.reshape(m, d)
    hn = None
    layer_norm = lambda i: e_norm[i // 2] if i % 2 == 0 else o_norm[i // 2]
    for i in range(depth):
        j = i // 2
        last = i == depth - 1
        xin, g = (h, layer_norm(i)) if hn is None else (hn, None)
        if i % 2 == 0:
            u = _in_proj(xin, g, e_w_in[j].astype(BF16)).reshape(bsz, s_len, -1)
            ya = _hgrn2(u, a_lb_logits, a_norm[j], layer=j, heads=a_heads, dk=a_dk)
            x_off = (2 * a_heads * a_dk + 2 * a_width) // LANES
            yb = _rglru(u, b_conv_w[j], b_conv_b[j], b_w_r[j].astype(BF16), b_b_r[j],
                        b_w_i[j].astype(BF16), b_b_i[j], b_lambda[j],
                        x_off=x_off, z_off=x_off + b_width // LANES)
            w_out = e_w_out[j].astype(BF16)
            ys = [ya.reshape(m, a_width), yb.reshape(m, b_width)]
        else:
            w_in_t = o_w_in[j].T
            pad = (0, LANES - c_heads)
            wg_t = jnp.concatenate([jnp.pad(w_in_t[c_main:c_main + c_heads], (pad, (0, 0))),
                                    jnp.pad(w_in_t[c_main + c_heads:], (pad, (0, 0)))]).astype(BF16)
            u, gates = _in_proj(xin, g, w_in_t.astype(BF16), wg_t, n=c_main, transposed=True)
            gate_bias = jnp.concatenate([jnp.pad(c_b_i[j], pad),
                                         jnp.pad(c_b_f[j], pad)]).reshape(1, 2 * LANES)
            gate_terms = _mlstm_gates(gates.reshape(bsz, s_len, 2 * LANES), gate_bias)
            yc = _mlstm(u.reshape(bsz, s_len, -1), gate_terms, c_conv_w[j], c_conv_b[j],
                        c_norm[j], heads=c_heads, dk=c_dk, dv=c_dv)
            ys = [yc.reshape(m, -1)]
            w_out = o_w_out[j].astype(BF16)
        res = _out_proj(h, ys, w_out, p.reshape(depth, m, -1), i, ple_w[i].astype(BF16),
                        ple_norm[i], ple_gate_w[i].astype(BF16),
                        final_norm if last else layer_norm(i + 1), last)
        h, hn = (res, None) if last else res
    return h.reshape(bsz, s_len, d)
```

```python
import functools

import jax
import jax.numpy as jnp
from jax import lax
from jax.experimental import pallas as pl
from jax.experimental.pallas import tpu as pltpu

EPS = 1e-6
LRU_C = 8.0
CONV_W = 4
LANES = 128
SUBLANES = 8
VMEM_LIMIT = 56 * 1024 * 1024
BF16 = jnp.bfloat16
F32 = jnp.float32


def _sigmoid(x):
    return jax.nn.sigmoid(x)


def _silu(x):
    return x * jax.nn.sigmoid(x)


def _softplus(x):
    return jnp.maximum(x, 0.0) + jnp.log1p(jnp.exp(-jnp.abs(x)))


def _log_sigmoid(x):
    return -_softplus(-x)


def _shift_rows(x, s, row):
    return jnp.where(row >= s, pltpu.roll(x, s, 0), 0.0)


def _cumsum_rows(x, row):
    s = 1
    while s < x.shape[0]:
        x = x + _shift_rows(x, s, row)
        s *= 2
    return x


def _cummax_rows(x, row):
    s = 1
    while s < x.shape[0]:
        x = jnp.maximum(x, jnp.where(row >= s, pltpu.roll(x, s, 0), -jnp.inf))
        s *= 2
    return x


def _causal_conv(x_ref, cw_ref, cb_ref):
    s_len = x_ref.shape[1]
    taps = [cw_ref[k:k + 1, :] for k in range(CONV_W)]
    acc = cb_ref[...] + x_ref[0, SUBLANES:, :] * taps[CONV_W - 1]
    for j in range(1, CONV_W):
        acc = acc + x_ref[0, SUBLANES - j:s_len - j, :] * taps[CONV_W - 1 - j]
    x0 = x_ref[0, 0:SUBLANES, :]
    row = lax.broadcasted_iota(jnp.int32, x0.shape, 0)
    acc0 = cb_ref[...] + x0 * taps[CONV_W - 1]
    for j in range(1, CONV_W):
        acc0 = acc0 + _shift_rows(x0, j, row) * taps[CONV_W - 1 - j]
    return jnp.concatenate([acc0, acc], axis=0)


def _dot(a, b):
    return jnp.dot(a.astype(BF16), b.astype(BF16), preferred_element_type=F32)


def _dot_nt(a, b):
    return lax.dot_general(a.astype(BF16), b.astype(BF16), (((1,), (1,)), ((), ())),
                           preferred_element_type=F32)


def _dot_tn(a, b):
    return lax.dot_general(a.astype(BF16), b.astype(BF16), (((0,), (0,)), ((), ())),
                           preferred_element_type=F32)


NORM_ROWS = 256


def _in_proj_kernel(*refs, with_norm, with_gates, transposed):
    refs = list(refs)
    x_ref = refs.pop(0)
    g_ref = refs.pop(0) if with_norm else None
    w_ref = refs.pop(0)
    wg_ref = refs.pop(0) if with_gates else None
    o_ref = refs.pop(0)
    og_ref = refs.pop(0) if with_gates else None
    hn_ref = refs.pop(0) if with_norm else x_ref
    mm = _dot_nt if transposed else functools.partial(jnp.dot, preferred_element_type=F32)

    first = pl.program_id(1) == 0

    @pl.when(first)
    def _():
        if with_norm:
            for r0 in range(0, x_ref.shape[0], NORM_ROWS):
                rows = slice(r0, r0 + NORM_ROWS)
                x = x_ref[rows, :]
                ms = jnp.mean(x * x, axis=-1, keepdims=True)
                hn = (x * lax.rsqrt(ms + EPS) * g_ref[...]).astype(BF16)
                hn_ref[rows, :] = hn
                o_ref[rows, :] = mm(hn, w_ref[...])
        if with_gates:
            og_ref[...] = mm(hn_ref[...], wg_ref[...])

    @pl.when(jnp.logical_not(first) if with_norm else True)
    def _():
        o_ref[...] = mm(hn_ref[...], w_ref[...])


IN_PROJ_ROWS = 1024
IN_PROJ_VMEM = 46 * 1024 * 1024


def _in_proj_cols(n, tm, d, gw, x_bytes, with_norm):
    for tn in range(n, 0, -512):
        need = (2 * tm * d * x_bytes + (tm * d * 2 if with_norm else 0) + 2 * d * tn * 2
                + 2 * tm * tn * 4 + 2 * (d * gw * 2 + tm * gw * 4))
        if n % tn == 0 and need <= IN_PROJ_VMEM:
            return tn
    raise ValueError("no column tile fits")


def _in_proj(x, g, w, wg=None, *, n=None, transposed=False, tm=IN_PROJ_ROWS):
    m, d = x.shape
    n = w.shape[0 if transposed else 1] if n is None else n
    with_gates = wg is not None
    with_norm = g is not None
    gw = wg.shape[0 if transposed else 1] if with_gates else 0
    w_spec = (lambda cols, idx: pl.BlockSpec((cols, d), lambda i, j: (idx(j), 0)) if transposed
              else pl.BlockSpec((d, cols), lambda i, j: (0, idx(j))))
    tn = _in_proj_cols(n, tm, d, gw, x.dtype.itemsize, with_norm)
    in_specs = [pl.BlockSpec((tm, d), lambda i, j: (i, 0))]
    args = [x]
    if with_norm:
        in_specs.append(pl.BlockSpec((1, d), lambda i, j: (0, 0)))
        args.append(g.reshape(1, d))
    in_specs.append(w_spec(tn, lambda j: j))
    args.append(w)
    out_shape = [jax.ShapeDtypeStruct((m, n), F32)]
    out_specs = [pl.BlockSpec((tm, tn), lambda i, j: (i, j))]
    if with_gates:
        in_specs.append(w_spec(gw, lambda j: 0))
        out_shape.append(jax.ShapeDtypeStruct((m, gw), F32))
        out_specs.append(pl.BlockSpec((tm, gw), lambda i, j: (i, 0)))
        args.append(wg)
    res = pl.pallas_call(
        functools.partial(_in_proj_kernel, with_norm=with_norm, with_gates=with_gates,
                          transposed=transposed),
        grid=(m // tm, n // tn),
        in_specs=in_specs,
        out_specs=out_specs,
        out_shape=out_shape,
        scratch_shapes=[pltpu.VMEM((tm, d), BF16)] if with_norm else [],
        compiler_params=pltpu.CompilerParams(
            dimension_semantics=("parallel", "arbitrary"), vmem_limit_bytes=VMEM_LIMIT),
        name="in_proj_gates" if with_gates else "in_proj",
    )(*args)
    return res if with_gates else res[0]


HG_CHUNK = 128
HG_GROUPS = HG_CHUNK // SUBLANES
HG_INFLIGHT = 4


def _hgrn2_kernel(q_ref, f_ref, i_ref, z_ref, lbl_ref, an_ref, o_ref,
                  st_ref, qs_ref, ks_ref, bs_ref, od_ref, tri_ref, *, layer):
    s_len = q_ref.shape[1]
    dk = q_ref.shape[2]
    c = HG_CHUNK
    ng = HG_GROUPS
    assert c == dk == LANES and i_ref.shape[2] == dk

    lg = lbl_ref[...]
    e = jnp.exp(lg - jnp.max(lg, axis=0, keepdims=True))
    sm = e / jnp.sum(e, axis=0, keepdims=True)
    lb = jnp.sum(sm[:layer + 1], axis=0, keepdims=True)
    an = an_ref[...]

    rowc = lax.broadcasted_iota(jnp.int32, (c, c), 0)
    colc = lax.broadcasted_iota(jnp.int32, (c, c), 1)
    tri_ref[...] = (colc <= rowc).astype(BF16)
    same_group = (colc % ng) == (rowc % ng)
    lane = lax.broadcasted_iota(jnp.int32, (SUBLANES, c), 1)
    diag_off = lane - lax.broadcasted_iota(jnp.int32, (SUBLANES, c), 0)
    zeros8 = jnp.zeros((SUBLANES, dk), F32)
    zeros16 = jnp.zeros((ng, dk), F32)

    st_ref[...] = jnp.zeros_like(st_ref)

    def rows8(x, r):
        return x[r * SUBLANES:(r + 1) * SUBLANES, :]

    def chunk(ci, slot):
        qs, ks, bs, od = qs_ref.at[slot], ks_ref.at[slot], bs_ref.at[slot], od_ref.at[slot]
        r0 = pl.multiple_of(ci * c, c)
        qp = q_ref[0, pl.ds(r0, c), :]
        fp = f_ref[0, pl.ds(r0, c), :]
        v = i_ref[0, pl.ds(r0, c), :]
        q = _silu(qp)
        gate = (1.0 - lb) * _sigmoid(fp)
        f = lb + gate
        k = (1.0 - lb) - gate

        lf = jnp.log2(f)
        hi = lf.astype(BF16)
        r1 = lf - hi.astype(F32)
        mid = r1.astype(BF16)
        lo = (r1 - mid.astype(F32)).astype(BF16)
        bb = jnp.dot(tri_ref[...], jnp.concatenate([hi, mid, lo], axis=1),
                     preferred_element_type=F32)
        yield
        b2 = bb[:, :dk] + bb[:, dk:2 * dk] + bb[:, 2 * dk:]
        qs[...] = q
        ks[...] = k
        bs[...] = b2
        b_last = bs[c - 1:c, :]
        upd = _dot_tn(v, k * jnp.exp2(b_last - b2))

        ps, rights = [], []
        for j in range(ng.bit_length() - 1):
            lhs, rhs, rights_j = [], [], []
            for r in range(ng):
                mid_grp = ((r >> (j + 1)) << (j + 1)) | ((1 << j) - 1)
                beta = bs[mid_grp * SUBLANES + SUBLANES - 1:(mid_grp + 1) * SUBLANES, :]
                if (r >> j) & 1:
                    lhs.append(rows8(q, r) * jnp.exp2(rows8(b2, r) - beta))
                    rhs.append(zeros8)
                    rights_j.append(r)
                else:
                    rhs.append(rows8(k, r) * jnp.exp2(beta - rows8(b2, r)))
            ps.append(_dot_nt(jnp.concatenate(lhs, axis=0), jnp.concatenate(rhs, axis=0)))
            rights.append(rights_j)
        yield

        st = st_ref[...]
        o = _dot_nt(q * jnp.exp2(b2), st)
        st_ref[...] = st * jnp.exp2(b_last) + upd

        def permuted(ref):
            return [ref[pl.ds(pos, ng, stride=SUBLANES), :] for pos in range(SUBLANES)]

        q_p, k_p, b_p = permuted(qs), permuted(ks), permuted(bs)
        lhs_slots, rhs_slots = [], []
        for j in range(SUBLANES.bit_length() - 1):
            for blk in range(SUBLANES >> (j + 1)):
                mid_pos = (blk << (j + 1)) | ((1 << j) - 1)
                lhs, rhs = [], []
                for pos in range(SUBLANES):
                    if pos >> (j + 1) != blk:
                        lhs.append(zeros16)
                        rhs.append(zeros16)
                    elif (pos >> j) & 1:
                        lhs.append(q_p[pos] * jnp.exp2(b_p[pos] - b_p[mid_pos]))
                        rhs.append(zeros16)
                    else:
                        lhs.append(zeros16)
                        rhs.append(k_p[pos] * jnp.exp2(b_p[mid_pos] - b_p[pos]))
                lhs_slots.append(jnp.concatenate(lhs, axis=0).astype(BF16))
                rhs_slots.append(jnp.concatenate(rhs, axis=0).astype(BF16))
        pd = _dot_nt(jnp.concatenate(lhs_slots, axis=1), jnp.concatenate(rhs_slots, axis=1))
        yield

        dg = jnp.sum(q * k, axis=1, keepdims=True)
        a_rows = [jnp.where(diag_off == r * SUBLANES, rows8(dg, r), 0.0) for r in range(ng)]
        for j, (p, rights_j) in enumerate(zip(ps, rights)):
            for i, r in enumerate(rights_j):
                right_start = (((r >> (j + 1)) << (j + 1)) | (1 << j)) * SUBLANES
                a_rows[r] = jnp.where(lane < right_start, rows8(p, i), a_rows[r])
        o = o + _dot(jnp.concatenate(a_rows, axis=0), v)
        v_p = [i_ref[0, pl.ds(r0 + pos, ng, stride=SUBLANES), :] for pos in range(SUBLANES)]
        od[...] = _dot(jnp.where(same_group, pd, 0.0), jnp.concatenate(v_p, axis=0))
        yield

        o = o + jnp.concatenate(
            [od[pl.ds(g, SUBLANES, stride=ng), :] for g in range(ng)], axis=0)
        yn = o * lax.rsqrt(jnp.mean(o * o, axis=-1, keepdims=True) + EPS) * an
        z = z_ref[0, pl.ds(r0, c), :]
        o_ref[0, pl.ds(r0, c), :] = (yn * _silu(z)).astype(o_ref.dtype)

    def chunks(i, carry):
        stages = [chunk(i * HG_INFLIGHT + slot, slot) for slot in range(HG_INFLIGHT)]
        while stages:
            stages = [g for g in stages if next(g, stages) is not stages]
        return carry

    lax.fori_loop(0, s_len // (c * HG_INFLIGHT), chunks, 0, unroll=2)


def _hgrn2(u, lb_logits, a_norm, *, layer, heads, dk):
    bsz, s_len, _ = u.shape
    nl = lb_logits.shape[0]
    blk = lambda off: pl.BlockSpec((1, s_len, dk), lambda b, h, off=off: (b, 0, off + h))
    return pl.pallas_call(
        functools.partial(_hgrn2_kernel, layer=layer),
        grid=(bsz, heads),
        in_specs=[blk(0), blk(heads), blk(2 * heads), blk(3 * heads),
                  pl.BlockSpec((nl, dk), lambda b, h: (0, h)),
                  pl.BlockSpec((1, dk), lambda b, h: (0, h))],
        out_specs=pl.BlockSpec((1, s_len, dk), lambda b, h: (b, 0, h)),
        out_shape=jax.ShapeDtypeStruct((bsz, s_len, heads * dk), BF16),
        scratch_shapes=[pltpu.VMEM((dk, dk), F32)]
        + [pltpu.VMEM((HG_INFLIGHT, HG_CHUNK, dk), F32)] * 4
        + [pltpu.VMEM((HG_CHUNK, HG_CHUNK), BF16)],
        compiler_params=pltpu.CompilerParams(
            dimension_semantics=("parallel", "parallel"), vmem_limit_bytes=VMEM_LIMIT),
        name="hgrn2",
    )(u, u, u, u, lb_logits, a_norm.reshape(1, -1))


def _rglru_kernel(x_ref, z_ref, cw_ref, cb_ref, wr_ref, br_ref, wi_ref, bi_ref, lam_ref,
                  o_ref, a_ref, u_ref, h_ref, ca_ref, cu_ref):
    s_len = x_ref.shape[1]
    w = x_ref.shape[2]
    ng = s_len // SUBLANES
    xc = _causal_conv(x_ref, cw_ref, cb_ref)
    r = _sigmoid(_dot(xc, wr_ref[0]) + br_ref[...])
    ig = _sigmoid(_dot(xc, wi_ref[0]) + bi_ref[...])
    log_a = -LRU_C * r * _softplus(-lam_ref[...])
    a = jnp.exp(log_a)
    a_ref[...] = a
    t = -jnp.tanh(log_a) * (a * a + 1.0)
    u_ref[...] = jnp.where(t > 0.0, t * lax.rsqrt(t), 0.0) * (ig * xc)

    ca = cu = None
    for pos in range(SUBLANES):
        ap = a_ref[pl.ds(pos, ng, stride=SUBLANES), :]
        up = u_ref[pl.ds(pos, ng, stride=SUBLANES), :]
        ca, cu = (ap, up) if pos == 0 else (ap * ca, ap * cu + up)
        ca_ref[pos] = ca
        cu_ref[pos] = cu
    grp = lax.broadcasted_iota(jnp.int32, (ng, w), 0)
    ta, tu = ca, cu
    s = 1
    while s < ng:
        m = grp >= s
        tu = tu + ta * jnp.where(m, pltpu.roll(tu, s, 0), 0.0)
        ta = ta * jnp.where(m, pltpu.roll(ta, s, 0), 1.0)
        s *= 2
    h_in = jnp.where(grp >= 1, pltpu.roll(tu, 1, 0), 0.0)
    for pos in range(SUBLANES):
        h_ref[pl.ds(pos, ng, stride=SUBLANES), :] = ca_ref[pos] * h_in + cu_ref[pos]
    o_ref[0] = (h_ref[...] * _silu(z_ref[0])).astype(o_ref.dtype)


def _rglru(u, conv_w, conv_b, w_r, b_r, w_i, b_i, lam, *, x_off, z_off):
    bsz, s_len, _ = u.shape
    nblk, blk, _ = w_r.shape
    vec = lambda rows: pl.BlockSpec((rows, blk), lambda b, n: (0, n))
    mat = pl.BlockSpec((1, blk, blk), lambda b, n: (n, 0, 0))
    return pl.pallas_call(
        _rglru_kernel,
        grid=(bsz, nblk),
        in_specs=[pl.BlockSpec((1, s_len, blk), lambda b, n: (b, 0, x_off + n)),
                  pl.BlockSpec((1, s_len, blk), lambda b, n: (b, 0, z_off + n)),
                  vec(CONV_W), vec(1), mat, vec(1), mat, vec(1), vec(1)],
        out_specs=pl.BlockSpec((1, s_len, blk), lambda b, n: (b, 0, n)),
        out_shape=jax.ShapeDtypeStruct((bsz, s_len, nblk * blk), BF16),
        scratch_shapes=[pltpu.VMEM((s_len, blk), F32)] * 3
        + [pltpu.VMEM((SUBLANES, s_len // SUBLANES, blk), F32)] * 2,
        compiler_params=pltpu.CompilerParams(
            dimension_semantics=("parallel", "parallel"), vmem_limit_bytes=VMEM_LIMIT),
        name="rglru",
    )(u, u, conv_w, conv_b.reshape(1, -1), w_r, b_r.reshape(1, -1), w_i, b_i.reshape(1, -1),
      lam.reshape(1, -1))


ML_CHUNK = 128
ML_UNROLL = 8


def _mlstm_gates_kernel(g_ref, gb_ref, mu_ref, gi_ref, en_ref, ks_ref, dec_ref, rt_ref):
    s_len = g_ref.shape[1]
    c = ML_CHUNK
    t = g_ref[0] + gb_ref[...]
    li = t[:, :LANES]
    lf = _log_sigmoid(t[:, LANES:])
    row = lax.broadcasted_iota(jnp.int32, (s_len, LANES), 0) % c

    def scan(x, op, fill):
        s = 1
        while s < c:
            x = op(x, jnp.where(row >= s, pltpu.roll(x, s, 0), fill))
            s *= 2
        return x

    b = scan(lf, jnp.add, 0.0)
    r = li - b
    rho = scan(r, jnp.maximum, -jnp.inf)
    m = jnp.zeros((1, LANES), F32)
    for ci in range(s_len // c):
        sl = slice(ci * c, (ci + 1) * c)
        b_c, r_c, rho_c = b[sl], r[sl], rho[sl]
        b_last = b_c[c - 1:c, :]
        m_new = b_last + jnp.maximum(m, rho_c[c - 1:c, :])
        mu = jnp.maximum(rho_c, m)
        mu_ref[0, sl, :] = mu
        gi_ref[0, sl, :] = jnp.exp(m - mu)
        en_ref[0, sl, :] = jnp.exp(-(b_c + mu))
        ks_ref[0, sl, :] = jnp.exp(b_last + r_c - m_new)
        dec_ref[0, ci:ci + 1, :] = jnp.exp(b_last + m - m_new)
        rt_ref[0, :, sl] = r_c.T
        m = m_new


def _mlstm_gates(gates, gate_bias):
    bsz, s_len, gw = gates.shape
    nc = s_len // ML_CHUNK
    col = jax.ShapeDtypeStruct((bsz, s_len, LANES), F32)
    col_spec = pl.BlockSpec((1, s_len, LANES), lambda b: (b, 0, 0))
    return pl.pallas_call(
        _mlstm_gates_kernel,
        grid=(bsz,),
        in_specs=[pl.BlockSpec((1, s_len, gw), lambda b: (b, 0, 0)),
                  pl.BlockSpec((1, gw), lambda b: (0, 0))],
        out_specs=[col_spec] * 4 + [pl.BlockSpec((1, nc, LANES), lambda b: (b, 0, 0)),
                                    pl.BlockSpec((1, LANES, s_len), lambda b: (b, 0, 0))],
        out_shape=[col] * 4 + [jax.ShapeDtypeStruct((bsz, nc, LANES), F32),
                               jax.ShapeDtypeStruct((bsz, LANES, s_len), F32)],
        compiler_params=pltpu.CompilerParams(
            dimension_semantics=("parallel",), vmem_limit_bytes=VMEM_LIMIT),
        name="mlstm_gates",
    )(gates, gate_bias)


def _mlstm_kernel(q_ref, k_ref, v_ref, og_ref, z_ref, mu_ref, gi_ref, en_ref, ks_ref, dec_ref,
                  rt_ref, cwq_ref, cbq_ref, cwk_ref, cbk_ref, cn_ref, o_ref, qs_ref, ks_s_ref,
                  cst_ref, nst_ref):
    s_len = q_ref.shape[1]
    dk = q_ref.shape[2]
    c = ML_CHUNK
    assert c == dk == LANES
    h = pl.program_id(1)

    qs_ref[...] = _silu(_causal_conv(q_ref, cwq_ref, cbq_ref)) * (dk ** -0.5)
    ks_s_ref[...] = _silu(_causal_conv(k_ref, cwk_ref, cbk_ref))
    cst_ref[...] = jnp.zeros_like(cst_ref)
    nst_ref[...] = jnp.zeros_like(nst_ref)

    row = lax.broadcasted_iota(jnp.int32, (c, c), 0)
    col = lax.broadcasted_iota(jnp.int32, (c, c), 1)
    causal = col <= row
    head = col == h
    cn = cn_ref[...]

    def pick(x):
        return jnp.sum(jnp.where(head, x, 0.0), axis=1, keepdims=True)

    dec_all = dec_ref[0]
    chunk_id = lax.broadcasted_iota(jnp.int32, dec_all.shape, 0)
    head_lane = lax.broadcasted_iota(jnp.int32, (1, LANES), 1) == h
    head_row = lax.broadcasted_iota(jnp.int32, (SUBLANES, c), 0) == h

    def chunk(ci, carry):
        r0 = pl.multiple_of(ci * c, c)
        q = qs_ref[pl.ds(r0, c), :]
        k = ks_s_ref[pl.ds(r0, c), :]
        v = v_ref[0, pl.ds(r0, c), :]
        mu = pick(mu_ref[0, pl.ds(r0, c), :])
        g_inter = pick(gi_ref[0, pl.ds(r0, c), :])
        e_negm = pick(en_ref[0, pl.ds(r0, c), :])
        kscale = pick(ks_ref[0, pl.ds(r0, c), :])
        decay = jnp.sum(jnp.where(chunk_id == ci, dec_all, 0.0), axis=0, keepdims=True)
        decay = jnp.sum(jnp.where(head_lane, decay, 0.0), axis=1, keepdims=True)
        r_row = jnp.sum(jnp.where(head_row, rt_ref[0, :, pl.ds(r0, c)], 0.0),
                        axis=0, keepdims=True)
        wts = jnp.where(causal, jnp.exp(jnp.minimum(r_row - mu, 0.0)), 0.0)
        qk = _dot_nt(q, k)
        kw = k * kscale
        upd = _dot_tn(kw, v)

        cst = cst_ref[...]
        n_st = nst_ref[...]
        sc = qk * wts
        inter = _dot(q, cst)
        intra = _dot(sc, v)
        cst_ref[...] = decay * cst + upd
        nst_ref[...] = decay * n_st + jnp.sum(kw, axis=0, keepdims=True)

        num = g_inter * inter + intra
        den = (g_inter * jnp.sum(q * n_st, axis=1, keepdims=True)
               + jnp.sum(sc, axis=1, keepdims=True))
        inv = 1.0 / jnp.maximum(jnp.abs(den), e_negm)
        msq = jnp.mean(num * num, axis=-1, keepdims=True)
        yn = num * (inv * lax.rsqrt(inv * inv * msq + EPS)) * cn
        og = og_ref[0, pl.ds(r0, c), :]
        z = z_ref[0, pl.ds(r0, c), :]
        o_ref[0, pl.ds(r0, c), :] = (yn * _sigmoid(og) * _silu(z)).astype(o_ref.dtype)
        return carry

    lax.fori_loop(0, s_len // c, chunk, 0, unroll=ML_UNROLL)


def _mlstm(u, gate_terms, conv_w, conv_b, c_norm, *, heads, dk, dv):
    bsz, s_len, _ = u.shape
    nc = s_len // ML_CHUNK
    col_spec = pl.BlockSpec((1, s_len, LANES), lambda b, h: (b, 0, 0))
    qk_blk = lambda off: pl.BlockSpec((1, s_len, dk), lambda b, h, off=off: (b, 0, off + h))
    v_blk = lambda off: pl.BlockSpec((1, s_len, dv), lambda b, h, off=off: (b, 0, off + h))
    cw = lambda off: pl.BlockSpec((CONV_W, dk), lambda b, h, off=off: (0, off + h))
    cb = lambda off: pl.BlockSpec((1, dk), lambda b, h, off=off: (0, off + h))
    nqk = 2 * heads * dk // dv
    return pl.pallas_call(
        _mlstm_kernel,
        grid=(bsz, heads),
        in_specs=[qk_blk(0), qk_blk(heads),
                  v_blk(nqk), v_blk(nqk + heads), v_blk(nqk + 2 * heads),
                  col_spec, col_spec, col_spec, col_spec,
                  pl.BlockSpec((1, nc, LANES), lambda b, h: (b, 0, 0)),
                  pl.BlockSpec((1, SUBLANES, s_len), lambda b, h: (b, 0, 0)),
                  cw(0), cb(0), cw(heads), cb(heads),
                  pl.BlockSpec((1, dv), lambda b, h: (0, h))],
        out_specs=pl.BlockSpec((1, s_len, dv), lambda b, h: (b, 0, h)),
        out_shape=jax.ShapeDtypeStruct((bsz, s_len, heads * dv), BF16),
        scratch_shapes=[pltpu.VMEM((s_len, dk), F32), pltpu.VMEM((s_len, dk), F32),
                        pltpu.VMEM((dk, dv), F32), pltpu.VMEM((1, dk), F32)],
        compiler_params=pltpu.CompilerParams(
            dimension_semantics=("parallel", "parallel"), vmem_limit_bytes=VMEM_LIMIT),
        name="mlstm",
    )(u, u, u, u, u, *gate_terms, conv_w, conv_b.reshape(1, -1), conv_w,
      conv_b.reshape(1, -1), c_norm.reshape(1, -1))


OUT_SUB_ROWS = 256


def _out_kernel(h_ref, *rest, n_y, final):
    y_refs = rest[:n_y]
    wo_refs = rest[n_y:2 * n_y]
    p_ref, pw_ref, pn_ref, gw_ref, norm_ref = rest[2 * n_y:2 * n_y + 5]
    out_refs = rest[2 * n_y + 5:]

    def rows_block(r0):
        rows = slice(r0, r0 + OUT_SUB_ROWS)
        mix = jnp.dot(y_refs[0][rows, :], wo_refs[0][...], preferred_element_type=F32)
        for y_ref, wo_ref in zip(y_refs[1:], wo_refs[1:]):
            mix = mix + jnp.dot(y_ref[rows, :], wo_ref[...], preferred_element_type=F32)
        yield
        h1 = h_ref[rows, :] + mix
        pe = _dot(p_ref[rows, :], pw_ref[...])
        gate = _dot(h1, gw_ref[...])
        yield
        pe = pe * lax.rsqrt(jnp.mean(pe * pe, axis=-1, keepdims=True) + EPS) * pn_ref[...]
        h2 = h1 + _sigmoid(gate) * pe
        hn = h2 * lax.rsqrt(jnp.mean(h2 * h2, axis=-1, keepdims=True) + EPS) * norm_ref[...]
        if final:
            out_refs[0][rows, :] = hn
        else:
            out_refs[0][rows, :] = h2
            out_refs[1][rows, :] = hn.astype(out_refs[1].dtype)

    stages = [rows_block(r0) for r0 in range(0, h_ref.shape[0], OUT_SUB_ROWS)]
    while stages:
        stages = [g for g in stages if next(g, stages) is not stages]


def _out_proj(h, ys, w_out, p, layer, ple_w, ple_norm, gate_w, norm, final, *, tm=512):
    m, d = h.shape
    const = lambda shape: pl.BlockSpec(shape, lambda i: (0, 0), pipeline_mode=pl.Buffered(1))
    in_specs = [pl.BlockSpec((tm, d), lambda i: (i, 0))]
    in_specs += [pl.BlockSpec((tm, y.shape[1]), lambda i: (i, 0)) for y in ys]
    assert all(y.shape[1] == ys[0].shape[1] for y in ys)
    in_specs += [pl.BlockSpec((y.shape[1], d), lambda i, r=r: (r, 0), pipeline_mode=pl.Buffered(1))
                 for r, y in enumerate(ys)]
    in_specs += [pl.BlockSpec((None, tm, p.shape[2]), lambda i: (layer, i, 0)),
                 const(ple_w.shape), const((1, d)), const(gate_w.shape), const((1, d))]
    args = [h, *ys, *([w_out] * len(ys)), p, ple_w, ple_norm.reshape(1, d), gate_w,
            norm.reshape(1, d)]
    row_spec = pl.BlockSpec((tm, d), lambda i: (i, 0))
    return pl.pallas_call(
        functools.partial(_out_kernel, n_y=len(ys), final=final),
        grid=(m // tm,),
        in_specs=in_specs,
        out_specs=row_spec if final else [row_spec, row_spec],
        out_shape=(jax.ShapeDtypeStruct((m, d), F32) if final else
                   [jax.ShapeDtypeStruct((m, d), F32), jax.ShapeDtypeStruct((m, d), BF16)]),
        compiler_params=pltpu.CompilerParams(
            dimension_semantics=("parallel",), vmem_limit_bytes=VMEM_LIMIT),
        name="out_proj_final" if final else "out_proj",
    )(*args)


def kernel(x, p, e_norm, e_w_in, a_lb_logits, a_norm, b_conv_w, b_conv_b, b_w_r, b_b_r, b_w_i, b_b_i, b_lambda, e_w_out, o_norm, o_w_in, c_conv_w, c_conv_b, c_b_i, c_b_f, c_norm, o_w_out, ple_w, ple_norm, ple_gate_w, final_norm):
    bsz, s_len, d = x.shape
    depth = p.shape[0]
    m = bsz * s_len
    a_width = a_norm.shape[1]
    b_width = b_lambda.shape[1]
    a_dk = LANES
    a_heads = a_width // a_dk
    c_heads = c_b_i.shape[1]
    c_dv = c_norm.shape[1] // c_heads
    c_dk = c_conv_w.shape[2] // (2 * c_heads)
    c_main = o_w_in.shape[2] - 2 * c_heads

    h = x.reshape(m, d)
    hn = None
    layer_norm = lambda i: e_norm[i // 2] if i % 2 == 0 else o_norm[i // 2]
    for i in range(depth):
        j = i // 2
        last = i == depth - 1
        xin, g = (h, layer_norm(i)) if hn is None else (hn, None)
        if i % 2 == 0:
            u = _in_proj(xin, g, e_w_in[j].astype(BF16)).reshape(bsz, s_len, -1)
            ya = _hgrn2(u, a_lb_logits, a_norm[j], layer=j, heads=a_heads, dk=a_dk)
            x_off = (2 * a_heads * a_dk + 2 * a_width) // LANES
            yb = _rglru(u, b_conv_w[j], b_conv_b[j], b_w_r[j].astype(BF16), b_b_r[j],
                        b_w_i[j].astype(BF16), b_b_i[j], b_lambda[j],
                        x_off=x_off, z_off=x_off + b_width // LANES)
            w_out = e_w_out[j].astype(BF16)
            ys = [ya.reshape(m, a_width), yb.reshape(m, b_width)]
        else:
            w_in_t = o_w_in[j].T
            pad = (0, LANES - c_heads)
            wg_t = jnp.concatenate([jnp.pad(w_in_t[c_main:c_main + c_heads], (pad, (0, 0))),
                                    jnp.pad(w_in_t[c_main + c_heads:], (pad, (0, 0)))]).astype(BF16)
            u, gates = _in_proj(xin, g, w_in_t.astype(BF16), wg_t, n=c_main, transposed=True)
            gate_bias = jnp.concatenate([jnp.pad(c_b_i[j], pad),
                                         jnp.pad(c_b_f[j], pad)]).reshape(1, 2 * LANES)
            gate_terms = _mlstm_gates(gates.reshape(bsz, s_len, 2 * LANES), gate_bias)
            yc = _mlstm(u.reshape(bsz, s_len, -1), gate_terms, c_conv_w[j], c_conv_b[j],
                        c_norm[j], heads=c_heads, dk=c_dk, dv=c_dv)
            ys = [yc.reshape(m, -1)]
            w_out = o_w_out[j].astype(BF16)
        res = _out_proj(h, ys, w_out, p.reshape(depth, m, -1), i, ple_w[i].astype(BF16),
                        ple_norm[i], ple_gate_w[i].astype(BF16),
                        final_norm if last else layer_norm(i + 1), last)
        h, hn = (res, None) if last else res
    return h.reshape(bsz, s_len, d)
```

```python
import functools

import jax
import jax.numpy as jnp
from jax import lax
from jax.experimental import pallas as pl
from jax.experimental.pallas import tpu as pltpu

EPS = 1e-6
LRU_C = 8.0
CONV_W = 4
LANES = 128
SUBLANES = 8
VMEM_LIMIT = 56 * 1024 * 1024
BF16 = jnp.bfloat16
F32 = jnp.float32


def _sigmoid(x):
    return jax.nn.sigmoid(x)


def _silu(x):
    return x * jax.nn.sigmoid(x)


def _softplus(x):
    return jnp.maximum(x, 0.0) + jnp.log1p(jnp.exp(-jnp.abs(x)))


def _log_sigmoid(x):
    return -_softplus(-x)


def _shift_rows(x, s, row):
    return jnp.where(row >= s, pltpu.roll(x, s, 0), 0.0)


def _cumsum_rows(x, row):
    s = 1
    while s < x.shape[0]:
        x = x + _shift_rows(x, s, row)
        s *= 2
    return x


def _cummax_rows(x, row):
    s = 1
    while s < x.shape[0]:
        x = jnp.maximum(x, jnp.where(row >= s, pltpu.roll(x, s, 0), -jnp.inf))
        s *= 2
    return x


def _causal_conv(x_ref, cw_ref, cb_ref):
    s_len = x_ref.shape[1]
    taps = [cw_ref[k:k + 1, :] for k in range(CONV_W)]
    acc = cb_ref[...] + x_ref[0, SUBLANES:, :] * taps[CONV_W - 1]
    for j in range(1, CONV_W):
        acc = acc + x_ref[0, SUBLANES - j:s_len - j, :] * taps[CONV_W - 1 - j]
    x0 = x_ref[0, 0:SUBLANES, :]
    row = lax.broadcasted_iota(jnp.int32, x0.shape, 0)
    acc0 = cb_ref[...] + x0 * taps[CONV_W - 1]
    for j in range(1, CONV_W):
        acc0 = acc0 + _shift_rows(x0, j, row) * taps[CONV_W - 1 - j]
    return jnp.concatenate([acc0, acc], axis=0)


def _dot(a, b):
    return jnp.dot(a.astype(BF16), b.astype(BF16), preferred_element_type=F32)


def _dot_nt(a, b):
    return lax.dot_general(a.astype(BF16), b.astype(BF16), (((1,), (1,)), ((), ())),
                           preferred_element_type=F32)


def _dot_tn(a, b):
    return lax.dot_general(a.astype(BF16), b.astype(BF16), (((0,), (0,)), ((), ())),
                           preferred_element_type=F32)


NORM_ROWS = 256


def _in_proj_kernel(*refs, with_norm, with_gates, transposed):
    refs = list(refs)
    x_ref = refs.pop(0)
    g_ref = refs.pop(0) if with_norm else None
    w_ref = refs.pop(0)
    wg_ref = refs.pop(0) if with_gates else None
    o_ref = refs.pop(0)
    og_ref = refs.pop(0) if with_gates else None
    hn_ref = refs.pop(0) if with_norm else x_ref
    mm = _dot_nt if transposed else functools.partial(jnp.dot, preferred_element_type=F32)

    first = pl.program_id(1) == 0

    @pl.when(first)
    def _():
        if with_norm:
            for r0 in range(0, x_ref.shape[0], NORM_ROWS):
                rows = slice(r0, r0 + NORM_ROWS)
                x = x_ref[rows, :]
                ms = jnp.mean(x * x, axis=-1, keepdims=True)
                hn = (x * lax.rsqrt(ms + EPS) * g_ref[...]).astype(BF16)
                hn_ref[rows, :] = hn
                o_ref[rows, :] = mm(hn, w_ref[...])
        if with_gates:
            og_ref[...] = mm(hn_ref[...], wg_ref[...])

    @pl.when(jnp.logical_not(first) if with_norm else True)
    def _():
        o_ref[...] = mm(hn_ref[...], w_ref[...])


IN_PROJ_ROWS = 1024
IN_PROJ_VMEM = 46 * 1024 * 1024


def _in_proj_cols(n, tm, d, gw, x_bytes, with_norm):
    for tn in range(n, 0, -512):
        need = (2 * tm * d * x_bytes + (tm * d * 2 if with_norm else 0) + 2 * d * tn * 2
                + 2 * tm * tn * 4 + 2 * (d * gw * 2 + tm * gw * 4))
        if n % tn == 0 and need <= IN_PROJ_VMEM:
            return tn
    raise ValueError("no column tile fits")


def _in_proj(x, g, w, wg=None, *, n=None, transposed=False, tm=IN_PROJ_ROWS):
    m, d = x.shape
    n = w.shape[0 if transposed else 1] if n is None else n
    with_gates = wg is not None
    with_norm = g is not None
    gw = wg.shape[0 if transposed else 1] if with_gates else 0
    w_spec = (lambda cols, idx: pl.BlockSpec((cols, d), lambda i, j: (idx(j), 0)) if transposed
              else pl.BlockSpec((d, cols), lambda i, j: (0, idx(j))))
    tn = _in_proj_cols(n, tm, d, gw, x.dtype.itemsize, with_norm)
    in_specs = [pl.BlockSpec((tm, d), lambda i, j: (i, 0))]
    args = [x]
    if with_norm:
        in_specs.append(pl.BlockSpec((1, d), lambda i, j: (0, 0)))
        args.append(g.reshape(1, d))
    in_specs.append(w_spec(tn, lambda j: j))
    args.append(w)
    out_shape = [jax.ShapeDtypeStruct((m, n), F32)]
    out_specs = [pl.BlockSpec((tm, tn), lambda i, j: (i, j))]
    if with_gates:
        in_specs.append(w_spec(gw, lambda j: 0))
        out_shape.append(jax.ShapeDtypeStruct((m, gw), F32))
        out_specs.append(pl.BlockSpec((tm, gw), lambda i, j: (i, 0)))
        args.append(wg)
    res = pl.pallas_call(
        functools.partial(_in_proj_kernel, with_norm=with_norm, with_gates=with_gates,
                          transposed=transposed),
        grid=(m // tm, n // tn),
        in_specs=in_specs,
        out_specs=out_specs,
        out_shape=out_shape,
        scratch_shapes=[pltpu.VMEM((tm, d), BF16)] if with_norm else [],
        compiler_params=pltpu.CompilerParams(
            dimension_semantics=("parallel", "arbitrary"), vmem_limit_bytes=VMEM_LIMIT),
        name="in_proj_gates" if with_gates else "in_proj",
    )(*args)
    return res if with_gates else res[0]


HG_CHUNK = 128
HG_GROUPS = HG_CHUNK // SUBLANES
HG_INFLIGHT = 8


def _hgrn2_kernel(q_ref, f_ref, i_ref, z_ref, lbl_ref, an_ref, o_ref,
                  st_ref, qs_ref, ks_ref, bs_ref, od_ref, tri_ref, *, layer):
    s_len = q_ref.shape[1]
    dk = q_ref.shape[2]
    c = HG_CHUNK
    ng = HG_GROUPS
    assert c == dk == LANES and i_ref.shape[2] == dk

    lg = lbl_ref[...]
    e = jnp.exp(lg - jnp.max(lg, axis=0, keepdims=True))
    sm = e / jnp.sum(e, axis=0, keepdims=True)
    lb = jnp.sum(sm[:layer + 1], axis=0, keepdims=True)
    an = an_ref[...]

    rowc = lax.broadcasted_iota(jnp.int32, (c, c), 0)
    colc = lax.broadcasted_iota(jnp.int32, (c, c), 1)
    tri_ref[...] = (colc <= rowc).astype(BF16)
    same_group = (colc % ng) == (rowc % ng)
    lane = lax.broadcasted_iota(jnp.int32, (SUBLANES, c), 1)
    diag_off = lane - lax.broadcasted_iota(jnp.int32, (SUBLANES, c), 0)
    zeros8 = jnp.zeros((SUBLANES, dk), F32)
    zeros16 = jnp.zeros((ng, dk), F32)

    st_ref[...] = jnp.zeros_like(st_ref)

    def rows8(x, r):
        return x[r * SUBLANES:(r + 1) * SUBLANES, :]

    def chunk(ci, slot):
        qs, ks, bs, od = qs_ref.at[slot], ks_ref.at[slot], bs_ref.at[slot], od_ref.at[slot]
        r0 = pl.multiple_of(ci * c, c)
        qp = q_ref[0, pl.ds(r0, c), :]
        fp = f_ref[0, pl.ds(r0, c), :]
        v = i_ref[0, pl.ds(r0, c), :]
        q = _silu(qp)
        gate = (1.0 - lb) * _sigmoid(fp)
        f = lb + gate
        k = (1.0 - lb) - gate

        lf = jnp.log2(f)
        hi = lf.astype(BF16)
        r1 = lf - hi.astype(F32)
        mid = r1.astype(BF16)
        lo = (r1 - mid.astype(F32)).astype(BF16)
        bb = jnp.dot(tri_ref[...], jnp.concatenate([hi, mid, lo], axis=1),
                     preferred_element_type=F32)
        yield
        b2 = bb[:, :dk] + bb[:, dk:2 * dk] + bb[:, 2 * dk:]
        qs[...] = q
        ks[...] = k
        bs[...] = b2
        b_last = bs[c - 1:c, :]
        upd = _dot_tn(v, k * jnp.exp2(b_last - b2))

        ps, rights = [], []
        for j in range(ng.bit_length() - 1):
            lhs, rhs, rights_j = [], [], []
            for r in range(ng):
                mid_grp = ((r >> (j + 1)) << (j + 1)) | ((1 << j) - 1)
                beta = bs[mid_grp * SUBLANES + SUBLANES - 1:(mid_grp + 1) * SUBLANES, :]
                if (r >> j) & 1:
                    lhs.append(rows8(q, r) * jnp.exp2(rows8(b2, r) - beta))
                    rhs.append(zeros8)
                    rights_j.append(r)
                else:
                    rhs.append(rows8(k, r) * jnp.exp2(beta - rows8(b2, r)))
            ps.append(_dot_nt(jnp.concatenate(lhs, axis=0), jnp.concatenate(rhs, axis=0)))
            rights.append(rights_j)
        yield

        st = st_ref[...]
        o = _dot_nt(q * jnp.exp2(b2), st)
        st_ref[...] = st * jnp.exp2(b_last) + upd

        def permuted(ref):
            return [ref[pl.ds(pos, ng, stride=SUBLANES), :] for pos in range(SUBLANES)]

        q_p, k_p, b_p = permuted(qs), permuted(ks), permuted(bs)
        lhs_slots, rhs_slots = [], []
        for j in range(SUBLANES.bit_length() - 1):
            for blk in range(SUBLANES >> (j + 1)):
                mid_pos = (blk << (j + 1)) | ((1 << j) - 1)
                lhs, rhs = [], []
                for pos in range(SUBLANES):
                    if pos >> (j + 1) != blk:
                        lhs.append(zeros16)
                        rhs.append(zeros16)
                    elif (pos >> j) & 1:
                        lhs.append(q_p[pos] * jnp.exp2(b_p[pos] - b_p[mid_pos]))
                        rhs.append(zeros16)
                    else:
                        lhs.append(zeros16)
                        rhs.append(k_p[pos] * jnp.exp2(b_p[mid_pos] - b_p[pos]))
                lhs_slots.append(jnp.concatenate(lhs, axis=0).astype(BF16))
                rhs_slots.append(jnp.concatenate(rhs, axis=0).astype(BF16))
        pd = _dot_nt(jnp.concatenate(lhs_slots, axis=1), jnp.concatenate(rhs_slots, axis=1))
        yield

        dg = jnp.sum(q * k, axis=1, keepdims=True)
        a_rows = [jnp.where(diag_off == r * SUBLANES, rows8(dg, r), 0.0) for r in range(ng)]
        for j, (p, rights_j) in enumerate(zip(ps, rights)):
            for i, r in enumerate(rights_j):
                right_start = (((r >> (j + 1)) << (j + 1)) | (1 << j)) * SUBLANES
                a_rows[r] = jnp.where(lane < right_start, rows8(p, i), a_rows[r])
        o = o + _dot(jnp.concatenate(a_rows, axis=0), v)
        v_p = [i_ref[0, pl.ds(r0 + pos, ng, stride=SUBLANES), :] for pos in range(SUBLANES)]
        od[...] = _dot(jnp.where(same_group, pd, 0.0), jnp.concatenate(v_p, axis=0))
        yield

        o = o + jnp.concatenate(
            [od[pl.ds(g, SUBLANES, stride=ng), :] for g in range(ng)], axis=0)
        yn = o * lax.rsqrt(jnp.mean(o * o, axis=-1, keepdims=True) + EPS) * an
        z = z_ref[0, pl.ds(r0, c), :]
        o_ref[0, pl.ds(r0, c), :] = (yn * _silu(z)).astype(o_ref.dtype)

    def chunks(i, carry):
        stages = [chunk(i * HG_INFLIGHT + slot, slot) for slot in range(HG_INFLIGHT)]
        while stages:
            stages = [g for g in stages if next(g, stages) is not stages]
        return carry

    lax.fori_loop(0, s_len // (c * HG_INFLIGHT), chunks, 0)


def _hgrn2(u, lb_logits, a_norm, *, layer, heads, dk):
    bsz, s_len, _ = u.shape
    nl = lb_logits.shape[0]
    blk = lambda off: pl.BlockSpec((1, s_len, dk), lambda b, h, off=off: (b, 0, off + h))
    return pl.pallas_call(
        functools.partial(_hgrn2_kernel, layer=layer),
        grid=(bsz, heads),
        in_specs=[blk(0), blk(heads), blk(2 * heads), blk(3 * heads),
                  pl.BlockSpec((nl, dk), lambda b, h: (0, h)),
                  pl.BlockSpec((1, dk), lambda b, h: (0, h))],
        out_specs=pl.BlockSpec((1, s_len, dk), lambda b, h: (b, 0, h)),
        out_shape=jax.ShapeDtypeStruct((bsz, s_len, heads * dk), BF16),
        scratch_shapes=[pltpu.VMEM((dk, dk), F32)]
        + [pltpu.VMEM((HG_INFLIGHT, HG_CHUNK, dk), F32)] * 4
        + [pltpu.VMEM((HG_CHUNK, HG_CHUNK), BF16)],
        compiler_params=pltpu.CompilerParams(
            dimension_semantics=("parallel", "parallel"), vmem_limit_bytes=VMEM_LIMIT),
        name="hgrn2",
    )(u, u, u, u, lb_logits, a_norm.reshape(1, -1))


def _rglru_kernel(x_ref, z_ref, cw_ref, cb_ref, wr_ref, br_ref, wi_ref, bi_ref, lam_ref,
                  o_ref, a_ref, u_ref, h_ref, ca_ref, cu_ref):
    s_len = x_ref.shape[1]
    w = x_ref.shape[2]
    ng = s_len // SUBLANES
    xc = _causal_conv(x_ref, cw_ref, cb_ref)
    r = _sigmoid(_dot(xc, wr_ref[0]) + br_ref[...])
    ig = _sigmoid(_dot(xc, wi_ref[0]) + bi_ref[...])
    log_a = -LRU_C * r * _softplus(-lam_ref[...])
    a = jnp.exp(log_a)
    a_ref[...] = a
    t = -jnp.tanh(log_a) * (a * a + 1.0)
    u_ref[...] = jnp.where(t > 0.0, t * lax.rsqrt(t), 0.0) * (ig * xc)

    ca = cu = None
    for pos in range(SUBLANES):
        ap = a_ref[pl.ds(pos, ng, stride=SUBLANES), :]
        up = u_ref[pl.ds(pos, ng, stride=SUBLANES), :]
        ca, cu = (ap, up) if pos == 0 else (ap * ca, ap * cu + up)
        ca_ref[pos] = ca
        cu_ref[pos] = cu
    grp = lax.broadcasted_iota(jnp.int32, (ng, w), 0)
    ta, tu = ca, cu
    s = 1
    while s < ng:
        m = grp >= s
        tu = tu + ta * jnp.where(m, pltpu.roll(tu, s, 0), 0.0)
        ta = ta * jnp.where(m, pltpu.roll(ta, s, 0), 1.0)
        s *= 2
    h_in = jnp.where(grp >= 1, pltpu.roll(tu, 1, 0), 0.0)
    for pos in range(SUBLANES):
        h_ref[pl.ds(pos, ng, stride=SUBLANES), :] = ca_ref[pos] * h_in + cu_ref[pos]
    o_ref[0] = (h_ref[...] * _silu(z_ref[0])).astype(o_ref.dtype)


def _rglru(u, conv_w, conv_b, w_r, b_r, w_i, b_i, lam, *, x_off, z_off):
    bsz, s_len, _ = u.shape
    nblk, blk, _ = w_r.shape
    vec = lambda rows: pl.BlockSpec((rows, blk), lambda b, n: (0, n))
    mat = pl.BlockSpec((1, blk, blk), lambda b, n: (n, 0, 0))
    return pl.pallas_call(
        _rglru_kernel,
        grid=(bsz, nblk),
        in_specs=[pl.BlockSpec((1, s_len, blk), lambda b, n: (b, 0, x_off + n)),
                  pl.BlockSpec((1, s_len, blk), lambda b, n: (b, 0, z_off + n)),
                  vec(CONV_W), vec(1), mat, vec(1), mat, vec(1), vec(1)],
        out_specs=pl.BlockSpec((1, s_len, blk), lambda b, n: (b, 0, n)),
        out_shape=jax.ShapeDtypeStruct((bsz, s_len, nblk * blk), BF16),
        scratch_shapes=[pltpu.VMEM((s_len, blk), F32)] * 3
        + [pltpu.VMEM((SUBLANES, s_len // SUBLANES, blk), F32)] * 2,
        compiler_params=pltpu.CompilerParams(
            dimension_semantics=("parallel", "parallel"), vmem_limit_bytes=VMEM_LIMIT),
        name="rglru",
    )(u, u, conv_w, conv_b.reshape(1, -1), w_r, b_r.reshape(1, -1), w_i, b_i.reshape(1, -1),
      lam.reshape(1, -1))


ML_CHUNK = 128
ML_UNROLL = 8


def _mlstm_gates_kernel(g_ref, gb_ref, mu_ref, gi_ref, en_ref, ks_ref, dec_ref, rt_ref):
    s_len = g_ref.shape[1]
    c = ML_CHUNK
    t = g_ref[0] + gb_ref[...]
    li = t[:, :LANES]
    lf = _log_sigmoid(t[:, LANES:])
    row = lax.broadcasted_iota(jnp.int32, (s_len, LANES), 0) % c

    def scan(x, op, fill):
        s = 1
        while s < c:
            x = op(x, jnp.where(row >= s, pltpu.roll(x, s, 0), fill))
            s *= 2
        return x

    b = scan(lf, jnp.add, 0.0)
    r = li - b
    rho = scan(r, jnp.maximum, -jnp.inf)
    m = jnp.zeros((1, LANES), F32)
    for ci in range(s_len // c):
        sl = slice(ci * c, (ci + 1) * c)
        b_c, r_c, rho_c = b[sl], r[sl], rho[sl]
        b_last = b_c[c - 1:c, :]
        m_new = b_last + jnp.maximum(m, rho_c[c - 1:c, :])
        mu = jnp.maximum(rho_c, m)
        mu_ref[0, sl, :] = mu
        gi_ref[0, sl, :] = jnp.exp(m - mu)
        en_ref[0, sl, :] = jnp.exp(-(b_c + mu))
        ks_ref[0, sl, :] = jnp.exp(b_last + r_c - m_new)
        dec_ref[0, ci:ci + 1, :] = jnp.exp(b_last + m - m_new)
        rt_ref[0, :, sl] = r_c.T
        m = m_new


def _mlstm_gates(gates, gate_bias):
    bsz, s_len, gw = gates.shape
    nc = s_len // ML_CHUNK
    col = jax.ShapeDtypeStruct((bsz, s_len, LANES), F32)
    col_spec = pl.BlockSpec((1, s_len, LANES), lambda b: (b, 0, 0))
    return pl.pallas_call(
        _mlstm_gates_kernel,
        grid=(bsz,),
        in_specs=[pl.BlockSpec((1, s_len, gw), lambda b: (b, 0, 0)),
                  pl.BlockSpec((1, gw), lambda b: (0, 0))],
        out_specs=[col_spec] * 4 + [pl.BlockSpec((1, nc, LANES), lambda b: (b, 0, 0)),
                                    pl.BlockSpec((1, LANES, s_len), lambda b: (b, 0, 0))],
        out_shape=[col] * 4 + [jax.ShapeDtypeStruct((bsz, nc, LANES), F32),
                               jax.ShapeDtypeStruct((bsz, LANES, s_len), F32)],
        compiler_params=pltpu.CompilerParams(
            dimension_semantics=("parallel",), vmem_limit_bytes=VMEM_LIMIT),
        name="mlstm_gates",
    )(gates, gate_bias)


def _mlstm_kernel(q_ref, k_ref, v_ref, og_ref, z_ref, mu_ref, gi_ref, en_ref, ks_ref, dec_ref,
                  rt_ref, cwq_ref, cbq_ref, cwk_ref, cbk_ref, cn_ref, o_ref, qs_ref, ks_s_ref,
                  cst_ref, nst_ref):
    s_len = q_ref.shape[1]
    dk = q_ref.shape[2]
    c = ML_CHUNK
    assert c == dk == LANES
    h = pl.program_id(1)

    qs_ref[...] = _silu(_causal_conv(q_ref, cwq_ref, cbq_ref)) * (dk ** -0.5)
    ks_s_ref[...] = _silu(_causal_conv(k_ref, cwk_ref, cbk_ref))
    cst_ref[...] = jnp.zeros_like(cst_ref)
    nst_ref[...] = jnp.zeros_like(nst_ref)

    row = lax.broadcasted_iota(jnp.int32, (c, c), 0)
    col = lax.broadcasted_iota(jnp.int32, (c, c), 1)
    causal = col <= row
    head = col == h
    cn = cn_ref[...]

    def pick(x):
        return jnp.sum(jnp.where(head, x, 0.0), axis=1, keepdims=True)

    dec_all = dec_ref[0]
    chunk_id = lax.broadcasted_iota(jnp.int32, dec_all.shape, 0)
    head_lane = lax.broadcasted_iota(jnp.int32, (1, LANES), 1) == h
    head_row = lax.broadcasted_iota(jnp.int32, (SUBLANES, c), 0) == h

    def chunk(ci, carry):
        r0 = pl.multiple_of(ci * c, c)
        q = qs_ref[pl.ds(r0, c), :]
        k = ks_s_ref[pl.ds(r0, c), :]
        v = v_ref[0, pl.ds(r0, c), :]
        mu = pick(mu_ref[0, pl.ds(r0, c), :])
        g_inter = pick(gi_ref[0, pl.ds(r0, c), :])
        e_negm = pick(en_ref[0, pl.ds(r0, c), :])
        kscale = pick(ks_ref[0, pl.ds(r0, c), :])
        decay = jnp.sum(jnp.where(chunk_id == ci, dec_all, 0.0), axis=0, keepdims=True)
        decay = jnp.sum(jnp.where(head_lane, decay, 0.0), axis=1, keepdims=True)
        r_row = jnp.sum(jnp.where(head_row, rt_ref[0, :, pl.ds(r0, c)], 0.0),
                        axis=0, keepdims=True)
        wts = jnp.where(causal, jnp.exp(jnp.minimum(r_row - mu, 0.0)), 0.0)
        qk = _dot_nt(q, k)
        kw = k * kscale
        upd = _dot_tn(kw, v)

        cst = cst_ref[...]
        n_st = nst_ref[...]
        sc = qk * wts
        inter = _dot(q, cst)
        intra = _dot(sc, v)
        cst_ref[...] = decay * cst + upd
        nst_ref[...] = decay * n_st + jnp.sum(kw, axis=0, keepdims=True)

        num = g_inter * inter + intra
        den = (g_inter * jnp.sum(q * n_st, axis=1, keepdims=True)
               + jnp.sum(sc, axis=1, keepdims=True))
        inv = 1.0 / jnp.maximum(jnp.abs(den), e_negm)
        msq = jnp.mean(num * num, axis=-1, keepdims=True)
        yn = num * (inv * lax.rsqrt(inv * inv * msq + EPS)) * cn
        og = og_ref[0, pl.ds(r0, c), :]
        z = z_ref[0, pl.ds(r0, c), :]
        o_ref[0, pl.ds(r0, c), :] = (yn * _sigmoid(og) * _silu(z)).astype(o_ref.dtype)
        return carry

    lax.fori_loop(0, s_len // c, chunk, 0, unroll=ML_UNROLL)


def _mlstm(u, gate_terms, conv_w, conv_b, c_norm, *, heads, dk, dv):
    bsz, s_len, _ = u.shape
    nc = s_len // ML_CHUNK
    col_spec = pl.BlockSpec((1, s_len, LANES), lambda b, h: (b, 0, 0))
    qk_blk = lambda off: pl.BlockSpec((1, s_len, dk), lambda b, h, off=off: (b, 0, off + h))
    v_blk = lambda off: pl.BlockSpec((1, s_len, dv), lambda b, h, off=off: (b, 0, off + h))
    cw = lambda off: pl.BlockSpec((CONV_W, dk), lambda b, h, off=off: (0, off + h))
    cb = lambda off: pl.BlockSpec((1, dk), lambda b, h, off=off: (0, off + h))
    nqk = 2 * heads * dk // dv
    return pl.pallas_call(
        _mlstm_kernel,
        grid=(bsz, heads),
        in_specs=[qk_blk(0), qk_blk(heads),
                  v_blk(nqk), v_blk(nqk + heads), v_blk(nqk + 2 * heads),
                  col_spec, col_spec, col_spec, col_spec,
                  pl.BlockSpec((1, nc, LANES), lambda b, h: (b, 0, 0)),
                  pl.BlockSpec((1, SUBLANES, s_len), lambda b, h: (b, 0, 0)),
                  cw(0), cb(0), cw(heads), cb(heads),
                  pl.BlockSpec((1, dv), lambda b, h: (0, h))],
        out_specs=pl.BlockSpec((1, s_len, dv), lambda b, h: (b, 0, h)),
        out_shape=jax.ShapeDtypeStruct((bsz, s_len, heads * dv), BF16),
        scratch_shapes=[pltpu.VMEM((s_len, dk), F32), pltpu.VMEM((s_len, dk), F32),
                        pltpu.VMEM((dk, dv), F32), pltpu.VMEM((1, dk), F32)],
        compiler_params=pltpu.CompilerParams(
            dimension_semantics=("parallel", "parallel"), vmem_limit_bytes=VMEM_LIMIT),
        name="mlstm",
    )(u, u, u, u, u, *gate_terms, conv_w, conv_b.reshape(1, -1), conv_w,
      conv_b.reshape(1, -1), c_norm.reshape(1, -1))


OUT_SUB_ROWS = 256


def _out_kernel(h_ref, *rest, n_y, final):
    y_refs = rest[:n_y]
    wo_refs = rest[n_y:2 * n_y]
    p_ref, pw_ref, pn_ref, gw_ref, norm_ref = rest[2 * n_y:2 * n_y + 5]
    out_refs = rest[2 * n_y + 5:]

    def rows_block(r0):
        rows = slice(r0, r0 + OUT_SUB_ROWS)
        mix = jnp.dot(y_refs[0][rows, :], wo_refs[0][...], preferred_element_type=F32)
        for y_ref, wo_ref in zip(y_refs[1:], wo_refs[1:]):
            mix = mix + jnp.dot(y_ref[rows, :], wo_ref[...], preferred_element_type=F32)
        yield
        h1 = h_ref[rows, :] + mix
        pe = _dot(p_ref[rows, :], pw_ref[...])
        gate = _dot(h1, gw_ref[...])
        yield
        pe = pe * lax.rsqrt(jnp.mean(pe * pe, axis=-1, keepdims=True) + EPS) * pn_ref[...]
        h2 = h1 + _sigmoid(gate) * pe
        hn = h2 * lax.rsqrt(jnp.mean(h2 * h2, axis=-1, keepdims=True) + EPS) * norm_ref[...]
        if final:
            out_refs[0][rows, :] = hn
        else:
            out_refs[0][rows, :] = h2
            out_refs[1][rows, :] = hn.astype(out_refs[1].dtype)

    stages = [rows_block(r0) for r0 in range(0, h_ref.shape[0], OUT_SUB_ROWS)]
    while stages:
        stages = [g for g in stages if next(g, stages) is not stages]


def _out_proj(h, ys, w_out, p, layer, ple_w, ple_norm, gate_w, norm, final, *, tm=512):
    m, d = h.shape
    const = lambda shape: pl.BlockSpec(shape, lambda i: (0, 0), pipeline_mode=pl.Buffered(1))
    in_specs = [pl.BlockSpec((tm, d), lambda i: (i, 0))]
    in_specs += [pl.BlockSpec((tm, y.shape[1]), lambda i: (i, 0)) for y in ys]
    assert all(y.shape[1] == ys[0].shape[1] for y in ys)
    in_specs += [pl.BlockSpec((y.shape[1], d), lambda i, r=r: (r, 0), pipeline_mode=pl.Buffered(1))
                 for r, y in enumerate(ys)]
    stacked = lambda a: pl.BlockSpec((None,) + a.shape[1:], lambda i: (layer, 0, 0),
                                     pipeline_mode=pl.Buffered(1))
    in_specs += [pl.BlockSpec((None, tm, p.shape[2]), lambda i: (layer, i, 0)),
                 stacked(ple_w), const((1, d)), stacked(gate_w), const((1, d))]
    args = [h, *ys, *([w_out] * len(ys)), p, ple_w, ple_norm.reshape(1, d), gate_w,
            norm.reshape(1, d)]
    row_spec = pl.BlockSpec((tm, d), lambda i: (i, 0))
    return pl.pallas_call(
        functools.partial(_out_kernel, n_y=len(ys), final=final),
        grid=(m // tm,),
        in_specs=in_specs,
        out_specs=row_spec if final else [row_spec, row_spec],
        out_shape=(jax.ShapeDtypeStruct((m, d), F32) if final else
                   [jax.ShapeDtypeStruct((m, d), F32), jax.ShapeDtypeStruct((m, d), BF16)]),
        compiler_params=pltpu.CompilerParams(
            dimension_semantics=("parallel",), vmem_limit_bytes=VMEM_LIMIT),
        name="out_proj_final" if final else "out_proj",
    )(*args)


def kernel(x, p, e_norm, e_w_in, a_lb_logits, a_norm, b_conv_w, b_conv_b, b_w_r, b_b_r, b_w_i, b_b_i, b_lambda, e_w_out, o_norm, o_w_in, c_conv_w, c_conv_b, c_b_i, c_b_f, c_norm, o_w_out, ple_w, ple_norm, ple_gate_w, final_norm):
    bsz, s_len, d = x.shape
    depth = p.shape[0]
    m = bsz * s_len
    a_width = a_norm.shape[1]
    b_width = b_lambda.shape[1]
    a_dk = LANES
    a_heads = a_width // a_dk
    c_heads = c_b_i.shape[1]
    c_dv = c_norm.shape[1] // c_heads
    c_dk = c_conv_w.shape[2] // (2 * c_heads)
    c_main = o_w_in.shape[2] - 2 * c_heads

    h = x.reshape(m, d)
    ple_w_bf, gate_w_bf = ple_w.astype(BF16), ple_gate_w.astype(BF16)
    hn = None
    layer_norm = lambda i: e_norm[i // 2] if i % 2 == 0 else o_norm[i // 2]
    for i in range(depth):
        j = i // 2
        last = i == depth - 1
        xin, g = (h, layer_norm(i)) if hn is None else (hn, None)
        if i % 2 == 0:
            u = _in_proj(xin, g, e_w_in[j].astype(BF16)).reshape(bsz, s_len, -1)
            ya = _hgrn2(u, a_lb_logits, a_norm[j], layer=j, heads=a_heads, dk=a_dk)
            x_off = (2 * a_heads * a_dk + 2 * a_width) // LANES
            yb = _rglru(u, b_conv_w[j], b_conv_b[j], b_w_r[j].astype(BF16), b_b_r[j],
                        b_w_i[j].astype(BF16), b_b_i[j], b_lambda[j],
                        x_off=x_off, z_off=x_off + b_width // LANES)
            w_out = e_w_out[j].astype(BF16)
            ys = [ya.reshape(m, a_width), yb.reshape(m, b_width)]
        else:
            w_in_t = o_w_in[j].T
            pad = (0, LANES - c_heads)
            wg_t = jnp.concatenate([jnp.pad(w_in_t[c_main:c_main + c_heads], (pad, (0, 0))),
                                    jnp.pad(w_in_t[c_main + c_heads:], (pad, (0, 0)))]).astype(BF16)
            u, gates = _in_proj(xin, g, w_in_t.astype(BF16), wg_t, n=c_main, transposed=True)
            gate_bias = jnp.concatenate([jnp.pad(c_b_i[j], pad),
                                         jnp.pad(c_b_f[j], pad)]).reshape(1, 2 * LANES)
            gate_terms = _mlstm_gates(gates.reshape(bsz, s_len, 2 * LANES), gate_bias)
            yc = _mlstm(u.reshape(bsz, s_len, -1), gate_terms, c_conv_w[j], c_conv_b[j],
                        c_norm[j], heads=c_heads, dk=c_dk, dv=c_dv)
            ys = [yc.reshape(m, -1)]
            w_out = o_w_out[j].astype(BF16)
        res = _out_proj(h, ys, w_out, p.reshape(depth, m, -1), i, ple_w_bf, ple_norm[i],
                        gate_w_bf, final_norm if last else layer_norm(i + 1), last)
        h, hn = (res, None) if last else res
    return h.reshape(bsz, s_len, d)
```

```python
import functools

import jax
import jax.numpy as jnp
from jax import lax
from jax.experimental import pallas as pl
from jax.experimental.pallas import tpu as pltpu

EPS = 1e-6
LRU_C = 8.0
CONV_W = 4
LANES = 128
SUBLANES = 8
VMEM_LIMIT = 56 * 1024 * 1024
BF16 = jnp.bfloat16
F32 = jnp.float32


def _sigmoid(x):
    return jax.nn.sigmoid(x)


def _silu(x):
    return x * jax.nn.sigmoid(x)


def _softplus(x):
    return jnp.maximum(x, 0.0) + jnp.log1p(jnp.exp(-jnp.abs(x)))


def _log_sigmoid(x):
    return -_softplus(-x)


def _shift_rows(x, s, row):
    return jnp.where(row >= s, pltpu.roll(x, s, 0), 0.0)


def _cumsum_rows(x, row):
    s = 1
    while s < x.shape[0]:
        x = x + _shift_rows(x, s, row)
        s *= 2
    return x


def _cummax_rows(x, row):
    s = 1
    while s < x.shape[0]:
        x = jnp.maximum(x, jnp.where(row >= s, pltpu.roll(x, s, 0), -jnp.inf))
        s *= 2
    return x


def _causal_conv(x_ref, cw_ref, cb_ref):
    s_len = x_ref.shape[1]
    taps = [cw_ref[k:k + 1, :] for k in range(CONV_W)]
    acc = cb_ref[...] + x_ref[0, SUBLANES:, :] * taps[CONV_W - 1]
    for j in range(1, CONV_W):
        acc = acc + x_ref[0, SUBLANES - j:s_len - j, :] * taps[CONV_W - 1 - j]
    x0 = x_ref[0, 0:SUBLANES, :]
    row = lax.broadcasted_iota(jnp.int32, x0.shape, 0)
    acc0 = cb_ref[...] + x0 * taps[CONV_W - 1]
    for j in range(1, CONV_W):
        acc0 = acc0 + _shift_rows(x0, j, row) * taps[CONV_W - 1 - j]
    return jnp.concatenate([acc0, acc], axis=0)


def _dot(a, b):
    return jnp.dot(a.astype(BF16), b.astype(BF16), preferred_element_type=F32)


def _dot_nt(a, b):
    return lax.dot_general(a.astype(BF16), b.astype(BF16), (((1,), (1,)), ((), ())),
                           preferred_element_type=F32)


def _dot_tn(a, b):
    return lax.dot_general(a.astype(BF16), b.astype(BF16), (((0,), (0,)), ((), ())),
                           preferred_element_type=F32)


SUB_ROWS = 256
COL_PIECE = 512


def _in_proj_kernel(*refs, with_norm, with_gates, transposed, spans, layer, seq_blocks, q_scale):
    kinds = {k for k, _ in spans}
    refs = list(refs)
    x_ref = refs.pop(0)
    g_ref = refs.pop(0) if with_norm else None
    w_ref = refs.pop(0)
    wg_ref = refs.pop(0) if with_gates else None
    lbl_ref = refs.pop(0) if "hgate" in kinds else None
    cw_ref, cb_ref = (refs.pop(0), refs.pop(0)) if "conv_q" in kinds else (None, None)
    o_ref = refs.pop(0)
    og_ref = refs.pop(0) if with_gates else None
    hn_ref = refs.pop(0) if with_norm else x_ref
    cs_ref = refs.pop(0) if "conv_q" in kinds else None
    mm = _dot_nt if transposed else functools.partial(jnp.dot, preferred_element_type=F32)
    tm, tn = o_ref.shape
    i, j = pl.program_id(0), pl.program_id(1)

    starts, c0 = [], 0
    for kind, width in spans:
        starts.append((kind, c0, width))
        c0 += width
    n = c0

    if lbl_ref is not None:
        lg = lbl_ref[...]
        e = jnp.exp(lg - jnp.max(lg, axis=0, keepdims=True))
        sm = e / jnp.sum(e, axis=0, keepdims=True)
        lb = jnp.sum(sm[:layer + 1], axis=0, keepdims=True)

    def conv_silu(raw, span_col):
        lt = span_col // LANES
        cs_ref[lt, SUBLANES:, :] = raw
        sl = slice(span_col, span_col + LANES)
        acc = cb_ref[:, sl] + raw * cw_ref[CONV_W - 1:CONV_W, sl]
        for k in range(1, CONV_W):
            acc = acc + (cs_ref[lt, SUBLANES - k:SUBLANES - k + SUB_ROWS, :]
                         * cw_ref[CONV_W - 1 - k:CONV_W - k, sl])
        cs_ref[lt, 0:SUBLANES, :] = raw[SUB_ROWS - SUBLANES:, :]
        return _silu(acc)

    def activate(kind, r, c0):
        if kind == "silu":
            return _silu(r)
        if kind == "sigmoid":
            return _sigmoid(r)
        if kind == "hgate":
            lbp = lb[:, c0:c0 + r.shape[1]]
            return lbp + (1.0 - lbp) * _sigmoid(r)
        if kind in ("conv_q", "conv_k"):
            conv_c0 = c0 + (0 if kind == "conv_q" else dict((k, w) for k, w in spans)["conv_q"])
            r = jnp.concatenate([conv_silu(r[:, t:t + LANES], conv_c0 + t)
                                 for t in range(0, r.shape[1], LANES)], axis=1)
            return r * q_scale if kind == "conv_q" else r
        return r

    def run_tile(jt):
        tile_c0 = jt * tn
        pieces = []
        for kind, c0, width in starts:
            lo, hi = max(c0, tile_c0), min(c0 + width, tile_c0 + tn)
            for p0 in range(lo, hi, COL_PIECE):
                pieces.append((kind, p0 - tile_c0, min(COL_PIECE, hi - p0), p0 - c0))
        if cs_ref is not None and any(kind.startswith("conv") for kind, *_ in pieces):
            @pl.when(i % seq_blocks == 0)
            def _():
                cs_ref[:, 0:SUBLANES, :] = jnp.zeros((cs_ref.shape[0], SUBLANES, LANES), F32)
        for r0 in range(0, tm, SUB_ROWS):
            rows = slice(r0, r0 + SUB_ROWS)
            if with_norm and jt == 0:
                x = x_ref[rows, :]
                ms = jnp.mean(x * x, axis=-1, keepdims=True)
                hn = (x * lax.rsqrt(ms + EPS) * g_ref[...]).astype(BF16)
                hn_ref[rows, :] = hn
            else:
                hn = hn_ref[rows, :]
            for kind, off, width, c0 in pieces:
                wp = w_ref[off:off + width, :] if transposed else w_ref[:, off:off + width]
                o_ref[rows, off:off + width] = activate(kind, mm(hn, wp), c0)
        if with_gates and jt == 0:
            og_ref[...] = mm(hn_ref[...], wg_ref[...])

    for jt in range(n // tn):
        pl.when(j == jt)(functools.partial(run_tile, jt))


IN_PROJ_ROWS = 1024
IN_PROJ_VMEM = 46 * 1024 * 1024


def _in_proj_cols(n, tm, d, gw, x_bytes, with_norm):
    for tn in range(n, 0, -512):
        need = (2 * tm * d * x_bytes + (tm * d * 2 if with_norm else 0) + 2 * d * tn * 2
                + 2 * tm * tn * 4 + 2 * (d * gw * 2 + tm * gw * 4))
        if n % tn == 0 and need <= IN_PROJ_VMEM:
            return tn
    raise ValueError("no column tile fits")


def _in_proj(x, g, w, wg=None, *, spans, seq_rows, transposed=False, lb_logits=None, layer=0,
             conv_w=None, conv_b=None, q_scale=1.0, tm=IN_PROJ_ROWS):
    m, d = x.shape
    n = sum(width for _, width in spans)
    with_gates = wg is not None
    with_norm = g is not None
    gw = wg.shape[0 if transposed else 1] if with_gates else 0
    w_spec = (lambda cols, idx: pl.BlockSpec((cols, d), lambda i, j: (idx(j), 0)) if transposed
              else pl.BlockSpec((d, cols), lambda i, j: (0, idx(j))))
    const = lambda a: pl.BlockSpec(a.shape, lambda i, j: (0,) * a.ndim)
    tn = _in_proj_cols(n, tm, d, gw, x.dtype.itemsize, with_norm)
    in_specs = [pl.BlockSpec((tm, d), lambda i, j: (i, 0))]
    args = [x]
    if with_norm:
        args.append(g.reshape(1, d))
        in_specs.append(const(args[-1]))
    in_specs.append(w_spec(tn, lambda j: j))
    args.append(w)
    out_shape = [jax.ShapeDtypeStruct((m, n), F32)]
    out_specs = [pl.BlockSpec((tm, tn), lambda i, j: (i, j))]
    if with_gates:
        in_specs.append(w_spec(gw, lambda j: 0))
        out_shape.append(jax.ShapeDtypeStruct((m, gw), F32))
        out_specs.append(pl.BlockSpec((tm, gw), lambda i, j: (i, 0)))
        args.append(wg)
    scratch = [pltpu.VMEM((tm, d), BF16)] if with_norm else []
    if lb_logits is not None:
        args.append(lb_logits)
        in_specs.append(const(lb_logits))
    if conv_w is not None:
        args += [conv_w, conv_b.reshape(1, -1)]
        in_specs += [const(args[-2]), const(args[-1])]
        scratch.append(pltpu.VMEM((conv_w.shape[1] // LANES, SUBLANES + SUB_ROWS, LANES), F32))
    res = pl.pallas_call(
        functools.partial(_in_proj_kernel, with_norm=with_norm, with_gates=with_gates,
                          transposed=transposed, spans=tuple(spans), layer=layer,
                          seq_blocks=seq_rows // tm, q_scale=q_scale),
        grid=(m // tm, n // tn),
        in_specs=in_specs,
        out_specs=out_specs,
        out_shape=out_shape,
        scratch_shapes=scratch,
        compiler_params=pltpu.CompilerParams(
            dimension_semantics=("arbitrary", "arbitrary"), vmem_limit_bytes=VMEM_LIMIT),
        name="in_proj_gates" if with_gates else "in_proj",
    )(*args)
    return res if with_gates else res[0]


HG_CHUNK = 128
HG_GROUPS = HG_CHUNK // SUBLANES
HG_INFLIGHT = 8


def _hgrn2_kernel(q_ref, f_ref, i_ref, z_ref, an_ref, o_ref,
                  st_ref, qs_ref, ks_ref, bs_ref, od_ref, tri_ref):
    s_len = q_ref.shape[1]
    dk = q_ref.shape[2]
    c = HG_CHUNK
    ng = HG_GROUPS
    assert c == dk == LANES and i_ref.shape[2] == dk

    an = an_ref[...]

    rowc = lax.broadcasted_iota(jnp.int32, (c, c), 0)
    colc = lax.broadcasted_iota(jnp.int32, (c, c), 1)
    tri_ref[...] = (colc <= rowc).astype(BF16)
    same_group = (colc % ng) == (rowc % ng)
    lane = lax.broadcasted_iota(jnp.int32, (SUBLANES, c), 1)
    diag_off = lane - lax.broadcasted_iota(jnp.int32, (SUBLANES, c), 0)
    zeros8 = jnp.zeros((SUBLANES, dk), F32)
    zeros16 = jnp.zeros((ng, dk), F32)

    st_ref[...] = jnp.zeros_like(st_ref)

    def rows8(x, r):
        return x[r * SUBLANES:(r + 1) * SUBLANES, :]

    def chunk(ci, slot):
        qs, ks, bs, od = qs_ref.at[slot], ks_ref.at[slot], bs_ref.at[slot], od_ref.at[slot]
        r0 = pl.multiple_of(ci * c, c)
        q = q_ref[0, pl.ds(r0, c), :]
        f = f_ref[0, pl.ds(r0, c), :]
        v = i_ref[0, pl.ds(r0, c), :]
        k = 1.0 - f

        lf = jnp.log2(f)
        hi = lf.astype(BF16)
        r1 = lf - hi.astype(F32)
        mid = r1.astype(BF16)
        lo = (r1 - mid.astype(F32)).astype(BF16)
        bb = jnp.dot(tri_ref[...], jnp.concatenate([hi, mid, lo], axis=1),
                     preferred_element_type=F32)
        yield
        b2 = bb[:, :dk] + bb[:, dk:2 * dk] + bb[:, 2 * dk:]
        qs[...] = q
        ks[...] = k
        bs[...] = b2
        b_last = bs[c - 1:c, :]
        upd = _dot_tn(v, k * jnp.exp2(b_last - b2))

        ps, rights = [], []
        for j in range(ng.bit_length() - 1):
            lhs, rhs, rights_j = [], [], []
            for r in range(ng):
                mid_grp = ((r >> (j + 1)) << (j + 1)) | ((1 << j) - 1)
                beta = bs[mid_grp * SUBLANES + SUBLANES - 1:(mid_grp + 1) * SUBLANES, :]
                if (r >> j) & 1:
                    lhs.append(rows8(q, r) * jnp.exp2(rows8(b2, r) - beta))
                    rhs.append(zeros8)
                    rights_j.append(r)
                else:
                    rhs.append(rows8(k, r) * jnp.exp2(beta - rows8(b2, r)))
            ps.append(_dot_nt(jnp.concatenate(lhs, axis=0), jnp.concatenate(rhs, axis=0)))
            rights.append(rights_j)
        yield

        st = st_ref[...]
        o = _dot_nt(q * jnp.exp2(b2), st)
        st_ref[...] = st * jnp.exp2(b_last) + upd

        def permuted(ref):
            return [ref[pl.ds(pos, ng, stride=SUBLANES), :] for pos in range(SUBLANES)]

        q_p, k_p, b_p = permuted(qs), permuted(ks), permuted(bs)
        lhs_slots, rhs_slots = [], []
        for j in range(SUBLANES.bit_length() - 1):
            for blk in range(SUBLANES >> (j + 1)):
                mid_pos = (blk << (j + 1)) | ((1 << j) - 1)
                lhs, rhs = [], []
                for pos in range(SUBLANES):
                    if pos >> (j + 1) != blk:
                        lhs.append(zeros16)
                        rhs.append(zeros16)
                    elif (pos >> j) & 1:
                        lhs.append(q_p[pos] * jnp.exp2(b_p[pos] - b_p[mid_pos]))
                        rhs.append(zeros16)
                    else:
                        lhs.append(zeros16)
                        rhs.append(k_p[pos] * jnp.exp2(b_p[mid_pos] - b_p[pos]))
                lhs_slots.append(jnp.concatenate(lhs, axis=0).astype(BF16))
                rhs_slots.append(jnp.concatenate(rhs, axis=0).astype(BF16))
        pd = _dot_nt(jnp.concatenate(lhs_slots, axis=1), jnp.concatenate(rhs_slots, axis=1))
        yield

        dg = jnp.sum(q * k, axis=1, keepdims=True)
        a_rows = [jnp.where(diag_off == r * SUBLANES, rows8(dg, r), 0.0) for r in range(ng)]
        for j, (p, rights_j) in enumerate(zip(ps, rights)):
            for i, r in enumerate(rights_j):
                right_start = (((r >> (j + 1)) << (j + 1)) | (1 << j)) * SUBLANES
                a_rows[r] = jnp.where(lane < right_start, rows8(p, i), a_rows[r])
        o = o + _dot(jnp.concatenate(a_rows, axis=0), v)
        v_p = [i_ref[0, pl.ds(r0 + pos, ng, stride=SUBLANES), :] for pos in range(SUBLANES)]
        od[...] = _dot(jnp.where(same_group, pd, 0.0), jnp.concatenate(v_p, axis=0))
        yield

        o = o + jnp.concatenate(
            [od[pl.ds(g, SUBLANES, stride=ng), :] for g in range(ng)], axis=0)
        yn = o * lax.rsqrt(jnp.mean(o * o, axis=-1, keepdims=True) + EPS) * an
        z = z_ref[0, pl.ds(r0, c), :]
        o_ref[0, pl.ds(r0, c), :] = (yn * z).astype(o_ref.dtype)

    def chunks(i, carry):
        stages = [chunk(i * HG_INFLIGHT + slot, slot) for slot in range(HG_INFLIGHT)]
        while stages:
            stages = [g for g in stages if next(g, stages) is not stages]
        return carry

    lax.fori_loop(0, s_len // (c * HG_INFLIGHT), chunks, 0)


def _hgrn2(u, a_norm, *, heads, dk):
    bsz, s_len, _ = u.shape
    blk = lambda off: pl.BlockSpec((1, s_len, dk), lambda b, h, off=off: (b, 0, off + h))
    return pl.pallas_call(
        _hgrn2_kernel,
        grid=(bsz, heads),
        in_specs=[blk(0), blk(heads), blk(2 * heads), blk(3 * heads),
                  pl.BlockSpec((1, dk), lambda b, h: (0, h))],
        out_specs=pl.BlockSpec((1, s_len, dk), lambda b, h: (b, 0, h)),
        out_shape=jax.ShapeDtypeStruct((bsz, s_len, heads * dk), BF16),
        scratch_shapes=[pltpu.VMEM((dk, dk), F32)]
        + [pltpu.VMEM((HG_INFLIGHT, HG_CHUNK, dk), F32)] * 4
        + [pltpu.VMEM((HG_CHUNK, HG_CHUNK), BF16)],
        compiler_params=pltpu.CompilerParams(
            dimension_semantics=("parallel", "parallel"), vmem_limit_bytes=VMEM_LIMIT),
        name="hgrn2",
    )(u, u, u, u, a_norm.reshape(1, -1))


def _rglru_kernel(x_ref, z_ref, cw_ref, cb_ref, wr_ref, br_ref, wi_ref, bi_ref, lam_ref,
                  o_ref, a_ref, u_ref, h_ref, ca_ref, cu_ref):
    s_len = x_ref.shape[1]
    w = x_ref.shape[2]
    ng = s_len // SUBLANES
    xc = _causal_conv(x_ref, cw_ref, cb_ref)
    r = _sigmoid(_dot(xc, wr_ref[0]) + br_ref[...])
    ig = _sigmoid(_dot(xc, wi_ref[0]) + bi_ref[...])
    log_a = -LRU_C * r * _softplus(-lam_ref[...])
    a = jnp.exp(log_a)
    a_ref[...] = a
    t = -jnp.tanh(log_a) * (a * a + 1.0)
    u_ref[...] = jnp.where(t > 0.0, t * lax.rsqrt(t), 0.0) * (ig * xc)

    ca = cu = None
    for pos in range(SUBLANES):
        ap = a_ref[pl.ds(pos, ng, stride=SUBLANES), :]
        up = u_ref[pl.ds(pos, ng, stride=SUBLANES), :]
        ca, cu = (ap, up) if pos == 0 else (ap * ca, ap * cu + up)
        ca_ref[pos] = ca
        cu_ref[pos] = cu
    grp = lax.broadcasted_iota(jnp.int32, (ng, w), 0)
    ta, tu = ca, cu
    s = 1
    while s < ng:
        m = grp >= s
        tu = tu + ta * jnp.where(m, pltpu.roll(tu, s, 0), 0.0)
        ta = ta * jnp.where(m, pltpu.roll(ta, s, 0), 1.0)
        s *= 2
    h_in = jnp.where(grp >= 1, pltpu.roll(tu, 1, 0), 0.0)
    for pos in range(SUBLANES):
        h_ref[pl.ds(pos, ng, stride=SUBLANES), :] = ca_ref[pos] * h_in + cu_ref[pos]
    o_ref[0] = (h_ref[...] * z_ref[0]).astype(o_ref.dtype)


def _rglru(u, conv_w, conv_b, w_r, b_r, w_i, b_i, lam, *, x_off, z_off):
    bsz, s_len, _ = u.shape
    nblk, blk, _ = w_r.shape
    vec = lambda rows: pl.BlockSpec((rows, blk), lambda b, n: (0, n))
    mat = pl.BlockSpec((1, blk, blk), lambda b, n: (n, 0, 0))
    return pl.pallas_call(
        _rglru_kernel,
        grid=(bsz, nblk),
        in_specs=[pl.BlockSpec((1, s_len, blk), lambda b, n: (b, 0, x_off + n)),
                  pl.BlockSpec((1, s_len, blk), lambda b, n: (b, 0, z_off + n)),
                  vec(CONV_W), vec(1), mat, vec(1), mat, vec(1), vec(1)],
        out_specs=pl.BlockSpec((1, s_len, blk), lambda b, n: (b, 0, n)),
        out_shape=jax.ShapeDtypeStruct((bsz, s_len, nblk * blk), BF16),
        scratch_shapes=[pltpu.VMEM((s_len, blk), F32)] * 3
        + [pltpu.VMEM((SUBLANES, s_len // SUBLANES, blk), F32)] * 2,
        compiler_params=pltpu.CompilerParams(
            dimension_semantics=("parallel", "parallel"), vmem_limit_bytes=VMEM_LIMIT),
        name="rglru",
    )(u, u, conv_w, conv_b.reshape(1, -1), w_r, b_r.reshape(1, -1), w_i, b_i.reshape(1, -1),
      lam.reshape(1, -1))


ML_CHUNK = 128
ML_UNROLL = 8


def _mlstm_gates_kernel(g_ref, gb_ref, mu_ref, gi_ref, en_ref, ks_ref, dec_ref, rt_ref):
    s_len = g_ref.shape[1]
    c = ML_CHUNK
    t = g_ref[0] + gb_ref[...]
    li = t[:, :LANES]
    lf = _log_sigmoid(t[:, LANES:])
    row = lax.broadcasted_iota(jnp.int32, (s_len, LANES), 0) % c

    def scan(x, op, fill):
        s = 1
        while s < c:
            x = op(x, jnp.where(row >= s, pltpu.roll(x, s, 0), fill))
            s *= 2
        return x

    b = scan(lf, jnp.add, 0.0)
    r = li - b
    rho = scan(r, jnp.maximum, -jnp.inf)
    m = jnp.zeros((1, LANES), F32)
    for ci in range(s_len // c):
        sl = slice(ci * c, (ci + 1) * c)
        b_c, r_c, rho_c = b[sl], r[sl], rho[sl]
        b_last = b_c[c - 1:c, :]
        m_new = b_last + jnp.maximum(m, rho_c[c - 1:c, :])
        mu = jnp.maximum(rho_c, m)
        mu_ref[0, sl, :] = mu
        gi_ref[0, sl, :] = jnp.exp(m - mu)
        en_ref[0, sl, :] = jnp.exp(-(b_c + mu))
        ks_ref[0, sl, :] = jnp.exp(b_last + r_c - m_new)
        dec_ref[0, ci:ci + 1, :] = jnp.exp(b_last + m - m_new)
        rt_ref[0, :, sl] = r_c.T
        m = m_new


def _mlstm_gates(gates, gate_bias):
    bsz, s_len, gw = gates.shape
    nc = s_len // ML_CHUNK
    col = jax.ShapeDtypeStruct((bsz, s_len, LANES), F32)
    col_spec = pl.BlockSpec((1, s_len, LANES), lambda b: (b, 0, 0))
    return pl.pallas_call(
        _mlstm_gates_kernel,
        grid=(bsz,),
        in_specs=[pl.BlockSpec((1, s_len, gw), lambda b: (b, 0, 0)),
                  pl.BlockSpec((1, gw), lambda b: (0, 0))],
        out_specs=[col_spec] * 4 + [pl.BlockSpec((1, nc, LANES), lambda b: (b, 0, 0)),
                                    pl.BlockSpec((1, LANES, s_len), lambda b: (b, 0, 0))],
        out_shape=[col] * 4 + [jax.ShapeDtypeStruct((bsz, nc, LANES), F32),
                               jax.ShapeDtypeStruct((bsz, LANES, s_len), F32)],
        compiler_params=pltpu.CompilerParams(
            dimension_semantics=("parallel",), vmem_limit_bytes=VMEM_LIMIT),
        name="mlstm_gates",
    )(gates, gate_bias)


def _mlstm_kernel(q_ref, k_ref, v_ref, og_ref, z_ref, mu_ref, gi_ref, en_ref, ks_ref, dec_ref,
                  rt_ref, cn_ref, o_ref, cst_ref, nst_ref):
    s_len = q_ref.shape[1]
    dk = q_ref.shape[2]
    c = ML_CHUNK
    assert c == dk == LANES
    h = pl.program_id(1)

    cst_ref[...] = jnp.zeros_like(cst_ref)
    nst_ref[...] = jnp.zeros_like(nst_ref)

    row = lax.broadcasted_iota(jnp.int32, (c, c), 0)
    col = lax.broadcasted_iota(jnp.int32, (c, c), 1)
    causal = col <= row
    head = col == h
    cn = cn_ref[...]

    def pick(x):
        return jnp.sum(jnp.where(head, x, 0.0), axis=1, keepdims=True)

    dec_all = dec_ref[0]
    chunk_id = lax.broadcasted_iota(jnp.int32, dec_all.shape, 0)
    head_lane = lax.broadcasted_iota(jnp.int32, (1, LANES), 1) == h
    head_row = lax.broadcasted_iota(jnp.int32, (SUBLANES, c), 0) == h

    def chunk(ci, carry):
        r0 = pl.multiple_of(ci * c, c)
        q = q_ref[0, pl.ds(r0, c), :]
        k = k_ref[0, pl.ds(r0, c), :]
        v = v_ref[0, pl.ds(r0, c), :]
        mu = pick(mu_ref[0, pl.ds(r0, c), :])
        g_inter = pick(gi_ref[0, pl.ds(r0, c), :])
        e_negm = pick(en_ref[0, pl.ds(r0, c), :])
        kscale = pick(ks_ref[0, pl.ds(r0, c), :])
        decay = jnp.sum(jnp.where(chunk_id == ci, dec_all, 0.0), axis=0, keepdims=True)
        decay = jnp.sum(jnp.where(head_lane, decay, 0.0), axis=1, keepdims=True)
        r_row = jnp.sum(jnp.where(head_row, rt_ref[0, :, pl.ds(r0, c)], 0.0),
                        axis=0, keepdims=True)
        wts = jnp.where(causal, jnp.exp(jnp.minimum(r_row - mu, 0.0)), 0.0)
        qk = _dot_nt(q, k)
        kw = k * kscale
        upd = _dot_tn(kw, v)

        cst = cst_ref[...]
        n_st = nst_ref[...]
        sc = qk * wts
        inter = _dot(q, cst)
        intra = _dot(sc, v)
        cst_ref[...] = decay * cst + upd
        nst_ref[...] = decay * n_st + jnp.sum(kw, axis=0, keepdims=True)

        num = g_inter * inter + intra
        den = (g_inter * jnp.sum(q * n_st, axis=1, keepdims=True)
               + jnp.sum(sc, axis=1, keepdims=True))
        inv = 1.0 / jnp.maximum(jnp.abs(den), e_negm)
        msq = jnp.mean(num * num, axis=-1, keepdims=True)
        yn = num * (inv * lax.rsqrt(inv * inv * msq + EPS)) * cn
        og = og_ref[0, pl.ds(r0, c), :]
        z = z_ref[0, pl.ds(r0, c), :]
        o_ref[0, pl.ds(r0, c), :] = (yn * og * z).astype(o_ref.dtype)
        return carry

    lax.fori_loop(0, s_len // c, chunk, 0, unroll=ML_UNROLL)


def _mlstm(u, gate_terms, c_norm, *, heads, dk, dv):
    bsz, s_len, _ = u.shape
    nc = s_len // ML_CHUNK
    col_spec = pl.BlockSpec((1, s_len, LANES), lambda b, h: (b, 0, 0))
    qk_blk = lambda off: pl.BlockSpec((1, s_len, dk), lambda b, h, off=off: (b, 0, off + h))
    v_blk = lambda off: pl.BlockSpec((1, s_len, dv), lambda b, h, off=off: (b, 0, off + h))
    nqk = 2 * heads * dk // dv
    return pl.pallas_call(
        _mlstm_kernel,
        grid=(bsz, heads),
        in_specs=[qk_blk(0), qk_blk(heads),
                  v_blk(nqk), v_blk(nqk + heads), v_blk(nqk + 2 * heads),
                  col_spec, col_spec, col_spec, col_spec,
                  pl.BlockSpec((1, nc, LANES), lambda b, h: (b, 0, 0)),
                  pl.BlockSpec((1, SUBLANES, s_len), lambda b, h: (b, 0, 0)),
                  pl.BlockSpec((1, dv), lambda b, h: (0, h))],
        out_specs=pl.BlockSpec((1, s_len, dv), lambda b, h: (b, 0, h)),
        out_shape=jax.ShapeDtypeStruct((bsz, s_len, heads * dv), BF16),
        scratch_shapes=[pltpu.VMEM((dk, dv), F32), pltpu.VMEM((1, dk), F32)],
        compiler_params=pltpu.CompilerParams(
            dimension_semantics=("parallel", "parallel"), vmem_limit_bytes=VMEM_LIMIT),
        name="mlstm",
    )(u, u, u, u, u, *gate_terms, c_norm.reshape(1, -1))


OUT_SUB_ROWS = 256


def _out_kernel(h_ref, *rest, n_y, final):
    y_refs = rest[:n_y]
    wo_refs = rest[n_y:2 * n_y]
    p_ref, pw_ref, pn_ref, gw_ref, norm_ref = rest[2 * n_y:2 * n_y + 5]
    out_refs = rest[2 * n_y + 5:]

    def rows_block(r0):
        rows = slice(r0, r0 + OUT_SUB_ROWS)
        mix = jnp.dot(y_refs[0][rows, :], wo_refs[0][...], preferred_element_type=F32)
        for y_ref, wo_ref in zip(y_refs[1:], wo_refs[1:]):
            mix = mix + jnp.dot(y_ref[rows, :], wo_ref[...], preferred_element_type=F32)
        yield
        h1 = h_ref[rows, :] + mix
        pe = _dot(p_ref[rows, :], pw_ref[...])
        gate = _dot(h1, gw_ref[...])
        yield
        pe = pe * lax.rsqrt(jnp.mean(pe * pe, axis=-1, keepdims=True) + EPS) * pn_ref[...]
        h2 = h1 + _sigmoid(gate) * pe
        hn = h2 * lax.rsqrt(jnp.mean(h2 * h2, axis=-1, keepdims=True) + EPS) * norm_ref[...]
        if final:
            out_refs[0][rows, :] = hn
        else:
            out_refs[0][rows, :] = h2
            out_refs[1][rows, :] = hn.astype(out_refs[1].dtype)

    stages = [rows_block(r0) for r0 in range(0, h_ref.shape[0], OUT_SUB_ROWS)]
    while stages:
        stages = [g for g in stages if next(g, stages) is not stages]


def _out_proj(h, ys, w_out, p, layer, ple_w, ple_norm, gate_w, norm, final, *, tm=512):
    m, d = h.shape
    const = lambda shape: pl.BlockSpec(shape, lambda i: (0, 0), pipeline_mode=pl.Buffered(1))
    in_specs = [pl.BlockSpec((tm, d), lambda i: (i, 0))]
    in_specs += [pl.BlockSpec((tm, y.shape[1]), lambda i: (i, 0)) for y in ys]
    assert all(y.shape[1] == ys[0].shape[1] for y in ys)
    in_specs += [pl.BlockSpec((y.shape[1], d), lambda i, r=r: (r, 0), pipeline_mode=pl.Buffered(1))
                 for r, y in enumerate(ys)]
    stacked = lambda a: pl.BlockSpec((None,) + a.shape[1:], lambda i: (layer, 0, 0),
                                     pipeline_mode=pl.Buffered(1))
    in_specs += [pl.BlockSpec((None, tm, p.shape[2]), lambda i: (layer, i, 0)),
                 stacked(ple_w), const((1, d)), stacked(gate_w), const((1, d))]
    args = [h, *ys, *([w_out] * len(ys)), p, ple_w, ple_norm.reshape(1, d), gate_w,
            norm.reshape(1, d)]
    row_spec = pl.BlockSpec((tm, d), lambda i: (i, 0))
    return pl.pallas_call(
        functools.partial(_out_kernel, n_y=len(ys), final=final),
        grid=(m // tm,),
        in_specs=in_specs,
        out_specs=row_spec if final else [row_spec, row_spec],
        out_shape=(jax.ShapeDtypeStruct((m, d), F32) if final else
                   [jax.ShapeDtypeStruct((m, d), F32), jax.ShapeDtypeStruct((m, d), BF16)]),
        compiler_params=pltpu.CompilerParams(
            dimension_semantics=("parallel",), vmem_limit_bytes=VMEM_LIMIT),
        name="out_proj_final" if final else "out_proj",
    )(*args)


def kernel(x, p, e_norm, e_w_in, a_lb_logits, a_norm, b_conv_w, b_conv_b, b_w_r, b_b_r, b_w_i, b_b_i, b_lambda, e_w_out, o_norm, o_w_in, c_conv_w, c_conv_b, c_b_i, c_b_f, c_norm, o_w_out, ple_w, ple_norm, ple_gate_w, final_norm):
    bsz, s_len, d = x.shape
    depth = p.shape[0]
    m = bsz * s_len
    a_width = a_norm.shape[1]
    b_width = b_lambda.shape[1]
    a_dk = LANES
    a_heads = a_width // a_dk
    c_heads = c_b_i.shape[1]
    c_dv = c_norm.shape[1] // c_heads
    c_dk = c_conv_w.shape[2] // (2 * c_heads)
    c_main = o_w_in.shape[2] - 2 * c_heads

    h = x.reshape(m, d)
    ple_w_bf, gate_w_bf = ple_w.astype(BF16), ple_gate_w.astype(BF16)
    hn = None
    layer_norm = lambda i: e_norm[i // 2] if i % 2 == 0 else o_norm[i // 2]
    for i in range(depth):
        j = i // 2
        last = i == depth - 1
        xin, g = (h, layer_norm(i)) if hn is None else (hn, None)
        if i % 2 == 0:
            a_qk = a_heads * a_dk
            spans = (("silu", a_qk), ("hgate", a_qk), ("id", a_width), ("silu", a_width),
                     ("id", b_width), ("silu", b_width))
            u = _in_proj(xin, g, e_w_in[j].astype(BF16), spans=spans, seq_rows=s_len,
                         lb_logits=a_lb_logits, layer=j).reshape(bsz, s_len, -1)
            ya = _hgrn2(u, a_norm[j], heads=a_heads, dk=a_dk)
            x_off = (2 * a_heads * a_dk + 2 * a_width) // LANES
            yb = _rglru(u, b_conv_w[j], b_conv_b[j], b_w_r[j].astype(BF16), b_b_r[j],
                        b_w_i[j].astype(BF16), b_b_i[j], b_lambda[j],
                        x_off=x_off, z_off=x_off + b_width // LANES)
            w_out = e_w_out[j].astype(BF16)
            ys = [ya.reshape(m, a_width), yb.reshape(m, b_width)]
        else:
            w_in_t = o_w_in[j].T
            pad = (0, LANES - c_heads)
            wg_t = jnp.concatenate([jnp.pad(w_in_t[c_main:c_main + c_heads], (pad, (0, 0))),
                                    jnp.pad(w_in_t[c_main + c_heads:], (pad, (0, 0)))]).astype(BF16)
            c_qk = c_heads * c_dk
            c_width = c_heads * c_dv
            spans = (("conv_q", c_qk), ("conv_k", c_qk), ("id", c_width), ("sigmoid", c_width),
                     ("silu", c_width))
            u, gates = _in_proj(xin, g, w_in_t.astype(BF16), wg_t, spans=spans, seq_rows=s_len,
                                transposed=True, conv_w=c_conv_w[j], conv_b=c_conv_b[j],
                                q_scale=c_dk ** -0.5)
            gate_bias = jnp.concatenate([jnp.pad(c_b_i[j], pad),
                                         jnp.pad(c_b_f[j], pad)]).reshape(1, 2 * LANES)
            gate_terms = _mlstm_gates(gates.reshape(bsz, s_len, 2 * LANES), gate_bias)
            yc = _mlstm(u.reshape(bsz, s_len, -1), gate_terms, c_norm[j],
                        heads=c_heads, dk=c_dk, dv=c_dv)
            ys = [yc.reshape(m, -1)]
            w_out = o_w_out[j].astype(BF16)
        res = _out_proj(h, ys, w_out, p.reshape(depth, m, -1), i, ple_w_bf, ple_norm[i],
                        gate_w_bf, final_norm if last else layer_norm(i + 1), last)
        h, hn = (res, None) if last else res
    return h.reshape(bsz, s_len, d)
```

```python
import functools

import jax
import jax.numpy as jnp
from jax import lax
from jax.experimental import pallas as pl
from jax.experimental.pallas import tpu as pltpu

EPS = 1e-6
LRU_C = 8.0
CONV_W = 4
LANES = 128
SUBLANES = 8
VMEM_LIMIT = 56 * 1024 * 1024
BF16 = jnp.bfloat16
F32 = jnp.float32


def _sigmoid(x):
    return jax.nn.sigmoid(x)


def _silu(x):
    return x * jax.nn.sigmoid(x)


def _softplus(x):
    return jnp.maximum(x, 0.0) + jnp.log1p(jnp.exp(-jnp.abs(x)))


def _log_sigmoid(x):
    return -_softplus(-x)


def _shift_rows(x, s, row):
    return jnp.where(row >= s, pltpu.roll(x, s, 0), 0.0)


def _cumsum_rows(x, row):
    s = 1
    while s < x.shape[0]:
        x = x + _shift_rows(x, s, row)
        s *= 2
    return x


def _cummax_rows(x, row):
    s = 1
    while s < x.shape[0]:
        x = jnp.maximum(x, jnp.where(row >= s, pltpu.roll(x, s, 0), -jnp.inf))
        s *= 2
    return x


def _causal_conv(x_ref, cw_ref, cb_ref):
    s_len = x_ref.shape[1]
    taps = [cw_ref[k:k + 1, :] for k in range(CONV_W)]
    acc = cb_ref[...] + x_ref[0, SUBLANES:, :] * taps[CONV_W - 1]
    for j in range(1, CONV_W):
        acc = acc + x_ref[0, SUBLANES - j:s_len - j, :] * taps[CONV_W - 1 - j]
    x0 = x_ref[0, 0:SUBLANES, :]
    row = lax.broadcasted_iota(jnp.int32, x0.shape, 0)
    acc0 = cb_ref[...] + x0 * taps[CONV_W - 1]
    for j in range(1, CONV_W):
        acc0 = acc0 + _shift_rows(x0, j, row) * taps[CONV_W - 1 - j]
    return jnp.concatenate([acc0, acc], axis=0)


def _dot(a, b):
    return jnp.dot(a.astype(BF16), b.astype(BF16), preferred_element_type=F32)


def _dot_nt(a, b):
    return lax.dot_general(a.astype(BF16), b.astype(BF16), (((1,), (1,)), ((), ())),
                           preferred_element_type=F32)


def _dot_tn(a, b):
    return lax.dot_general(a.astype(BF16), b.astype(BF16), (((0,), (0,)), ((), ())),
                           preferred_element_type=F32)


SUB_ROWS = 256
COL_PIECE = 512


def _in_proj_kernel(*refs, with_norm, with_gates, transposed, spans, layer):
    kinds = {k for k, _ in spans}
    refs = list(refs)
    x_ref = refs.pop(0)
    g_ref = refs.pop(0) if with_norm else None
    w_ref = refs.pop(0)
    wg_ref = refs.pop(0) if with_gates else None
    lbl_ref = refs.pop(0) if "hgate" in kinds else None
    o_ref = refs.pop(0)
    og_ref = refs.pop(0) if with_gates else None
    hn_ref = refs.pop(0) if with_norm else x_ref
    mm = _dot_nt if transposed else functools.partial(jnp.dot, preferred_element_type=F32)
    tm, tn = o_ref.shape
    j = pl.program_id(1)

    starts, c0 = [], 0
    for kind, width in spans:
        starts.append((kind, c0, width))
        c0 += width
    n = c0

    if lbl_ref is not None:
        lg = lbl_ref[...]
        e = jnp.exp(lg - jnp.max(lg, axis=0, keepdims=True))
        sm = e / jnp.sum(e, axis=0, keepdims=True)
        lb = jnp.sum(sm[:layer + 1], axis=0, keepdims=True)

    def activate(kind, r, c0):
        if kind == "silu":
            return _silu(r)
        if kind == "sigmoid":
            return _sigmoid(r)
        if kind == "hgate":
            lbp = lb[:, c0:c0 + r.shape[1]]
            return lbp + (1.0 - lbp) * _sigmoid(r)
        return r

    def run_tile(jt, blocked):
        tile_c0 = jt * tn
        pieces = []
        for kind, c0, width in starts:
            lo, hi = max(c0, tile_c0), min(c0 + width, tile_c0 + tn)
            for p0 in range(lo, hi, COL_PIECE if blocked else hi - lo):
                pieces.append((kind, p0 - tile_c0, min(COL_PIECE, hi - p0) if blocked else hi - lo,
                               p0 - c0))
        for r0 in range(0, tm, SUB_ROWS if blocked else tm):
            rows = slice(r0, r0 + (SUB_ROWS if blocked else tm))
            if with_norm and jt == 0:
                x = x_ref[rows, :]
                ms = jnp.mean(x * x, axis=-1, keepdims=True)
                hn = (x * lax.rsqrt(ms + EPS) * g_ref[...]).astype(BF16)
                hn_ref[rows, :] = hn
            else:
                hn = hn_ref[rows, :]
            for kind, off, width, c0 in pieces:
                wp = w_ref[off:off + width, :] if transposed else w_ref[:, off:off + width]
                o_ref[rows, off:off + width] = activate(kind, mm(hn, wp), c0)
        if with_gates and jt == 0:
            og_ref[...] = mm(hn_ref[...], wg_ref[...])

    if kinds == {"id"}:
        pl.when(j == 0)(functools.partial(run_tile, 0, with_norm))
        pl.when(j != 0)(functools.partial(run_tile, 1, False))
    else:
        for jt in range(n // tn):
            pl.when(j == jt)(functools.partial(run_tile, jt, True))


IN_PROJ_ROWS = 1024
IN_PROJ_VMEM = 46 * 1024 * 1024


def _in_proj_cols(n, tm, d, gw, x_bytes, with_norm):
    for tn in range(n, 0, -512):
        need = (2 * tm * d * x_bytes + (tm * d * 2 if with_norm else 0) + 2 * d * tn * 2
                + 2 * tm * tn * 4 + 2 * (d * gw * 2 + tm * gw * 4))
        if n % tn == 0 and need <= IN_PROJ_VMEM:
            return tn
    raise ValueError("no column tile fits")


def _in_proj(x, g, w, wg=None, *, spans, transposed=False, lb_logits=None, layer=0,
             tm=IN_PROJ_ROWS):
    m, d = x.shape
    n = sum(width for _, width in spans)
    with_gates = wg is not None
    with_norm = g is not None
    gw = wg.shape[0 if transposed else 1] if with_gates else 0
    w_spec = (lambda cols, idx: pl.BlockSpec((cols, d), lambda i, j: (idx(j), 0)) if transposed
              else pl.BlockSpec((d, cols), lambda i, j: (0, idx(j))))
    const = lambda a: pl.BlockSpec(a.shape, lambda i, j: (0,) * a.ndim)
    tn = _in_proj_cols(n, tm, d, gw, x.dtype.itemsize, with_norm)
    in_specs = [pl.BlockSpec((tm, d), lambda i, j: (i, 0))]
    args = [x]
    if with_norm:
        args.append(g.reshape(1, d))
        in_specs.append(const(args[-1]))
    in_specs.append(w_spec(tn, lambda j: j))
    args.append(w)
    out_shape = [jax.ShapeDtypeStruct((m, n), F32)]
    out_specs = [pl.BlockSpec((tm, tn), lambda i, j: (i, j))]
    if with_gates:
        in_specs.append(w_spec(gw, lambda j: 0))
        out_shape.append(jax.ShapeDtypeStruct((m, gw), F32))
        out_specs.append(pl.BlockSpec((tm, gw), lambda i, j: (i, 0)))
        args.append(wg)
    if lb_logits is not None:
        args.append(lb_logits)
        in_specs.append(const(lb_logits))
    res = pl.pallas_call(
        functools.partial(_in_proj_kernel, with_norm=with_norm, with_gates=with_gates,
                          transposed=transposed, spans=tuple(spans), layer=layer),
        grid=(m // tm, n // tn),
        in_specs=in_specs,
        out_specs=out_specs,
        out_shape=out_shape,
        scratch_shapes=[pltpu.VMEM((tm, d), BF16)] if with_norm else [],
        compiler_params=pltpu.CompilerParams(
            dimension_semantics=("parallel", "arbitrary"), vmem_limit_bytes=VMEM_LIMIT),
        name="in_proj_gates" if with_gates else "in_proj",
    )(*args)
    return res if with_gates else res[0]


HG_CHUNK = 128
HG_GROUPS = HG_CHUNK // SUBLANES
HG_INFLIGHT = 8


def _hgrn2_kernel(q_ref, f_ref, i_ref, z_ref, an_ref, o_ref,
                  st_ref, qs_ref, ks_ref, bs_ref, od_ref, tri_ref):
    s_len = q_ref.shape[1]
    dk = q_ref.shape[2]
    c = HG_CHUNK
    ng = HG_GROUPS
    assert c == dk == LANES and i_ref.shape[2] == dk

    an = an_ref[...]

    rowc = lax.broadcasted_iota(jnp.int32, (c, c), 0)
    colc = lax.broadcasted_iota(jnp.int32, (c, c), 1)
    tri_ref[...] = (colc <= rowc).astype(BF16)
    same_group = (colc % ng) == (rowc % ng)
    lane = lax.broadcasted_iota(jnp.int32, (SUBLANES, c), 1)
    diag_off = lane - lax.broadcasted_iota(jnp.int32, (SUBLANES, c), 0)
    zeros8 = jnp.zeros((SUBLANES, dk), F32)
    zeros16 = jnp.zeros((ng, dk), F32)

    st_ref[...] = jnp.zeros_like(st_ref)

    def rows8(x, r):
        return x[r * SUBLANES:(r + 1) * SUBLANES, :]

    def chunk(ci, slot):
        qs, ks, bs, od = qs_ref.at[slot], ks_ref.at[slot], bs_ref.at[slot], od_ref.at[slot]
        r0 = pl.multiple_of(ci * c, c)
        q = q_ref[0, pl.ds(r0, c), :]
        f = f_ref[0, pl.ds(r0, c), :]
        v = i_ref[0, pl.ds(r0, c), :]
        k = 1.0 - f

        lf = jnp.log2(f)
        hi = lf.astype(BF16)
        r1 = lf - hi.astype(F32)
        mid = r1.astype(BF16)
        lo = (r1 - mid.astype(F32)).astype(BF16)
        bb = jnp.dot(tri_ref[...], jnp.concatenate([hi, mid, lo], axis=1),
                     preferred_element_type=F32)
        yield
        b2 = bb[:, :dk] + bb[:, dk:2 * dk] + bb[:, 2 * dk:]
        qs[...] = q
        ks[...] = k
        bs[...] = b2
        b_last = bs[c - 1:c, :]
        upd = _dot_tn(v, k * jnp.exp2(b_last - b2))

        ps, rights = [], []
        for j in range(ng.bit_length() - 1):
            lhs, rhs, rights_j = [], [], []
            for r in range(ng):
                mid_grp = ((r >> (j + 1)) << (j + 1)) | ((1 << j) - 1)
                beta = bs[mid_grp * SUBLANES + SUBLANES - 1:(mid_grp + 1) * SUBLANES, :]
                if (r >> j) & 1:
                    lhs.append(rows8(q, r) * jnp.exp2(rows8(b2, r) - beta))
                    rhs.append(zeros8)
                    rights_j.append(r)
                else:
                    rhs.append(rows8(k, r) * jnp.exp2(beta - rows8(b2, r)))
            ps.append(_dot_nt(jnp.concatenate(lhs, axis=0), jnp.concatenate(rhs, axis=0)))
            rights.append(rights_j)
        yield

        st = st_ref[...]
        o = _dot_nt(q * jnp.exp2(b2), st)
        st_ref[...] = st * jnp.exp2(b_last) + upd

        def permuted(ref):
            return [ref[pl.ds(pos, ng, stride=SUBLANES), :] for pos in range(SUBLANES)]

        q_p, k_p, b_p = permuted(qs), permuted(ks), permuted(bs)
        lhs_slots, rhs_slots = [], []
        for j in range(SUBLANES.bit_length() - 1):
            for blk in range(SUBLANES >> (j + 1)):
                mid_pos = (blk << (j + 1)) | ((1 << j) - 1)
                lhs, rhs = [], []
                for pos in range(SUBLANES):
                    if pos >> (j + 1) != blk:
                        lhs.append(zeros16)
                        rhs.append(zeros16)
                    elif (pos >> j) & 1:
                        lhs.append(q_p[pos] * jnp.exp2(b_p[pos] - b_p[mid_pos]))
                        rhs.append(zeros16)
                    else:
                        lhs.append(zeros16)
                        rhs.append(k_p[pos] * jnp.exp2(b_p[mid_pos] - b_p[pos]))
                lhs_slots.append(jnp.concatenate(lhs, axis=0).astype(BF16))
                rhs_slots.append(jnp.concatenate(rhs, axis=0).astype(BF16))
        pd = _dot_nt(jnp.concatenate(lhs_slots, axis=1), jnp.concatenate(rhs_slots, axis=1))
        yield

        dg = jnp.sum(q * k, axis=1, keepdims=True)
        a_rows = [jnp.where(diag_off == r * SUBLANES, rows8(dg, r), 0.0) for r in range(ng)]
        for j, (p, rights_j) in enumerate(zip(ps, rights)):
            for i, r in enumerate(rights_j):
                right_start = (((r >> (j + 1)) << (j + 1)) | (1 << j)) * SUBLANES
                a_rows[r] = jnp.where(lane < right_start, rows8(p, i), a_rows[r])
        o = o + _dot(jnp.concatenate(a_rows, axis=0), v)
        v_p = [i_ref[0, pl.ds(r0 + pos, ng, stride=SUBLANES), :] for pos in range(SUBLANES)]
        od[...] = _dot(jnp.where(same_group, pd, 0.0), jnp.concatenate(v_p, axis=0))
        yield

        o = o + jnp.concatenate(
            [od[pl.ds(g, SUBLANES, stride=ng), :] for g in range(ng)], axis=0)
        yn = o * lax.rsqrt(jnp.mean(o * o, axis=-1, keepdims=True) + EPS) * an
        z = z_ref[0, pl.ds(r0, c), :]
        o_ref[0, pl.ds(r0, c), :] = (yn * z).astype(o_ref.dtype)

    def chunks(i, carry):
        stages = [chunk(i * HG_INFLIGHT + slot, slot) for slot in range(HG_INFLIGHT)]
        while stages:
            stages = [g for g in stages if next(g, stages) is not stages]
        return carry

    lax.fori_loop(0, s_len // (c * HG_INFLIGHT), chunks, 0)


def _hgrn2(u, a_norm, *, heads, dk):
    bsz, s_len, _ = u.shape
    blk = lambda off: pl.BlockSpec((1, s_len, dk), lambda b, h, off=off: (b, 0, off + h))
    return pl.pallas_call(
        _hgrn2_kernel,
        grid=(bsz, heads),
        in_specs=[blk(0), blk(heads), blk(2 * heads), blk(3 * heads),
                  pl.BlockSpec((1, dk), lambda b, h: (0, h))],
        out_specs=pl.BlockSpec((1, s_len, dk), lambda b, h: (b, 0, h)),
        out_shape=jax.ShapeDtypeStruct((bsz, s_len, heads * dk), BF16),
        scratch_shapes=[pltpu.VMEM((dk, dk), F32)]
        + [pltpu.VMEM((HG_INFLIGHT, HG_CHUNK, dk), F32)] * 4
        + [pltpu.VMEM((HG_CHUNK, HG_CHUNK), BF16)],
        compiler_params=pltpu.CompilerParams(
            dimension_semantics=("parallel", "parallel"), vmem_limit_bytes=VMEM_LIMIT),
        name="hgrn2",
    )(u, u, u, u, a_norm.reshape(1, -1))


def _rglru_kernel(x_ref, z_ref, cw_ref, cb_ref, wr_ref, br_ref, wi_ref, bi_ref, lam_ref,
                  o_ref, a_ref, u_ref, h_ref, ca_ref, cu_ref):
    s_len = x_ref.shape[1]
    w = x_ref.shape[2]
    ng = s_len // SUBLANES
    xc = _causal_conv(x_ref, cw_ref, cb_ref)
    r = _sigmoid(_dot(xc, wr_ref[0]) + br_ref[...])
    ig = _sigmoid(_dot(xc, wi_ref[0]) + bi_ref[...])
    log_a = -LRU_C * r * _softplus(-lam_ref[...])
    a = jnp.exp(log_a)
    a_ref[...] = a
    t = -jnp.tanh(log_a) * (a * a + 1.0)
    u_ref[...] = jnp.where(t > 0.0, t * lax.rsqrt(t), 0.0) * (ig * xc)

    ca = cu = None
    for pos in range(SUBLANES):
        ap = a_ref[pl.ds(pos, ng, stride=SUBLANES), :]
        up = u_ref[pl.ds(pos, ng, stride=SUBLANES), :]
        ca, cu = (ap, up) if pos == 0 else (ap * ca, ap * cu + up)
        ca_ref[pos] = ca
        cu_ref[pos] = cu
    grp = lax.broadcasted_iota(jnp.int32, (ng, w), 0)
    ta, tu = ca, cu
    s = 1
    while s < ng:
        m = grp >= s
        tu = tu + ta * jnp.where(m, pltpu.roll(tu, s, 0), 0.0)
        ta = ta * jnp.where(m, pltpu.roll(ta, s, 0), 1.0)
        s *= 2
    h_in = jnp.where(grp >= 1, pltpu.roll(tu, 1, 0), 0.0)
    for pos in range(SUBLANES):
        h_ref[pl.ds(pos, ng, stride=SUBLANES), :] = ca_ref[pos] * h_in + cu_ref[pos]
    o_ref[0] = (h_ref[...] * z_ref[0]).astype(o_ref.dtype)


def _rglru(u, conv_w, conv_b, w_r, b_r, w_i, b_i, lam, *, x_off, z_off):
    bsz, s_len, _ = u.shape
    nblk, blk, _ = w_r.shape
    vec = lambda rows: pl.BlockSpec((rows, blk), lambda b, n: (0, n))
    mat = pl.BlockSpec((1, blk, blk), lambda b, n: (n, 0, 0))
    return pl.pallas_call(
        _rglru_kernel,
        grid=(bsz, nblk),
        in_specs=[pl.BlockSpec((1, s_len, blk), lambda b, n: (b, 0, x_off + n)),
                  pl.BlockSpec((1, s_len, blk), lambda b, n: (b, 0, z_off + n)),
                  vec(CONV_W), vec(1), mat, vec(1), mat, vec(1), vec(1)],
        out_specs=pl.BlockSpec((1, s_len, blk), lambda b, n: (b, 0, n)),
        out_shape=jax.ShapeDtypeStruct((bsz, s_len, nblk * blk), BF16),
        scratch_shapes=[pltpu.VMEM((s_len, blk), F32)] * 3
        + [pltpu.VMEM((SUBLANES, s_len // SUBLANES, blk), F32)] * 2,
        compiler_params=pltpu.CompilerParams(
            dimension_semantics=("parallel", "parallel"), vmem_limit_bytes=VMEM_LIMIT),
        name="rglru",
    )(u, u, conv_w, conv_b.reshape(1, -1), w_r, b_r.reshape(1, -1), w_i, b_i.reshape(1, -1),
      lam.reshape(1, -1))


ML_CHUNK = 128
ML_UNROLL = 8


def _mlstm_gates_kernel(g_ref, gb_ref, mu_ref, gi_ref, en_ref, ks_ref, dec_ref, rt_ref):
    s_len = g_ref.shape[1]
    c = ML_CHUNK
    t = g_ref[0] + gb_ref[...]
    li = t[:, :LANES]
    lf = _log_sigmoid(t[:, LANES:])
    row = lax.broadcasted_iota(jnp.int32, (s_len, LANES), 0) % c

    def scan(x, op, fill):
        s = 1
        while s < c:
            x = op(x, jnp.where(row >= s, pltpu.roll(x, s, 0), fill))
            s *= 2
        return x

    b = scan(lf, jnp.add, 0.0)
    r = li - b
    rho = scan(r, jnp.maximum, -jnp.inf)
    m = jnp.zeros((1, LANES), F32)
    for ci in range(s_len // c):
        sl = slice(ci * c, (ci + 1) * c)
        b_c, r_c, rho_c = b[sl], r[sl], rho[sl]
        b_last = b_c[c - 1:c, :]
        m_new = b_last + jnp.maximum(m, rho_c[c - 1:c, :])
        mu = jnp.maximum(rho_c, m)
        mu_ref[0, sl, :] = mu
        gi_ref[0, sl, :] = jnp.exp(m - mu)
        en_ref[0, sl, :] = jnp.exp(-(b_c + mu))
        ks_ref[0, sl, :] = jnp.exp(b_last + r_c - m_new)
        dec_ref[0, ci:ci + 1, :] = jnp.exp(b_last + m - m_new)
        rt_ref[0, :, sl] = r_c.T
        m = m_new


def _mlstm_gates(gates, gate_bias):
    bsz, s_len, gw = gates.shape
    nc = s_len // ML_CHUNK
    col = jax.ShapeDtypeStruct((bsz, s_len, LANES), F32)
    col_spec = pl.BlockSpec((1, s_len, LANES), lambda b: (b, 0, 0))
    return pl.pallas_call(
        _mlstm_gates_kernel,
        grid=(bsz,),
        in_specs=[pl.BlockSpec((1, s_len, gw), lambda b: (b, 0, 0)),
                  pl.BlockSpec((1, gw), lambda b: (0, 0))],
        out_specs=[col_spec] * 4 + [pl.BlockSpec((1, nc, LANES), lambda b: (b, 0, 0)),
                                    pl.BlockSpec((1, LANES, s_len), lambda b: (b, 0, 0))],
        out_shape=[col] * 4 + [jax.ShapeDtypeStruct((bsz, nc, LANES), F32),
                               jax.ShapeDtypeStruct((bsz, LANES, s_len), F32)],
        compiler_params=pltpu.CompilerParams(
            dimension_semantics=("parallel",), vmem_limit_bytes=VMEM_LIMIT),
        name="mlstm_gates",
    )(gates, gate_bias)


def _mlstm_kernel(q_ref, k_ref, v_ref, og_ref, z_ref, mu_ref, gi_ref, en_ref, ks_ref, dec_ref,
                  rt_ref, cwq_ref, cbq_ref, cwk_ref, cbk_ref, cn_ref, o_ref, qs_ref, ks_s_ref,
                  cst_ref, nst_ref):
    s_len = q_ref.shape[1]
    dk = q_ref.shape[2]
    c = ML_CHUNK
    assert c == dk == LANES
    h = pl.program_id(1)

    qs_ref[...] = _silu(_causal_conv(q_ref, cwq_ref, cbq_ref)) * (dk ** -0.5)
    ks_s_ref[...] = _silu(_causal_conv(k_ref, cwk_ref, cbk_ref))
    cst_ref[...] = jnp.zeros_like(cst_ref)
    nst_ref[...] = jnp.zeros_like(nst_ref)

    row = lax.broadcasted_iota(jnp.int32, (c, c), 0)
    col = lax.broadcasted_iota(jnp.int32, (c, c), 1)
    causal = col <= row
    head = col == h
    cn = cn_ref[...]

    def pick(x):
        return jnp.sum(jnp.where(head, x, 0.0), axis=1, keepdims=True)

    dec_all = dec_ref[0]
    chunk_id = lax.broadcasted_iota(jnp.int32, dec_all.shape, 0)
    head_lane = lax.broadcasted_iota(jnp.int32, (1, LANES), 1) == h
    head_row = lax.broadcasted_iota(jnp.int32, (SUBLANES, c), 0) == h

    def chunk(ci, carry):
        r0 = pl.multiple_of(ci * c, c)
        q = qs_ref[pl.ds(r0, c), :]
        k = ks_s_ref[pl.ds(r0, c), :]
        v = v_ref[0, pl.ds(r0, c), :]
        mu = pick(mu_ref[0, pl.ds(r0, c), :])
        g_inter = pick(gi_ref[0, pl.ds(r0, c), :])
        e_negm = pick(en_ref[0, pl.ds(r0, c), :])
        kscale = pick(ks_ref[0, pl.ds(r0, c), :])
        decay = jnp.sum(jnp.where(chunk_id == ci, dec_all, 0.0), axis=0, keepdims=True)
        decay = jnp.sum(jnp.where(head_lane, decay, 0.0), axis=1, keepdims=True)
        r_row = jnp.sum(jnp.where(head_row, rt_ref[0, :, pl.ds(r0, c)], 0.0),
                        axis=0, keepdims=True)
        wts = jnp.where(causal, jnp.exp(jnp.minimum(r_row - mu, 0.0)), 0.0)
        qk = _dot_nt(q, k)
        kw = k * kscale
        upd = _dot_tn(kw, v)

        cst = cst_ref[...]
        n_st = nst_ref[...]
        sc = qk * wts
        inter = _dot(q, cst)
        intra = _dot(sc, v)
        cst_ref[...] = decay * cst + upd
        nst_ref[...] = decay * n_st + jnp.sum(kw, axis=0, keepdims=True)

        num = g_inter * inter + intra
        den = (g_inter * jnp.sum(q * n_st, axis=1, keepdims=True)
               + jnp.sum(sc, axis=1, keepdims=True))
        inv = 1.0 / jnp.maximum(jnp.abs(den), e_negm)
        msq = jnp.mean(num * num, axis=-1, keepdims=True)
        yn = num * (inv * lax.rsqrt(inv * inv * msq + EPS)) * cn
        og = og_ref[0, pl.ds(r0, c), :]
        z = z_ref[0, pl.ds(r0, c), :]
        o_ref[0, pl.ds(r0, c), :] = (yn * _sigmoid(og) * _silu(z)).astype(o_ref.dtype)
        return carry

    lax.fori_loop(0, s_len // c, chunk, 0, unroll=ML_UNROLL)


def _mlstm(u, gate_terms, conv_w, conv_b, c_norm, *, heads, dk, dv):
    bsz, s_len, _ = u.shape
    nc = s_len // ML_CHUNK
    col_spec = pl.BlockSpec((1, s_len, LANES), lambda b, h: (b, 0, 0))
    qk_blk = lambda off: pl.BlockSpec((1, s_len, dk), lambda b, h, off=off: (b, 0, off + h))
    v_blk = lambda off: pl.BlockSpec((1, s_len, dv), lambda b, h, off=off: (b, 0, off + h))
    cw = lambda off: pl.BlockSpec((CONV_W, dk), lambda b, h, off=off: (0, off + h))
    cb = lambda off: pl.BlockSpec((1, dk), lambda b, h, off=off: (0, off + h))
    nqk = 2 * heads * dk // dv
    return pl.pallas_call(
        _mlstm_kernel,
        grid=(bsz, heads),
        in_specs=[qk_blk(0), qk_blk(heads),
                  v_blk(nqk), v_blk(nqk + heads), v_blk(nqk + 2 * heads),
                  col_spec, col_spec, col_spec, col_spec,
                  pl.BlockSpec((1, nc, LANES), lambda b, h: (b, 0, 0)),
                  pl.BlockSpec((1, SUBLANES, s_len), lambda b, h: (b, 0, 0)),
                  cw(0), cb(0), cw(heads), cb(heads),
                  pl.BlockSpec((1, dv), lambda b, h: (0, h))],
        out_specs=pl.BlockSpec((1, s_len, dv), lambda b, h: (b, 0, h)),
        out_shape=jax.ShapeDtypeStruct((bsz, s_len, heads * dv), BF16),
        scratch_shapes=[pltpu.VMEM((s_len, dk), F32), pltpu.VMEM((s_len, dk), F32),
                        pltpu.VMEM((dk, dv), F32), pltpu.VMEM((1, dk), F32)],
        compiler_params=pltpu.CompilerParams(
            dimension_semantics=("parallel", "parallel"), vmem_limit_bytes=VMEM_LIMIT),
        name="mlstm",
    )(u, u, u, u, u, *gate_terms, conv_w, conv_b.reshape(1, -1), conv_w,
      conv_b.reshape(1, -1), c_norm.reshape(1, -1))


OUT_SUB_ROWS = 256


def _out_kernel(h_ref, *rest, n_y, final):
    y_refs = rest[:n_y]
    wo_refs = rest[n_y:2 * n_y]
    p_ref, pw_ref, pn_ref, gw_ref, norm_ref = rest[2 * n_y:2 * n_y + 5]
    out_refs = rest[2 * n_y + 5:]

    def rows_block(r0):
        rows = slice(r0, r0 + OUT_SUB_ROWS)
        mix = jnp.dot(y_refs[0][rows, :], wo_refs[0][...], preferred_element_type=F32)
        for y_ref, wo_ref in zip(y_refs[1:], wo_refs[1:]):
            mix = mix + jnp.dot(y_ref[rows, :], wo_ref[...], preferred_element_type=F32)
        yield
        h1 = h_ref[rows, :] + mix
        pe = _dot(p_ref[rows, :], pw_ref[...])
        gate = _dot(h1, gw_ref[...])
        yield
        pe = pe * lax.rsqrt(jnp.mean(pe * pe, axis=-1, keepdims=True) + EPS) * pn_ref[...]
        h2 = h1 + _sigmoid(gate) * pe
        hn = h2 * lax.rsqrt(jnp.mean(h2 * h2, axis=-1, keepdims=True) + EPS) * norm_ref[...]
        if final:
            out_refs[0][rows, :] = hn
        else:
            out_refs[0][rows, :] = h2
            out_refs[1][rows, :] = hn.astype(out_refs[1].dtype)

    stages = [rows_block(r0) for r0 in range(0, h_ref.shape[0], OUT_SUB_ROWS)]
    while stages:
        stages = [g for g in stages if next(g, stages) is not stages]


def _out_proj(h, ys, w_out, p, layer, ple_w, ple_norm, gate_w, norm, final, *, tm=512):
    m, d = h.shape
    const = lambda shape: pl.BlockSpec(shape, lambda i: (0, 0), pipeline_mode=pl.Buffered(1))
    in_specs = [pl.BlockSpec((tm, d), lambda i: (i, 0))]
    in_specs += [pl.BlockSpec((tm, y.shape[1]), lambda i: (i, 0)) for y in ys]
    assert all(y.shape[1] == ys[0].shape[1] for y in ys)
    in_specs += [pl.BlockSpec((y.shape[1], d), lambda i, r=r: (r, 0), pipeline_mode=pl.Buffered(1))
                 for r, y in enumerate(ys)]
    stacked = lambda a: pl.BlockSpec((None,) + a.shape[1:], lambda i: (layer, 0, 0),
                                     pipeline_mode=pl.Buffered(1))
    in_specs += [pl.BlockSpec((None, tm, p.shape[2]), lambda i: (layer, i, 0)),
                 stacked(ple_w), const((1, d)), stacked(gate_w), const((1, d))]
    args = [h, *ys, *([w_out] * len(ys)), p, ple_w, ple_norm.reshape(1, d), gate_w,
            norm.reshape(1, d)]
    row_spec = pl.BlockSpec((tm, d), lambda i: (i, 0))
    return pl.pallas_call(
        functools.partial(_out_kernel, n_y=len(ys), final=final),
        grid=(m // tm,),
        in_specs=in_specs,
        out_specs=row_spec if final else [row_spec, row_spec],
        out_shape=(jax.ShapeDtypeStruct((m, d), F32) if final else
                   [jax.ShapeDtypeStruct((m, d), F32), jax.ShapeDtypeStruct((m, d), BF16)]),
        compiler_params=pltpu.CompilerParams(
            dimension_semantics=("parallel",), vmem_limit_bytes=VMEM_LIMIT),
        name="out_proj_final" if final else "out_proj",
    )(*args)


def kernel(x, p, e_norm, e_w_in, a_lb_logits, a_norm, b_conv_w, b_conv_b, b_w_r, b_b_r, b_w_i, b_b_i, b_lambda, e_w_out, o_norm, o_w_in, c_conv_w, c_conv_b, c_b_i, c_b_f, c_norm, o_w_out, ple_w, ple_norm, ple_gate_w, final_norm):
    bsz, s_len, d = x.shape
    depth = p.shape[0]
    m = bsz * s_len
    a_width = a_norm.shape[1]
    b_width = b_lambda.shape[1]
    a_dk = LANES
    a_heads = a_width // a_dk
    c_heads = c_b_i.shape[1]
    c_dv = c_norm.shape[1] // c_heads
    c_dk = c_conv_w.shape[2] // (2 * c_heads)
    c_main = o_w_in.shape[2] - 2 * c_heads

    h = x.reshape(m, d)
    ple_w_bf, gate_w_bf = ple_w.astype(BF16), ple_gate_w.astype(BF16)
    hn = None
    layer_norm = lambda i: e_norm[i // 2] if i % 2 == 0 else o_norm[i // 2]
    for i in range(depth):
        j = i // 2
        last = i == depth - 1
        xin, g = (h, layer_norm(i)) if hn is None else (hn, None)
        if i % 2 == 0:
            a_qk = a_heads * a_dk
            spans = (("silu", a_qk), ("hgate", a_qk), ("id", a_width), ("silu", a_width),
                     ("id", b_width), ("silu", b_width))
            u = _in_proj(xin, g, e_w_in[j].astype(BF16), spans=spans,
                         lb_logits=a_lb_logits, layer=j).reshape(bsz, s_len, -1)
            ya = _hgrn2(u, a_norm[j], heads=a_heads, dk=a_dk)
            x_off = (2 * a_heads * a_dk + 2 * a_width) // LANES
            yb = _rglru(u, b_conv_w[j], b_conv_b[j], b_w_r[j].astype(BF16), b_b_r[j],
                        b_w_i[j].astype(BF16), b_b_i[j], b_lambda[j],
                        x_off=x_off, z_off=x_off + b_width // LANES)
            w_out = e_w_out[j].astype(BF16)
            ys = [ya.reshape(m, a_width), yb.reshape(m, b_width)]
        else:
            w_in_t = o_w_in[j].T
            pad = (0, LANES - c_heads)
            wg_t = jnp.concatenate([jnp.pad(w_in_t[c_main:c_main + c_heads], (pad, (0, 0))),
                                    jnp.pad(w_in_t[c_main + c_heads:], (pad, (0, 0)))]).astype(BF16)
            u, gates = _in_proj(xin, g, w_in_t.astype(BF16), wg_t, spans=(("id", c_main),),
                                transposed=True)
            gate_bias = jnp.concatenate([jnp.pad(c_b_i[j], pad),
                                         jnp.pad(c_b_f[j], pad)]).reshape(1, 2 * LANES)
            gate_terms = _mlstm_gates(gates.reshape(bsz, s_len, 2 * LANES), gate_bias)
            yc = _mlstm(u.reshape(bsz, s_len, -1), gate_terms, c_conv_w[j], c_conv_b[j],
                        c_norm[j], heads=c_heads, dk=c_dk, dv=c_dv)
            ys = [yc.reshape(m, -1)]
            w_out = o_w_out[j].astype(BF16)
        res = _out_proj(h, ys, w_out, p.reshape(depth, m, -1), i, ple_w_bf, ple_norm[i],
                        gate_w_bf, final_norm if last else layer_norm(i + 1), last)
        h, hn = (res, None) if last else res
    return h.reshape(bsz, s_len, d)
```

```python
import functools

import jax
import jax.numpy as jnp
from jax import lax
from jax.experimental import pallas as pl
from jax.experimental.pallas import tpu as pltpu

EPS = 1e-6
LRU_C = 8.0
CONV_W = 4
LANES = 128
SUBLANES = 8
VMEM_LIMIT = 56 * 1024 * 1024
BF16 = jnp.bfloat16
F32 = jnp.float32


def _sigmoid(x):
    return jax.nn.sigmoid(x)


def _silu(x):
    return x * jax.nn.sigmoid(x)


def _softplus(x):
    return jnp.maximum(x, 0.0) + jnp.log1p(jnp.exp(-jnp.abs(x)))


def _log_sigmoid(x):
    return -_softplus(-x)


def _shift_rows(x, s, row):
    return jnp.where(row >= s, pltpu.roll(x, s, 0), 0.0)


def _cumsum_rows(x, row):
    s = 1
    while s < x.shape[0]:
        x = x + _shift_rows(x, s, row)
        s *= 2
    return x


def _cummax_rows(x, row):
    s = 1
    while s < x.shape[0]:
        x = jnp.maximum(x, jnp.where(row >= s, pltpu.roll(x, s, 0), -jnp.inf))
        s *= 2
    return x


def _causal_conv(x_ref, cw_ref, cb_ref):
    s_len = x_ref.shape[1]
    taps = [cw_ref[k:k + 1, :] for k in range(CONV_W)]
    acc = cb_ref[...] + x_ref[0, SUBLANES:, :] * taps[CONV_W - 1]
    for j in range(1, CONV_W):
        acc = acc + x_ref[0, SUBLANES - j:s_len - j, :] * taps[CONV_W - 1 - j]
    x0 = x_ref[0, 0:SUBLANES, :]
    row = lax.broadcasted_iota(jnp.int32, x0.shape, 0)
    acc0 = cb_ref[...] + x0 * taps[CONV_W - 1]
    for j in range(1, CONV_W):
        acc0 = acc0 + _shift_rows(x0, j, row) * taps[CONV_W - 1 - j]
    return jnp.concatenate([acc0, acc], axis=0)


def _with_casts(body, n_in, n_out, casts, grid):
    steps = grid[0] * grid[1]

    def kernel(*refs):
        ins, refs = refs[:n_in], refs[n_in:]
        cast_in, refs = refs[:len(casts)], refs[len(casts):]
        outs, refs = refs[:n_out], refs[n_out:]
        cast_out, scratch = refs[:len(casts)], refs[len(casts):]
        for src, dst in zip(cast_in, cast_out):
            dst[...] = src[...].astype(dst.dtype)
        body(*ins, *outs, *scratch)

    rows = [w.shape[0] // (steps * 16) * 16 for w in casts]
    specs = [pl.BlockSpec((r, w.shape[1]), lambda a, b: (a * grid[1] + b, 0))
             for r, w in zip(rows, casts)]
    shapes = [jax.ShapeDtypeStruct((r * steps, w.shape[1]), BF16) for r, w in zip(rows, casts)]
    return kernel, specs, list(specs), shapes


def _dot(a, b):
    return jnp.dot(a.astype(BF16), b.astype(BF16), preferred_element_type=F32)


def _dot_nt(a, b):
    return lax.dot_general(a.astype(BF16), b.astype(BF16), (((1,), (1,)), ((), ())),
                           preferred_element_type=F32)


def _dot_tn(a, b):
    return lax.dot_general(a.astype(BF16), b.astype(BF16), (((0,), (0,)), ((), ())),
                           preferred_element_type=F32)


SUB_ROWS = 256
COL_PIECE = 512


def _in_proj_kernel(*refs, with_norm, with_gates, transposed, spans, layer):
    kinds = {k for k, _ in spans}
    refs = list(refs)
    x_ref = refs.pop(0)
    g_ref = refs.pop(0) if with_norm else None
    w_ref = refs.pop(0)
    wg_ref = refs.pop(0) if with_gates else None
    lbl_ref = refs.pop(0) if "hgate" in kinds else None
    o_ref = refs.pop(0)
    og_ref = refs.pop(0) if with_gates else None
    hn_ref = refs.pop(0) if with_norm else x_ref
    mm = _dot_nt if transposed else functools.partial(jnp.dot, preferred_element_type=F32)
    tm, tn = o_ref.shape
    j = pl.program_id(1)

    starts, c0 = [], 0
    for kind, width in spans:
        starts.append((kind, c0, width))
        c0 += width
    n = c0

    if lbl_ref is not None:
        lg = lbl_ref[...]
        e = jnp.exp(lg - jnp.max(lg, axis=0, keepdims=True))
        sm = e / jnp.sum(e, axis=0, keepdims=True)
        lb = jnp.sum(sm[:layer + 1], axis=0, keepdims=True)

    def activate(kind, r, c0):
        if kind == "silu":
            return _silu(r)
        if kind == "sigmoid":
            return _sigmoid(r)
        if kind == "hgate":
            lbp = lb[:, c0:c0 + r.shape[1]]
            return lbp + (1.0 - lbp) * _sigmoid(r)
        return r

    def run_tile(jt, blocked):
        tile_c0 = jt * tn
        pieces = []
        for kind, c0, width in starts:
            lo, hi = max(c0, tile_c0), min(c0 + width, tile_c0 + tn)
            for p0 in range(lo, hi, COL_PIECE if blocked else hi - lo):
                pieces.append((kind, p0 - tile_c0, min(COL_PIECE, hi - p0) if blocked else hi - lo,
                               p0 - c0))
        for r0 in range(0, tm, SUB_ROWS if blocked else tm):
            rows = slice(r0, r0 + (SUB_ROWS if blocked else tm))
            if with_norm and jt == 0:
                x = x_ref[rows, :]
                ms = jnp.mean(x * x, axis=-1, keepdims=True)
                hn = (x * lax.rsqrt(ms + EPS) * g_ref[...]).astype(BF16)
                hn_ref[rows, :] = hn
            else:
                hn = hn_ref[rows, :]
            for kind, off, width, c0 in pieces:
                wp = w_ref[off:off + width, :] if transposed else w_ref[:, off:off + width]
                o_ref[rows, off:off + width] = activate(kind, mm(hn, wp), c0)
        if with_gates and jt == 0:
            og_ref[...] = mm(hn_ref[...], wg_ref[...])

    if kinds == {"id"}:
        pl.when(j == 0)(functools.partial(run_tile, 0, with_norm))
        pl.when(j != 0)(functools.partial(run_tile, 1, False))
    else:
        for jt in range(n // tn):
            pl.when(j == jt)(functools.partial(run_tile, jt, True))


IN_PROJ_ROWS = 1024
IN_PROJ_VMEM = 46 * 1024 * 1024


def _in_proj_cols(n, tm, d, gw, x_bytes, with_norm):
    for tn in range(n, 0, -512):
        need = (2 * tm * d * x_bytes + (tm * d * 2 if with_norm else 0) + 2 * d * tn * 2
                + 2 * tm * tn * 4 + 2 * (d * gw * 2 + tm * gw * 4))
        if n % tn == 0 and need <= IN_PROJ_VMEM:
            return tn
    raise ValueError("no column tile fits")


def _in_proj(x, g, w, wg=None, *, spans, transposed=False, lb_logits=None, layer=0,
             tm=IN_PROJ_ROWS):
    m, d = x.shape
    n = sum(width for _, width in spans)
    with_gates = wg is not None
    with_norm = g is not None
    gw = wg.shape[0 if transposed else 1] if with_gates else 0
    w_spec = (lambda cols, idx: pl.BlockSpec((cols, d), lambda i, j: (idx(j), 0)) if transposed
              else pl.BlockSpec((d, cols), lambda i, j: (0, idx(j))))
    const = lambda a: pl.BlockSpec(a.shape, lambda i, j: (0,) * a.ndim)
    tn = _in_proj_cols(n, tm, d, gw, x.dtype.itemsize, with_norm)
    in_specs = [pl.BlockSpec((tm, d), lambda i, j: (i, 0))]
    args = [x]
    if with_norm:
        args.append(g.reshape(1, d))
        in_specs.append(const(args[-1]))
    in_specs.append(w_spec(tn, lambda j: j))
    args.append(w)
    out_shape = [jax.ShapeDtypeStruct((m, n), F32)]
    out_specs = [pl.BlockSpec((tm, tn), lambda i, j: (i, j))]
    if with_gates:
        in_specs.append(w_spec(gw, lambda j: 0))
        out_shape.append(jax.ShapeDtypeStruct((m, gw), F32))
        out_specs.append(pl.BlockSpec((tm, gw), lambda i, j: (i, 0)))
        args.append(wg)
    if lb_logits is not None:
        args.append(lb_logits)
        in_specs.append(const(lb_logits))
    res = pl.pallas_call(
        functools.partial(_in_proj_kernel, with_norm=with_norm, with_gates=with_gates,
                          transposed=transposed, spans=tuple(spans), layer=layer),
        grid=(m // tm, n // tn),
        in_specs=in_specs,
        out_specs=out_specs,
        out_shape=out_shape,
        scratch_shapes=[pltpu.VMEM((tm, d), BF16)] if with_norm else [],
        compiler_params=pltpu.CompilerParams(
            dimension_semantics=("parallel", "arbitrary"), vmem_limit_bytes=VMEM_LIMIT),
        name="in_proj_gates" if with_gates else "in_proj",
    )(*args)
    return res if with_gates else res[0]


HG_CHUNK = 128
HG_GROUPS = HG_CHUNK // SUBLANES
HG_INFLIGHT = 8


def _hgrn2_kernel(q_ref, f_ref, i_ref, z_ref, an_ref, o_ref,
                  st_ref, qs_ref, ks_ref, bs_ref, od_ref, tri_ref):
    s_len = q_ref.shape[1]
    dk = q_ref.shape[2]
    c = HG_CHUNK
    ng = HG_GROUPS
    assert c == dk == LANES and i_ref.shape[2] == dk

    an = an_ref[...]

    rowc = lax.broadcasted_iota(jnp.int32, (c, c), 0)
    colc = lax.broadcasted_iota(jnp.int32, (c, c), 1)
    tri_ref[...] = (colc <= rowc).astype(BF16)
    same_group = (colc % ng) == (rowc % ng)
    lane = lax.broadcasted_iota(jnp.int32, (SUBLANES, c), 1)
    diag_off = lane - lax.broadcasted_iota(jnp.int32, (SUBLANES, c), 0)
    zeros8 = jnp.zeros((SUBLANES, dk), F32)
    zeros16 = jnp.zeros((ng, dk), F32)

    st_ref[...] = jnp.zeros_like(st_ref)

    def rows8(x, r):
        return x[r * SUBLANES:(r + 1) * SUBLANES, :]

    def chunk(ci, slot):
        qs, ks, bs, od = qs_ref.at[slot], ks_ref.at[slot], bs_ref.at[slot], od_ref.at[slot]
        r0 = pl.multiple_of(ci * c, c)
        q = q_ref[0, pl.ds(r0, c), :]
        f = f_ref[0, pl.ds(r0, c), :]
        v = i_ref[0, pl.ds(r0, c), :]
        k = 1.0 - f

        lf = jnp.log2(f)
        hi = lf.astype(BF16)
        r1 = lf - hi.astype(F32)
        mid = r1.astype(BF16)
        lo = (r1 - mid.astype(F32)).astype(BF16)
        bb = jnp.dot(tri_ref[...], jnp.concatenate([hi, mid, lo], axis=1),
                     preferred_element_type=F32)
        yield
        b2 = bb[:, :dk] + bb[:, dk:2 * dk] + bb[:, 2 * dk:]
        qs[...] = q
        ks[...] = k
        bs[...] = b2
        b_last = bs[c - 1:c, :]
        upd = _dot_tn(v, k * jnp.exp2(b_last - b2))

        ps, rights = [], []
        for j in range(ng.bit_length() - 1):
            lhs, rhs, rights_j = [], [], []
            for r in range(ng):
                mid_grp = ((r >> (j + 1)) << (j + 1)) | ((1 << j) - 1)
                beta = bs[mid_grp * SUBLANES + SUBLANES - 1:(mid_grp + 1) * SUBLANES, :]
                if (r >> j) & 1:
                    lhs.append(rows8(q, r) * jnp.exp2(rows8(b2, r) - beta))
                    rhs.append(zeros8)
                    rights_j.append(r)
                else:
                    rhs.append(rows8(k, r) * jnp.exp2(beta - rows8(b2, r)))
            ps.append(_dot_nt(jnp.concatenate(lhs, axis=0), jnp.concatenate(rhs, axis=0)))
            rights.append(rights_j)
        yield

        st = st_ref[...]
        o = _dot_nt(q * jnp.exp2(b2), st)
        st_ref[...] = st * jnp.exp2(b_last) + upd

        def permuted(ref):
            return [ref[pl.ds(pos, ng, stride=SUBLANES), :] for pos in range(SUBLANES)]

        q_p, k_p, b_p = permuted(qs), permuted(ks), permuted(bs)
        lhs_slots, rhs_slots = [], []
        for j in range(SUBLANES.bit_length() - 1):
            for blk in range(SUBLANES >> (j + 1)):
                mid_pos = (blk << (j + 1)) | ((1 << j) - 1)
                lhs, rhs = [], []
                for pos in range(SUBLANES):
                    if pos >> (j + 1) != blk:
                        lhs.append(zeros16)
                        rhs.append(zeros16)
                    elif (pos >> j) & 1:
                        lhs.append(q_p[pos] * jnp.exp2(b_p[pos] - b_p[mid_pos]))
                        rhs.append(zeros16)
                    else:
                        lhs.append(zeros16)
                        rhs.append(k_p[pos] * jnp.exp2(b_p[mid_pos] - b_p[pos]))
                lhs_slots.append(jnp.concatenate(lhs, axis=0).astype(BF16))
                rhs_slots.append(jnp.concatenate(rhs, axis=0).astype(BF16))
        pd = _dot_nt(jnp.concatenate(lhs_slots, axis=1), jnp.concatenate(rhs_slots, axis=1))
        yield

        dg = jnp.sum(q * k, axis=1, keepdims=True)
        a_rows = [jnp.where(diag_off == r * SUBLANES, rows8(dg, r), 0.0) for r in range(ng)]
        for j, (p, rights_j) in enumerate(zip(ps, rights)):
            for i, r in enumerate(rights_j):
                right_start = (((r >> (j + 1)) << (j + 1)) | (1 << j)) * SUBLANES
                a_rows[r] = jnp.where(lane < right_start, rows8(p, i), a_rows[r])
        o = o + _dot(jnp.concatenate(a_rows, axis=0), v)
        v_p = [i_ref[0, pl.ds(r0 + pos, ng, stride=SUBLANES), :] for pos in range(SUBLANES)]
        od[...] = _dot(jnp.where(same_group, pd, 0.0), jnp.concatenate(v_p, axis=0))
        yield

        o = o + jnp.concatenate(
            [od[pl.ds(g, SUBLANES, stride=ng), :] for g in range(ng)], axis=0)
        yn = o * lax.rsqrt(jnp.mean(o * o, axis=-1, keepdims=True) + EPS) * an
        z = z_ref[0, pl.ds(r0, c), :]
        o_ref[0, pl.ds(r0, c), :] = (yn * z).astype(o_ref.dtype)

    def chunks(i, carry):
        stages = [chunk(i * HG_INFLIGHT + slot, slot) for slot in range(HG_INFLIGHT)]
        while stages:
            stages = [g for g in stages if next(g, stages) is not stages]
        return carry

    lax.fori_loop(0, s_len // (c * HG_INFLIGHT), chunks, 0)


def _hgrn2(u, a_norm, *, heads, dk, casts=()):
    bsz, s_len, _ = u.shape
    blk = lambda off: pl.BlockSpec((1, s_len, dk), lambda b, h, off=off: (b, 0, off + h))
    grid = (bsz, heads)
    body, c_in, c_out, c_shapes = _with_casts(_hgrn2_kernel, 5, 1, casts, grid)
    return pl.pallas_call(
        body,
        grid=grid,
        in_specs=[blk(0), blk(heads), blk(2 * heads), blk(3 * heads),
                  pl.BlockSpec((1, dk), lambda b, h: (0, h))] + c_in,
        out_specs=[pl.BlockSpec((1, s_len, dk), lambda b, h: (b, 0, h))] + c_out,
        out_shape=[jax.ShapeDtypeStruct((bsz, s_len, heads * dk), BF16)] + c_shapes,
        scratch_shapes=[pltpu.VMEM((dk, dk), F32)]
        + [pltpu.VMEM((HG_INFLIGHT, HG_CHUNK, dk), F32)] * 4
        + [pltpu.VMEM((HG_CHUNK, HG_CHUNK), BF16)],
        compiler_params=pltpu.CompilerParams(
            dimension_semantics=("parallel", "parallel"), vmem_limit_bytes=VMEM_LIMIT),
        name="hgrn2",
    )(u, u, u, u, a_norm.reshape(1, -1), *casts)


def _rglru_kernel(x_ref, z_ref, cw_ref, cb_ref, wr_ref, br_ref, wi_ref, bi_ref, lam_ref,
                  o_ref, a_ref, u_ref, h_ref, ca_ref, cu_ref):
    s_len = x_ref.shape[1]
    w = x_ref.shape[2]
    ng = s_len // SUBLANES
    xc = _causal_conv(x_ref, cw_ref, cb_ref)
    r = _sigmoid(_dot(xc, wr_ref[0]) + br_ref[...])
    ig = _sigmoid(_dot(xc, wi_ref[0]) + bi_ref[...])
    log_a = -LRU_C * r * _softplus(-lam_ref[...])
    a = jnp.exp(log_a)
    a_ref[...] = a
    t = -jnp.tanh(log_a) * (a * a + 1.0)
    u_ref[...] = jnp.where(t > 0.0, t * lax.rsqrt(t), 0.0) * (ig * xc)

    ca = cu = None
    for pos in range(SUBLANES):
        ap = a_ref[pl.ds(pos, ng, stride=SUBLANES), :]
        up = u_ref[pl.ds(pos, ng, stride=SUBLANES), :]
        ca, cu = (ap, up) if pos == 0 else (ap * ca, ap * cu + up)
        ca_ref[pos] = ca
        cu_ref[pos] = cu
    grp = lax.broadcasted_iota(jnp.int32, (ng, w), 0)
    ta, tu = ca, cu
    s = 1
    while s < ng:
        m = grp >= s
        tu = tu + ta * jnp.where(m, pltpu.roll(tu, s, 0), 0.0)
        ta = ta * jnp.where(m, pltpu.roll(ta, s, 0), 1.0)
        s *= 2
    h_in = jnp.where(grp >= 1, pltpu.roll(tu, 1, 0), 0.0)
    for pos in range(SUBLANES):
        h_ref[pl.ds(pos, ng, stride=SUBLANES), :] = ca_ref[pos] * h_in + cu_ref[pos]
    o_ref[0] = (h_ref[...] * z_ref[0]).astype(o_ref.dtype)


def _rglru(u, conv_w, conv_b, w_r, b_r, w_i, b_i, lam, *, x_off, z_off, casts=()):
    bsz, s_len, _ = u.shape
    nblk, blk, _ = w_r.shape
    vec = lambda rows: pl.BlockSpec((rows, blk), lambda b, n: (0, n))
    mat = pl.BlockSpec((1, blk, blk), lambda b, n: (n, 0, 0))
    grid = (bsz, nblk)
    body, c_in, c_out, c_shapes = _with_casts(_rglru_kernel, 9, 1, casts, grid)
    return pl.pallas_call(
        body,
        grid=grid,
        in_specs=[pl.BlockSpec((1, s_len, blk), lambda b, n: (b, 0, x_off + n)),
                  pl.BlockSpec((1, s_len, blk), lambda b, n: (b, 0, z_off + n)),
                  vec(CONV_W), vec(1), mat, vec(1), mat, vec(1), vec(1)] + c_in,
        out_specs=[pl.BlockSpec((1, s_len, blk), lambda b, n: (b, 0, n))] + c_out,
        out_shape=[jax.ShapeDtypeStruct((bsz, s_len, nblk * blk), BF16)] + c_shapes,
        scratch_shapes=[pltpu.VMEM((s_len, blk), F32)] * 3
        + [pltpu.VMEM((SUBLANES, s_len // SUBLANES, blk), F32)] * 2,
        compiler_params=pltpu.CompilerParams(
            dimension_semantics=("parallel", "parallel"), vmem_limit_bytes=VMEM_LIMIT),
        name="rglru",
    )(u, u, conv_w, conv_b.reshape(1, -1), w_r, b_r.reshape(1, -1), w_i, b_i.reshape(1, -1),
      lam.reshape(1, -1), *casts)


ML_CHUNK = 128
ML_UNROLL = 8


def _mlstm_gates_kernel(g_ref, gb_ref, mu_ref, gi_ref, en_ref, ks_ref, dec_ref, rt_ref):
    s_len = g_ref.shape[1]
    c = ML_CHUNK
    t = g_ref[0] + gb_ref[...]
    li = t[:, :LANES]
    lf = _log_sigmoid(t[:, LANES:])
    row = lax.broadcasted_iota(jnp.int32, (s_len, LANES), 0) % c

    def scan(x, op, fill):
        s = 1
        while s < c:
            x = op(x, jnp.where(row >= s, pltpu.roll(x, s, 0), fill))
            s *= 2
        return x

    b = scan(lf, jnp.add, 0.0)
    r = li - b
    rho = scan(r, jnp.maximum, -jnp.inf)
    m = jnp.zeros((1, LANES), F32)
    for ci in range(s_len // c):
        sl = slice(ci * c, (ci + 1) * c)
        b_c, r_c, rho_c = b[sl], r[sl], rho[sl]
        b_last = b_c[c - 1:c, :]
        m_new = b_last + jnp.maximum(m, rho_c[c - 1:c, :])
        mu = jnp.maximum(rho_c, m)
        mu_ref[0, sl, :] = mu
        gi_ref[0, sl, :] = jnp.exp(m - mu)
        en_ref[0, sl, :] = jnp.exp(-(b_c + mu))
        ks_ref[0, sl, :] = jnp.exp(b_last + r_c - m_new)
        dec_ref[0, ci:ci + 1, :] = jnp.exp(b_last + m - m_new)
        rt_ref[0, :, sl] = r_c.T
        m = m_new


def _mlstm_gates(gates, gate_bias):
    bsz, s_len, gw = gates.shape
    nc = s_len // ML_CHUNK
    col = jax.ShapeDtypeStruct((bsz, s_len, LANES), F32)
    col_spec = pl.BlockSpec((1, s_len, LANES), lambda b: (b, 0, 0))
    return pl.pallas_call(
        _mlstm_gates_kernel,
        grid=(bsz,),
        in_specs=[pl.BlockSpec((1, s_len, gw), lambda b: (b, 0, 0)),
                  pl.BlockSpec((1, gw), lambda b: (0, 0))],
        out_specs=[col_spec] * 4 + [pl.BlockSpec((1, nc, LANES), lambda b: (b, 0, 0)),
                                    pl.BlockSpec((1, LANES, s_len), lambda b: (b, 0, 0))],
        out_shape=[col] * 4 + [jax.ShapeDtypeStruct((bsz, nc, LANES), F32),
                               jax.ShapeDtypeStruct((bsz, LANES, s_len), F32)],
        compiler_params=pltpu.CompilerParams(
            dimension_semantics=("parallel",), vmem_limit_bytes=VMEM_LIMIT),
        name="mlstm_gates",
    )(gates, gate_bias)


def _mlstm_kernel(q_ref, k_ref, v_ref, og_ref, z_ref, mu_ref, gi_ref, en_ref, ks_ref, dec_ref,
                  rt_ref, cwq_ref, cbq_ref, cwk_ref, cbk_ref, cn_ref, o_ref, qs_ref, ks_s_ref,
                  cst_ref, nst_ref):
    s_len = q_ref.shape[1]
    dk = q_ref.shape[2]
    c = ML_CHUNK
    assert c == dk == LANES
    h = pl.program_id(1)

    qs_ref[...] = _silu(_causal_conv(q_ref, cwq_ref, cbq_ref)) * (dk ** -0.5)
    ks_s_ref[...] = _silu(_causal_conv(k_ref, cwk_ref, cbk_ref))
    cst_ref[...] = jnp.zeros_like(cst_ref)
    nst_ref[...] = jnp.zeros_like(nst_ref)

    row = lax.broadcasted_iota(jnp.int32, (c, c), 0)
    col = lax.broadcasted_iota(jnp.int32, (c, c), 1)
    causal = col <= row
    head = col == h
    cn = cn_ref[...]

    def pick(x):
        return jnp.sum(jnp.where(head, x, 0.0), axis=1, keepdims=True)

    dec_all = dec_ref[0]
    chunk_id = lax.broadcasted_iota(jnp.int32, dec_all.shape, 0)
    head_lane = lax.broadcasted_iota(jnp.int32, (1, LANES), 1) == h
    head_row = lax.broadcasted_iota(jnp.int32, (SUBLANES, c), 0) == h

    def chunk(ci, carry):
        r0 = pl.multiple_of(ci * c, c)
        q = qs_ref[pl.ds(r0, c), :]
        k = ks_s_ref[pl.ds(r0, c), :]
        v = v_ref[0, pl.ds(r0, c), :]
        mu = pick(mu_ref[0, pl.ds(r0, c), :])
        g_inter = pick(gi_ref[0, pl.ds(r0, c), :])
        e_negm = pick(en_ref[0, pl.ds(r0, c), :])
        kscale = pick(ks_ref[0, pl.ds(r0, c), :])
        decay = jnp.sum(jnp.where(chunk_id == ci, dec_all, 0.0), axis=0, keepdims=True)
        decay = jnp.sum(jnp.where(head_lane, decay, 0.0), axis=1, keepdims=True)
        r_row = jnp.sum(jnp.where(head_row, rt_ref[0, :, pl.ds(r0, c)], 0.0),
                        axis=0, keepdims=True)
        wts = jnp.where(causal, jnp.exp(jnp.minimum(r_row - mu, 0.0)), 0.0)
        qk = _dot_nt(q, k)
        kw = k * kscale
        upd = _dot_tn(kw, v)

        cst = cst_ref[...]
        n_st = nst_ref[...]
        sc = qk * wts
        inter = _dot(q, cst)
        intra = _dot(sc, v)
        cst_ref[...] = decay * cst + upd
        nst_ref[...] = decay * n_st + jnp.sum(kw, axis=0, keepdims=True)

        num = g_inter * inter + intra
        den = (g_inter * jnp.sum(q * n_st, axis=1, keepdims=True)
               + jnp.sum(sc, axis=1, keepdims=True))
        inv = 1.0 / jnp.maximum(jnp.abs(den), e_negm)
        msq = jnp.mean(num * num, axis=-1, keepdims=True)
        yn = num * (inv * lax.rsqrt(inv * inv * msq + EPS)) * cn
        og = og_ref[0, pl.ds(r0, c), :]
        z = z_ref[0, pl.ds(r0, c), :]
        o_ref[0, pl.ds(r0, c), :] = (yn * _sigmoid(og) * _silu(z)).astype(o_ref.dtype)
        return carry

    lax.fori_loop(0, s_len // c, chunk, 0, unroll=ML_UNROLL)


def _mlstm(u, gate_terms, conv_w, conv_b, c_norm, *, heads, dk, dv, casts=()):
    bsz, s_len, _ = u.shape
    grid = (bsz, heads)
    body, c_in, c_out, c_shapes = _with_casts(_mlstm_kernel, 16, 1, casts, grid)
    nc = s_len // ML_CHUNK
    col_spec = pl.BlockSpec((1, s_len, LANES), lambda b, h: (b, 0, 0))
    qk_blk = lambda off: pl.BlockSpec((1, s_len, dk), lambda b, h, off=off: (b, 0, off + h))
    v_blk = lambda off: pl.BlockSpec((1, s_len, dv), lambda b, h, off=off: (b, 0, off + h))
    cw = lambda off: pl.BlockSpec((CONV_W, dk), lambda b, h, off=off: (0, off + h))
    cb = lambda off: pl.BlockSpec((1, dk), lambda b, h, off=off: (0, off + h))
    nqk = 2 * heads * dk // dv
    return pl.pallas_call(
        body,
        grid=grid,
        in_specs=[qk_blk(0), qk_blk(heads),
                  v_blk(nqk), v_blk(nqk + heads), v_blk(nqk + 2 * heads),
                  col_spec, col_spec, col_spec, col_spec,
                  pl.BlockSpec((1, nc, LANES), lambda b, h: (b, 0, 0)),
                  pl.BlockSpec((1, SUBLANES, s_len), lambda b, h: (b, 0, 0)),
                  cw(0), cb(0), cw(heads), cb(heads),
                  pl.BlockSpec((1, dv), lambda b, h: (0, h))] + c_in,
        out_specs=[pl.BlockSpec((1, s_len, dv), lambda b, h: (b, 0, h))] + c_out,
        out_shape=[jax.ShapeDtypeStruct((bsz, s_len, heads * dv), BF16)] + c_shapes,
        scratch_shapes=[pltpu.VMEM((s_len, dk), F32), pltpu.VMEM((s_len, dk), F32),
                        pltpu.VMEM((dk, dv), F32), pltpu.VMEM((1, dk), F32)],
        compiler_params=pltpu.CompilerParams(
            dimension_semantics=("parallel", "parallel"), vmem_limit_bytes=VMEM_LIMIT),
        name="mlstm",
    )(u, u, u, u, u, *gate_terms, conv_w, conv_b.reshape(1, -1), conv_w,
      conv_b.reshape(1, -1), c_norm.reshape(1, -1), *casts)


OUT_SUB_ROWS = 256


def _out_kernel(h_ref, *rest, n_y, final):
    y_refs = rest[:n_y]
    wo_refs = rest[n_y:2 * n_y]
    p_ref, pw_ref, pn_ref, gw_ref, norm_ref = rest[2 * n_y:2 * n_y + 5]
    out_refs = rest[2 * n_y + 5:]

    def rows_block(r0):
        rows = slice(r0, r0 + OUT_SUB_ROWS)
        mix = jnp.dot(y_refs[0][rows, :], wo_refs[0][...], preferred_element_type=F32)
        for y_ref, wo_ref in zip(y_refs[1:], wo_refs[1:]):
            mix = mix + jnp.dot(y_ref[rows, :], wo_ref[...], preferred_element_type=F32)
        yield
        h1 = h_ref[rows, :] + mix
        pe = _dot(p_ref[rows, :], pw_ref[...])
        gate = _dot(h1, gw_ref[...])
        yield
        pe = pe * lax.rsqrt(jnp.mean(pe * pe, axis=-1, keepdims=True) + EPS) * pn_ref[...]
        h2 = h1 + _sigmoid(gate) * pe
        hn = h2 * lax.rsqrt(jnp.mean(h2 * h2, axis=-1, keepdims=True) + EPS) * norm_ref[...]
        if final:
            out_refs[0][rows, :] = hn
        else:
            out_refs[0][rows, :] = h2
            out_refs[1][rows, :] = hn.astype(out_refs[1].dtype)

    stages = [rows_block(r0) for r0 in range(0, h_ref.shape[0], OUT_SUB_ROWS)]
    while stages:
        stages = [g for g in stages if next(g, stages) is not stages]


def _out_proj(h, ys, w_out, p, layer, ple_w, ple_norm, gate_w, norm, final, *, tm=512):
    m, d = h.shape
    const = lambda shape: pl.BlockSpec(shape, lambda i: (0, 0), pipeline_mode=pl.Buffered(1))
    in_specs = [pl.BlockSpec((tm, d), lambda i: (i, 0))]
    in_specs += [pl.BlockSpec((tm, y.shape[1]), lambda i: (i, 0)) for y in ys]
    assert all(y.shape[1] == ys[0].shape[1] for y in ys)
    in_specs += [pl.BlockSpec((y.shape[1], d), lambda i, r=r: (r, 0), pipeline_mode=pl.Buffered(1))
                 for r, y in enumerate(ys)]
    stacked = lambda a: pl.BlockSpec((None,) + a.shape[1:], lambda i: (layer, 0, 0),
                                     pipeline_mode=pl.Buffered(1))
    in_specs += [pl.BlockSpec((None, tm, p.shape[2]), lambda i: (layer, i, 0)),
                 stacked(ple_w), const((1, d)), stacked(gate_w), const((1, d))]
    args = [h, *ys, *([w_out] * len(ys)), p, ple_w, ple_norm.reshape(1, d), gate_w,
            norm.reshape(1, d)]
    row_spec = pl.BlockSpec((tm, d), lambda i: (i, 0))
    return pl.pallas_call(
        functools.partial(_out_kernel, n_y=len(ys), final=final),
        grid=(m // tm,),
        in_specs=in_specs,
        out_specs=row_spec if final else [row_spec, row_spec],
        out_shape=(jax.ShapeDtypeStruct((m, d), F32) if final else
                   [jax.ShapeDtypeStruct((m, d), F32), jax.ShapeDtypeStruct((m, d), BF16)]),
        compiler_params=pltpu.CompilerParams(
            dimension_semantics=("parallel",), vmem_limit_bytes=VMEM_LIMIT),
        name="out_proj_final" if final else "out_proj",
    )(*args)


def kernel(x, p, e_norm, e_w_in, a_lb_logits, a_norm, b_conv_w, b_conv_b, b_w_r, b_b_r, b_w_i, b_b_i, b_lambda, e_w_out, o_norm, o_w_in, c_conv_w, c_conv_b, c_b_i, c_b_f, c_norm, o_w_out, ple_w, ple_norm, ple_gate_w, final_norm):
    bsz, s_len, d = x.shape
    depth = p.shape[0]
    m = bsz * s_len
    a_width = a_norm.shape[1]
    b_width = b_lambda.shape[1]
    a_dk = LANES
    a_heads = a_width // a_dk
    c_heads = c_b_i.shape[1]
    c_dv = c_norm.shape[1] // c_heads
    c_dk = c_conv_w.shape[2] // (2 * c_heads)
    c_main = o_w_in.shape[2] - 2 * c_heads

    h = x.reshape(m, d)
    ready = {}
    take = lambda name, w: ready.pop(name) if name in ready else w.astype(BF16)
    w_in_t = lambda j: o_w_in[j].T
    hn = None
    layer_norm = lambda i: e_norm[i // 2] if i % 2 == 0 else o_norm[i // 2]
    for i in range(depth):
        j = i // 2
        last = i == depth - 1
        xin, g = (h, layer_norm(i)) if hn is None else (hn, None)
        if i % 2 == 0:
            a_qk = a_heads * a_dk
            spans = (("silu", a_qk), ("hgate", a_qk), ("id", a_width), ("silu", a_width),
                     ("id", b_width), ("silu", b_width))
            u = _in_proj(xin, g, take(("e_w_in", j), e_w_in[j]), spans=spans,
                         lb_logits=a_lb_logits, layer=j).reshape(bsz, s_len, -1)
            jobs = {("o_w_in", j): w_in_t(j)} if not last else {}
            ya, *done = _hgrn2(u, a_norm[j], heads=a_heads, dk=a_dk, casts=list(jobs.values()))
            ready.update(zip(jobs, done))
            jobs = {("e_w_out", j): e_w_out[j]}
            if i == 0:
                jobs.update(gate_w=ple_gate_w.reshape(depth * d, d),
                            ple_w=ple_w.reshape(-1, d))
            x_off = (2 * a_heads * a_dk + 2 * a_width) // LANES
            yb, *done = _rglru(u, b_conv_w[j], b_conv_b[j], b_w_r[j].astype(BF16), b_b_r[j],
                               b_w_i[j].astype(BF16), b_b_i[j], b_lambda[j], x_off=x_off,
                               z_off=x_off + b_width // LANES, casts=list(jobs.values()))
            ready.update(zip(jobs, done))
            if i == 0:
                gate_w_bf = ready.pop("gate_w").reshape(depth, d, d)
                ple_w_bf = ready.pop("ple_w").reshape(ple_w.shape)
            w_out = take(("e_w_out", j), e_w_out[j])
            ys = [ya.reshape(m, a_width), yb.reshape(m, b_width)]
        else:
            pad = (0, LANES - c_heads)
            gate_rows = lax.optimization_barrier(o_w_in[j][:, c_main:]).T
            wg_t = jnp.concatenate([jnp.pad(gate_rows[:c_heads], (pad, (0, 0))),
                                    jnp.pad(gate_rows[c_heads:], (pad, (0, 0)))]).astype(BF16)
            u, gates = _in_proj(xin, g, take(("o_w_in", j), w_in_t(j)[:c_main]), wg_t,
                                spans=(("id", c_main),), transposed=True)
            gate_bias = jnp.concatenate([jnp.pad(c_b_i[j], pad),
                                         jnp.pad(c_b_f[j], pad)]).reshape(1, 2 * LANES)
            gate_terms = _mlstm_gates(gates.reshape(bsz, s_len, 2 * LANES), gate_bias)
            jobs = {("o_w_out", j): o_w_out[j]}
            if not last:
                jobs[("e_w_in", j + 1)] = e_w_in[j + 1]
            yc, *done = _mlstm(u.reshape(bsz, s_len, -1), gate_terms, c_conv_w[j], c_conv_b[j],
                               c_norm[j], heads=c_heads, dk=c_dk, dv=c_dv,
                               casts=list(jobs.values()))
            ready.update(zip(jobs, done))
            ys = [yc.reshape(m, -1)]
            w_out = take(("o_w_out", j), o_w_out[j])
        res = _out_proj(h, ys, w_out, p.reshape(depth, m, -1), i, ple_w_bf, ple_norm[i],
                        gate_w_bf, final_norm if last else layer_norm(i + 1), last)
        h, hn = (res, None) if last else res
    return h.reshape(bsz, s_len, d)
```

```python
import functools

import jax
import jax.numpy as jnp
from jax import lax
from jax.experimental import pallas as pl
from jax.experimental.pallas import tpu as pltpu

EPS = 1e-6
LRU_C = 8.0
CONV_W = 4
LANES = 128
SUBLANES = 8
VMEM_LIMIT = 56 * 1024 * 1024
BF16 = jnp.bfloat16
F32 = jnp.float32


def _sigmoid(x):
    return jax.nn.sigmoid(x)


def _silu(x):
    return x * jax.nn.sigmoid(x)


def _softplus(x):
    return jnp.maximum(x, 0.0) + jnp.log1p(jnp.exp(-jnp.abs(x)))


def _log_sigmoid(x):
    return -_softplus(-x)


def _shift_rows(x, s, row):
    return jnp.where(row >= s, pltpu.roll(x, s, 0), 0.0)


def _cumsum_rows(x, row):
    s = 1
    while s < x.shape[0]:
        x = x + _shift_rows(x, s, row)
        s *= 2
    return x


def _cummax_rows(x, row):
    s = 1
    while s < x.shape[0]:
        x = jnp.maximum(x, jnp.where(row >= s, pltpu.roll(x, s, 0), -jnp.inf))
        s *= 2
    return x


def _causal_conv(x_ref, cw_ref, cb_ref):
    s_len = x_ref.shape[1]
    taps = [cw_ref[k:k + 1, :] for k in range(CONV_W)]
    acc = cb_ref[...] + x_ref[0, SUBLANES:, :] * taps[CONV_W - 1]
    for j in range(1, CONV_W):
        acc = acc + x_ref[0, SUBLANES - j:s_len - j, :] * taps[CONV_W - 1 - j]
    x0 = x_ref[0, 0:SUBLANES, :]
    row = lax.broadcasted_iota(jnp.int32, x0.shape, 0)
    acc0 = cb_ref[...] + x0 * taps[CONV_W - 1]
    for j in range(1, CONV_W):
        acc0 = acc0 + _shift_rows(x0, j, row) * taps[CONV_W - 1 - j]
    return jnp.concatenate([acc0, acc], axis=0)


def _with_casts(body, n_in, n_out, casts, grid):
    steps = grid[0] * grid[1]

    def kernel(*refs):
        ins, refs = refs[:n_in], refs[n_in:]
        cast_in, refs = refs[:len(casts)], refs[len(casts):]
        outs, refs = refs[:n_out], refs[n_out:]
        cast_out, scratch = refs[:len(casts)], refs[len(casts):]
        for src, dst in zip(cast_in, cast_out):
            dst[...] = src[...].astype(dst.dtype)
        body(*ins, *outs, *scratch)

    rows = [w.shape[0] // (steps * 16) * 16 for w in casts]
    specs = [pl.BlockSpec((r, w.shape[1]), lambda a, b: (a * grid[1] + b, 0))
             for r, w in zip(rows, casts)]
    shapes = [jax.ShapeDtypeStruct((r * steps, w.shape[1]), BF16) for r, w in zip(rows, casts)]
    return kernel, specs, list(specs), shapes


def _dot(a, b):
    return jnp.dot(a.astype(BF16), b.astype(BF16), preferred_element_type=F32)


def _dot_nt(a, b):
    return lax.dot_general(a.astype(BF16), b.astype(BF16), (((1,), (1,)), ((), ())),
                           preferred_element_type=F32)


def _dot_tn(a, b):
    return lax.dot_general(a.astype(BF16), b.astype(BF16), (((0,), (0,)), ((), ())),
                           preferred_element_type=F32)


SUB_ROWS = 256
COL_PIECE = 512


def _in_proj_kernel(*refs, with_norm, with_gates, transposed, spans, layer):
    kinds = {k for k, _ in spans}
    refs = list(refs)
    x_ref = refs.pop(0)
    g_ref = refs.pop(0) if with_norm else None
    w_ref = refs.pop(0)
    wg_ref = refs.pop(0) if with_gates else None
    lbl_ref = refs.pop(0) if "hgate" in kinds else None
    o_ref = refs.pop(0)
    og_ref = refs.pop(0) if with_gates else None
    hn_ref = refs.pop(0) if with_norm else x_ref
    mm = _dot_nt if transposed else functools.partial(jnp.dot, preferred_element_type=F32)
    tm, tn = o_ref.shape
    j = pl.program_id(1)

    starts, c0 = [], 0
    for kind, width in spans:
        starts.append((kind, c0, width))
        c0 += width
    n = c0

    if lbl_ref is not None:
        lg = lbl_ref[...]
        e = jnp.exp(lg - jnp.max(lg, axis=0, keepdims=True))
        sm = e / jnp.sum(e, axis=0, keepdims=True)
        lb = jnp.sum(sm[:layer + 1], axis=0, keepdims=True)

    def activate(kind, r, c0):
        if kind == "silu":
            return _silu(r)
        if kind == "sigmoid":
            return _sigmoid(r)
        if kind == "hgate":
            lbp = lb[:, c0:c0 + r.shape[1]]
            return lbp + (1.0 - lbp) * _sigmoid(r)
        return r

    def run_tile(jt, blocked):
        tile_c0 = jt * tn
        pieces = []
        for kind, c0, width in starts:
            lo, hi = max(c0, tile_c0), min(c0 + width, tile_c0 + tn)
            for p0 in range(lo, hi, COL_PIECE if blocked else hi - lo):
                pieces.append((kind, p0 - tile_c0, min(COL_PIECE, hi - p0) if blocked else hi - lo,
                               p0 - c0))
        for r0 in range(0, tm, SUB_ROWS if blocked else tm):
            rows = slice(r0, r0 + (SUB_ROWS if blocked else tm))
            if with_norm and jt == 0:
                x = x_ref[rows, :]
                ms = jnp.mean(x * x, axis=-1, keepdims=True)
                hn = (x * lax.rsqrt(ms + EPS) * g_ref[...]).astype(BF16)
                hn_ref[rows, :] = hn
            else:
                hn = hn_ref[rows, :]
            for kind, off, width, c0 in pieces:
                wp = w_ref[off:off + width, :] if transposed else w_ref[:, off:off + width]
                o_ref[rows, off:off + width] = activate(kind, mm(hn, wp), c0)
        if with_gates and jt == 0:
            og_ref[...] = mm(hn_ref[...], wg_ref[...])

    if kinds == {"id"}:
        pl.when(j == 0)(functools.partial(run_tile, 0, with_norm))
        pl.when(j != 0)(functools.partial(run_tile, 1, False))
    else:
        for jt in range(n // tn):
            pl.when(j == jt)(functools.partial(run_tile, jt, True))


IN_PROJ_ROWS = 1024
IN_PROJ_VMEM = 46 * 1024 * 1024


def _in_proj_cols(n, tm, d, gw, x_bytes, with_norm):
    for tn in range(n, 0, -512):
        need = (2 * tm * d * x_bytes + (tm * d * 2 if with_norm else 0) + 2 * d * tn * 2
                + 2 * tm * tn * 4 + 2 * (d * gw * 2 + tm * gw * 4))
        if n % tn == 0 and need <= IN_PROJ_VMEM:
            return tn
    raise ValueError("no column tile fits")


def _in_proj(x, g, w, wg=None, *, spans, transposed=False, lb_logits=None, layer=0,
             tm=IN_PROJ_ROWS):
    m, d = x.shape
    n = sum(width for _, width in spans)
    with_gates = wg is not None
    with_norm = g is not None
    gw = wg.shape[0 if transposed else 1] if with_gates else 0
    w_spec = (lambda cols, idx: pl.BlockSpec((cols, d), lambda i, j: (idx(j), 0)) if transposed
              else pl.BlockSpec((d, cols), lambda i, j: (0, idx(j))))
    const = lambda a: pl.BlockSpec(a.shape, lambda i, j: (0,) * a.ndim)
    tn = _in_proj_cols(n, tm, d, gw, x.dtype.itemsize, with_norm)
    in_specs = [pl.BlockSpec((tm, d), lambda i, j: (i, 0))]
    args = [x]
    if with_norm:
        args.append(g.reshape(1, d))
        in_specs.append(const(args[-1]))
    in_specs.append(w_spec(tn, lambda j: j))
    args.append(w)
    out_shape = [jax.ShapeDtypeStruct((m, n), F32)]
    out_specs = [pl.BlockSpec((tm, tn), lambda i, j: (i, j))]
    if with_gates:
        in_specs.append(w_spec(gw, lambda j: 0))
        out_shape.append(jax.ShapeDtypeStruct((m, gw), F32))
        out_specs.append(pl.BlockSpec((tm, gw), lambda i, j: (i, 0)))
        args.append(wg)
    if lb_logits is not None:
        args.append(lb_logits)
        in_specs.append(const(lb_logits))
    res = pl.pallas_call(
        functools.partial(_in_proj_kernel, with_norm=with_norm, with_gates=with_gates,
                          transposed=transposed, spans=tuple(spans), layer=layer),
        grid=(m // tm, n // tn),
        in_specs=in_specs,
        out_specs=out_specs,
        out_shape=out_shape,
        scratch_shapes=[pltpu.VMEM((tm, d), BF16)] if with_norm else [],
        compiler_params=pltpu.CompilerParams(
            dimension_semantics=("parallel", "arbitrary"), vmem_limit_bytes=VMEM_LIMIT),
        name="in_proj_gates" if with_gates else "in_proj",
    )(*args)
    return res if with_gates else res[0]


HG_CHUNK = 128
HG_GROUPS = HG_CHUNK // SUBLANES
HG_INFLIGHT = 8


def _hgrn2_kernel(q_ref, f_ref, i_ref, z_ref, an_ref, o_ref,
                  st_ref, qs_ref, ks_ref, bs_ref, od_ref, tri_ref):
    s_len = q_ref.shape[1]
    dk = q_ref.shape[2]
    c = HG_CHUNK
    ng = HG_GROUPS
    assert c == dk == LANES and i_ref.shape[2] == dk

    an = an_ref[...]

    rowc = lax.broadcasted_iota(jnp.int32, (c, c), 0)
    colc = lax.broadcasted_iota(jnp.int32, (c, c), 1)
    tri_ref[...] = (colc <= rowc).astype(BF16)
    same_group = (colc % ng) == (rowc % ng)
    lane = lax.broadcasted_iota(jnp.int32, (SUBLANES, c), 1)
    diag_off = lane - lax.broadcasted_iota(jnp.int32, (SUBLANES, c), 0)
    zeros8 = jnp.zeros((SUBLANES, dk), F32)
    zeros16 = jnp.zeros((ng, dk), F32)

    st_ref[...] = jnp.zeros_like(st_ref)

    def rows8(x, r):
        return x[r * SUBLANES:(r + 1) * SUBLANES, :]

    def chunk(ci, slot):
        qs, ks, bs, od = qs_ref.at[slot], ks_ref.at[slot], bs_ref.at[slot], od_ref.at[slot]
        r0 = pl.multiple_of(ci * c, c)
        q = q_ref[0, pl.ds(r0, c), :]
        f = f_ref[0, pl.ds(r0, c), :]
        v = i_ref[0, pl.ds(r0, c), :]
        k = 1.0 - f

        lf = jnp.log2(f)
        hi = lf.astype(BF16)
        r1 = lf - hi.astype(F32)
        mid = r1.astype(BF16)
        lo = (r1 - mid.astype(F32)).astype(BF16)
        bb = jnp.dot(tri_ref[...], jnp.concatenate([hi, mid, lo], axis=1),
                     preferred_element_type=F32)
        yield
        b2 = bb[:, :dk] + bb[:, dk:2 * dk] + bb[:, 2 * dk:]
        qs[...] = q
        ks[...] = k
        bs[...] = b2
        b_last = bs[c - 1:c, :]
        upd = _dot_tn(v, k * jnp.exp2(b_last - b2))

        ps, rights = [], []
        for j in range(ng.bit_length() - 1):
            lhs, rhs, rights_j = [], [], []
            for r in range(ng):
                mid_grp = ((r >> (j + 1)) << (j + 1)) | ((1 << j) - 1)
                beta = bs[mid_grp * SUBLANES + SUBLANES - 1:(mid_grp + 1) * SUBLANES, :]
                if (r >> j) & 1:
                    lhs.append(rows8(q, r) * jnp.exp2(rows8(b2, r) - beta))
                    rhs.append(zeros8)
                    rights_j.append(r)
                else:
                    rhs.append(rows8(k, r) * jnp.exp2(beta - rows8(b2, r)))
            ps.append(_dot_nt(jnp.concatenate(lhs, axis=0), jnp.concatenate(rhs, axis=0)))
            rights.append(rights_j)
        yield

        st = st_ref[...]
        o = _dot_nt(q * jnp.exp2(b2), st)
        st_ref[...] = st * jnp.exp2(b_last) + upd

        def permuted(ref):
            return [ref[pl.ds(pos, ng, stride=SUBLANES), :] for pos in range(SUBLANES)]

        q_p, k_p, b_p = permuted(qs), permuted(ks), permuted(bs)
        lhs_slots, rhs_slots = [], []
        for j in range(SUBLANES.bit_length() - 1):
            for blk in range(SUBLANES >> (j + 1)):
                mid_pos = (blk << (j + 1)) | ((1 << j) - 1)
                lhs, rhs = [], []
                for pos in range(SUBLANES):
                    if pos >> (j + 1) != blk:
                        lhs.append(zeros16)
                        rhs.append(zeros16)
                    elif (pos >> j) & 1:
                        lhs.append(q_p[pos] * jnp.exp2(b_p[pos] - b_p[mid_pos]))
                        rhs.append(zeros16)
                    else:
                        lhs.append(zeros16)
                        rhs.append(k_p[pos] * jnp.exp2(b_p[mid_pos] - b_p[pos]))
                lhs_slots.append(jnp.concatenate(lhs, axis=0).astype(BF16))
                rhs_slots.append(jnp.concatenate(rhs, axis=0).astype(BF16))
        pd = _dot_nt(jnp.concatenate(lhs_slots, axis=1), jnp.concatenate(rhs_slots, axis=1))
        yield

        dg = jnp.sum(q * k, axis=1, keepdims=True)
        a_rows = [jnp.where(diag_off == r * SUBLANES, rows8(dg, r), 0.0) for r in range(ng)]
        for j, (p, rights_j) in enumerate(zip(ps, rights)):
            for i, r in enumerate(rights_j):
                right_start = (((r >> (j + 1)) << (j + 1)) | (1 << j)) * SUBLANES
                a_rows[r] = jnp.where(lane < right_start, rows8(p, i), a_rows[r])
        o = o + _dot(jnp.concatenate(a_rows, axis=0), v)
        v_p = [i_ref[0, pl.ds(r0 + pos, ng, stride=SUBLANES), :] for pos in range(SUBLANES)]
        od[...] = _dot(jnp.where(same_group, pd, 0.0), jnp.concatenate(v_p, axis=0))
        yield

        o = o + jnp.concatenate(
            [od[pl.ds(g, SUBLANES, stride=ng), :] for g in range(ng)], axis=0)
        yn = o * lax.rsqrt(jnp.mean(o * o, axis=-1, keepdims=True) + EPS) * an
        z = z_ref[0, pl.ds(r0, c), :]
        o_ref[0, pl.ds(r0, c), :] = (yn * z).astype(o_ref.dtype)

    def chunks(i, carry):
        stages = [chunk(i * HG_INFLIGHT + slot, slot) for slot in range(HG_INFLIGHT)]
        while stages:
            stages = [g for g in stages if next(g, stages) is not stages]
        return carry

    lax.fori_loop(0, s_len // (c * HG_INFLIGHT), chunks, 0)


def _hgrn2(u, a_norm, *, heads, dk, casts=()):
    bsz, s_len, _ = u.shape
    blk = lambda off: pl.BlockSpec((1, s_len, dk), lambda b, h, off=off: (b, 0, off + h))
    grid = (bsz, heads)
    body, c_in, c_out, c_shapes = _with_casts(_hgrn2_kernel, 5, 1, casts, grid)
    return pl.pallas_call(
        body,
        grid=grid,
        in_specs=[blk(0), blk(heads), blk(2 * heads), blk(3 * heads),
                  pl.BlockSpec((1, dk), lambda b, h: (0, h))] + c_in,
        out_specs=[pl.BlockSpec((1, s_len, dk), lambda b, h: (b, 0, h))] + c_out,
        out_shape=[jax.ShapeDtypeStruct((bsz, s_len, heads * dk), BF16)] + c_shapes,
        scratch_shapes=[pltpu.VMEM((dk, dk), F32)]
        + [pltpu.VMEM((HG_INFLIGHT, HG_CHUNK, dk), F32)] * 4
        + [pltpu.VMEM((HG_CHUNK, HG_CHUNK), BF16)],
        compiler_params=pltpu.CompilerParams(
            dimension_semantics=("parallel", "parallel"), vmem_limit_bytes=VMEM_LIMIT),
        name="hgrn2",
    )(u, u, u, u, a_norm.reshape(1, -1), *casts)


def _rglru_kernel(x_ref, z_ref, cw_ref, cb_ref, wr_ref, br_ref, wi_ref, bi_ref, lam_ref,
                  o_ref, a_ref, u_ref, h_ref, ca_ref, cu_ref):
    s_len = x_ref.shape[1]
    w = x_ref.shape[2]
    ng = s_len // SUBLANES
    xc = _causal_conv(x_ref, cw_ref, cb_ref)
    r = _sigmoid(_dot(xc, wr_ref[0]) + br_ref[...])
    ig = _sigmoid(_dot(xc, wi_ref[0]) + bi_ref[...])
    log_a = -LRU_C * r * _softplus(-lam_ref[...])
    a = jnp.exp(log_a)
    a_ref[...] = a
    t = -jnp.tanh(log_a) * (a * a + 1.0)
    u_ref[...] = jnp.where(t > 0.0, t * lax.rsqrt(t), 0.0) * (ig * xc)

    ca = cu = None
    for pos in range(SUBLANES):
        ap = a_ref[pl.ds(pos, ng, stride=SUBLANES), :]
        up = u_ref[pl.ds(pos, ng, stride=SUBLANES), :]
        ca, cu = (ap, up) if pos == 0 else (ap * ca, ap * cu + up)
        ca_ref[pos] = ca
        cu_ref[pos] = cu
    grp = lax.broadcasted_iota(jnp.int32, (ng, w), 0)
    ta, tu = ca, cu
    s = 1
    while s < ng:
        m = grp >= s
        tu = tu + ta * jnp.where(m, pltpu.roll(tu, s, 0), 0.0)
        ta = ta * jnp.where(m, pltpu.roll(ta, s, 0), 1.0)
        s *= 2
    h_in = jnp.where(grp >= 1, pltpu.roll(tu, 1, 0), 0.0)
    for pos in range(SUBLANES):
        h_ref[pl.ds(pos, ng, stride=SUBLANES), :] = ca_ref[pos] * h_in + cu_ref[pos]
    o_ref[0] = (h_ref[...] * z_ref[0]).astype(o_ref.dtype)


def _rglru(u, conv_w, conv_b, w_r, b_r, w_i, b_i, lam, *, x_off, z_off, casts=()):
    bsz, s_len, _ = u.shape
    nblk, blk, _ = w_r.shape
    vec = lambda rows: pl.BlockSpec((rows, blk), lambda b, n: (0, n))
    mat = pl.BlockSpec((1, blk, blk), lambda b, n: (n, 0, 0))
    grid = (bsz, nblk)
    body, c_in, c_out, c_shapes = _with_casts(_rglru_kernel, 9, 1, casts, grid)
    return pl.pallas_call(
        body,
        grid=grid,
        in_specs=[pl.BlockSpec((1, s_len, blk), lambda b, n: (b, 0, x_off + n)),
                  pl.BlockSpec((1, s_len, blk), lambda b, n: (b, 0, z_off + n)),
                  vec(CONV_W), vec(1), mat, vec(1), mat, vec(1), vec(1)] + c_in,
        out_specs=[pl.BlockSpec((1, s_len, blk), lambda b, n: (b, 0, n))] + c_out,
        out_shape=[jax.ShapeDtypeStruct((bsz, s_len, nblk * blk), BF16)] + c_shapes,
        scratch_shapes=[pltpu.VMEM((s_len, blk), F32)] * 3
        + [pltpu.VMEM((SUBLANES, s_len // SUBLANES, blk), F32)] * 2,
        compiler_params=pltpu.CompilerParams(
            dimension_semantics=("parallel", "parallel"), vmem_limit_bytes=VMEM_LIMIT),
        name="rglru",
    )(u, u, conv_w, conv_b.reshape(1, -1), w_r, b_r.reshape(1, -1), w_i, b_i.reshape(1, -1),
      lam.reshape(1, -1), *casts)


ML_CHUNK = 128
ML_UNROLL = 8


def _mlstm_gates_kernel(g_ref, gb_ref, mu_ref, gi_ref, en_ref, ks_ref, dec_ref, rt_ref):
    s_len = g_ref.shape[1]
    c = ML_CHUNK
    t = g_ref[0] + gb_ref[...]
    li = t[:, :LANES]
    lf = _log_sigmoid(t[:, LANES:])
    row = lax.broadcasted_iota(jnp.int32, (s_len, LANES), 0) % c

    def scan(x, op, fill):
        s = 1
        while s < c:
            x = op(x, jnp.where(row >= s, pltpu.roll(x, s, 0), fill))
            s *= 2
        return x

    b = scan(lf, jnp.add, 0.0)
    r = li - b
    rho = scan(r, jnp.maximum, -jnp.inf)
    m = jnp.zeros((1, LANES), F32)
    for ci in range(s_len // c):
        sl = slice(ci * c, (ci + 1) * c)
        b_c, r_c, rho_c = b[sl], r[sl], rho[sl]
        b_last = b_c[c - 1:c, :]
        m_new = b_last + jnp.maximum(m, rho_c[c - 1:c, :])
        mu = jnp.maximum(rho_c, m)
        mu_ref[0, sl, :] = mu
        gi_ref[0, sl, :] = jnp.exp(m - mu)
        en_ref[0, sl, :] = jnp.exp(-(b_c + mu))
        ks_ref[0, sl, :] = jnp.exp(b_last + r_c - m_new)
        dec_ref[0, ci:ci + 1, :] = jnp.exp(b_last + m - m_new)
        rt_ref[0, :, sl] = r_c.T
        m = m_new


def _mlstm_gates(gates, gate_bias):
    bsz, s_len, gw = gates.shape
    nc = s_len // ML_CHUNK
    col = jax.ShapeDtypeStruct((bsz, s_len, LANES), F32)
    col_spec = pl.BlockSpec((1, s_len, LANES), lambda b: (b, 0, 0))
    return pl.pallas_call(
        _mlstm_gates_kernel,
        grid=(bsz,),
        in_specs=[pl.BlockSpec((1, s_len, gw), lambda b: (b, 0, 0)),
                  pl.BlockSpec((1, gw), lambda b: (0, 0))],
        out_specs=[col_spec] * 4 + [pl.BlockSpec((1, nc, LANES), lambda b: (b, 0, 0)),
                                    pl.BlockSpec((1, LANES, s_len), lambda b: (b, 0, 0))],
        out_shape=[col] * 4 + [jax.ShapeDtypeStruct((bsz, nc, LANES), F32),
                               jax.ShapeDtypeStruct((bsz, LANES, s_len), F32)],
        compiler_params=pltpu.CompilerParams(
            dimension_semantics=("parallel",), vmem_limit_bytes=VMEM_LIMIT),
        name="mlstm_gates",
    )(gates, gate_bias)


def _mlstm_kernel(q_ref, k_ref, v_ref, og_ref, z_ref, mu_ref, gi_ref, en_ref, ks_ref, dec_ref,
                  rt_ref, cwq_ref, cbq_ref, cwk_ref, cbk_ref, cn_ref, o_ref, qs_ref, ks_s_ref,
                  cst_ref, nst_ref):
    s_len = q_ref.shape[1]
    dk = q_ref.shape[2]
    c = ML_CHUNK
    assert c == dk == LANES
    h = pl.program_id(1)

    qs_ref[...] = _silu(_causal_conv(q_ref, cwq_ref, cbq_ref)) * (dk ** -0.5)
    ks_s_ref[...] = _silu(_causal_conv(k_ref, cwk_ref, cbk_ref))
    cst_ref[...] = jnp.zeros_like(cst_ref)
    nst_ref[...] = jnp.zeros_like(nst_ref)

    row = lax.broadcasted_iota(jnp.int32, (c, c), 0)
    col = lax.broadcasted_iota(jnp.int32, (c, c), 1)
    causal = col <= row
    head = col == h
    cn = cn_ref[...]

    def pick(x):
        return jnp.sum(jnp.where(head, x, 0.0), axis=1, keepdims=True)

    dec_all = dec_ref[0]
    chunk_id = lax.broadcasted_iota(jnp.int32, dec_all.shape, 0)
    head_lane = lax.broadcasted_iota(jnp.int32, (1, LANES), 1) == h
    head_row = lax.broadcasted_iota(jnp.int32, (SUBLANES, c), 0) == h

    def chunk(ci, carry):
        r0 = pl.multiple_of(ci * c, c)
        q = qs_ref[pl.ds(r0, c), :]
        k = ks_s_ref[pl.ds(r0, c), :]
        v = v_ref[0, pl.ds(r0, c), :]
        mu = pick(mu_ref[0, pl.ds(r0, c), :])
        g_inter = pick(gi_ref[0, pl.ds(r0, c), :])
        e_negm = pick(en_ref[0, pl.ds(r0, c), :])
        kscale = pick(ks_ref[0, pl.ds(r0, c), :])
        decay = jnp.sum(jnp.where(chunk_id == ci, dec_all, 0.0), axis=0, keepdims=True)
        decay = jnp.sum(jnp.where(head_lane, decay, 0.0), axis=1, keepdims=True)
        r_row = jnp.sum(jnp.where(head_row, rt_ref[0, :, pl.ds(r0, c)], 0.0),
                        axis=0, keepdims=True)
        wts = jnp.where(causal, jnp.exp(jnp.minimum(r_row - mu, 0.0)), 0.0)
        qk = _dot_nt(q, k)
        kw = k * kscale
        upd = _dot_tn(kw, v)

        cst = cst_ref[...]
        n_st = nst_ref[...]
        sc = qk * wts
        inter = _dot(q, cst)
        intra = _dot(sc, v)
        cst_ref[...] = decay * cst + upd
        nst_ref[...] = decay * n_st + jnp.sum(kw, axis=0, keepdims=True)

        num = g_inter * inter + intra
        den = (g_inter * jnp.sum(q * n_st, axis=1, keepdims=True)
               + jnp.sum(sc, axis=1, keepdims=True))
        inv = 1.0 / jnp.maximum(jnp.abs(den), e_negm)
        msq = jnp.mean(num * num, axis=-1, keepdims=True)
        yn = num * (inv * lax.rsqrt(inv * inv * msq + EPS)) * cn
        og = og_ref[0, pl.ds(r0, c), :]
        z = z_ref[0, pl.ds(r0, c), :]
        o_ref[0, pl.ds(r0, c), :] = (yn * _sigmoid(og) * _silu(z)).astype(o_ref.dtype)
        return carry

    lax.fori_loop(0, s_len // c, chunk, 0, unroll=ML_UNROLL)


def _mlstm(u, gate_terms, conv_w, conv_b, c_norm, *, heads, dk, dv, casts=()):
    bsz, s_len, _ = u.shape
    grid = (bsz, heads)
    body, c_in, c_out, c_shapes = _with_casts(_mlstm_kernel, 16, 1, casts, grid)
    nc = s_len // ML_CHUNK
    col_spec = pl.BlockSpec((1, s_len, LANES), lambda b, h: (b, 0, 0))
    qk_blk = lambda off: pl.BlockSpec((1, s_len, dk), lambda b, h, off=off: (b, 0, off + h))
    v_blk = lambda off: pl.BlockSpec((1, s_len, dv), lambda b, h, off=off: (b, 0, off + h))
    cw = lambda off: pl.BlockSpec((CONV_W, dk), lambda b, h, off=off: (0, off + h))
    cb = lambda off: pl.BlockSpec((1, dk), lambda b, h, off=off: (0, off + h))
    nqk = 2 * heads * dk // dv
    return pl.pallas_call(
        body,
        grid=grid,
        in_specs=[qk_blk(0), qk_blk(heads),
                  v_blk(nqk), v_blk(nqk + heads), v_blk(nqk + 2 * heads),
                  col_spec, col_spec, col_spec, col_spec,
                  pl.BlockSpec((1, nc, LANES), lambda b, h: (b, 0, 0)),
                  pl.BlockSpec((1, SUBLANES, s_len), lambda b, h: (b, 0, 0)),
                  cw(0), cb(0), cw(heads), cb(heads),
                  pl.BlockSpec((1, dv), lambda b, h: (0, h))] + c_in,
        out_specs=[pl.BlockSpec((1, s_len, dv), lambda b, h: (b, 0, h))] + c_out,
        out_shape=[jax.ShapeDtypeStruct((bsz, s_len, heads * dv), BF16)] + c_shapes,
        scratch_shapes=[pltpu.VMEM((s_len, dk), F32), pltpu.VMEM((s_len, dk), F32),
                        pltpu.VMEM((dk, dv), F32), pltpu.VMEM((1, dk), F32)],
        compiler_params=pltpu.CompilerParams(
            dimension_semantics=("parallel", "parallel"), vmem_limit_bytes=VMEM_LIMIT),
        name="mlstm",
    )(u, u, u, u, u, *gate_terms, conv_w, conv_b.reshape(1, -1), conv_w,
      conv_b.reshape(1, -1), c_norm.reshape(1, -1), *casts)


OUT_SUB_ROWS = 256


def _out_kernel(h_ref, *rest, n_y, final):
    y_refs = rest[:n_y]
    wo_refs = rest[n_y:2 * n_y]
    p_ref, pw_ref, pn_ref, gw_ref, norm_ref = rest[2 * n_y:2 * n_y + 5]
    out_refs = rest[2 * n_y + 5:]

    def rows_block(r0):
        rows = slice(r0, r0 + OUT_SUB_ROWS)
        mix = jnp.dot(y_refs[0][rows, :], wo_refs[0][...], preferred_element_type=F32)
        for y_ref, wo_ref in zip(y_refs[1:], wo_refs[1:]):
            mix = mix + jnp.dot(y_ref[rows, :], wo_ref[...], preferred_element_type=F32)
        yield
        h1 = h_ref[rows, :] + mix
        pe = _dot(p_ref[rows, :], pw_ref[...])
        gate = _dot(h1, gw_ref[...])
        yield
        pe = pe * lax.rsqrt(jnp.mean(pe * pe, axis=-1, keepdims=True) + EPS) * pn_ref[...]
        h2 = h1 + _sigmoid(gate) * pe
        hn = h2 * lax.rsqrt(jnp.mean(h2 * h2, axis=-1, keepdims=True) + EPS) * norm_ref[...]
        if final:
            out_refs[0][rows, :] = hn
        else:
            out_refs[0][rows, :] = h2
            out_refs[1][rows, :] = hn.astype(out_refs[1].dtype)

    stages = [rows_block(r0) for r0 in range(0, h_ref.shape[0], OUT_SUB_ROWS)]
    while stages:
        stages = [g for g in stages if next(g, stages) is not stages]


def _out_proj(h, ys, w_out, p, layer, ple_w, ple_norm, gate_w, norm, final, *, tm=512):
    m, d = h.shape
    const = lambda shape: pl.BlockSpec(shape, lambda i: (0, 0), pipeline_mode=pl.Buffered(1))
    in_specs = [pl.BlockSpec((tm, d), lambda i: (i, 0))]
    in_specs += [pl.BlockSpec((tm, y.shape[1]), lambda i: (i, 0)) for y in ys]
    assert all(y.shape[1] == ys[0].shape[1] for y in ys)
    in_specs += [pl.BlockSpec((y.shape[1], d), lambda i, r=r: (r, 0), pipeline_mode=pl.Buffered(1))
                 for r, y in enumerate(ys)]
    stacked = lambda a: pl.BlockSpec((None,) + a.shape[1:], lambda i: (layer, 0, 0),
                                     pipeline_mode=pl.Buffered(1))
    in_specs += [pl.BlockSpec((None, tm, p.shape[2]), lambda i: (layer, i, 0)),
                 stacked(ple_w), const((1, d)), stacked(gate_w), const((1, d))]
    args = [h, *ys, *([w_out] * len(ys)), p, ple_w, ple_norm.reshape(1, d), gate_w,
            norm.reshape(1, d)]
    row_spec = pl.BlockSpec((tm, d), lambda i: (i, 0))
    return pl.pallas_call(
        functools.partial(_out_kernel, n_y=len(ys), final=final),
        grid=(m // tm,),
        in_specs=in_specs,
        out_specs=row_spec if final else [row_spec, row_spec],
        out_shape=(jax.ShapeDtypeStruct((m, d), F32) if final else
                   [jax.ShapeDtypeStruct((m, d), F32), jax.ShapeDtypeStruct((m, d), BF16)]),
        compiler_params=pltpu.CompilerParams(
            dimension_semantics=("parallel",), vmem_limit_bytes=VMEM_LIMIT),
        name="out_proj_final" if final else "out_proj",
    )(*args)


def kernel(x, p, e_norm, e_w_in, a_lb_logits, a_norm, b_conv_w, b_conv_b, b_w_r, b_b_r, b_w_i, b_b_i, b_lambda, e_w_out, o_norm, o_w_in, c_conv_w, c_conv_b, c_b_i, c_b_f, c_norm, o_w_out, ple_w, ple_norm, ple_gate_w, final_norm):
    bsz, s_len, d = x.shape
    depth = p.shape[0]
    m = bsz * s_len
    a_width = a_norm.shape[1]
    b_width = b_lambda.shape[1]
    a_dk = LANES
    a_heads = a_width // a_dk
    c_heads = c_b_i.shape[1]
    c_dv = c_norm.shape[1] // c_heads
    c_dk = c_conv_w.shape[2] // (2 * c_heads)
    c_main = o_w_in.shape[2] - 2 * c_heads

    h = x.reshape(m, d)
    ready = {}
    take = lambda name, w: ready.pop(name) if name in ready else w.astype(BF16)
    w_in_t = lambda j: o_w_in[j].T
    hn = None
    layer_norm = lambda i: e_norm[i // 2] if i % 2 == 0 else o_norm[i // 2]
    for i in range(depth):
        j = i // 2
        last = i == depth - 1
        xin, g = (h, layer_norm(i)) if hn is None else (hn, None)
        if i % 2 == 0:
            a_qk = a_heads * a_dk
            spans = (("silu", a_qk), ("hgate", a_qk), ("id", a_width), ("silu", a_width),
                     ("id", b_width), ("silu", b_width))
            u = _in_proj(xin, g, take(("e_w_in", j), e_w_in[j]), spans=spans,
                         lb_logits=a_lb_logits, layer=j).reshape(bsz, s_len, -1)
            jobs = {("o_w_in", j): w_in_t(j)} if not last else {}
            if i == 0:
                jobs.update(gate_w=ple_gate_w.reshape(depth * d, d))
            ya, *done = _hgrn2(u, a_norm[j], heads=a_heads, dk=a_dk, casts=list(jobs.values()))
            ready.update(zip(jobs, done))
            jobs = {("e_w_out", j): e_w_out[j]}
            if i == 0:
                jobs.update(ple_w=ple_w.reshape(-1, d))
            x_off = (2 * a_heads * a_dk + 2 * a_width) // LANES
            yb, *done = _rglru(u, b_conv_w[j], b_conv_b[j], b_w_r[j].astype(BF16), b_b_r[j],
                               b_w_i[j].astype(BF16), b_b_i[j], b_lambda[j], x_off=x_off,
                               z_off=x_off + b_width // LANES, casts=list(jobs.values()))
            ready.update(zip(jobs, done))
            if i == 0:
                gate_w_bf = ready.pop("gate_w").reshape(depth, d, d)
                ple_w_bf = ready.pop("ple_w").reshape(ple_w.shape)
            w_out = take(("e_w_out", j), e_w_out[j])
            ys = [ya.reshape(m, a_width), yb.reshape(m, b_width)]
        else:
            pad = (0, LANES - c_heads)
            gate_rows = lax.optimization_barrier(o_w_in[j][:, c_main:]).T
            wg_t = jnp.concatenate([jnp.pad(gate_rows[:c_heads], (pad, (0, 0))),
                                    jnp.pad(gate_rows[c_heads:], (pad, (0, 0)))]).astype(BF16)
            u, gates = _in_proj(xin, g, take(("o_w_in", j), w_in_t(j)[:c_main]), wg_t,
                                spans=(("id", c_main),), transposed=True)
            gate_bias = jnp.concatenate([jnp.pad(c_b_i[j], pad),
                                         jnp.pad(c_b_f[j], pad)]).reshape(1, 2 * LANES)
            gate_terms = _mlstm_gates(gates.reshape(bsz, s_len, 2 * LANES), gate_bias)
            jobs = {("o_w_out", j): o_w_out[j]}
            if not last:
                jobs[("e_w_in", j + 1)] = e_w_in[j + 1]
            yc, *done = _mlstm(u.reshape(bsz, s_len, -1), gate_terms, c_conv_w[j], c_conv_b[j],
                               c_norm[j], heads=c_heads, dk=c_dk, dv=c_dv,
                               casts=list(jobs.values()))
            ready.update(zip(jobs, done))
            ys = [yc.reshape(m, -1)]
            w_out = take(("o_w_out", j), o_w_out[j])
        res = _out_proj(h, ys, w_out, p.reshape(depth, m, -1), i, ple_w_bf, ple_norm[i],
                        gate_w_bf, final_norm if last else layer_norm(i + 1), last)
        h, hn = (res, None) if last else res
    return h.reshape(bsz, s_len, d)
```

```python
import functools

import jax
import jax.numpy as jnp
from jax import lax
from jax.experimental import pallas as pl
from jax.experimental.pallas import tpu as pltpu

EPS = 1e-6
LRU_C = 8.0
LOG2E = 1.4426950408889634
CONV_W = 4
LANES = 128
SUBLANES = 8
VMEM_LIMIT = 56 * 1024 * 1024
BF16 = jnp.bfloat16
F32 = jnp.float32


def _sigmoid(x):
    return jax.nn.sigmoid(x)


def _silu(x):
    return x * jax.nn.sigmoid(x)


def _softplus(x):
    return jnp.maximum(x, 0.0) + jnp.log1p(jnp.exp(-jnp.abs(x)))


def _log_sigmoid(x):
    return -_softplus(-x)


def _shift_rows(x, s, row):
    return jnp.where(row >= s, pltpu.roll(x, s, 0), 0.0)


def _causal_conv(x_ref, cw_ref, cb_ref):
    s_len = x_ref.shape[1]
    taps = [cw_ref[k:k + 1, :] for k in range(CONV_W)]
    acc = cb_ref[...] + x_ref[0, SUBLANES:, :] * taps[CONV_W - 1]
    for j in range(1, CONV_W):
        acc = acc + x_ref[0, SUBLANES - j:s_len - j, :] * taps[CONV_W - 1 - j]
    x0 = x_ref[0, 0:SUBLANES, :]
    row = lax.broadcasted_iota(jnp.int32, x0.shape, 0)
    acc0 = cb_ref[...] + x0 * taps[CONV_W - 1]
    for j in range(1, CONV_W):
        acc0 = acc0 + _shift_rows(x0, j, row) * taps[CONV_W - 1 - j]
    return jnp.concatenate([acc0, acc], axis=0)


def _with_casts(body, n_in, n_out, casts, grid):
    steps = grid[0] * grid[1]

    def kernel(*refs):
        ins, refs = refs[:n_in], refs[n_in:]
        cast_in, refs = refs[:len(casts)], refs[len(casts):]
        outs, refs = refs[:n_out], refs[n_out:]
        cast_out, scratch = refs[:len(casts)], refs[len(casts):]
        for src, dst in zip(cast_in, cast_out):
            dst[...] = src[...].astype(dst.dtype)
        body(*ins, *outs, *scratch)

    rows = [w.shape[0] // (steps * 16) * 16 for w in casts]
    specs = [pl.BlockSpec((r, w.shape[1]), lambda a, b: (a * grid[1] + b, 0))
             for r, w in zip(rows, casts)]
    shapes = [jax.ShapeDtypeStruct((r * steps, w.shape[1]), BF16) for r, w in zip(rows, casts)]
    return kernel, specs, list(specs), shapes


def _dot(a, b):
    return jnp.dot(a.astype(BF16), b.astype(BF16), preferred_element_type=F32)


def _dot_nt(a, b):
    return lax.dot_general(a.astype(BF16), b.astype(BF16), (((1,), (1,)), ((), ())),
                           preferred_element_type=F32)


def _dot_tn(a, b):
    return lax.dot_general(a.astype(BF16), b.astype(BF16), (((0,), (0,)), ((), ())),
                           preferred_element_type=F32)


SUB_ROWS = 256
COL_PIECE = 512


def _in_proj_kernel(*refs, with_norm, with_gates, transposed, spans, layer):
    kinds = {k for k, _ in spans}
    refs = list(refs)
    x_ref = refs.pop(0)
    g_ref = refs.pop(0) if with_norm else None
    w_ref = refs.pop(0)
    wg_ref = refs.pop(0) if with_gates else None
    lbl_ref = refs.pop(0) if "hgate" in kinds else None
    o_ref = refs.pop(0)
    og_ref = refs.pop(0) if with_gates else None
    hn_ref = refs.pop(0) if with_norm else x_ref
    mm = _dot_nt if transposed else functools.partial(jnp.dot, preferred_element_type=F32)
    tm, tn = o_ref.shape
    j = pl.program_id(1)

    starts, c0 = [], 0
    for kind, width in spans:
        starts.append((kind, c0, width))
        c0 += width
    n = c0

    if lbl_ref is not None:
        lg = lbl_ref[...]
        e = jnp.exp(lg - jnp.max(lg, axis=0, keepdims=True))
        sm = e / jnp.sum(e, axis=0, keepdims=True)
        lb = jnp.sum(sm[:layer + 1], axis=0, keepdims=True)

    def activate(kind, r, c0):
        if kind == "silu":
            return _silu(r)
        if kind == "sigmoid":
            return _sigmoid(r)
        if kind == "hgate":
            lbp = lb[:, c0:c0 + r.shape[1]]
            return lbp + (1.0 - lbp) * _sigmoid(r)
        return r

    def run_tile(jt, blocked):
        tile_c0 = jt * tn
        pieces = []
        for kind, c0, width in starts:
            lo, hi = max(c0, tile_c0), min(c0 + width, tile_c0 + tn)
            for p0 in range(lo, hi, COL_PIECE if blocked else hi - lo):
                pieces.append((kind, p0 - tile_c0, min(COL_PIECE, hi - p0) if blocked else hi - lo,
                               p0 - c0))
        for r0 in range(0, tm, SUB_ROWS if blocked else tm):
            rows = slice(r0, r0 + (SUB_ROWS if blocked else tm))
            if with_norm and jt == 0:
                x = x_ref[rows, :]
                ms = jnp.mean(x * x, axis=-1, keepdims=True)
                hn = (x * lax.rsqrt(ms + EPS) * g_ref[...]).astype(BF16)
                hn_ref[rows, :] = hn
            else:
                hn = hn_ref[rows, :]
            for kind, off, width, c0 in pieces:
                wp = w_ref[off:off + width, :] if transposed else w_ref[:, off:off + width]
                o_ref[rows, off:off + width] = activate(kind, mm(hn, wp), c0)
        if with_gates and jt == 0:
            og_ref[...] = mm(hn_ref[...], wg_ref[...])

    if kinds == {"id"}:
        pl.when(j == 0)(functools.partial(run_tile, 0, with_norm))
        pl.when(j != 0)(functools.partial(run_tile, 1, False))
    else:
        for jt in range(n // tn):
            pl.when(j == jt)(functools.partial(run_tile, jt, True))


IN_PROJ_ROWS = 1024
IN_PROJ_VMEM = 46 * 1024 * 1024


def _in_proj_cols(n, tm, d, gw, x_bytes, with_norm):
    for tn in range(n, 0, -512):
        need = (2 * tm * d * x_bytes + (tm * d * 2 if with_norm else 0) + 2 * d * tn * 2
                + 2 * tm * tn * 4 + 2 * (d * gw * 2 + tm * gw * 4))
        if n % tn == 0 and need <= IN_PROJ_VMEM:
            return tn
    raise ValueError("no column tile fits")


def _in_proj(x, g, w, wg=None, *, spans, transposed=False, lb_logits=None, layer=0,
             tm=IN_PROJ_ROWS):
    m, d = x.shape
    n = sum(width for _, width in spans)
    with_gates = wg is not None
    with_norm = g is not None
    gw = wg.shape[0 if transposed else 1] if with_gates else 0
    w_spec = (lambda cols, idx: pl.BlockSpec((cols, d), lambda i, j: (idx(j), 0)) if transposed
              else pl.BlockSpec((d, cols), lambda i, j: (0, idx(j))))
    const = lambda a: pl.BlockSpec(a.shape, lambda i, j: (0,) * a.ndim)
    tn = _in_proj_cols(n, tm, d, gw, x.dtype.itemsize, with_norm)
    in_specs = [pl.BlockSpec((tm, d), lambda i, j: (i, 0))]
    args = [x]
    if with_norm:
        args.append(g.reshape(1, d))
        in_specs.append(const(args[-1]))
    in_specs.append(w_spec(tn, lambda j: j))
    args.append(w)
    out_shape = [jax.ShapeDtypeStruct((m, n), F32)]
    out_specs = [pl.BlockSpec((tm, tn), lambda i, j: (i, j))]
    if with_gates:
        in_specs.append(w_spec(gw, lambda j: 0))
        out_shape.append(jax.ShapeDtypeStruct((m, gw), F32))
        out_specs.append(pl.BlockSpec((tm, gw), lambda i, j: (i, 0)))
        args.append(wg)
    if lb_logits is not None:
        args.append(lb_logits)
        in_specs.append(const(lb_logits))
    res = pl.pallas_call(
        functools.partial(_in_proj_kernel, with_norm=with_norm, with_gates=with_gates,
                          transposed=transposed, spans=tuple(spans), layer=layer),
        grid=(m // tm, n // tn),
        in_specs=in_specs,
        out_specs=out_specs,
        out_shape=out_shape,
        scratch_shapes=[pltpu.VMEM((tm, d), BF16)] if with_norm else [],
        compiler_params=pltpu.CompilerParams(
            dimension_semantics=("parallel", "arbitrary"), vmem_limit_bytes=VMEM_LIMIT),
        name="in_proj_gates" if with_gates else "in_proj",
    )(*args)
    return res if with_gates else res[0]


HG_CHUNK = 128
HG_GROUPS = HG_CHUNK // SUBLANES
HG_INFLIGHT = 8


def _hgrn2_kernel(q_ref, f_ref, i_ref, z_ref, an_ref, o_ref,
                  st_ref, qs_ref, ks_ref, bs_ref, od_ref, tri_ref):
    s_len = q_ref.shape[1]
    dk = q_ref.shape[2]
    c = HG_CHUNK
    ng = HG_GROUPS
    assert c == dk == LANES and i_ref.shape[2] == dk

    an = an_ref[...]

    rowc = lax.broadcasted_iota(jnp.int32, (c, c), 0)
    colc = lax.broadcasted_iota(jnp.int32, (c, c), 1)
    tri_ref[...] = (colc <= rowc).astype(BF16)
    same_group = (colc % ng) == (rowc % ng)
    lane = lax.broadcasted_iota(jnp.int32, (SUBLANES, c), 1)
    diag_off = lane - lax.broadcasted_iota(jnp.int32, (SUBLANES, c), 0)
    zeros8 = jnp.zeros((SUBLANES, dk), F32)
    zeros16 = jnp.zeros((ng, dk), F32)

    st_ref[...] = jnp.zeros_like(st_ref)

    def rows8(x, r):
        return x[r * SUBLANES:(r + 1) * SUBLANES, :]

    def chunk(ci, slot):
        qs, ks, bs, od = qs_ref.at[slot], ks_ref.at[slot], bs_ref.at[slot], od_ref.at[slot]
        r0 = pl.multiple_of(ci * c, c)
        q = q_ref[0, pl.ds(r0, c), :]
        f = f_ref[0, pl.ds(r0, c), :]
        v = i_ref[0, pl.ds(r0, c), :]
        k = 1.0 - f

        lf = jnp.log2(f)
        hi = lf.astype(BF16)
        r1 = lf - hi.astype(F32)
        mid = r1.astype(BF16)
        lo = (r1 - mid.astype(F32)).astype(BF16)
        bb = jnp.dot(tri_ref[...], jnp.concatenate([hi, mid, lo], axis=1),
                     preferred_element_type=F32)
        yield
        b2 = bb[:, :dk] + bb[:, dk:2 * dk] + bb[:, 2 * dk:]
        qs[...] = q
        ks[...] = k
        bs[...] = b2
        b_last = bs[c - 1:c, :]
        upd = _dot_tn(v, k * jnp.exp2(b_last - b2))

        ps, rights = [], []
        for j in range(ng.bit_length() - 1):
            lhs, rhs, rights_j = [], [], []
            for r in range(ng):
                mid_grp = ((r >> (j + 1)) << (j + 1)) | ((1 << j) - 1)
                beta = bs[mid_grp * SUBLANES + SUBLANES - 1:(mid_grp + 1) * SUBLANES, :]
                if (r >> j) & 1:
                    lhs.append(rows8(q, r) * jnp.exp2(rows8(b2, r) - beta))
                    rhs.append(zeros8)
                    rights_j.append(r)
                else:
                    rhs.append(rows8(k, r) * jnp.exp2(beta - rows8(b2, r)))
            ps.append(_dot_nt(jnp.concatenate(lhs, axis=0), jnp.concatenate(rhs, axis=0)))
            rights.append(rights_j)
        yield

        st = st_ref[...]
        o = _dot_nt(q * jnp.exp2(b2), st)
        st_ref[...] = st * jnp.exp2(b_last) + upd

        def permuted(ref):
            return [ref[pl.ds(pos, ng, stride=SUBLANES), :] for pos in range(SUBLANES)]

        q_p, k_p, b_p = permuted(qs), permuted(ks), permuted(bs)
        lhs_slots, rhs_slots = [], []
        for j in range(SUBLANES.bit_length() - 1):
            for blk in range(SUBLANES >> (j + 1)):
                mid_pos = (blk << (j + 1)) | ((1 << j) - 1)
                lhs, rhs = [], []
                for pos in range(SUBLANES):
                    if pos >> (j + 1) != blk:
                        lhs.append(zeros16)
                        rhs.append(zeros16)
                    elif (pos >> j) & 1:
                        lhs.append(q_p[pos] * jnp.exp2(b_p[pos] - b_p[mid_pos]))
                        rhs.append(zeros16)
                    else:
                        lhs.append(zeros16)
                        rhs.append(k_p[pos] * jnp.exp2(b_p[mid_pos] - b_p[pos]))
                lhs_slots.append(jnp.concatenate(lhs, axis=0).astype(BF16))
                rhs_slots.append(jnp.concatenate(rhs, axis=0).astype(BF16))
        pd = _dot_nt(jnp.concatenate(lhs_slots, axis=1), jnp.concatenate(rhs_slots, axis=1))
        yield

        dg = jnp.sum(q * k, axis=1, keepdims=True)
        a_rows = [jnp.where(diag_off == r * SUBLANES, rows8(dg, r), 0.0) for r in range(ng)]
        for j, (p, rights_j) in enumerate(zip(ps, rights)):
            for i, r in enumerate(rights_j):
                right_start = (((r >> (j + 1)) << (j + 1)) | (1 << j)) * SUBLANES
                a_rows[r] = jnp.where(lane < right_start, rows8(p, i), a_rows[r])
        o = o + _dot(jnp.concatenate(a_rows, axis=0), v)
        v_p = [i_ref[0, pl.ds(r0 + pos, ng, stride=SUBLANES), :] for pos in range(SUBLANES)]
        od[...] = _dot(jnp.where(same_group, pd, 0.0), jnp.concatenate(v_p, axis=0))
        yield

        o = o + jnp.concatenate(
            [od[pl.ds(g, SUBLANES, stride=ng), :] for g in range(ng)], axis=0)
        yn = o * lax.rsqrt(jnp.mean(o * o, axis=-1, keepdims=True) + EPS) * an
        z = z_ref[0, pl.ds(r0, c), :]
        o_ref[0, pl.ds(r0, c), :] = (yn * z).astype(o_ref.dtype)

    def chunks(i, carry):
        stages = [chunk(i * HG_INFLIGHT + slot, slot) for slot in range(HG_INFLIGHT)]
        while stages:
            stages = [g for g in stages if next(g, stages) is not stages]
        return carry

    lax.fori_loop(0, s_len // (c * HG_INFLIGHT), chunks, 0)


def _hgrn2(u, a_norm, *, heads, dk, casts=()):
    bsz, s_len, _ = u.shape
    blk = lambda off: pl.BlockSpec((1, s_len, dk), lambda b, h, off=off: (b, 0, off + h))
    grid = (bsz, heads)
    body, c_in, c_out, c_shapes = _with_casts(_hgrn2_kernel, 5, 1, casts, grid)
    return pl.pallas_call(
        body,
        grid=grid,
        in_specs=[blk(0), blk(heads), blk(2 * heads), blk(3 * heads),
                  pl.BlockSpec((1, dk), lambda b, h: (0, h))] + c_in,
        out_specs=[pl.BlockSpec((1, s_len, dk), lambda b, h: (b, 0, h))] + c_out,
        out_shape=[jax.ShapeDtypeStruct((bsz, s_len, heads * dk), BF16)] + c_shapes,
        scratch_shapes=[pltpu.VMEM((dk, dk), F32)]
        + [pltpu.VMEM((HG_INFLIGHT, HG_CHUNK, dk), F32)] * 4
        + [pltpu.VMEM((HG_CHUNK, HG_CHUNK), BF16)],
        compiler_params=pltpu.CompilerParams(
            dimension_semantics=("parallel", "parallel"), vmem_limit_bytes=VMEM_LIMIT),
        name="hgrn2",
    )(u, u, u, u, a_norm.reshape(1, -1), *casts)


def _rglru_kernel(x_ref, z_ref, cw_ref, cb_ref, wr_ref, br_ref, wi_ref, bi_ref, lam_ref,
                  o_ref, a_ref, u_ref, h_ref, ca_ref, cu_ref):
    s_len = x_ref.shape[1]
    w = x_ref.shape[2]
    ng = s_len // SUBLANES
    xc = _causal_conv(x_ref, cw_ref, cb_ref)
    r = _sigmoid(_dot(xc, wr_ref[0]) + br_ref[...])
    ig = _sigmoid(_dot(xc, wi_ref[0]) + bi_ref[...])
    log_a = -LRU_C * r * _softplus(-lam_ref[...])
    a = jnp.exp(log_a)
    a_ref[...] = a
    t = -jnp.tanh(log_a) * (a * a + 1.0)
    u_ref[...] = jnp.where(t > 0.0, t * lax.rsqrt(t), 0.0) * (ig * xc)

    ca = cu = None
    for pos in range(SUBLANES):
        ap = a_ref[pl.ds(pos, ng, stride=SUBLANES), :]
        up = u_ref[pl.ds(pos, ng, stride=SUBLANES), :]
        ca, cu = (ap, up) if pos == 0 else (ap * ca, ap * cu + up)
        ca_ref[pos] = ca
        cu_ref[pos] = cu
    grp = lax.broadcasted_iota(jnp.int32, (ng, w), 0)
    ta, tu = ca, cu
    s = 1
    while s < ng:
        m = grp >= s
        tu = tu + ta * jnp.where(m, pltpu.roll(tu, s, 0), 0.0)
        ta = ta * jnp.where(m, pltpu.roll(ta, s, 0), 1.0)
        s *= 2
    h_in = jnp.where(grp >= 1, pltpu.roll(tu, 1, 0), 0.0)
    for pos in range(SUBLANES):
        h_ref[pl.ds(pos, ng, stride=SUBLANES), :] = ca_ref[pos] * h_in + cu_ref[pos]
    o_ref[0] = (h_ref[...] * z_ref[0]).astype(o_ref.dtype)


def _rglru(u, conv_w, conv_b, w_r, b_r, w_i, b_i, lam, *, x_off, z_off, casts=()):
    bsz, s_len, _ = u.shape
    nblk, blk, _ = w_r.shape
    vec = lambda rows: pl.BlockSpec((rows, blk), lambda b, n: (0, n))
    mat = pl.BlockSpec((1, blk, blk), lambda b, n: (n, 0, 0))
    grid = (bsz, nblk)
    body, c_in, c_out, c_shapes = _with_casts(_rglru_kernel, 9, 1, casts, grid)
    return pl.pallas_call(
        body,
        grid=grid,
        in_specs=[pl.BlockSpec((1, s_len, blk), lambda b, n: (b, 0, x_off + n)),
                  pl.BlockSpec((1, s_len, blk), lambda b, n: (b, 0, z_off + n)),
                  vec(CONV_W), vec(1), mat, vec(1), mat, vec(1), vec(1)] + c_in,
        out_specs=[pl.BlockSpec((1, s_len, blk), lambda b, n: (b, 0, n))] + c_out,
        out_shape=[jax.ShapeDtypeStruct((bsz, s_len, nblk * blk), BF16)] + c_shapes,
        scratch_shapes=[pltpu.VMEM((s_len, blk), F32)] * 3
        + [pltpu.VMEM((SUBLANES, s_len // SUBLANES, blk), F32)] * 2,
        compiler_params=pltpu.CompilerParams(
            dimension_semantics=("parallel", "parallel"), vmem_limit_bytes=VMEM_LIMIT),
        name="rglru",
    )(u, u, conv_w, conv_b.reshape(1, -1), w_r, b_r.reshape(1, -1), w_i, b_i.reshape(1, -1),
      lam.reshape(1, -1), *casts)


ML_CHUNK = 128
ML_UNROLL = 8


def _mlstm_gates_kernel(g_ref, gb_ref, mu_ref, gi_ref, en_ref, ks_ref, dec_ref, rt_ref):
    s_len = g_ref.shape[1]
    c = ML_CHUNK
    t = g_ref[0] + gb_ref[...]
    li = t[:, :LANES]
    lf = _log_sigmoid(t[:, LANES:])
    row = lax.broadcasted_iota(jnp.int32, (s_len, LANES), 0) % c

    def scan(x, op, fill):
        s = 1
        while s < c:
            x = op(x, jnp.where(row >= s, pltpu.roll(x, s, 0), fill))
            s *= 2
        return x

    b = scan(lf, jnp.add, 0.0)
    r = li - b
    rho = scan(r, jnp.maximum, -jnp.inf)
    m = jnp.zeros((1, LANES), F32)
    for ci in range(s_len // c):
        sl = slice(ci * c, (ci + 1) * c)
        b_c, r_c, rho_c = b[sl], r[sl], rho[sl]
        b_last = b_c[c - 1:c, :]
        m_new = b_last + jnp.maximum(m, rho_c[c - 1:c, :])
        mu = jnp.maximum(rho_c, m)
        mu_ref[0, sl, :] = mu * LOG2E
        gi_ref[0, sl, :] = jnp.exp(m - mu)
        en_ref[0, sl, :] = jnp.exp(-(b_c + mu))
        ks_ref[0, sl, :] = jnp.exp(b_last + r_c - m_new)
        dec_ref[0, ci:ci + 1, :] = jnp.exp(b_last + m - m_new)
        rt_ref[0, :, sl] = (r_c * LOG2E).T
        m = m_new


def _mlstm_gates(gates, gate_bias):
    bsz, s_len, gw = gates.shape
    nc = s_len // ML_CHUNK
    col = jax.ShapeDtypeStruct((bsz, s_len, LANES), F32)
    col_spec = pl.BlockSpec((1, s_len, LANES), lambda b: (b, 0, 0))
    return pl.pallas_call(
        _mlstm_gates_kernel,
        grid=(bsz,),
        in_specs=[pl.BlockSpec((1, s_len, gw), lambda b: (b, 0, 0)),
                  pl.BlockSpec((1, gw), lambda b: (0, 0))],
        out_specs=[col_spec] * 4 + [pl.BlockSpec((1, nc, LANES), lambda b: (b, 0, 0)),
                                    pl.BlockSpec((1, LANES, s_len), lambda b: (b, 0, 0))],
        out_shape=[col] * 4 + [jax.ShapeDtypeStruct((bsz, nc, LANES), F32),
                               jax.ShapeDtypeStruct((bsz, LANES, s_len), F32)],
        compiler_params=pltpu.CompilerParams(
            dimension_semantics=("parallel",), vmem_limit_bytes=VMEM_LIMIT),
        name="mlstm_gates",
    )(gates, gate_bias)


def _mlstm_kernel(q_ref, k_ref, v_ref, og_ref, z_ref, mu_ref, gi_ref, en_ref, ks_ref, dec_ref,
                  rt_ref, cwq_ref, cbq_ref, cwk_ref, cbk_ref, cn_ref, o_ref, qs_ref, ks_s_ref,
                  cst_ref, nst_ref):
    s_len = q_ref.shape[1]
    dk = q_ref.shape[2]
    c = ML_CHUNK
    assert c == dk == LANES
    h = pl.program_id(1)

    qs_ref[...] = _silu(_causal_conv(q_ref, cwq_ref, cbq_ref)) * (dk ** -0.5)
    ks_s_ref[...] = _silu(_causal_conv(k_ref, cwk_ref, cbk_ref))
    cst_ref[...] = jnp.zeros_like(cst_ref)
    nst_ref[...] = jnp.zeros_like(nst_ref)

    row = lax.broadcasted_iota(jnp.int32, (c, c), 0)
    col = lax.broadcasted_iota(jnp.int32, (c, c), 1)
    causal = col <= row
    head = col == h
    cn = cn_ref[...]

    def pick(x):
        return jnp.sum(jnp.where(head, x, 0.0), axis=1, keepdims=True)

    dec_all = dec_ref[0]
    chunk_id = lax.broadcasted_iota(jnp.int32, dec_all.shape, 0)
    head_lane = lax.broadcasted_iota(jnp.int32, (1, LANES), 1) == h
    head_row = lax.broadcasted_iota(jnp.int32, (SUBLANES, c), 0) == h

    def chunk(ci, carry):
        r0 = pl.multiple_of(ci * c, c)
        q = qs_ref[pl.ds(r0, c), :]
        k = ks_s_ref[pl.ds(r0, c), :]
        v = v_ref[0, pl.ds(r0, c), :]
        mu = pick(mu_ref[0, pl.ds(r0, c), :])
        g_inter = pick(gi_ref[0, pl.ds(r0, c), :])
        e_negm = pick(en_ref[0, pl.ds(r0, c), :])
        kscale = pick(ks_ref[0, pl.ds(r0, c), :])
        decay = jnp.sum(jnp.where(chunk_id == ci, dec_all, 0.0), axis=0, keepdims=True)
        decay = jnp.sum(jnp.where(head_lane, decay, 0.0), axis=1, keepdims=True)
        r_row = jnp.sum(jnp.where(head_row, rt_ref[0, :, pl.ds(r0, c)], 0.0),
                        axis=0, keepdims=True)
        wts = jnp.where(causal, jnp.exp2(r_row - mu), 0.0)
        qk = _dot_nt(q, k)
        kw = k * kscale
        upd = _dot_tn(kw, v)

        cst = cst_ref[...]
        n_st = nst_ref[...]
        sc = qk * wts
        inter = _dot(q, cst)
        intra = _dot(sc, v)
        cst_ref[...] = decay * cst + upd
        nst_ref[...] = decay * n_st + jnp.sum(kw, axis=0, keepdims=True)

        num = g_inter * inter + intra
        den = (g_inter * jnp.sum(q * n_st, axis=1, keepdims=True)
               + jnp.sum(sc, axis=1, keepdims=True))
        inv = 1.0 / jnp.maximum(jnp.abs(den), e_negm)
        msq = jnp.mean(num * num, axis=-1, keepdims=True)
        yn = num * (inv * lax.rsqrt(inv * inv * msq + EPS)) * cn
        og = og_ref[0, pl.ds(r0, c), :]
        z = z_ref[0, pl.ds(r0, c), :]
        o_ref[0, pl.ds(r0, c), :] = (yn * _sigmoid(og) * _silu(z)).astype(o_ref.dtype)
        return carry

    lax.fori_loop(0, s_len // c, chunk, 0, unroll=ML_UNROLL)


def _mlstm(u, gate_terms, conv_w, conv_b, c_norm, *, heads, dk, dv, casts=()):
    bsz, s_len, _ = u.shape
    grid = (bsz, heads)
    body, c_in, c_out, c_shapes = _with_casts(_mlstm_kernel, 16, 1, casts, grid)
    nc = s_len // ML_CHUNK
    col_spec = pl.BlockSpec((1, s_len, LANES), lambda b, h: (b, 0, 0))
    qk_blk = lambda off: pl.BlockSpec((1, s_len, dk), lambda b, h, off=off: (b, 0, off + h))
    v_blk = lambda off: pl.BlockSpec((1, s_len, dv), lambda b, h, off=off: (b, 0, off + h))
    cw = lambda off: pl.BlockSpec((CONV_W, dk), lambda b, h, off=off: (0, off + h))
    cb = lambda off: pl.BlockSpec((1, dk), lambda b, h, off=off: (0, off + h))
    nqk = 2 * heads * dk // dv
    return pl.pallas_call(
        body,
        grid=grid,
        in_specs=[qk_blk(0), qk_blk(heads),
                  v_blk(nqk), v_blk(nqk + heads), v_blk(nqk + 2 * heads),
                  col_spec, col_spec, col_spec, col_spec,
                  pl.BlockSpec((1, nc, LANES), lambda b, h: (b, 0, 0)),
                  pl.BlockSpec((1, SUBLANES, s_len), lambda b, h: (b, 0, 0)),
                  cw(0), cb(0), cw(heads), cb(heads),
                  pl.BlockSpec((1, dv), lambda b, h: (0, h))] + c_in,
        out_specs=[pl.BlockSpec((1, s_len, dv), lambda b, h: (b, 0, h))] + c_out,
        out_shape=[jax.ShapeDtypeStruct((bsz, s_len, heads * dv), BF16)] + c_shapes,
        scratch_shapes=[pltpu.VMEM((s_len, dk), F32), pltpu.VMEM((s_len, dk), F32),
                        pltpu.VMEM((dk, dv), F32), pltpu.VMEM((1, dk), F32)],
        compiler_params=pltpu.CompilerParams(
            dimension_semantics=("parallel", "parallel"), vmem_limit_bytes=VMEM_LIMIT),
        name="mlstm",
    )(u, u, u, u, u, *gate_terms, conv_w, conv_b.reshape(1, -1), conv_w,
      conv_b.reshape(1, -1), c_norm.reshape(1, -1), *casts)


OUT_SUB_ROWS = 256


def _out_kernel(h_ref, *rest, n_y, final):
    y_refs = rest[:n_y]
    wo_refs = rest[n_y:2 * n_y]
    p_ref, pw_ref, pn_ref, gw_ref, norm_ref = rest[2 * n_y:2 * n_y + 5]
    out_refs = rest[2 * n_y + 5:]

    def rows_block(r0):
        rows = slice(r0, r0 + OUT_SUB_ROWS)
        mix = jnp.dot(y_refs[0][rows, :], wo_refs[0][...], preferred_element_type=F32)
        for y_ref, wo_ref in zip(y_refs[1:], wo_refs[1:]):
            mix = mix + jnp.dot(y_ref[rows, :], wo_ref[...], preferred_element_type=F32)
        yield
        h1 = h_ref[rows, :] + mix
        pe = _dot(p_ref[rows, :], pw_ref[...])
        gate = _dot(h1, gw_ref[...])
        yield
        pe = pe * lax.rsqrt(jnp.mean(pe * pe, axis=-1, keepdims=True) + EPS) * pn_ref[...]
        h2 = h1 + _sigmoid(gate) * pe
        hn = h2 * lax.rsqrt(jnp.mean(h2 * h2, axis=-1, keepdims=True) + EPS) * norm_ref[...]
        if final:
            out_refs[0][rows, :] = hn
        else:
            out_refs[0][rows, :] = h2
            out_refs[1][rows, :] = hn.astype(out_refs[1].dtype)

    stages = [rows_block(r0) for r0 in range(0, h_ref.shape[0], OUT_SUB_ROWS)]
    while stages:
        stages = [g for g in stages if next(g, stages) is not stages]


def _out_proj(h, ys, w_out, p, layer, ple_w, ple_norm, gate_w, norm, final, *, tm=512):
    m, d = h.shape
    const = lambda shape: pl.BlockSpec(shape, lambda i: (0, 0), pipeline_mode=pl.Buffered(1))
    in_specs = [pl.BlockSpec((tm, d), lambda i: (i, 0))]
    in_specs += [pl.BlockSpec((tm, y.shape[1]), lambda i: (i, 0)) for y in ys]
    assert all(y.shape[1] == ys[0].shape[1] for y in ys)
    in_specs += [pl.BlockSpec((y.shape[1], d), lambda i, r=r: (r, 0), pipeline_mode=pl.Buffered(1))
                 for r, y in enumerate(ys)]
    stacked = lambda a: pl.BlockSpec((None,) + a.shape[1:], lambda i: (layer, 0, 0),
                                     pipeline_mode=pl.Buffered(1))
    in_specs += [pl.BlockSpec((None, tm, p.shape[2]), lambda i: (layer, i, 0)),
                 stacked(ple_w), const((1, d)), stacked(gate_w), const((1, d))]
    args = [h, *ys, *([w_out] * len(ys)), p, ple_w, ple_norm.reshape(1, d), gate_w,
            norm.reshape(1, d)]
    row_spec = pl.BlockSpec((tm, d), lambda i: (i, 0))
    return pl.pallas_call(
        functools.partial(_out_kernel, n_y=len(ys), final=final),
        grid=(m // tm,),
        in_specs=in_specs,
        out_specs=row_spec if final else [row_spec, row_spec],
        out_shape=(jax.ShapeDtypeStruct((m, d), F32) if final else
                   [jax.ShapeDtypeStruct((m, d), F32), jax.ShapeDtypeStruct((m, d), BF16)]),
        compiler_params=pltpu.CompilerParams(
            dimension_semantics=("parallel",), vmem_limit_bytes=VMEM_LIMIT),
        name="out_proj_final" if final else "out_proj",
    )(*args)


def kernel(x, p, e_norm, e_w_in, a_lb_logits, a_norm, b_conv_w, b_conv_b, b_w_r, b_b_r, b_w_i, b_b_i, b_lambda, e_w_out, o_norm, o_w_in, c_conv_w, c_conv_b, c_b_i, c_b_f, c_norm, o_w_out, ple_w, ple_norm, ple_gate_w, final_norm):
    bsz, s_len, d = x.shape
    depth = p.shape[0]
    m = bsz * s_len
    a_width = a_norm.shape[1]
    b_width = b_lambda.shape[1]
    a_dk = LANES
    a_heads = a_width // a_dk
    c_heads = c_b_i.shape[1]
    c_dv = c_norm.shape[1] // c_heads
    c_dk = c_conv_w.shape[2] // (2 * c_heads)
    c_main = o_w_in.shape[2] - 2 * c_heads

    h = x.reshape(m, d)
    ready = {}
    take = lambda name, w: ready.pop(name) if name in ready else w.astype(BF16)
    w_in_t = lambda j: o_w_in[j].T
    hn = None
    layer_norm = lambda i: e_norm[i // 2] if i % 2 == 0 else o_norm[i // 2]
    for i in range(depth):
        j = i // 2
        last = i == depth - 1
        xin, g = (h, layer_norm(i)) if hn is None else (hn, None)
        if i % 2 == 0:
            a_qk = a_heads * a_dk
            spans = (("silu", a_qk), ("hgate", a_qk), ("id", a_width), ("silu", a_width),
                     ("id", b_width), ("silu", b_width))
            u = _in_proj(xin, g, take(("e_w_in", j), e_w_in[j]), spans=spans,
                         lb_logits=a_lb_logits, layer=j).reshape(bsz, s_len, -1)
            jobs = {("o_w_in", j): w_in_t(j)} if not last else {}
            if i == 0:
                jobs.update(gate_w=ple_gate_w.reshape(depth * d, d))
            ya, *done = _hgrn2(u, a_norm[j], heads=a_heads, dk=a_dk, casts=list(jobs.values()))
            ready.update(zip(jobs, done))
            jobs = {("e_w_out", j): e_w_out[j]}
            if i == 0:
                jobs.update(ple_w=ple_w.reshape(-1, d))
            x_off = (2 * a_heads * a_dk + 2 * a_width) // LANES
            yb, *done = _rglru(u, b_conv_w[j], b_conv_b[j], b_w_r[j].astype(BF16), b_b_r[j],
                               b_w_i[j].astype(BF16), b_b_i[j], b_lambda[j], x_off=x_off,
                               z_off=x_off + b_width // LANES, casts=list(jobs.values()))
            ready.update(zip(jobs, done))
            if i == 0:
                gate_w_bf = ready.pop("gate_w").reshape(depth, d, d)
                ple_w_bf = ready.pop("ple_w").reshape(ple_w.shape)
            w_out = take(("e_w_out", j), e_w_out[j])
            ys = [ya.reshape(m, a_width), yb.reshape(m, b_width)]
        else:
            pad = (0, LANES - c_heads)
            gate_rows = lax.optimization_barrier(o_w_in[j][:, c_main:]).T
            wg_t = jnp.concatenate([jnp.pad(gate_rows[:c_heads], (pad, (0, 0))),
                                    jnp.pad(gate_rows[c_heads:], (pad, (0, 0)))]).astype(BF16)
            u, gates = _in_proj(xin, g, take(("o_w_in", j), w_in_t(j)[:c_main]), wg_t,
                                spans=(("id", c_main),), transposed=True)
            gate_bias = jnp.concatenate([jnp.pad(c_b_i[j], pad),
                                         jnp.pad(c_b_f[j], pad)]).reshape(1, 2 * LANES)
            gate_terms = _mlstm_gates(gates.reshape(bsz, s_len, 2 * LANES), gate_bias)
            jobs = {("o_w_out", j): o_w_out[j]}
            if not last:
                jobs[("e_w_in", j + 1)] = e_w_in[j + 1]
            yc, *done = _mlstm(u.reshape(bsz, s_len, -1), gate_terms, c_conv_w[j], c_conv_b[j],
                               c_norm[j], heads=c_heads, dk=c_dk, dv=c_dv,
                               casts=list(jobs.values()))
            ready.update(zip(jobs, done))
            ys = [yc.reshape(m, -1)]
            w_out = take(("o_w_out", j), o_w_out[j])
        res = _out_proj(h, ys, w_out, p.reshape(depth, m, -1), i, ple_w_bf, ple_norm[i],
                        gate_w_bf, final_norm if last else layer_norm(i + 1), last)
        h, hn = (res, None) if last else res
    return h.reshape(bsz, s_len, d)
```

```python
import functools

import jax
import jax.numpy as jnp
from jax import lax
from jax.experimental import pallas as pl
from jax.experimental.pallas import tpu as pltpu

EPS = 1e-6
LRU_C = 8.0
CONV_W = 4
LANES = 128
SUBLANES = 8
VMEM_LIMIT = 56 * 1024 * 1024
BF16 = jnp.bfloat16
F32 = jnp.float32


def _sigmoid(x):
    return jax.nn.sigmoid(x)


def _silu(x):
    return x * jax.nn.sigmoid(x)


def _softplus(x):
    return jnp.maximum(x, 0.0) + jnp.log1p(jnp.exp(-jnp.abs(x)))


def _log_sigmoid(x):
    return -_softplus(-x)


def _shift_rows(x, s, row):
    return jnp.where(row >= s, pltpu.roll(x, s, 0), 0.0)


def _causal_conv(x_ref, cw_ref, cb_ref):
    s_len = x_ref.shape[1]
    taps = [cw_ref[k:k + 1, :] for k in range(CONV_W)]
    acc = cb_ref[...] + x_ref[0, SUBLANES:, :] * taps[CONV_W - 1]
    for j in range(1, CONV_W):
        acc = acc + x_ref[0, SUBLANES - j:s_len - j, :] * taps[CONV_W - 1 - j]
    x0 = x_ref[0, 0:SUBLANES, :]
    row = lax.broadcasted_iota(jnp.int32, x0.shape, 0)
    acc0 = cb_ref[...] + x0 * taps[CONV_W - 1]
    for j in range(1, CONV_W):
        acc0 = acc0 + _shift_rows(x0, j, row) * taps[CONV_W - 1 - j]
    return jnp.concatenate([acc0, acc], axis=0)


def _with_casts(body, n_in, n_out, casts, grid):
    steps = grid[0] * grid[1]

    def kernel(*refs):
        ins, refs = refs[:n_in], refs[n_in:]
        cast_in, refs = refs[:len(casts)], refs[len(casts):]
        outs, refs = refs[:n_out], refs[n_out:]
        cast_out, scratch = refs[:len(casts)], refs[len(casts):]
        for src, dst in zip(cast_in, cast_out):
            dst[...] = src[...].astype(dst.dtype)
        body(*ins, *outs, *scratch)

    rows = [w.shape[0] // (steps * 16) * 16 for w in casts]
    specs = [pl.BlockSpec((r, w.shape[1]), lambda a, b: (a * grid[1] + b, 0))
             for r, w in zip(rows, casts)]
    shapes = [jax.ShapeDtypeStruct((r * steps, w.shape[1]), BF16) for r, w in zip(rows, casts)]
    return kernel, specs, list(specs), shapes


def _dot(a, b):
    return jnp.dot(a.astype(BF16), b.astype(BF16), preferred_element_type=F32)


def _dot_nt(a, b):
    return lax.dot_general(a.astype(BF16), b.astype(BF16), (((1,), (1,)), ((), ())),
                           preferred_element_type=F32)


def _dot_tn(a, b):
    return lax.dot_general(a.astype(BF16), b.astype(BF16), (((0,), (0,)), ((), ())),
                           preferred_element_type=F32)


SUB_ROWS = 256
COL_PIECE = 512


def _in_proj_kernel(*refs, with_norm, with_gates, transposed, spans, layer):
    kinds = {k for k, _ in spans}
    refs = list(refs)
    x_ref = refs.pop(0)
    g_ref = refs.pop(0) if with_norm else None
    w_ref = refs.pop(0)
    wg_ref = refs.pop(0) if with_gates else None
    lbl_ref = refs.pop(0) if "hgate" in kinds else None
    o_ref = refs.pop(0)
    og_ref = refs.pop(0) if with_gates else None
    hn_ref = refs.pop(0) if with_norm else x_ref
    mm = _dot_nt if transposed else functools.partial(jnp.dot, preferred_element_type=F32)
    tm, tn = o_ref.shape
    j = pl.program_id(1)

    starts, c0 = [], 0
    for kind, width in spans:
        starts.append((kind, c0, width))
        c0 += width
    n = c0

    if lbl_ref is not None:
        lg = lbl_ref[...]
        e = jnp.exp(lg - jnp.max(lg, axis=0, keepdims=True))
        sm = e / jnp.sum(e, axis=0, keepdims=True)
        lb = jnp.sum(sm[:layer + 1], axis=0, keepdims=True)

    def activate(kind, r, c0):
        if kind == "silu":
            return _silu(r)
        if kind == "sigmoid":
            return _sigmoid(r)
        if kind == "hgate":
            lbp = lb[:, c0:c0 + r.shape[1]]
            return lbp + (1.0 - lbp) * _sigmoid(r)
        return r

    def run_tile(jt, blocked):
        tile_c0 = jt * tn
        pieces = []
        for kind, c0, width in starts:
            lo, hi = max(c0, tile_c0), min(c0 + width, tile_c0 + tn)
            for p0 in range(lo, hi, COL_PIECE if blocked else hi - lo):
                pieces.append((kind, p0 - tile_c0, min(COL_PIECE, hi - p0) if blocked else hi - lo,
                               p0 - c0))
        for r0 in range(0, tm, SUB_ROWS if blocked else tm):
            rows = slice(r0, r0 + (SUB_ROWS if blocked else tm))
            if with_norm and jt == 0:
                x = x_ref[rows, :]
                ms = jnp.mean(x * x, axis=-1, keepdims=True)
                hn = (x * lax.rsqrt(ms + EPS) * g_ref[...]).astype(BF16)
                hn_ref[rows, :] = hn
            else:
                hn = hn_ref[rows, :]
            for kind, off, width, c0 in pieces:
                wp = w_ref[off:off + width, :] if transposed else w_ref[:, off:off + width]
                o_ref[rows, off:off + width] = activate(kind, mm(hn, wp), c0)
        if with_gates and jt == 0:
            og_ref[...] = mm(hn_ref[...], wg_ref[...])

    if kinds == {"id"}:
        pl.when(j == 0)(functools.partial(run_tile, 0, with_norm))
        pl.when(j != 0)(functools.partial(run_tile, 1, False))
    else:
        for jt in range(n // tn):
            pl.when(j == jt)(functools.partial(run_tile, jt, True))


IN_PROJ_ROWS = 1024
IN_PROJ_VMEM = 46 * 1024 * 1024


def _in_proj_cols(n, tm, d, gw, x_bytes, with_norm):
    for tn in range(n, 0, -512):
        need = (2 * tm * d * x_bytes + (tm * d * 2 if with_norm else 0) + 2 * d * tn * 2
                + 2 * tm * tn * 4 + 2 * (d * gw * 2 + tm * gw * 4))
        if n % tn == 0 and need <= IN_PROJ_VMEM:
            return tn
    raise ValueError("no column tile fits")


def _in_proj(x, g, w, wg=None, *, spans, transposed=False, lb_logits=None, layer=0,
             tm=IN_PROJ_ROWS):
    m, d = x.shape
    n = sum(width for _, width in spans)
    with_gates = wg is not None
    with_norm = g is not None
    gw = wg.shape[0 if transposed else 1] if with_gates else 0
    w_spec = (lambda cols, idx: pl.BlockSpec((cols, d), lambda i, j: (idx(j), 0)) if transposed
              else pl.BlockSpec((d, cols), lambda i, j: (0, idx(j))))
    const = lambda a: pl.BlockSpec(a.shape, lambda i, j: (0,) * a.ndim)
    tn = _in_proj_cols(n, tm, d, gw, x.dtype.itemsize, with_norm)
    in_specs = [pl.BlockSpec((tm, d), lambda i, j: (i, 0))]
    args = [x]
    if with_norm:
        args.append(g.reshape(1, d))
        in_specs.append(const(args[-1]))
    in_specs.append(w_spec(tn, lambda j: j))
    args.append(w)
    out_shape = [jax.ShapeDtypeStruct((m, n), F32)]
    out_specs = [pl.BlockSpec((tm, tn), lambda i, j: (i, j))]
    if with_gates:
        in_specs.append(w_spec(gw, lambda j: 0))
        out_shape.append(jax.ShapeDtypeStruct((m, gw), F32))
        out_specs.append(pl.BlockSpec((tm, gw), lambda i, j: (i, 0)))
        args.append(wg)
    if lb_logits is not None:
        args.append(lb_logits)
        in_specs.append(const(lb_logits))
    res = pl.pallas_call(
        functools.partial(_in_proj_kernel, with_norm=with_norm, with_gates=with_gates,
                          transposed=transposed, spans=tuple(spans), layer=layer),
        grid=(m // tm, n // tn),
        in_specs=in_specs,
        out_specs=out_specs,
        out_shape=out_shape,
        scratch_shapes=[pltpu.VMEM((tm, d), BF16)] if with_norm else [],
        compiler_params=pltpu.CompilerParams(
            dimension_semantics=("parallel", "arbitrary"), vmem_limit_bytes=VMEM_LIMIT),
        name="in_proj_gates" if with_gates else "in_proj",
    )(*args)
    return res if with_gates else res[0]


HG_CHUNK = 128
HG_GROUPS = HG_CHUNK // SUBLANES
HG_INFLIGHT = 8


def _hgrn2_kernel(q_ref, f_ref, i_ref, z_ref, an_ref, o_ref,
                  st_ref, qs_ref, ks_ref, bs_ref, od_ref, tri_ref):
    s_len = q_ref.shape[1]
    dk = q_ref.shape[2]
    c = HG_CHUNK
    ng = HG_GROUPS
    assert c == dk == LANES and i_ref.shape[2] == dk

    an = an_ref[...]

    rowc = lax.broadcasted_iota(jnp.int32, (c, c), 0)
    colc = lax.broadcasted_iota(jnp.int32, (c, c), 1)
    tri_ref[...] = (colc <= rowc).astype(BF16)
    same_group = (colc % ng) == (rowc % ng)
    lane = lax.broadcasted_iota(jnp.int32, (SUBLANES, c), 1)
    diag_off = lane - lax.broadcasted_iota(jnp.int32, (SUBLANES, c), 0)
    zeros8 = jnp.zeros((SUBLANES, dk), F32)
    zeros16 = jnp.zeros((ng, dk), F32)

    st_ref[...] = jnp.zeros_like(st_ref)

    def rows8(x, r):
        return x[r * SUBLANES:(r + 1) * SUBLANES, :]

    def chunk(ci, slot):
        qs, ks, bs, od = qs_ref.at[slot], ks_ref.at[slot], bs_ref.at[slot], od_ref.at[slot]
        r0 = pl.multiple_of(ci * c, c)
        q = q_ref[0, pl.ds(r0, c), :]
        f = f_ref[0, pl.ds(r0, c), :]
        v = i_ref[0, pl.ds(r0, c), :]
        k = 1.0 - f

        lf = jnp.log2(f)
        hi = lf.astype(BF16)
        r1 = lf - hi.astype(F32)
        mid = r1.astype(BF16)
        lo = (r1 - mid.astype(F32)).astype(BF16)
        bb = jnp.dot(tri_ref[...], jnp.concatenate([hi, mid, lo], axis=1),
                     preferred_element_type=F32)
        yield
        b2 = bb[:, :dk] + bb[:, dk:2 * dk] + bb[:, 2 * dk:]
        qs[...] = q
        ks[...] = k
        bs[...] = b2
        b_last = bs[c - 1:c, :]
        upd = _dot_tn(v, k * jnp.exp2(b_last - b2))

        ps, rights = [], []
        for j in range(ng.bit_length() - 1):
            lhs, rhs, rights_j = [], [], []
            for r in range(ng):
                mid_grp = ((r >> (j + 1)) << (j + 1)) | ((1 << j) - 1)
                beta = bs[mid_grp * SUBLANES + SUBLANES - 1:(mid_grp + 1) * SUBLANES, :]
                if (r >> j) & 1:
                    lhs.append(rows8(q, r) * jnp.exp2(rows8(b2, r) - beta))
                    rhs.append(zeros8)
                    rights_j.append(r)
                else:
                    rhs.append(rows8(k, r) * jnp.exp2(beta - rows8(b2, r)))
            ps.append(_dot_nt(jnp.concatenate(lhs, axis=0), jnp.concatenate(rhs, axis=0)))
            rights.append(rights_j)
        yield

        st = st_ref[...]
        o = _dot_nt(q * jnp.exp2(b2), st)
        st_ref[...] = st * jnp.exp2(b_last) + upd

        def permuted(ref):
            return [ref[pl.ds(pos, ng, stride=SUBLANES), :] for pos in range(SUBLANES)]

        q_p, k_p, b_p = permuted(qs), permuted(ks), permuted(bs)
        lhs_slots, rhs_slots = [], []
        for j in range(SUBLANES.bit_length() - 1):
            for blk in range(SUBLANES >> (j + 1)):
                mid_pos = (blk << (j + 1)) | ((1 << j) - 1)
                lhs, rhs = [], []
                for pos in range(SUBLANES):
                    if pos >> (j + 1) != blk:
                        lhs.append(zeros16)
                        rhs.append(zeros16)
                    elif (pos >> j) & 1:
                        lhs.append(q_p[pos] * jnp.exp2(b_p[pos] - b_p[mid_pos]))
                        rhs.append(zeros16)
                    else:
                        lhs.append(zeros16)
                        rhs.append(k_p[pos] * jnp.exp2(b_p[mid_pos] - b_p[pos]))
                lhs_slots.append(jnp.concatenate(lhs, axis=0).astype(BF16))
                rhs_slots.append(jnp.concatenate(rhs, axis=0).astype(BF16))
        pd = _dot_nt(jnp.concatenate(lhs_slots, axis=1), jnp.concatenate(rhs_slots, axis=1))
        yield

        dg = jnp.sum(q * k, axis=1, keepdims=True)
        a_rows = [jnp.where(diag_off == r * SUBLANES, rows8(dg, r), 0.0) for r in range(ng)]
        for j, (p, rights_j) in enumerate(zip(ps, rights)):
            for i, r in enumerate(rights_j):
                right_start = (((r >> (j + 1)) << (j + 1)) | (1 << j)) * SUBLANES
                a_rows[r] = jnp.where(lane < right_start, rows8(p, i), a_rows[r])
        o = o + _dot(jnp.concatenate(a_rows, axis=0), v)
        v_p = [i_ref[0, pl.ds(r0 + pos, ng, stride=SUBLANES), :] for pos in range(SUBLANES)]
        od[...] = _dot(jnp.where(same_group, pd, 0.0), jnp.concatenate(v_p, axis=0))
        yield

        o = o + jnp.concatenate(
            [od[pl.ds(g, SUBLANES, stride=ng), :] for g in range(ng)], axis=0)
        yn = o * lax.rsqrt(jnp.mean(o * o, axis=-1, keepdims=True) + EPS) * an
        z = z_ref[0, pl.ds(r0, c), :]
        o_ref[0, pl.ds(r0, c), :] = (yn * z).astype(o_ref.dtype)

    def chunks(i, carry):
        stages = [chunk(i * HG_INFLIGHT + slot, slot) for slot in range(HG_INFLIGHT)]
        while stages:
            stages = [g for g in stages if next(g, stages) is not stages]
        return carry

    lax.fori_loop(0, s_len // (c * HG_INFLIGHT), chunks, 0)


def _hgrn2(u, a_norm, *, heads, dk, casts=()):
    bsz, s_len, _ = u.shape
    blk = lambda off: pl.BlockSpec((1, s_len, dk), lambda b, h, off=off: (b, 0, off + h))
    grid = (bsz, heads)
    body, c_in, c_out, c_shapes = _with_casts(_hgrn2_kernel, 5, 1, casts, grid)
    return pl.pallas_call(
        body,
        grid=grid,
        in_specs=[blk(0), blk(heads), blk(2 * heads), blk(3 * heads),
                  pl.BlockSpec((1, dk), lambda b, h: (0, h))] + c_in,
        out_specs=[pl.BlockSpec((1, s_len, dk), lambda b, h: (b, 0, h))] + c_out,
        out_shape=[jax.ShapeDtypeStruct((bsz, s_len, heads * dk), BF16)] + c_shapes,
        scratch_shapes=[pltpu.VMEM((dk, dk), F32)]
        + [pltpu.VMEM((HG_INFLIGHT, HG_CHUNK, dk), F32)] * 4
        + [pltpu.VMEM((HG_CHUNK, HG_CHUNK), BF16)],
        compiler_params=pltpu.CompilerParams(
            dimension_semantics=("parallel", "parallel"), vmem_limit_bytes=VMEM_LIMIT),
        name="hgrn2",
    )(u, u, u, u, a_norm.reshape(1, -1), *casts)


def _rglru_kernel(x_ref, z_ref, cw_ref, cb_ref, wr_ref, br_ref, wi_ref, bi_ref, lam_ref,
                  o_ref, a_ref, u_ref, h_ref, ca_ref, cu_ref):
    s_len = x_ref.shape[1]
    w = x_ref.shape[2]
    ng = s_len // SUBLANES
    xc = _causal_conv(x_ref, cw_ref, cb_ref)
    r = _sigmoid(_dot(xc, wr_ref[0]) + br_ref[...])
    ig = _sigmoid(_dot(xc, wi_ref[0]) + bi_ref[...])
    log_a = -LRU_C * r * _softplus(-lam_ref[...])
    a = jnp.exp(log_a)
    a_ref[...] = a
    t = -jnp.tanh(log_a) * (a * a + 1.0)
    u_ref[...] = jnp.where(t > 0.0, t * lax.rsqrt(t), 0.0) * (ig * xc)

    ca = cu = None
    for pos in range(SUBLANES):
        ap = a_ref[pl.ds(pos, ng, stride=SUBLANES), :]
        up = u_ref[pl.ds(pos, ng, stride=SUBLANES), :]
        ca, cu = (ap, up) if pos == 0 else (ap * ca, ap * cu + up)
        ca_ref[pos] = ca
        cu_ref[pos] = cu
    nb = ng // SUBLANES
    pa = pu = None
    pre_a, pre_u = [], []
    for pos in range(SUBLANES):
        ap = ca_ref[SUBLANES - 1, pl.ds(pos, nb, stride=SUBLANES), :]
        up = cu_ref[SUBLANES - 1, pl.ds(pos, nb, stride=SUBLANES), :]
        pa, pu = (ap, up) if pos == 0 else (ap * pa, ap * pu + up)
        pre_a.append(pa)
        pre_u.append(pu)
    blk_row = lax.broadcasted_iota(jnp.int32, (nb, w), 0)
    ta, tu = pa, pu
    s = 1
    while s < nb:
        m = blk_row >= s
        tu = tu + ta * jnp.where(m, pltpu.roll(tu, s, 0), 0.0)
        ta = ta * jnp.where(m, pltpu.roll(ta, s, 0), 1.0)
        s *= 2
    blk_in = jnp.where(blk_row >= 1, pltpu.roll(tu, 1, 0), 0.0)
    for pos in range(SUBLANES):
        a_ref[pl.ds(pos, nb, stride=SUBLANES), :] = pre_a[pos] * blk_in + pre_u[pos]
    grp = lax.broadcasted_iota(jnp.int32, (ng, w), 0)
    h_in = jnp.where(grp >= 1, pltpu.roll(a_ref[0:ng, :], 1, 0), 0.0)
    for pos in range(SUBLANES):
        h_ref[pl.ds(pos, ng, stride=SUBLANES), :] = ca_ref[pos] * h_in + cu_ref[pos]
    o_ref[0] = (h_ref[...] * z_ref[0]).astype(o_ref.dtype)


def _rglru(u, conv_w, conv_b, w_r, b_r, w_i, b_i, lam, *, x_off, z_off, casts=()):
    bsz, s_len, _ = u.shape
    nblk, blk, _ = w_r.shape
    vec = lambda rows: pl.BlockSpec((rows, blk), lambda b, n: (0, n))
    mat = pl.BlockSpec((1, blk, blk), lambda b, n: (n, 0, 0))
    grid = (bsz, nblk)
    body, c_in, c_out, c_shapes = _with_casts(_rglru_kernel, 9, 1, casts, grid)
    return pl.pallas_call(
        body,
        grid=grid,
        in_specs=[pl.BlockSpec((1, s_len, blk), lambda b, n: (b, 0, x_off + n)),
                  pl.BlockSpec((1, s_len, blk), lambda b, n: (b, 0, z_off + n)),
                  vec(CONV_W), vec(1), mat, vec(1), mat, vec(1), vec(1)] + c_in,
        out_specs=[pl.BlockSpec((1, s_len, blk), lambda b, n: (b, 0, n))] + c_out,
        out_shape=[jax.ShapeDtypeStruct((bsz, s_len, nblk * blk), BF16)] + c_shapes,
        scratch_shapes=[pltpu.VMEM((s_len, blk), F32)] * 3
        + [pltpu.VMEM((SUBLANES, s_len // SUBLANES, blk), F32)] * 2,
        compiler_params=pltpu.CompilerParams(
            dimension_semantics=("parallel", "parallel"), vmem_limit_bytes=VMEM_LIMIT),
        name="rglru",
    )(u, u, conv_w, conv_b.reshape(1, -1), w_r, b_r.reshape(1, -1), w_i, b_i.reshape(1, -1),
      lam.reshape(1, -1), *casts)


ML_CHUNK = 128
ML_UNROLL = 8


def _mlstm_gates_kernel(g_ref, gb_ref, mu_ref, gi_ref, en_ref, ks_ref, dec_ref, rt_ref):
    s_len = g_ref.shape[1]
    c = ML_CHUNK
    t = g_ref[0] + gb_ref[...]
    li = t[:, :LANES]
    lf = _log_sigmoid(t[:, LANES:])
    row = lax.broadcasted_iota(jnp.int32, (s_len, LANES), 0) % c

    def scan(x, op, fill):
        s = 1
        while s < c:
            x = op(x, jnp.where(row >= s, pltpu.roll(x, s, 0), fill))
            s *= 2
        return x

    b = scan(lf, jnp.add, 0.0)
    r = li - b
    rho = scan(r, jnp.maximum, -jnp.inf)
    m = jnp.zeros((1, LANES), F32)
    for ci in range(s_len // c):
        sl = slice(ci * c, (ci + 1) * c)
        b_c, r_c, rho_c = b[sl], r[sl], rho[sl]
        b_last = b_c[c - 1:c, :]
        m_new = b_last + jnp.maximum(m, rho_c[c - 1:c, :])
        mu = jnp.maximum(rho_c, m)
        mu_ref[0, sl, :] = mu
        gi_ref[0, sl, :] = jnp.exp(m - mu)
        en_ref[0, sl, :] = jnp.exp(-(b_c + mu))
        ks_ref[0, sl, :] = jnp.exp(b_last + r_c - m_new)
        dec_ref[0, ci:ci + 1, :] = jnp.exp(b_last + m - m_new)
        rt_ref[0, :, sl] = r_c.T
        m = m_new


def _mlstm_gates(gates, gate_bias):
    bsz, s_len, gw = gates.shape
    nc = s_len // ML_CHUNK
    col = jax.ShapeDtypeStruct((bsz, s_len, LANES), F32)
    col_spec = pl.BlockSpec((1, s_len, LANES), lambda b: (b, 0, 0))
    return pl.pallas_call(
        _mlstm_gates_kernel,
        grid=(bsz,),
        in_specs=[pl.BlockSpec((1, s_len, gw), lambda b: (b, 0, 0)),
                  pl.BlockSpec((1, gw), lambda b: (0, 0))],
        out_specs=[col_spec] * 4 + [pl.BlockSpec((1, nc, LANES), lambda b: (b, 0, 0)),
                                    pl.BlockSpec((1, LANES, s_len), lambda b: (b, 0, 0))],
        out_shape=[col] * 4 + [jax.ShapeDtypeStruct((bsz, nc, LANES), F32),
                               jax.ShapeDtypeStruct((bsz, LANES, s_len), F32)],
        compiler_params=pltpu.CompilerParams(
            dimension_semantics=("parallel",), vmem_limit_bytes=VMEM_LIMIT),
        name="mlstm_gates",
    )(gates, gate_bias)


def _mlstm_kernel(q_ref, k_ref, v_ref, og_ref, z_ref, mu_ref, gi_ref, en_ref, ks_ref, dec_ref,
                  rt_ref, cwq_ref, cbq_ref, cwk_ref, cbk_ref, cn_ref, o_ref, qs_ref, ks_s_ref,
                  cst_ref, nst_ref):
    s_len = q_ref.shape[1]
    dk = q_ref.shape[2]
    c = ML_CHUNK
    assert c == dk == LANES
    h = pl.program_id(1)

    qs_ref[...] = _silu(_causal_conv(q_ref, cwq_ref, cbq_ref)) * (dk ** -0.5)
    ks_s_ref[...] = _silu(_causal_conv(k_ref, cwk_ref, cbk_ref))
    cst_ref[...] = jnp.zeros_like(cst_ref)
    nst_ref[...] = jnp.zeros_like(nst_ref)

    row = lax.broadcasted_iota(jnp.int32, (c, c), 0)
    col = lax.broadcasted_iota(jnp.int32, (c, c), 1)
    causal = col <= row
    head = col == h
    cn = cn_ref[...]

    def pick(x):
        return jnp.sum(jnp.where(head, x, 0.0), axis=1, keepdims=True)

    dec_all = dec_ref[0]
    chunk_id = lax.broadcasted_iota(jnp.int32, dec_all.shape, 0)
    head_lane = lax.broadcasted_iota(jnp.int32, (1, LANES), 1) == h
    head_row = lax.broadcasted_iota(jnp.int32, (SUBLANES, c), 0) == h

    def chunk(ci, carry):
        r0 = pl.multiple_of(ci * c, c)
        q = qs_ref[pl.ds(r0, c), :]
        k = ks_s_ref[pl.ds(r0, c), :]
        v = v_ref[0, pl.ds(r0, c), :]
        mu = pick(mu_ref[0, pl.ds(r0, c), :])
        g_inter = pick(gi_ref[0, pl.ds(r0, c), :])
        e_negm = pick(en_ref[0, pl.ds(r0, c), :])
        kscale = pick(ks_ref[0, pl.ds(r0, c), :])
        decay = jnp.sum(jnp.where(chunk_id == ci, dec_all, 0.0), axis=0, keepdims=True)
        decay = jnp.sum(jnp.where(head_lane, decay, 0.0), axis=1, keepdims=True)
        r_row = jnp.sum(jnp.where(head_row, rt_ref[0, :, pl.ds(r0, c)], 0.0),
                        axis=0, keepdims=True)
        wts = jnp.where(causal, jnp.exp(jnp.minimum(r_row - mu, 0.0)), 0.0)
        qk = _dot_nt(q, k)
        kw = k * kscale
        upd = _dot_tn(kw, v)

        cst = cst_ref[...]
        n_st = nst_ref[...]
        sc = qk * wts
        inter = _dot(q, cst)
        intra = _dot(sc, v)
        cst_ref[...] = decay * cst + upd
        nst_ref[...] = decay * n_st + jnp.sum(kw, axis=0, keepdims=True)

        num = g_inter * inter + intra
        den = (g_inter * jnp.sum(q * n_st, axis=1, keepdims=True)
               + jnp.sum(sc, axis=1, keepdims=True))
        inv = 1.0 / jnp.maximum(jnp.abs(den), e_negm)
        msq = jnp.mean(num * num, axis=-1, keepdims=True)
        yn = num * (inv * lax.rsqrt(inv * inv * msq + EPS)) * cn
        og = og_ref[0, pl.ds(r0, c), :]
        z = z_ref[0, pl.ds(r0, c), :]
        o_ref[0, pl.ds(r0, c), :] = (yn * _sigmoid(og) * _silu(z)).astype(o_ref.dtype)
        return carry

    lax.fori_loop(0, s_len // c, chunk, 0, unroll=ML_UNROLL)


def _mlstm(u, gate_terms, conv_w, conv_b, c_norm, *, heads, dk, dv, casts=()):
    bsz, s_len, _ = u.shape
    grid = (bsz, heads)
    body, c_in, c_out, c_shapes = _with_casts(_mlstm_kernel, 16, 1, casts, grid)
    nc = s_len // ML_CHUNK
    col_spec = pl.BlockSpec((1, s_len, LANES), lambda b, h: (b, 0, 0))
    qk_blk = lambda off: pl.BlockSpec((1, s_len, dk), lambda b, h, off=off: (b, 0, off + h))
    v_blk = lambda off: pl.BlockSpec((1, s_len, dv), lambda b, h, off=off: (b, 0, off + h))
    cw = lambda off: pl.BlockSpec((CONV_W, dk), lambda b, h, off=off: (0, off + h))
    cb = lambda off: pl.BlockSpec((1, dk), lambda b, h, off=off: (0, off + h))
    nqk = 2 * heads * dk // dv
    return pl.pallas_call(
        body,
        grid=grid,
        in_specs=[qk_blk(0), qk_blk(heads),
                  v_blk(nqk), v_blk(nqk + heads), v_blk(nqk + 2 * heads),
                  col_spec, col_spec, col_spec, col_spec,
                  pl.BlockSpec((1, nc, LANES), lambda b, h: (b, 0, 0)),
                  pl.BlockSpec((1, SUBLANES, s_len), lambda b, h: (b, 0, 0)),
                  cw(0), cb(0), cw(heads), cb(heads),
                  pl.BlockSpec((1, dv), lambda b, h: (0, h))] + c_in,
        out_specs=[pl.BlockSpec((1, s_len, dv), lambda b, h: (b, 0, h))] + c_out,
        out_shape=[jax.ShapeDtypeStruct((bsz, s_len, heads * dv), BF16)] + c_shapes,
        scratch_shapes=[pltpu.VMEM((s_len, dk), F32), pltpu.VMEM((s_len, dk), F32),
                        pltpu.VMEM((dk, dv), F32), pltpu.VMEM((1, dk), F32)],
        compiler_params=pltpu.CompilerParams(
            dimension_semantics=("parallel", "parallel"), vmem_limit_bytes=VMEM_LIMIT),
        name="mlstm",
    )(u, u, u, u, u, *gate_terms, conv_w, conv_b.reshape(1, -1), conv_w,
      conv_b.reshape(1, -1), c_norm.reshape(1, -1), *casts)


OUT_SUB_ROWS = 256


def _out_kernel(h_ref, *rest, n_y, final):
    y_refs = rest[:n_y]
    wo_refs = rest[n_y:2 * n_y]
    p_ref, pw_ref, pn_ref, gw_ref, norm_ref = rest[2 * n_y:2 * n_y + 5]
    out_refs = rest[2 * n_y + 5:]

    def rows_block(r0):
        rows = slice(r0, r0 + OUT_SUB_ROWS)
        mix = jnp.dot(y_refs[0][rows, :], wo_refs[0][...], preferred_element_type=F32)
        for y_ref, wo_ref in zip(y_refs[1:], wo_refs[1:]):
            mix = mix + jnp.dot(y_ref[rows, :], wo_ref[...], preferred_element_type=F32)
        yield
        h1 = h_ref[rows, :] + mix
        pe = _dot(p_ref[rows, :], pw_ref[...])
        gate = _dot(h1, gw_ref[...])
        yield
        pe = pe * lax.rsqrt(jnp.mean(pe * pe, axis=-1, keepdims=True) + EPS) * pn_ref[...]
        h2 = h1 + _sigmoid(gate) * pe
        hn = h2 * lax.rsqrt(jnp.mean(h2 * h2, axis=-1, keepdims=True) + EPS) * norm_ref[...]
        if final:
            out_refs[0][rows, :] = hn
        else:
            out_refs[0][rows, :] = h2
            out_refs[1][rows, :] = hn.astype(out_refs[1].dtype)

    stages = [rows_block(r0) for r0 in range(0, h_ref.shape[0], OUT_SUB_ROWS)]
    while stages:
        stages = [g for g in stages if next(g, stages) is not stages]


def _out_proj(h, ys, w_out, p, layer, ple_w, ple_norm, gate_w, norm, final, *, tm=512):
    m, d = h.shape
    const = lambda shape: pl.BlockSpec(shape, lambda i: (0, 0), pipeline_mode=pl.Buffered(1))
    in_specs = [pl.BlockSpec((tm, d), lambda i: (i, 0))]
    in_specs += [pl.BlockSpec((tm, y.shape[1]), lambda i: (i, 0)) for y in ys]
    assert all(y.shape[1] == ys[0].shape[1] for y in ys)
    in_specs += [pl.BlockSpec((y.shape[1], d), lambda i, r=r: (r, 0), pipeline_mode=pl.Buffered(1))
                 for r, y in enumerate(ys)]
    stacked = lambda a: pl.BlockSpec((None,) + a.shape[1:], lambda i: (layer, 0, 0),
                                     pipeline_mode=pl.Buffered(1))
    in_specs += [pl.BlockSpec((None, tm, p.shape[2]), lambda i: (layer, i, 0)),
                 stacked(ple_w), const((1, d)), stacked(gate_w), const((1, d))]
    args = [h, *ys, *([w_out] * len(ys)), p, ple_w, ple_norm.reshape(1, d), gate_w,
            norm.reshape(1, d)]
    row_spec = pl.BlockSpec((tm, d), lambda i: (i, 0))
    return pl.pallas_call(
        functools.partial(_out_kernel, n_y=len(ys), final=final),
        grid=(m // tm,),
        in_specs=in_specs,
        out_specs=row_spec if final else [row_spec, row_spec],
        out_shape=(jax.ShapeDtypeStruct((m, d), F32) if final else
                   [jax.ShapeDtypeStruct((m, d), F32), jax.ShapeDtypeStruct((m, d), BF16)]),
        compiler_params=pltpu.CompilerParams(
            dimension_semantics=("parallel",), vmem_limit_bytes=VMEM_LIMIT),
        name="out_proj_final" if final else "out_proj",
    )(*args)


def kernel(x, p, e_norm, e_w_in, a_lb_logits, a_norm, b_conv_w, b_conv_b, b_w_r, b_b_r, b_w_i, b_b_i, b_lambda, e_w_out, o_norm, o_w_in, c_conv_w, c_conv_b, c_b_i, c_b_f, c_norm, o_w_out, ple_w, ple_norm, ple_gate_w, final_norm):
    bsz, s_len, d = x.shape
    depth = p.shape[0]
    m = bsz * s_len
    a_width = a_norm.shape[1]
    b_width = b_lambda.shape[1]
    a_dk = LANES
    a_heads = a_width // a_dk
    c_heads = c_b_i.shape[1]
    c_dv = c_norm.shape[1] // c_heads
    c_dk = c_conv_w.shape[2] // (2 * c_heads)
    c_main = o_w_in.shape[2] - 2 * c_heads

    h = x.reshape(m, d)
    ready = {}
    take = lambda name, w: ready.pop(name) if name in ready else w.astype(BF16)
    w_in_t = lambda j: o_w_in[j].T
    hn = None
    layer_norm = lambda i: e_norm[i // 2] if i % 2 == 0 else o_norm[i // 2]
    for i in range(depth):
        j = i // 2
        last = i == depth - 1
        xin, g = (h, layer_norm(i)) if hn is None else (hn, None)
        if i % 2 == 0:
            a_qk = a_heads * a_dk
            spans = (("silu", a_qk), ("hgate", a_qk), ("id", a_width), ("silu", a_width),
                     ("id", b_width), ("silu", b_width))
            u = _in_proj(xin, g, take(("e_w_in", j), e_w_in[j]), spans=spans,
                         lb_logits=a_lb_logits, layer=j).reshape(bsz, s_len, -1)
            jobs = {("o_w_in", j): w_in_t(j)} if not last else {}
            if i == 0:
                jobs.update(gate_w=ple_gate_w.reshape(depth * d, d))
            ya, *done = _hgrn2(u, a_norm[j], heads=a_heads, dk=a_dk, casts=list(jobs.values()))
            ready.update(zip(jobs, done))
            jobs = {("e_w_out", j): e_w_out[j]}
            if i == 0:
                jobs.update(ple_w=ple_w.reshape(-1, d))
            x_off = (2 * a_heads * a_dk + 2 * a_width) // LANES
            yb, *done = _rglru(u, b_conv_w[j], b_conv_b[j], b_w_r[j].astype(BF16), b_b_r[j],
                               b_w_i[j].astype(BF16), b_b_i[j], b_lambda[j], x_off=x_off,
                               z_off=x_off + b_width // LANES, casts=list(jobs.values()))
            ready.update(zip(jobs, done))
            if i == 0:
                gate_w_bf = ready.pop("gate_w").reshape(depth, d, d)
                ple_w_bf = ready.pop("ple_w").reshape(ple_w.shape)
            w_out = take(("e_w_out", j), e_w_out[j])
            ys = [ya.reshape(m, a_width), yb.reshape(m, b_width)]
        else:
            pad = (0, LANES - c_heads)
            gate_rows = lax.optimization_barrier(o_w_in[j][:, c_main:]).T
            wg_t = jnp.concatenate([jnp.pad(gate_rows[:c_heads], (pad, (0, 0))),
                                    jnp.pad(gate_rows[c_heads:], (pad, (0, 0)))]).astype(BF16)
            u, gates = _in_proj(xin, g, take(("o_w_in", j), w_in_t(j)[:c_main]), wg_t,
                                spans=(("id", c_main),), transposed=True)
            gate_bias = jnp.concatenate([jnp.pad(c_b_i[j], pad),
                                         jnp.pad(c_b_f[j], pad)]).reshape(1, 2 * LANES)
            gate_terms = _mlstm_gates(gates.reshape(bsz, s_len, 2 * LANES), gate_bias)
            jobs = {("o_w_out", j): o_w_out[j]}
            if not last:
                jobs[("e_w_in", j + 1)] = e_w_in[j + 1]
            yc, *done = _mlstm(u.reshape(bsz, s_len, -1), gate_terms, c_conv_w[j], c_conv_b[j],
                               c_norm[j], heads=c_heads, dk=c_dk, dv=c_dv,
                               casts=list(jobs.values()))
            ready.update(zip(jobs, done))
            ys = [yc.reshape(m, -1)]
            w_out = take(("o_w_out", j), o_w_out[j])
        res = _out_proj(h, ys, w_out, p.reshape(depth, m, -1), i, ple_w_bf, ple_norm[i],
                        gate_w_bf, final_norm if last else layer_norm(i + 1), last)
        h, hn = (res, None) if last else res
    return h.reshape(bsz, s_len, d)
```

```python
import functools

import jax
import jax.numpy as jnp
from jax import lax
from jax.experimental import pallas as pl
from jax.experimental.pallas import tpu as pltpu

EPS = 1e-6
LRU_C = 8.0
CONV_W = 4
LANES = 128
SUBLANES = 8
VMEM_LIMIT = 56 * 1024 * 1024
BF16 = jnp.bfloat16
F32 = jnp.float32


def _sigmoid(x):
    return jax.nn.sigmoid(x)


def _silu(x):
    return x * jax.nn.sigmoid(x)


def _softplus(x):
    return jnp.maximum(x, 0.0) + jnp.log1p(jnp.exp(-jnp.abs(x)))


def _log_sigmoid(x):
    return -_softplus(-x)


def _shift_rows(x, s, row):
    return jnp.where(row >= s, pltpu.roll(x, s, 0), 0.0)


def _cumsum_rows(x, row):
    s = 1
    while s < x.shape[0]:
        x = x + _shift_rows(x, s, row)
        s *= 2
    return x


def _cummax_rows(x, row):
    s = 1
    while s < x.shape[0]:
        x = jnp.maximum(x, jnp.where(row >= s, pltpu.roll(x, s, 0), -jnp.inf))
        s *= 2
    return x


def _causal_conv(x_ref, cw_ref, cb_ref):
    s_len = x_ref.shape[1]
    taps = [cw_ref[k:k + 1, :] for k in range(CONV_W)]
    acc = cb_ref[...] + x_ref[0, SUBLANES:, :] * taps[CONV_W - 1]
    for j in range(1, CONV_W):
        acc = acc + x_ref[0, SUBLANES - j:s_len - j, :] * taps[CONV_W - 1 - j]
    x0 = x_ref[0, 0:SUBLANES, :]
    row = lax.broadcasted_iota(jnp.int32, x0.shape, 0)
    acc0 = cb_ref[...] + x0 * taps[CONV_W - 1]
    for j in range(1, CONV_W):
        acc0 = acc0 + _shift_rows(x0, j, row) * taps[CONV_W - 1 - j]
    return jnp.concatenate([acc0, acc], axis=0)


def _with_casts(body, n_in, n_out, casts, grid):
    steps = grid[0] * grid[1]

    def kernel(*refs):
        ins, refs = refs[:n_in], refs[n_in:]
        cast_in, refs = refs[:len(casts)], refs[len(casts):]
        outs, refs = refs[:n_out], refs[n_out:]
        cast_out, scratch = refs[:len(casts)], refs[len(casts):]
        for src, dst in zip(cast_in, cast_out):
            dst[...] = src[...].astype(dst.dtype)
        body(*ins, *outs, *scratch)

    rows = [w.shape[0] // (steps * 16) * 16 for w in casts]
    specs = [pl.BlockSpec((r, w.shape[1]), lambda a, b: (a * grid[1] + b, 0))
             for r, w in zip(rows, casts)]
    shapes = [jax.ShapeDtypeStruct((r * steps, w.shape[1]), BF16) for r, w in zip(rows, casts)]
    return kernel, specs, list(specs), shapes


def _dot(a, b):
    return jnp.dot(a.astype(BF16), b.astype(BF16), preferred_element_type=F32)


def _dot_nt(a, b):
    return lax.dot_general(a.astype(BF16), b.astype(BF16), (((1,), (1,)), ((), ())),
                           preferred_element_type=F32)


def _dot_tn(a, b):
    return lax.dot_general(a.astype(BF16), b.astype(BF16), (((0,), (0,)), ((), ())),
                           preferred_element_type=F32)


SUB_ROWS = 256
COL_PIECE = 512


def _gate_terms(t, m, mu_ref, gi_ref, en_ref, ks_ref, dec_ref, rt_ref):
    rows = t.shape[0]
    c = ML_CHUNK
    li = t[:, :LANES]
    lf = _log_sigmoid(t[:, LANES:])
    row = lax.broadcasted_iota(jnp.int32, (rows, LANES), 0) % c

    def scan(x, op, fill):
        s = 1
        while s < c:
            x = op(x, jnp.where(row >= s, pltpu.roll(x, s, 0), fill))
            s *= 2
        return x

    b = scan(lf, jnp.add, 0.0)
    r = li - b
    rho = scan(r, jnp.maximum, -jnp.inf)
    for ci in range(rows // c):
        sl = slice(ci * c, (ci + 1) * c)
        b_c, r_c, rho_c = b[sl], r[sl], rho[sl]
        b_last = b_c[c - 1:c, :]
        m_new = b_last + jnp.maximum(m, rho_c[c - 1:c, :])
        mu = jnp.maximum(rho_c, m)
        mu_ref[sl, :] = mu
        gi_ref[sl, :] = jnp.exp(m - mu)
        en_ref[sl, :] = jnp.exp(-(b_c + mu))
        ks_ref[sl, :] = jnp.exp(b_last + r_c - m_new)
        dec_ref[ci:ci + 1, :] = jnp.exp(b_last + m - m_new)
        rt_ref[0, :, sl] = r_c.T
        m = m_new
    return m


def _in_proj_kernel(*refs, with_norm, with_gates, transposed, spans, layer, seq_blocks):
    kinds = {k for k, _ in spans}
    refs = list(refs)
    x_ref = refs.pop(0)
    g_ref = refs.pop(0) if with_norm else None
    w_ref = refs.pop(0)
    wg_ref = refs.pop(0) if with_gates else None
    lbl_ref = refs.pop(0) if "hgate" in kinds else None
    gb_ref = refs.pop(0) if with_gates else None
    o_ref = refs.pop(0)
    gate_refs = [refs.pop(0) for _ in range(6)] if with_gates else None
    hn_ref = refs.pop(0) if with_norm else x_ref
    m_ref = refs.pop(0) if with_gates else None
    mm = _dot_nt if transposed else functools.partial(jnp.dot, preferred_element_type=F32)
    tm, tn = o_ref.shape
    i, j = pl.program_id(0), pl.program_id(1)

    starts, c0 = [], 0
    for kind, width in spans:
        starts.append((kind, c0, width))
        c0 += width
    n = c0

    if lbl_ref is not None:
        lg = lbl_ref[...]
        e = jnp.exp(lg - jnp.max(lg, axis=0, keepdims=True))
        sm = e / jnp.sum(e, axis=0, keepdims=True)
        lb = jnp.sum(sm[:layer + 1], axis=0, keepdims=True)

    def activate(kind, r, c0):
        if kind == "silu":
            return _silu(r)
        if kind == "sigmoid":
            return _sigmoid(r)
        if kind == "hgate":
            lbp = lb[:, c0:c0 + r.shape[1]]
            return lbp + (1.0 - lbp) * _sigmoid(r)
        return r

    def run_tile(jt, blocked):
        tile_c0 = jt * tn
        pieces = []
        for kind, c0, width in starts:
            lo, hi = max(c0, tile_c0), min(c0 + width, tile_c0 + tn)
            for p0 in range(lo, hi, COL_PIECE if blocked else hi - lo):
                pieces.append((kind, p0 - tile_c0, min(COL_PIECE, hi - p0) if blocked else hi - lo,
                               p0 - c0))
        def gates():
            @pl.when(i % seq_blocks == 0)
            def _():
                m_ref[...] = jnp.zeros_like(m_ref)
            t = mm(hn_ref[...], wg_ref[...]) + gb_ref[...]
            m_ref[...] = _gate_terms(t, m_ref[...], *gate_refs)

        if with_gates and jt == 0 and not with_norm:
            gates()
        for r0 in range(0, tm, SUB_ROWS if blocked else tm):
            rows = slice(r0, r0 + (SUB_ROWS if blocked else tm))
            if with_norm and jt == 0:
                x = x_ref[rows, :]
                ms = jnp.mean(x * x, axis=-1, keepdims=True)
                hn = (x * lax.rsqrt(ms + EPS) * g_ref[...]).astype(BF16)
                hn_ref[rows, :] = hn
            else:
                hn = hn_ref[rows, :]
            for kind, off, width, c0 in pieces:
                wp = w_ref[off:off + width, :] if transposed else w_ref[:, off:off + width]
                o_ref[rows, off:off + width] = activate(kind, mm(hn, wp), c0)
        if with_gates and jt == 0 and with_norm:
            gates()

    if kinds == {"id"}:
        pl.when(j == 0)(functools.partial(run_tile, 0, with_norm))
        pl.when(j != 0)(functools.partial(run_tile, 1, False))
    else:
        for jt in range(n // tn):
            pl.when(j == jt)(functools.partial(run_tile, jt, True))


IN_PROJ_ROWS = 1024
IN_PROJ_VMEM = 46 * 1024 * 1024


def _in_proj_cols(n, tm, d, gw, x_bytes, with_norm):
    for tn in range(n, 0, -512):
        need = (2 * tm * d * x_bytes + (tm * d * 2 if with_norm else 0) + 2 * d * tn * 2
                + 2 * tm * tn * 4 + 2 * (d * gw * 2 + tm * gw * 4))
        if n % tn == 0 and need <= IN_PROJ_VMEM:
            return tn
    raise ValueError("no column tile fits")


def _in_proj(x, g, w, wg=None, gate_bias=None, *, spans, seq_rows=None, transposed=False,
             lb_logits=None, layer=0, tm=IN_PROJ_ROWS):
    m, d = x.shape
    n = sum(width for _, width in spans)
    with_gates = wg is not None
    with_norm = g is not None
    gw = wg.shape[0 if transposed else 1] if with_gates else 0
    w_spec = (lambda cols, idx: pl.BlockSpec((cols, d), lambda i, j: (idx(j), 0)) if transposed
              else pl.BlockSpec((d, cols), lambda i, j: (0, idx(j))))
    const = lambda a: pl.BlockSpec(a.shape, lambda i, j: (0,) * a.ndim)
    tn = _in_proj_cols(n, tm, d, gw, x.dtype.itemsize, with_norm)
    in_specs = [pl.BlockSpec((tm, d), lambda i, j: (i, 0))]
    args = [x]
    if with_norm:
        args.append(g.reshape(1, d))
        in_specs.append(const(args[-1]))
    in_specs.append(w_spec(tn, lambda j: j))
    args.append(w)
    out_shape = [jax.ShapeDtypeStruct((m, n), F32)]
    out_specs = [pl.BlockSpec((tm, tn), lambda i, j: (i, j))]
    scratch = [pltpu.VMEM((tm, d), BF16)] if with_norm else []
    if with_gates:
        in_specs.append(w_spec(gw, lambda j: 0))
        args.append(wg)
    if lb_logits is not None:
        args.append(lb_logits)
        in_specs.append(const(lb_logits))
    if with_gates:
        assert gw == 2 * LANES and seq_rows % tm == 0 and tm % ML_CHUNK == 0
        spb = seq_rows // tm
        args.append(gate_bias)
        in_specs.append(const(gate_bias))
        col = pl.BlockSpec((tm, LANES), lambda i, j: (i, 0))
        out_specs += [col] * 4 + [pl.BlockSpec((tm // ML_CHUNK, LANES), lambda i, j: (i, 0)),
                                  pl.BlockSpec((1, LANES, tm), lambda i, j: (i // spb, 0, i % spb))]
        out_shape += ([jax.ShapeDtypeStruct((m, LANES), F32)] * 4
                      + [jax.ShapeDtypeStruct((m // ML_CHUNK, LANES), F32),
                         jax.ShapeDtypeStruct((m // seq_rows, LANES, seq_rows), F32)])
        scratch.append(pltpu.VMEM((1, LANES), F32))
    res = pl.pallas_call(
        functools.partial(_in_proj_kernel, with_norm=with_norm, with_gates=with_gates,
                          transposed=transposed, spans=tuple(spans), layer=layer,
                          seq_blocks=seq_rows // tm if with_gates else 1),
        grid=(m // tm, n // tn),
        in_specs=in_specs,
        out_specs=out_specs,
        out_shape=out_shape,
        scratch_shapes=scratch,
        compiler_params=pltpu.CompilerParams(
            dimension_semantics=("arbitrary" if with_gates else "parallel", "arbitrary"),
            vmem_limit_bytes=VMEM_LIMIT),
        name="in_proj_gates" if with_gates else "in_proj",
    )(*args)
    return res if with_gates else res[0]


HG_CHUNK = 128
HG_GROUPS = HG_CHUNK // SUBLANES
HG_INFLIGHT = 8


def _hgrn2_kernel(q_ref, f_ref, i_ref, z_ref, an_ref, o_ref,
                  st_ref, qs_ref, ks_ref, bs_ref, od_ref, tri_ref):
    s_len = q_ref.shape[1]
    dk = q_ref.shape[2]
    c = HG_CHUNK
    ng = HG_GROUPS
    assert c == dk == LANES and i_ref.shape[2] == dk

    an = an_ref[...]

    rowc = lax.broadcasted_iota(jnp.int32, (c, c), 0)
    colc = lax.broadcasted_iota(jnp.int32, (c, c), 1)
    tri_ref[...] = (colc <= rowc).astype(BF16)
    same_group = (colc % ng) == (rowc % ng)
    lane = lax.broadcasted_iota(jnp.int32, (SUBLANES, c), 1)
    diag_off = lane - lax.broadcasted_iota(jnp.int32, (SUBLANES, c), 0)
    zeros8 = jnp.zeros((SUBLANES, dk), F32)
    zeros16 = jnp.zeros((ng, dk), F32)

    st_ref[...] = jnp.zeros_like(st_ref)

    def rows8(x, r):
        return x[r * SUBLANES:(r + 1) * SUBLANES, :]

    def chunk(ci, slot):
        qs, ks, bs, od = qs_ref.at[slot], ks_ref.at[slot], bs_ref.at[slot], od_ref.at[slot]
        r0 = pl.multiple_of(ci * c, c)
        q = q_ref[0, pl.ds(r0, c), :]
        f = f_ref[0, pl.ds(r0, c), :]
        v = i_ref[0, pl.ds(r0, c), :]
        k = 1.0 - f

        lf = jnp.log2(f)
        hi = lf.astype(BF16)
        r1 = lf - hi.astype(F32)
        mid = r1.astype(BF16)
        lo = (r1 - mid.astype(F32)).astype(BF16)
        bb = jnp.dot(tri_ref[...], jnp.concatenate([hi, mid, lo], axis=1),
                     preferred_element_type=F32)
        yield
        b2 = bb[:, :dk] + bb[:, dk:2 * dk] + bb[:, 2 * dk:]
        qs[...] = q
        ks[...] = k
        bs[...] = b2
        b_last = bs[c - 1:c, :]
        upd = _dot_tn(v, k * jnp.exp2(b_last - b2))

        ps, rights = [], []
        for j in range(ng.bit_length() - 1):
            lhs, rhs, rights_j = [], [], []
            for r in range(ng):
                mid_grp = ((r >> (j + 1)) << (j + 1)) | ((1 << j) - 1)
                beta = bs[mid_grp * SUBLANES + SUBLANES - 1:(mid_grp + 1) * SUBLANES, :]
                if (r >> j) & 1:
                    lhs.append(rows8(q, r) * jnp.exp2(rows8(b2, r) - beta))
                    rhs.append(zeros8)
                    rights_j.append(r)
                else:
                    rhs.append(rows8(k, r) * jnp.exp2(beta - rows8(b2, r)))
            ps.append(_dot_nt(jnp.concatenate(lhs, axis=0), jnp.concatenate(rhs, axis=0)))
            rights.append(rights_j)
        yield

        st = st_ref[...]
        o = _dot_nt(q * jnp.exp2(b2), st)
        st_ref[...] = st * jnp.exp2(b_last) + upd

        def permuted(ref):
            return [ref[pl.ds(pos, ng, stride=SUBLANES), :] for pos in range(SUBLANES)]

        q_p, k_p, b_p = permuted(qs), permuted(ks), permuted(bs)
        lhs_slots, rhs_slots = [], []
        for j in range(SUBLANES.bit_length() - 1):
            for blk in range(SUBLANES >> (j + 1)):
                mid_pos = (blk << (j + 1)) | ((1 << j) - 1)
                lhs, rhs = [], []
                for pos in range(SUBLANES):
                    if pos >> (j + 1) != blk:
                        lhs.append(zeros16)
                        rhs.append(zeros16)
                    elif (pos >> j) & 1:
                        lhs.append(q_p[pos] * jnp.exp2(b_p[pos] - b_p[mid_pos]))
                        rhs.append(zeros16)
                    else:
                        lhs.append(zeros16)
                        rhs.append(k_p[pos] * jnp.exp2(b_p[mid_pos] - b_p[pos]))
                lhs_slots.append(jnp.concatenate(lhs, axis=0).astype(BF16))
                rhs_slots.append(jnp.concatenate(rhs, axis=0).astype(BF16))
        pd = _dot_nt(jnp.concatenate(lhs_slots, axis=1), jnp.concatenate(rhs_slots, axis=1))
        yield

        dg = jnp.sum(q * k, axis=1, keepdims=True)
        a_rows = [jnp.where(diag_off == r * SUBLANES, rows8(dg, r), 0.0) for r in range(ng)]
        for j, (p, rights_j) in enumerate(zip(ps, rights)):
            for i, r in enumerate(rights_j):
                right_start = (((r >> (j + 1)) << (j + 1)) | (1 << j)) * SUBLANES
                a_rows[r] = jnp.where(lane < right_start, rows8(p, i), a_rows[r])
        o = o + _dot(jnp.concatenate(a_rows, axis=0), v)
        v_p = [i_ref[0, pl.ds(r0 + pos, ng, stride=SUBLANES), :] for pos in range(SUBLANES)]
        od[...] = _dot(jnp.where(same_group, pd, 0.0), jnp.concatenate(v_p, axis=0))
        yield

        o = o + jnp.concatenate(
            [od[pl.ds(g, SUBLANES, stride=ng), :] for g in range(ng)], axis=0)
        yn = o * lax.rsqrt(jnp.mean(o * o, axis=-1, keepdims=True) + EPS) * an
        z = z_ref[0, pl.ds(r0, c), :]
        o_ref[0, pl.ds(r0, c), :] = (yn * z).astype(o_ref.dtype)

    def chunks(i, carry):
        stages = [chunk(i * HG_INFLIGHT + slot, slot) for slot in range(HG_INFLIGHT)]
        while stages:
            stages = [g for g in stages if next(g, stages) is not stages]
        return carry

    lax.fori_loop(0, s_len // (c * HG_INFLIGHT), chunks, 0)


def _hgrn2(u, a_norm, *, heads, dk, casts=()):
    bsz, s_len, _ = u.shape
    blk = lambda off: pl.BlockSpec((1, s_len, dk), lambda b, h, off=off: (b, 0, off + h))
    grid = (bsz, heads)
    body, c_in, c_out, c_shapes = _with_casts(_hgrn2_kernel, 5, 1, casts, grid)
    return pl.pallas_call(
        body,
        grid=grid,
        in_specs=[blk(0), blk(heads), blk(2 * heads), blk(3 * heads),
                  pl.BlockSpec((1, dk), lambda b, h: (0, h))] + c_in,
        out_specs=[pl.BlockSpec((1, s_len, dk), lambda b, h: (b, 0, h))] + c_out,
        out_shape=[jax.ShapeDtypeStruct((bsz, s_len, heads * dk), BF16)] + c_shapes,
        scratch_shapes=[pltpu.VMEM((dk, dk), F32)]
        + [pltpu.VMEM((HG_INFLIGHT, HG_CHUNK, dk), F32)] * 4
        + [pltpu.VMEM((HG_CHUNK, HG_CHUNK), BF16)],
        compiler_params=pltpu.CompilerParams(
            dimension_semantics=("parallel", "parallel"), vmem_limit_bytes=VMEM_LIMIT),
        name="hgrn2",
    )(u, u, u, u, a_norm.reshape(1, -1), *casts)


def _rglru_kernel(x_ref, z_ref, cw_ref, cb_ref, wr_ref, br_ref, wi_ref, bi_ref, lam_ref,
                  o_ref, a_ref, u_ref, h_ref, ca_ref, cu_ref):
    s_len = x_ref.shape[1]
    w = x_ref.shape[2]
    ng = s_len // SUBLANES
    xc = _causal_conv(x_ref, cw_ref, cb_ref)
    r = _sigmoid(_dot(xc, wr_ref[0]) + br_ref[...])
    ig = _sigmoid(_dot(xc, wi_ref[0]) + bi_ref[...])
    log_a = -LRU_C * r * _softplus(-lam_ref[...])
    a = jnp.exp(log_a)
    a_ref[...] = a
    t = -jnp.tanh(log_a) * (a * a + 1.0)
    u_ref[...] = jnp.where(t > 0.0, t * lax.rsqrt(t), 0.0) * (ig * xc)

    ca = cu = None
    for pos in range(SUBLANES):
        ap = a_ref[pl.ds(pos, ng, stride=SUBLANES), :]
        up = u_ref[pl.ds(pos, ng, stride=SUBLANES), :]
        ca, cu = (ap, up) if pos == 0 else (ap * ca, ap * cu + up)
        ca_ref[pos] = ca
        cu_ref[pos] = cu
    grp = lax.broadcasted_iota(jnp.int32, (ng, w), 0)
    ta, tu = ca, cu
    s = 1
    while s < ng:
        m = grp >= s
        tu = tu + ta * jnp.where(m, pltpu.roll(tu, s, 0), 0.0)
        ta = ta * jnp.where(m, pltpu.roll(ta, s, 0), 1.0)
        s *= 2
    h_in = jnp.where(grp >= 1, pltpu.roll(tu, 1, 0), 0.0)
    for pos in range(SUBLANES):
        h_ref[pl.ds(pos, ng, stride=SUBLANES), :] = ca_ref[pos] * h_in + cu_ref[pos]
    o_ref[0] = (h_ref[...] * z_ref[0]).astype(o_ref.dtype)


def _rglru(u, conv_w, conv_b, w_r, b_r, w_i, b_i, lam, *, x_off, z_off, casts=()):
    bsz, s_len, _ = u.shape
    nblk, blk, _ = w_r.shape
    vec = lambda rows: pl.BlockSpec((rows, blk), lambda b, n: (0, n))
    mat = pl.BlockSpec((1, blk, blk), lambda b, n: (n, 0, 0))
    grid = (bsz, nblk)
    body, c_in, c_out, c_shapes = _with_casts(_rglru_kernel, 9, 1, casts, grid)
    return pl.pallas_call(
        body,
        grid=grid,
        in_specs=[pl.BlockSpec((1, s_len, blk), lambda b, n: (b, 0, x_off + n)),
                  pl.BlockSpec((1, s_len, blk), lambda b, n: (b, 0, z_off + n)),
                  vec(CONV_W), vec(1), mat, vec(1), mat, vec(1), vec(1)] + c_in,
        out_specs=[pl.BlockSpec((1, s_len, blk), lambda b, n: (b, 0, n))] + c_out,
        out_shape=[jax.ShapeDtypeStruct((bsz, s_len, nblk * blk), BF16)] + c_shapes,
        scratch_shapes=[pltpu.VMEM((s_len, blk), F32)] * 3
        + [pltpu.VMEM((SUBLANES, s_len // SUBLANES, blk), F32)] * 2,
        compiler_params=pltpu.CompilerParams(
            dimension_semantics=("parallel", "parallel"), vmem_limit_bytes=VMEM_LIMIT),
        name="rglru",
    )(u, u, conv_w, conv_b.reshape(1, -1), w_r, b_r.reshape(1, -1), w_i, b_i.reshape(1, -1),
      lam.reshape(1, -1), *casts)


ML_CHUNK = 128
ML_UNROLL = 8


def _mlstm_kernel(q_ref, k_ref, v_ref, og_ref, z_ref, mu_ref, gi_ref, en_ref, ks_ref, dec_ref,
                  rt_ref, cwq_ref, cbq_ref, cwk_ref, cbk_ref, cn_ref, o_ref, qs_ref, ks_s_ref,
                  cst_ref, nst_ref):
    s_len = q_ref.shape[1]
    dk = q_ref.shape[2]
    c = ML_CHUNK
    assert c == dk == LANES
    h = pl.program_id(1)

    qs_ref[...] = _silu(_causal_conv(q_ref, cwq_ref, cbq_ref)) * (dk ** -0.5)
    ks_s_ref[...] = _silu(_causal_conv(k_ref, cwk_ref, cbk_ref))
    cst_ref[...] = jnp.zeros_like(cst_ref)
    nst_ref[...] = jnp.zeros_like(nst_ref)

    row = lax.broadcasted_iota(jnp.int32, (c, c), 0)
    col = lax.broadcasted_iota(jnp.int32, (c, c), 1)
    causal = col <= row
    head = col == h
    cn = cn_ref[...]

    def pick(x):
        return jnp.sum(jnp.where(head, x, 0.0), axis=1, keepdims=True)

    dec_all = dec_ref[0]
    chunk_id = lax.broadcasted_iota(jnp.int32, dec_all.shape, 0)
    head_lane = lax.broadcasted_iota(jnp.int32, (1, LANES), 1) == h
    head_row = lax.broadcasted_iota(jnp.int32, (SUBLANES, c), 0) == h

    def chunk(ci, carry):
        r0 = pl.multiple_of(ci * c, c)
        q = qs_ref[pl.ds(r0, c), :]
        k = ks_s_ref[pl.ds(r0, c), :]
        v = v_ref[0, pl.ds(r0, c), :]
        mu = pick(mu_ref[0, pl.ds(r0, c), :])
        g_inter = pick(gi_ref[0, pl.ds(r0, c), :])
        e_negm = pick(en_ref[0, pl.ds(r0, c), :])
        kscale = pick(ks_ref[0, pl.ds(r0, c), :])
        decay = jnp.sum(jnp.where(chunk_id == ci, dec_all, 0.0), axis=0, keepdims=True)
        decay = jnp.sum(jnp.where(head_lane, decay, 0.0), axis=1, keepdims=True)
        r_row = jnp.sum(jnp.where(head_row, rt_ref[0, :, pl.ds(r0, c)], 0.0),
                        axis=0, keepdims=True)
        wts = jnp.where(causal, jnp.exp(jnp.minimum(r_row - mu, 0.0)), 0.0)
        qk = _dot_nt(q, k)
        kw = k * kscale
        upd = _dot_tn(kw, v)

        cst = cst_ref[...]
        n_st = nst_ref[...]
        sc = qk * wts
        inter = _dot(q, cst)
        intra = _dot(sc, v)
        cst_ref[...] = decay * cst + upd
        nst_ref[...] = decay * n_st + jnp.sum(kw, axis=0, keepdims=True)

        num = g_inter * inter + intra
        den = (g_inter * jnp.sum(q * n_st, axis=1, keepdims=True)
               + jnp.sum(sc, axis=1, keepdims=True))
        inv = 1.0 / jnp.maximum(jnp.abs(den), e_negm)
        msq = jnp.mean(num * num, axis=-1, keepdims=True)
        yn = num * (inv * lax.rsqrt(inv * inv * msq + EPS)) * cn
        og = og_ref[0, pl.ds(r0, c), :]
        z = z_ref[0, pl.ds(r0, c), :]
        o_ref[0, pl.ds(r0, c), :] = (yn * _sigmoid(og) * _silu(z)).astype(o_ref.dtype)
        return carry

    lax.fori_loop(0, s_len // c, chunk, 0, unroll=ML_UNROLL)


def _mlstm(u, gate_terms, conv_w, conv_b, c_norm, *, heads, dk, dv, casts=()):
    bsz, s_len, _ = u.shape
    grid = (bsz, heads)
    body, c_in, c_out, c_shapes = _with_casts(_mlstm_kernel, 16, 1, casts, grid)
    nc = s_len // ML_CHUNK
    col_spec = pl.BlockSpec((1, s_len, LANES), lambda b, h: (b, 0, 0))
    qk_blk = lambda off: pl.BlockSpec((1, s_len, dk), lambda b, h, off=off: (b, 0, off + h))
    v_blk = lambda off: pl.BlockSpec((1, s_len, dv), lambda b, h, off=off: (b, 0, off + h))
    cw = lambda off: pl.BlockSpec((CONV_W, dk), lambda b, h, off=off: (0, off + h))
    cb = lambda off: pl.BlockSpec((1, dk), lambda b, h, off=off: (0, off + h))
    nqk = 2 * heads * dk // dv
    return pl.pallas_call(
        body,
        grid=grid,
        in_specs=[qk_blk(0), qk_blk(heads),
                  v_blk(nqk), v_blk(nqk + heads), v_blk(nqk + 2 * heads),
                  col_spec, col_spec, col_spec, col_spec,
                  pl.BlockSpec((1, nc, LANES), lambda b, h: (b, 0, 0)),
                  pl.BlockSpec((1, SUBLANES, s_len), lambda b, h: (b, 0, 0)),
                  cw(0), cb(0), cw(heads), cb(heads),
                  pl.BlockSpec((1, dv), lambda b, h: (0, h))] + c_in,
        out_specs=[pl.BlockSpec((1, s_len, dv), lambda b, h: (b, 0, h))] + c_out,
        out_shape=[jax.ShapeDtypeStruct((bsz, s_len, heads * dv), BF16)] + c_shapes,
        scratch_shapes=[pltpu.VMEM((s_len, dk), F32), pltpu.VMEM((s_len, dk), F32),
                        pltpu.VMEM((dk, dv), F32), pltpu.VMEM((1, dk), F32)],
        compiler_params=pltpu.CompilerParams(
            dimension_semantics=("parallel", "parallel"), vmem_limit_bytes=VMEM_LIMIT),
        name="mlstm",
    )(u, u, u, u, u, *gate_terms, conv_w, conv_b.reshape(1, -1), conv_w,
      conv_b.reshape(1, -1), c_norm.reshape(1, -1), *casts)


OUT_SUB_ROWS = 256


def _out_kernel(h_ref, *rest, n_y, final):
    y_refs = rest[:n_y]
    wo_refs = rest[n_y:2 * n_y]
    p_ref, pw_ref, pn_ref, gw_ref, norm_ref = rest[2 * n_y:2 * n_y + 5]
    out_refs = rest[2 * n_y + 5:]

    def rows_block(r0):
        rows = slice(r0, r0 + OUT_SUB_ROWS)
        mix = jnp.dot(y_refs[0][rows, :], wo_refs[0][...], preferred_element_type=F32)
        for y_ref, wo_ref in zip(y_refs[1:], wo_refs[1:]):
            mix = mix + jnp.dot(y_ref[rows, :], wo_ref[...], preferred_element_type=F32)
        yield
        h1 = h_ref[rows, :] + mix
        pe = _dot(p_ref[rows, :], pw_ref[...])
        gate = _dot(h1, gw_ref[...])
        yield
        pe = pe * lax.rsqrt(jnp.mean(pe * pe, axis=-1, keepdims=True) + EPS) * pn_ref[...]
        h2 = h1 + _sigmoid(gate) * pe
        hn = h2 * lax.rsqrt(jnp.mean(h2 * h2, axis=-1, keepdims=True) + EPS) * norm_ref[...]
        if final:
            out_refs[0][rows, :] = hn
        else:
            out_refs[0][rows, :] = h2
            out_refs[1][rows, :] = hn.astype(out_refs[1].dtype)

    stages = [rows_block(r0) for r0 in range(0, h_ref.shape[0], OUT_SUB_ROWS)]
    while stages:
        stages = [g for g in stages if next(g, stages) is not stages]


def _out_proj(h, ys, w_out, p, layer, ple_w, ple_norm, gate_w, norm, final, *, tm=512):
    m, d = h.shape
    const = lambda shape: pl.BlockSpec(shape, lambda i: (0, 0), pipeline_mode=pl.Buffered(1))
    in_specs = [pl.BlockSpec((tm, d), lambda i: (i, 0))]
    in_specs += [pl.BlockSpec((tm, y.shape[1]), lambda i: (i, 0)) for y in ys]
    assert all(y.shape[1] == ys[0].shape[1] for y in ys)
    in_specs += [pl.BlockSpec((y.shape[1], d), lambda i, r=r: (r, 0), pipeline_mode=pl.Buffered(1))
                 for r, y in enumerate(ys)]
    stacked = lambda a: pl.BlockSpec((None,) + a.shape[1:], lambda i: (layer, 0, 0),
                                     pipeline_mode=pl.Buffered(1))
    in_specs += [pl.BlockSpec((None, tm, p.shape[2]), lambda i: (layer, i, 0)),
                 stacked(ple_w), const((1, d)), stacked(gate_w), const((1, d))]
    args = [h, *ys, *([w_out] * len(ys)), p, ple_w, ple_norm.reshape(1, d), gate_w,
            norm.reshape(1, d)]
    row_spec = pl.BlockSpec((tm, d), lambda i: (i, 0))
    return pl.pallas_call(
        functools.partial(_out_kernel, n_y=len(ys), final=final),
        grid=(m // tm,),
        in_specs=in_specs,
        out_specs=row_spec if final else [row_spec, row_spec],
        out_shape=(jax.ShapeDtypeStruct((m, d), F32) if final else
                   [jax.ShapeDtypeStruct((m, d), F32), jax.ShapeDtypeStruct((m, d), BF16)]),
        compiler_params=pltpu.CompilerParams(
            dimension_semantics=("parallel",), vmem_limit_bytes=VMEM_LIMIT),
        name="out_proj_final" if final else "out_proj",
    )(*args)


def kernel(x, p, e_norm, e_w_in, a_lb_logits, a_norm, b_conv_w, b_conv_b, b_w_r, b_b_r, b_w_i, b_b_i, b_lambda, e_w_out, o_norm, o_w_in, c_conv_w, c_conv_b, c_b_i, c_b_f, c_norm, o_w_out, ple_w, ple_norm, ple_gate_w, final_norm):
    bsz, s_len, d = x.shape
    depth = p.shape[0]
    m = bsz * s_len
    a_width = a_norm.shape[1]
    b_width = b_lambda.shape[1]
    a_dk = LANES
    a_heads = a_width // a_dk
    c_heads = c_b_i.shape[1]
    c_dv = c_norm.shape[1] // c_heads
    c_dk = c_conv_w.shape[2] // (2 * c_heads)
    c_main = o_w_in.shape[2] - 2 * c_heads

    h = x.reshape(m, d)
    ready = {}
    take = lambda name, w: ready.pop(name) if name in ready else w.astype(BF16)
    w_in_t = lambda j: o_w_in[j].T
    hn = None
    layer_norm = lambda i: e_norm[i // 2] if i % 2 == 0 else o_norm[i // 2]
    for i in range(depth):
        j = i // 2
        last = i == depth - 1
        xin, g = (h, layer_norm(i)) if hn is None else (hn, None)
        if i % 2 == 0:
            a_qk = a_heads * a_dk
            spans = (("silu", a_qk), ("hgate", a_qk), ("id", a_width), ("silu", a_width),
                     ("id", b_width), ("silu", b_width))
            u = _in_proj(xin, g, take(("e_w_in", j), e_w_in[j]), spans=spans,
                         lb_logits=a_lb_logits, layer=j).reshape(bsz, s_len, -1)
            jobs = {("o_w_in", j): w_in_t(j)} if not last else {}
            if i == 0:
                jobs.update(gate_w=ple_gate_w.reshape(depth * d, d))
            ya, *done = _hgrn2(u, a_norm[j], heads=a_heads, dk=a_dk, casts=list(jobs.values()))
            ready.update(zip(jobs, done))
            jobs = {("e_w_out", j): e_w_out[j]}
            if i == 0:
                jobs.update(ple_w=ple_w.reshape(-1, d))
            x_off = (2 * a_heads * a_dk + 2 * a_width) // LANES
            yb, *done = _rglru(u, b_conv_w[j], b_conv_b[j], b_w_r[j].astype(BF16), b_b_r[j],
                               b_w_i[j].astype(BF16), b_b_i[j], b_lambda[j], x_off=x_off,
                               z_off=x_off + b_width // LANES, casts=list(jobs.values()))
            ready.update(zip(jobs, done))
            if i == 0:
                gate_w_bf = ready.pop("gate_w").reshape(depth, d, d)
                ple_w_bf = ready.pop("ple_w").reshape(ple_w.shape)
            w_out = take(("e_w_out", j), e_w_out[j])
            ys = [ya.reshape(m, a_width), yb.reshape(m, b_width)]
        else:
            pad = (0, LANES - c_heads)
            gate_rows = lax.optimization_barrier(o_w_in[j][:, c_main:]).T
            wg_t = jnp.concatenate([jnp.pad(gate_rows[:c_heads], (pad, (0, 0))),
                                    jnp.pad(gate_rows[c_heads:], (pad, (0, 0)))]).astype(BF16)
            gate_bias = jnp.concatenate([jnp.pad(c_b_i[j], pad),
                                         jnp.pad(c_b_f[j], pad)]).reshape(1, 2 * LANES)
            u, mu, gi, en, ks, dec, rt = _in_proj(
                xin, g, take(("o_w_in", j), w_in_t(j)[:c_main]), wg_t, gate_bias,
                spans=(("id", c_main),), seq_rows=s_len, transposed=True)
            col = lambda a: a.reshape(bsz, s_len, LANES)
            gate_terms = (col(mu), col(gi), col(en), col(ks), dec.reshape(bsz, -1, LANES), rt)
            jobs = {("o_w_out", j): o_w_out[j]}
            if not last:
                jobs[("e_w_in", j + 1)] = e_w_in[j + 1]
            yc, *done = _mlstm(u.reshape(bsz, s_len, -1), gate_terms, c_conv_w[j], c_conv_b[j],
                               c_norm[j], heads=c_heads, dk=c_dk, dv=c_dv,
                               casts=list(jobs.values()))
            ready.update(zip(jobs, done))
            ys = [yc.reshape(m, -1)]
            w_out = take(("o_w_out", j), o_w_out[j])
        res = _out_proj(h, ys, w_out, p.reshape(depth, m, -1), i, ple_w_bf, ple_norm[i],
                        gate_w_bf, final_norm if last else layer_norm(i + 1), last)
        h, hn = (res, None) if last else res
    return h.reshape(bsz, s_len, d)
```

```python
import functools

import jax
import jax.numpy as jnp
from jax import lax
from jax.experimental import pallas as pl
from jax.experimental.pallas import tpu as pltpu

EPS = 1e-6
LRU_C = 8.0
CONV_W = 4
LANES = 128
SUBLANES = 8
VMEM_LIMIT = 56 * 1024 * 1024
BF16 = jnp.bfloat16
F32 = jnp.float32


def _sigmoid(x):
    return jax.nn.sigmoid(x)


def _silu(x):
    return x * jax.nn.sigmoid(x)


def _softplus(x):
    return jnp.maximum(x, 0.0) + jnp.log1p(jnp.exp(-jnp.abs(x)))


def _log_sigmoid(x):
    return -_softplus(-x)


def _shift_rows(x, s, row):
    return jnp.where(row >= s, pltpu.roll(x, s, 0), 0.0)


def _cumsum_rows(x, row):
    s = 1
    while s < x.shape[0]:
        x = x + _shift_rows(x, s, row)
        s *= 2
    return x


def _cummax_rows(x, row):
    s = 1
    while s < x.shape[0]:
        x = jnp.maximum(x, jnp.where(row >= s, pltpu.roll(x, s, 0), -jnp.inf))
        s *= 2
    return x


def _causal_conv(x_ref, cw_ref, cb_ref):
    s_len = x_ref.shape[1]
    taps = [cw_ref[k:k + 1, :] for k in range(CONV_W)]
    acc = cb_ref[...] + x_ref[0, SUBLANES:, :] * taps[CONV_W - 1]
    for j in range(1, CONV_W):
        acc = acc + x_ref[0, SUBLANES - j:s_len - j, :] * taps[CONV_W - 1 - j]
    x0 = x_ref[0, 0:SUBLANES, :]
    row = lax.broadcasted_iota(jnp.int32, x0.shape, 0)
    acc0 = cb_ref[...] + x0 * taps[CONV_W - 1]
    for j in range(1, CONV_W):
        acc0 = acc0 + _shift_rows(x0, j, row) * taps[CONV_W - 1 - j]
    return jnp.concatenate([acc0, acc], axis=0)


def _with_casts(body, n_in, n_out, casts, grid):
    steps = grid[0] * grid[1]

    def kernel(*refs):
        ins, refs = refs[:n_in], refs[n_in:]
        cast_in, refs = refs[:len(casts)], refs[len(casts):]
        outs, refs = refs[:n_out], refs[n_out:]
        cast_out, scratch = refs[:len(casts)], refs[len(casts):]
        for src, dst in zip(cast_in, cast_out):
            dst[...] = src[...].astype(dst.dtype)
        body(*ins, *outs, *scratch)

    rows = [w.shape[0] // (steps * 16) * 16 for w in casts]
    specs = [pl.BlockSpec((r, w.shape[1]), lambda a, b: (a * grid[1] + b, 0))
             for r, w in zip(rows, casts)]
    shapes = [jax.ShapeDtypeStruct((r * steps, w.shape[1]), BF16) for r, w in zip(rows, casts)]
    return kernel, specs, list(specs), shapes


def _dot(a, b):
    return jnp.dot(a.astype(BF16), b.astype(BF16), preferred_element_type=F32)


def _dot_nt(a, b):
    return lax.dot_general(a.astype(BF16), b.astype(BF16), (((1,), (1,)), ((), ())),
                           preferred_element_type=F32)


def _dot_tn(a, b):
    return lax.dot_general(a.astype(BF16), b.astype(BF16), (((0,), (0,)), ((), ())),
                           preferred_element_type=F32)


SUB_ROWS = 256
COL_PIECE = 512


def _gate_terms(t, m, mu_ref, gi_ref, en_ref, ks_ref, dec_ref, rt_ref):
    rows = t.shape[0]
    c = ML_CHUNK
    li = t[:, :LANES]
    lf = _log_sigmoid(t[:, LANES:])
    row = lax.broadcasted_iota(jnp.int32, (rows, LANES), 0) % c

    def scan(x, op, fill):
        s = 1
        while s < c:
            x = op(x, jnp.where(row >= s, pltpu.roll(x, s, 0), fill))
            s *= 2
        return x

    b = scan(lf, jnp.add, 0.0)
    r = li - b
    rho = scan(r, jnp.maximum, -jnp.inf)
    for ci in range(rows // c):
        sl = slice(ci * c, (ci + 1) * c)
        b_c, r_c, rho_c = b[sl], r[sl], rho[sl]
        b_last = b_c[c - 1:c, :]
        m_new = b_last + jnp.maximum(m, rho_c[c - 1:c, :])
        mu = jnp.maximum(rho_c, m)
        mu_ref[sl, :] = mu
        gi_ref[sl, :] = jnp.exp(m - mu)
        en_ref[sl, :] = jnp.exp(-(b_c + mu))
        ks_ref[sl, :] = jnp.exp(b_last + r_c - m_new)
        dec_ref[ci:ci + 1, :] = jnp.exp(b_last + m - m_new)
        rt_ref[0, :, sl] = r_c.T
        m = m_new
    return m


def _in_proj_kernel(*refs, with_norm, with_gates, transposed, spans, layer, seq_blocks):
    kinds = {k for k, _ in spans}
    refs = list(refs)
    x_ref = refs.pop(0)
    g_ref = refs.pop(0) if with_norm else None
    w_ref = refs.pop(0)
    wg_ref = refs.pop(0) if with_gates else None
    lbl_ref = refs.pop(0) if "hgate" in kinds else None
    gb_ref = refs.pop(0) if with_gates else None
    o_ref = refs.pop(0)
    gate_refs = [refs.pop(0) for _ in range(6)] if with_gates else None
    hn_ref = refs.pop(0) if with_norm else x_ref
    m_ref = refs.pop(0) if with_gates else None
    mm = _dot_nt if transposed else functools.partial(jnp.dot, preferred_element_type=F32)
    tm, tn = o_ref.shape
    i, j = pl.program_id(0), pl.program_id(1)

    starts, c0 = [], 0
    for kind, width in spans:
        starts.append((kind, c0, width))
        c0 += width
    n = c0

    if lbl_ref is not None:
        lg = lbl_ref[...]
        e = jnp.exp(lg - jnp.max(lg, axis=0, keepdims=True))
        sm = e / jnp.sum(e, axis=0, keepdims=True)
        lb = jnp.sum(sm[:layer + 1], axis=0, keepdims=True)

    def activate(kind, r, c0):
        if kind == "silu":
            return _silu(r)
        if kind == "sigmoid":
            return _sigmoid(r)
        if kind == "hgate":
            lbp = lb[:, c0:c0 + r.shape[1]]
            return lbp + (1.0 - lbp) * _sigmoid(r)
        return r

    def run_tile(jt, blocked):
        tile_c0 = jt * tn
        pieces = []
        for kind, c0, width in starts:
            lo, hi = max(c0, tile_c0), min(c0 + width, tile_c0 + tn)
            for p0 in range(lo, hi, COL_PIECE if blocked else hi - lo):
                pieces.append((kind, p0 - tile_c0, min(COL_PIECE, hi - p0) if blocked else hi - lo,
                               p0 - c0))
        def gates():
            @pl.when(i % seq_blocks == 0)
            def _():
                m_ref[...] = jnp.zeros_like(m_ref)
            t = mm(hn_ref[...], wg_ref[...]) + gb_ref[...]
            m_ref[...] = _gate_terms(t, m_ref[...], *gate_refs)

        if with_gates and jt == 0 and not with_norm:
            gates()
        for r0 in range(0, tm, SUB_ROWS if blocked else tm):
            rows = slice(r0, r0 + (SUB_ROWS if blocked else tm))
            if with_norm and jt == 0:
                x = x_ref[rows, :]
                ms = jnp.mean(x * x, axis=-1, keepdims=True)
                hn = (x * lax.rsqrt(ms + EPS) * g_ref[...]).astype(BF16)
                hn_ref[rows, :] = hn
            else:
                hn = hn_ref[rows, :]
            for kind, off, width, c0 in pieces:
                wp = w_ref[off:off + width, :] if transposed else w_ref[:, off:off + width]
                o_ref[rows, off:off + width] = activate(kind, mm(hn, wp), c0)
        if with_gates and jt == 0 and with_norm:
            gates()

    if kinds == {"id"}:
        pl.when(j == 0)(functools.partial(run_tile, 0, with_norm))
        pl.when(j != 0)(functools.partial(run_tile, 1, False))
    else:
        for jt in range(n // tn):
            pl.when(j == jt)(functools.partial(run_tile, jt, True))


IN_PROJ_ROWS = 1024
IN_PROJ_VMEM = 46 * 1024 * 1024


def _in_proj_cols(n, tm, d, gw, x_bytes, with_norm):
    for tn in range(n, 0, -512):
        need = (2 * tm * d * x_bytes + (tm * d * 2 if with_norm else 0) + 2 * d * tn * 2
                + 2 * tm * tn * 4 + 2 * (d * gw * 2 + tm * gw * 4))
        if n % tn == 0 and need <= IN_PROJ_VMEM:
            return tn
    raise ValueError("no column tile fits")


def _in_proj(x, g, w, wg=None, gate_bias=None, *, spans, seq_rows=None, transposed=False,
             lb_logits=None, layer=0, casts=(), tm=IN_PROJ_ROWS):
    m, d = x.shape
    n = sum(width for _, width in spans)
    with_gates = wg is not None
    with_norm = g is not None
    gw = wg.shape[0 if transposed else 1] if with_gates else 0
    w_spec = (lambda cols, idx: pl.BlockSpec((cols, d), lambda i, j: (idx(j), 0)) if transposed
              else pl.BlockSpec((d, cols), lambda i, j: (0, idx(j))))
    const = lambda a: pl.BlockSpec(a.shape, lambda i, j: (0,) * a.ndim)
    tn = _in_proj_cols(n, tm, d, gw, x.dtype.itemsize, with_norm)
    in_specs = [pl.BlockSpec((tm, d), lambda i, j: (i, 0))]
    args = [x]
    if with_norm:
        args.append(g.reshape(1, d))
        in_specs.append(const(args[-1]))
    in_specs.append(w_spec(tn, lambda j: j))
    args.append(w)
    out_shape = [jax.ShapeDtypeStruct((m, n), F32)]
    out_specs = [pl.BlockSpec((tm, tn), lambda i, j: (i, j))]
    scratch = [pltpu.VMEM((tm, d), BF16)] if with_norm else []
    if with_gates:
        in_specs.append(w_spec(gw, lambda j: 0))
        args.append(wg)
    if lb_logits is not None:
        args.append(lb_logits)
        in_specs.append(const(lb_logits))
    if with_gates:
        assert gw == 2 * LANES and seq_rows % tm == 0 and tm % ML_CHUNK == 0
        spb = seq_rows // tm
        args.append(gate_bias)
        in_specs.append(const(gate_bias))
        col = pl.BlockSpec((tm, LANES), lambda i, j: (i, 0))
        out_specs += [col] * 4 + [pl.BlockSpec((tm // ML_CHUNK, LANES), lambda i, j: (i, 0)),
                                  pl.BlockSpec((1, LANES, tm), lambda i, j: (i // spb, 0, i % spb))]
        out_shape += ([jax.ShapeDtypeStruct((m, LANES), F32)] * 4
                      + [jax.ShapeDtypeStruct((m // ML_CHUNK, LANES), F32),
                         jax.ShapeDtypeStruct((m // seq_rows, LANES, seq_rows), F32)])
        scratch.append(pltpu.VMEM((1, LANES), F32))
    grid = (m // tm, n // tn)
    body = functools.partial(_in_proj_kernel, with_norm=with_norm, with_gates=with_gates,
                             transposed=transposed, spans=tuple(spans), layer=layer,
                             seq_blocks=seq_rows // tm if with_gates else 1)
    body, c_in, c_out, c_shapes = _with_casts(body, len(args), len(out_shape), casts, grid)
    res = pl.pallas_call(
        body,
        grid=grid,
        in_specs=in_specs + c_in,
        out_specs=out_specs + c_out,
        out_shape=out_shape + c_shapes,
        scratch_shapes=scratch,
        compiler_params=pltpu.CompilerParams(
            dimension_semantics=("arbitrary" if with_gates else "parallel", "arbitrary"),
            vmem_limit_bytes=VMEM_LIMIT),
        name="in_proj_gates" if with_gates else "in_proj",
    )(*args, *casts)
    return res if with_gates or casts else res[0]


HG_CHUNK = 128
HG_GROUPS = HG_CHUNK // SUBLANES
HG_INFLIGHT = 8


def _hgrn2_kernel(q_ref, f_ref, i_ref, z_ref, an_ref, o_ref,
                  st_ref, qs_ref, ks_ref, bs_ref, od_ref, tri_ref):
    s_len = q_ref.shape[1]
    dk = q_ref.shape[2]
    c = HG_CHUNK
    ng = HG_GROUPS
    assert c == dk == LANES and i_ref.shape[2] == dk

    an = an_ref[...]

    rowc = lax.broadcasted_iota(jnp.int32, (c, c), 0)
    colc = lax.broadcasted_iota(jnp.int32, (c, c), 1)
    tri_ref[...] = (colc <= rowc).astype(BF16)
    same_group = (colc % ng) == (rowc % ng)
    lane = lax.broadcasted_iota(jnp.int32, (SUBLANES, c), 1)
    diag_off = lane - lax.broadcasted_iota(jnp.int32, (SUBLANES, c), 0)
    zeros8 = jnp.zeros((SUBLANES, dk), F32)
    zeros16 = jnp.zeros((ng, dk), F32)

    st_ref[...] = jnp.zeros_like(st_ref)

    def rows8(x, r):
        return x[r * SUBLANES:(r + 1) * SUBLANES, :]

    def chunk(ci, slot):
        qs, ks, bs, od = qs_ref.at[slot], ks_ref.at[slot], bs_ref.at[slot], od_ref.at[slot]
        r0 = pl.multiple_of(ci * c, c)
        q = q_ref[0, pl.ds(r0, c), :]
        f = f_ref[0, pl.ds(r0, c), :]
        v = i_ref[0, pl.ds(r0, c), :]
        k = 1.0 - f

        lf = jnp.log2(f)
        hi = lf.astype(BF16)
        r1 = lf - hi.astype(F32)
        mid = r1.astype(BF16)
        lo = (r1 - mid.astype(F32)).astype(BF16)
        bb = jnp.dot(tri_ref[...], jnp.concatenate([hi, mid, lo], axis=1),
                     preferred_element_type=F32)
        yield
        b2 = bb[:, :dk] + bb[:, dk:2 * dk] + bb[:, 2 * dk:]
        qs[...] = q
        ks[...] = k
        bs[...] = b2
        b_last = bs[c - 1:c, :]
        upd = _dot_tn(v, k * jnp.exp2(b_last - b2))

        ps, rights = [], []
        for j in range(ng.bit_length() - 1):
            lhs, rhs, rights_j = [], [], []
            for r in range(ng):
                mid_grp = ((r >> (j + 1)) << (j + 1)) | ((1 << j) - 1)
                beta = bs[mid_grp * SUBLANES + SUBLANES - 1:(mid_grp + 1) * SUBLANES, :]
                if (r >> j) & 1:
                    lhs.append(rows8(q, r) * jnp.exp2(rows8(b2, r) - beta))
                    rhs.append(zeros8)
                    rights_j.append(r)
                else:
                    rhs.append(rows8(k, r) * jnp.exp2(beta - rows8(b2, r)))
            ps.append(_dot_nt(jnp.concatenate(lhs, axis=0), jnp.concatenate(rhs, axis=0)))
            rights.append(rights_j)
        yield

        st = st_ref[...]
        o = _dot_nt(q * jnp.exp2(b2), st)
        st_ref[...] = st * jnp.exp2(b_last) + upd

        def permuted(ref):
            return [ref[pl.ds(pos, ng, stride=SUBLANES), :] for pos in range(SUBLANES)]

        q_p, k_p, b_p = permuted(qs), permuted(ks), permuted(bs)
        lhs_slots, rhs_slots = [], []
        for j in range(SUBLANES.bit_length() - 1):
            for blk in range(SUBLANES >> (j + 1)):
                mid_pos = (blk << (j + 1)) | ((1 << j) - 1)
                lhs, rhs = [], []
                for pos in range(SUBLANES):
                    if pos >> (j + 1) != blk:
                        lhs.append(zeros16)
                        rhs.append(zeros16)
                    elif (pos >> j) & 1:
                        lhs.append(q_p[pos] * jnp.exp2(b_p[pos] - b_p[mid_pos]))
                        rhs.append(zeros16)
                    else:
                        lhs.append(zeros16)
                        rhs.append(k_p[pos] * jnp.exp2(b_p[mid_pos] - b_p[pos]))
                lhs_slots.append(jnp.concatenate(lhs, axis=0).astype(BF16))
                rhs_slots.append(jnp.concatenate(rhs, axis=0).astype(BF16))
        pd = _dot_nt(jnp.concatenate(lhs_slots, axis=1), jnp.concatenate(rhs_slots, axis=1))
        yield

        dg = jnp.sum(q * k, axis=1, keepdims=True)
        a_rows = [jnp.where(diag_off == r * SUBLANES, rows8(dg, r), 0.0) for r in range(ng)]
        for j, (p, rights_j) in enumerate(zip(ps, rights)):
            for i, r in enumerate(rights_j):
                right_start = (((r >> (j + 1)) << (j + 1)) | (1 << j)) * SUBLANES
                a_rows[r] = jnp.where(lane < right_start, rows8(p, i), a_rows[r])
        o = o + _dot(jnp.concatenate(a_rows, axis=0), v)
        v_p = [i_ref[0, pl.ds(r0 + pos, ng, stride=SUBLANES), :] for pos in range(SUBLANES)]
        od[...] = _dot(jnp.where(same_group, pd, 0.0), jnp.concatenate(v_p, axis=0))
        yield

        o = o + jnp.concatenate(
            [od[pl.ds(g, SUBLANES, stride=ng), :] for g in range(ng)], axis=0)
        yn = o * lax.rsqrt(jnp.mean(o * o, axis=-1, keepdims=True) + EPS) * an
        z = z_ref[0, pl.ds(r0, c), :]
        o_ref[0, pl.ds(r0, c), :] = (yn * z).astype(o_ref.dtype)

    def chunks(i, carry):
        stages = [chunk(i * HG_INFLIGHT + slot, slot) for slot in range(HG_INFLIGHT)]
        while stages:
            stages = [g for g in stages if next(g, stages) is not stages]
        return carry

    lax.fori_loop(0, s_len // (c * HG_INFLIGHT), chunks, 0)


def _hgrn2(u, a_norm, *, heads, dk, casts=()):
    bsz, s_len, _ = u.shape
    blk = lambda off: pl.BlockSpec((1, s_len, dk), lambda b, h, off=off: (b, 0, off + h))
    grid = (bsz, heads)
    body, c_in, c_out, c_shapes = _with_casts(_hgrn2_kernel, 5, 1, casts, grid)
    return pl.pallas_call(
        body,
        grid=grid,
        in_specs=[blk(0), blk(heads), blk(2 * heads), blk(3 * heads),
                  pl.BlockSpec((1, dk), lambda b, h: (0, h))] + c_in,
        out_specs=[pl.BlockSpec((1, s_len, dk), lambda b, h: (b, 0, h))] + c_out,
        out_shape=[jax.ShapeDtypeStruct((bsz, s_len, heads * dk), BF16)] + c_shapes,
        scratch_shapes=[pltpu.VMEM((dk, dk), F32)]
        + [pltpu.VMEM((HG_INFLIGHT, HG_CHUNK, dk), F32)] * 4
        + [pltpu.VMEM((HG_CHUNK, HG_CHUNK), BF16)],
        compiler_params=pltpu.CompilerParams(
            dimension_semantics=("parallel", "parallel"), vmem_limit_bytes=VMEM_LIMIT),
        name="hgrn2",
    )(u, u, u, u, a_norm.reshape(1, -1), *casts)


def _rglru_kernel(x_ref, z_ref, cw_ref, cb_ref, wr_ref, br_ref, wi_ref, bi_ref, lam_ref,
                  o_ref, a_ref, u_ref, h_ref, ca_ref, cu_ref):
    s_len = x_ref.shape[1]
    w = x_ref.shape[2]
    ng = s_len // SUBLANES
    xc = _causal_conv(x_ref, cw_ref, cb_ref)
    r = _sigmoid(_dot(xc, wr_ref[0]) + br_ref[...])
    ig = _sigmoid(_dot(xc, wi_ref[0]) + bi_ref[...])
    log_a = -LRU_C * r * _softplus(-lam_ref[...])
    a = jnp.exp(log_a)
    a_ref[...] = a
    t = -jnp.tanh(log_a) * (a * a + 1.0)
    u_ref[...] = jnp.where(t > 0.0, t * lax.rsqrt(t), 0.0) * (ig * xc)

    ca = cu = None
    for pos in range(SUBLANES):
        ap = a_ref[pl.ds(pos, ng, stride=SUBLANES), :]
        up = u_ref[pl.ds(pos, ng, stride=SUBLANES), :]
        ca, cu = (ap, up) if pos == 0 else (ap * ca, ap * cu + up)
        ca_ref[pos] = ca
        cu_ref[pos] = cu
    grp = lax.broadcasted_iota(jnp.int32, (ng, w), 0)
    ta, tu = ca, cu
    s = 1
    while s < ng:
        m = grp >= s
        tu = tu + ta * jnp.where(m, pltpu.roll(tu, s, 0), 0.0)
        ta = ta * jnp.where(m, pltpu.roll(ta, s, 0), 1.0)
        s *= 2
    h_in = jnp.where(grp >= 1, pltpu.roll(tu, 1, 0), 0.0)
    for pos in range(SUBLANES):
        h_ref[pl.ds(pos, ng, stride=SUBLANES), :] = ca_ref[pos] * h_in + cu_ref[pos]
    o_ref[0] = (h_ref[...] * z_ref[0]).astype(o_ref.dtype)


def _rglru(u, conv_w, conv_b, w_r, b_r, w_i, b_i, lam, *, x_off, z_off, casts=()):
    bsz, s_len, _ = u.shape
    nblk, blk, _ = w_r.shape
    vec = lambda rows: pl.BlockSpec((rows, blk), lambda b, n: (0, n))
    mat = pl.BlockSpec((1, blk, blk), lambda b, n: (n, 0, 0))
    grid = (bsz, nblk)
    body, c_in, c_out, c_shapes = _with_casts(_rglru_kernel, 9, 1, casts, grid)
    return pl.pallas_call(
        body,
        grid=grid,
        in_specs=[pl.BlockSpec((1, s_len, blk), lambda b, n: (b, 0, x_off + n)),
                  pl.BlockSpec((1, s_len, blk), lambda b, n: (b, 0, z_off + n)),
                  vec(CONV_W), vec(1), mat, vec(1), mat, vec(1), vec(1)] + c_in,
        out_specs=[pl.BlockSpec((1, s_len, blk), lambda b, n: (b, 0, n))] + c_out,
        out_shape=[jax.ShapeDtypeStruct((bsz, s_len, nblk * blk), BF16)] + c_shapes,
        scratch_shapes=[pltpu.VMEM((s_len, blk), F32)] * 3
        + [pltpu.VMEM((SUBLANES, s_len // SUBLANES, blk), F32)] * 2,
        compiler_params=pltpu.CompilerParams(
            dimension_semantics=("parallel", "parallel"), vmem_limit_bytes=VMEM_LIMIT),
        name="rglru",
    )(u, u, conv_w, conv_b.reshape(1, -1), w_r, b_r.reshape(1, -1), w_i, b_i.reshape(1, -1),
      lam.reshape(1, -1), *casts)


ML_CHUNK = 128
ML_UNROLL = 8


def _mlstm_kernel(q_ref, k_ref, v_ref, og_ref, z_ref, mu_ref, gi_ref, en_ref, ks_ref, dec_ref,
                  rt_ref, cwq_ref, cbq_ref, cwk_ref, cbk_ref, cn_ref, o_ref, qs_ref, ks_s_ref,
                  cst_ref, nst_ref):
    s_len = q_ref.shape[1]
    dk = q_ref.shape[2]
    c = ML_CHUNK
    assert c == dk == LANES
    h = pl.program_id(1)

    qs_ref[...] = _silu(_causal_conv(q_ref, cwq_ref, cbq_ref)) * (dk ** -0.5)
    ks_s_ref[...] = _silu(_causal_conv(k_ref, cwk_ref, cbk_ref))
    cst_ref[...] = jnp.zeros_like(cst_ref)
    nst_ref[...] = jnp.zeros_like(nst_ref)

    row = lax.broadcasted_iota(jnp.int32, (c, c), 0)
    col = lax.broadcasted_iota(jnp.int32, (c, c), 1)
    causal = col <= row
    head = col == h
    cn = cn_ref[...]

    def pick(x):
        return jnp.sum(jnp.where(head, x, 0.0), axis=1, keepdims=True)

    dec_all = dec_ref[0]
    chunk_id = lax.broadcasted_iota(jnp.int32, dec_all.shape, 0)
    head_lane = lax.broadcasted_iota(jnp.int32, (1, LANES), 1) == h
    head_row = lax.broadcasted_iota(jnp.int32, (SUBLANES, c), 0) == h

    def chunk(ci, carry):
        r0 = pl.multiple_of(ci * c, c)
        q = qs_ref[pl.ds(r0, c), :]
        k = ks_s_ref[pl.ds(r0, c), :]
        v = v_ref[0, pl.ds(r0, c), :]
        mu = pick(mu_ref[0, pl.ds(r0, c), :])
        g_inter = pick(gi_ref[0, pl.ds(r0, c), :])
        e_negm = pick(en_ref[0, pl.ds(r0, c), :])
        kscale = pick(ks_ref[0, pl.ds(r0, c), :])
        decay = jnp.sum(jnp.where(chunk_id == ci, dec_all, 0.0), axis=0, keepdims=True)
        decay = jnp.sum(jnp.where(head_lane, decay, 0.0), axis=1, keepdims=True)
        r_row = jnp.sum(jnp.where(head_row, rt_ref[0, :, pl.ds(r0, c)], 0.0),
                        axis=0, keepdims=True)
        wts = jnp.where(causal, jnp.exp(jnp.minimum(r_row - mu, 0.0)), 0.0)
        qk = _dot_nt(q, k)
        kw = k * kscale
        upd = _dot_tn(kw, v)

        cst = cst_ref[...]
        n_st = nst_ref[...]
        sc = qk * wts
        inter = _dot(q, cst)
        intra = _dot(sc, v)
        cst_ref[...] = decay * cst + upd
        nst_ref[...] = decay * n_st + jnp.sum(kw, axis=0, keepdims=True)

        num = g_inter * inter + intra
        den = (g_inter * jnp.sum(q * n_st, axis=1, keepdims=True)
               + jnp.sum(sc, axis=1, keepdims=True))
        inv = 1.0 / jnp.maximum(jnp.abs(den), e_negm)
        msq = jnp.mean(num * num, axis=-1, keepdims=True)
        yn = num * (inv * lax.rsqrt(inv * inv * msq + EPS)) * cn
        og = og_ref[0, pl.ds(r0, c), :]
        z = z_ref[0, pl.ds(r0, c), :]
        o_ref[0, pl.ds(r0, c), :] = (yn * _sigmoid(og) * _silu(z)).astype(o_ref.dtype)
        return carry

    lax.fori_loop(0, s_len // c, chunk, 0, unroll=ML_UNROLL)


def _mlstm(u, gate_terms, conv_w, conv_b, c_norm, *, heads, dk, dv, casts=()):
    bsz, s_len, _ = u.shape
    grid = (bsz, heads)
    body, c_in, c_out, c_shapes = _with_casts(_mlstm_kernel, 16, 1, casts, grid)
    nc = s_len // ML_CHUNK
    col_spec = pl.BlockSpec((1, s_len, LANES), lambda b, h: (b, 0, 0))
    qk_blk = lambda off: pl.BlockSpec((1, s_len, dk), lambda b, h, off=off: (b, 0, off + h))
    v_blk = lambda off: pl.BlockSpec((1, s_len, dv), lambda b, h, off=off: (b, 0, off + h))
    cw = lambda off: pl.BlockSpec((CONV_W, dk), lambda b, h, off=off: (0, off + h))
    cb = lambda off: pl.BlockSpec((1, dk), lambda b, h, off=off: (0, off + h))
    nqk = 2 * heads * dk // dv
    return pl.pallas_call(
        body,
        grid=grid,
        in_specs=[qk_blk(0), qk_blk(heads),
                  v_blk(nqk), v_blk(nqk + heads), v_blk(nqk + 2 * heads),
                  col_spec, col_spec, col_spec, col_spec,
                  pl.BlockSpec((1, nc, LANES), lambda b, h: (b, 0, 0)),
                  pl.BlockSpec((1, SUBLANES, s_len), lambda b, h: (b, 0, 0)),
                  cw(0), cb(0), cw(heads), cb(heads),
                  pl.BlockSpec((1, dv), lambda b, h: (0, h))] + c_in,
        out_specs=[pl.BlockSpec((1, s_len, dv), lambda b, h: (b, 0, h))] + c_out,
        out_shape=[jax.ShapeDtypeStruct((bsz, s_len, heads * dv), BF16)] + c_shapes,
        scratch_shapes=[pltpu.VMEM((s_len, dk), F32), pltpu.VMEM((s_len, dk), F32),
                        pltpu.VMEM((dk, dv), F32), pltpu.VMEM((1, dk), F32)],
        compiler_params=pltpu.CompilerParams(
            dimension_semantics=("parallel", "parallel"), vmem_limit_bytes=VMEM_LIMIT),
        name="mlstm",
    )(u, u, u, u, u, *gate_terms, conv_w, conv_b.reshape(1, -1), conv_w,
      conv_b.reshape(1, -1), c_norm.reshape(1, -1), *casts)


OUT_SUB_ROWS = 256


def _out_kernel(h_ref, *rest, n_y, final):
    y_refs = rest[:n_y]
    wo_refs = rest[n_y:2 * n_y]
    p_ref, pw_ref, pn_ref, gw_ref, norm_ref = rest[2 * n_y:2 * n_y + 5]
    out_refs = rest[2 * n_y + 5:]

    def rows_block(r0):
        rows = slice(r0, r0 + OUT_SUB_ROWS)
        mix = jnp.dot(y_refs[0][rows, :], wo_refs[0][...], preferred_element_type=F32)
        for y_ref, wo_ref in zip(y_refs[1:], wo_refs[1:]):
            mix = mix + jnp.dot(y_ref[rows, :], wo_ref[...], preferred_element_type=F32)
        yield
        h1 = h_ref[rows, :] + mix
        pe = _dot(p_ref[rows, :], pw_ref[...])
        gate = _dot(h1, gw_ref[...])
        yield
        pe = pe * lax.rsqrt(jnp.mean(pe * pe, axis=-1, keepdims=True) + EPS) * pn_ref[...]
        h2 = h1 + _sigmoid(gate) * pe
        hn = h2 * lax.rsqrt(jnp.mean(h2 * h2, axis=-1, keepdims=True) + EPS) * norm_ref[...]
        if final:
            out_refs[0][rows, :] = hn
        else:
            out_refs[0][rows, :] = h2
            out_refs[1][rows, :] = hn.astype(out_refs[1].dtype)

    stages = [rows_block(r0) for r0 in range(0, h_ref.shape[0], OUT_SUB_ROWS)]
    while stages:
        stages = [g for g in stages if next(g, stages) is not stages]


def _out_proj(h, ys, w_out, p, layer, ple_w, ple_norm, gate_w, norm, final, *, tm=512):
    m, d = h.shape
    const = lambda shape: pl.BlockSpec(shape, lambda i: (0, 0), pipeline_mode=pl.Buffered(1))
    in_specs = [pl.BlockSpec((tm, d), lambda i: (i, 0))]
    in_specs += [pl.BlockSpec((tm, y.shape[1]), lambda i: (i, 0)) for y in ys]
    assert all(y.shape[1] == ys[0].shape[1] for y in ys)
    in_specs += [pl.BlockSpec((y.shape[1], d), lambda i, r=r: (r, 0), pipeline_mode=pl.Buffered(1))
                 for r, y in enumerate(ys)]
    stacked = lambda a: pl.BlockSpec((None,) + a.shape[1:], lambda i: (layer, 0, 0),
                                     pipeline_mode=pl.Buffered(1))
    in_specs += [pl.BlockSpec((None, tm, p.shape[2]), lambda i: (layer, i, 0)),
                 stacked(ple_w), const((1, d)), stacked(gate_w), const((1, d))]
    args = [h, *ys, *([w_out] * len(ys)), p, ple_w, ple_norm.reshape(1, d), gate_w,
            norm.reshape(1, d)]
    row_spec = pl.BlockSpec((tm, d), lambda i: (i, 0))
    return pl.pallas_call(
        functools.partial(_out_kernel, n_y=len(ys), final=final),
        grid=(m // tm,),
        in_specs=in_specs,
        out_specs=row_spec if final else [row_spec, row_spec],
        out_shape=(jax.ShapeDtypeStruct((m, d), F32) if final else
                   [jax.ShapeDtypeStruct((m, d), F32), jax.ShapeDtypeStruct((m, d), BF16)]),
        compiler_params=pltpu.CompilerParams(
            dimension_semantics=("parallel",), vmem_limit_bytes=VMEM_LIMIT),
        name="out_proj_final" if final else "out_proj",
    )(*args)


def kernel(x, p, e_norm, e_w_in, a_lb_logits, a_norm, b_conv_w, b_conv_b, b_w_r, b_b_r, b_w_i, b_b_i, b_lambda, e_w_out, o_norm, o_w_in, c_conv_w, c_conv_b, c_b_i, c_b_f, c_norm, o_w_out, ple_w, ple_norm, ple_gate_w, final_norm):
    bsz, s_len, d = x.shape
    depth = p.shape[0]
    m = bsz * s_len
    a_width = a_norm.shape[1]
    b_width = b_lambda.shape[1]
    a_dk = LANES
    a_heads = a_width // a_dk
    c_heads = c_b_i.shape[1]
    c_dv = c_norm.shape[1] // c_heads
    c_dk = c_conv_w.shape[2] // (2 * c_heads)
    c_main = o_w_in.shape[2] - 2 * c_heads

    h = x.reshape(m, d)
    ready = {}
    take = lambda name, w: ready.pop(name) if name in ready else w.astype(BF16)
    w_in_t = lambda j: o_w_in[j].T
    hn = None
    layer_norm = lambda i: e_norm[i // 2] if i % 2 == 0 else o_norm[i // 2]
    for i in range(depth):
        j = i // 2
        last = i == depth - 1
        xin, g = (h, layer_norm(i)) if hn is None else (hn, None)
        if i % 2 == 0:
            a_qk = a_heads * a_dk
            spans = (("silu", a_qk), ("hgate", a_qk), ("id", a_width), ("silu", a_width),
                     ("id", b_width), ("silu", b_width))
            jobs = {("e_w_out", j): e_w_out[j]}
            if i == 0:
                jobs.update(ple_w=ple_w.reshape(-1, d))
            if not last:
                jobs[("o_w_out", j)] = o_w_out[j]
            u, *done = _in_proj(xin, g, take(("e_w_in", j), e_w_in[j]), spans=spans,
                                lb_logits=a_lb_logits, layer=j, casts=list(jobs.values()))
            ready.update(zip(jobs, done))
            u = u.reshape(bsz, s_len, -1)
            jobs = {("o_w_in", j): w_in_t(j)} if not last else {}
            if i == 0:
                jobs.update(gate_w=ple_gate_w.reshape(depth * d, d))
            ya, *done = _hgrn2(u, a_norm[j], heads=a_heads, dk=a_dk, casts=list(jobs.values()))
            ready.update(zip(jobs, done))
            x_off = (2 * a_heads * a_dk + 2 * a_width) // LANES
            (yb,) = _rglru(u, b_conv_w[j], b_conv_b[j], b_w_r[j].astype(BF16), b_b_r[j],
                           b_w_i[j].astype(BF16), b_b_i[j], b_lambda[j], x_off=x_off,
                           z_off=x_off + b_width // LANES)
            if i == 0:
                gate_w_bf = ready.pop("gate_w").reshape(depth, d, d)
                ple_w_bf = ready.pop("ple_w").reshape(ple_w.shape)
            w_out = take(("e_w_out", j), e_w_out[j])
            ys = [ya.reshape(m, a_width), yb.reshape(m, b_width)]
        else:
            pad = (0, LANES - c_heads)
            gate_rows = lax.optimization_barrier(o_w_in[j][:, c_main:]).T
            wg_t = jnp.concatenate([jnp.pad(gate_rows[:c_heads], (pad, (0, 0))),
                                    jnp.pad(gate_rows[c_heads:], (pad, (0, 0)))]).astype(BF16)
            gate_bias = jnp.concatenate([jnp.pad(c_b_i[j], pad),
                                         jnp.pad(c_b_f[j], pad)]).reshape(1, 2 * LANES)
            u, mu, gi, en, ks, dec, rt = _in_proj(
                xin, g, take(("o_w_in", j), w_in_t(j)[:c_main]), wg_t, gate_bias,
                spans=(("id", c_main),), seq_rows=s_len, transposed=True)
            col = lambda a: a.reshape(bsz, s_len, LANES)
            gate_terms = (col(mu), col(gi), col(en), col(ks), dec.reshape(bsz, -1, LANES), rt)
            jobs = {} if ("o_w_out", j) in ready else {("o_w_out", j): o_w_out[j]}
            if not last:
                jobs[("e_w_in", j + 1)] = e_w_in[j + 1]
            yc, *done = _mlstm(u.reshape(bsz, s_len, -1), gate_terms, c_conv_w[j], c_conv_b[j],
                               c_norm[j], heads=c_heads, dk=c_dk, dv=c_dv,
                               casts=list(jobs.values()))
            ready.update(zip(jobs, done))
            ys = [yc.reshape(m, -1)]
            w_out = take(("o_w_out", j), o_w_out[j])
        res = _out_proj(h, ys, w_out, p.reshape(depth, m, -1), i, ple_w_bf, ple_norm[i],
                        gate_w_bf, final_norm if last else layer_norm(i + 1), last)
        h, hn = (res, None) if last else res
    return h.reshape(bsz, s_len, d)
```

```python
import functools

import jax
import jax.numpy as jnp
from jax import lax
from jax.experimental import pallas as pl
from jax.experimental.pallas import tpu as pltpu

EPS = 1e-6
LRU_C = 8.0
CONV_W = 4
LANES = 128
SUBLANES = 8
VMEM_LIMIT = 56 * 1024 * 1024
BF16 = jnp.bfloat16
F32 = jnp.float32


def _sigmoid(x):
    return jax.nn.sigmoid(x)


def _silu(x):
    return x * jax.nn.sigmoid(x)


def _softplus(x):
    return jnp.maximum(x, 0.0) + jnp.log1p(jnp.exp(-jnp.abs(x)))


def _log_sigmoid(x):
    return -_softplus(-x)


def _shift_rows(x, s, row):
    return jnp.where(row >= s, pltpu.roll(x, s, 0), 0.0)


def _cumsum_rows(x, row):
    s = 1
    while s < x.shape[0]:
        x = x + _shift_rows(x, s, row)
        s *= 2
    return x


def _cummax_rows(x, row):
    s = 1
    while s < x.shape[0]:
        x = jnp.maximum(x, jnp.where(row >= s, pltpu.roll(x, s, 0), -jnp.inf))
        s *= 2
    return x


def _causal_conv(x_ref, cw_ref, cb_ref):
    s_len = x_ref.shape[1]
    taps = [cw_ref[k:k + 1, :] for k in range(CONV_W)]
    acc = cb_ref[...] + x_ref[0, SUBLANES:, :] * taps[CONV_W - 1]
    for j in range(1, CONV_W):
        acc = acc + x_ref[0, SUBLANES - j:s_len - j, :] * taps[CONV_W - 1 - j]
    x0 = x_ref[0, 0:SUBLANES, :]
    row = lax.broadcasted_iota(jnp.int32, x0.shape, 0)
    acc0 = cb_ref[...] + x0 * taps[CONV_W - 1]
    for j in range(1, CONV_W):
        acc0 = acc0 + _shift_rows(x0, j, row) * taps[CONV_W - 1 - j]
    return jnp.concatenate([acc0, acc], axis=0)


def _with_casts(body, n_in, n_out, casts, grid):
    steps = grid[0] * grid[1]

    def kernel(*refs):
        ins, refs = refs[:n_in], refs[n_in:]
        cast_in, refs = refs[:len(casts)], refs[len(casts):]
        outs, refs = refs[:n_out], refs[n_out:]
        cast_out, scratch = refs[:len(casts)], refs[len(casts):]
        for src, dst in zip(cast_in, cast_out):
            dst[...] = src[...].astype(dst.dtype)
        body(*ins, *outs, *scratch)

    rows = [w.shape[0] // (steps * 16) * 16 for w in casts]
    specs = [pl.BlockSpec((r, w.shape[1]), lambda a, b: (a * grid[1] + b, 0))
             for r, w in zip(rows, casts)]
    shapes = [jax.ShapeDtypeStruct((r * steps, w.shape[1]), BF16) for r, w in zip(rows, casts)]
    return kernel, specs, list(specs), shapes


def _dot(a, b):
    return jnp.dot(a.astype(BF16), b.astype(BF16), preferred_element_type=F32)


def _dot_nt(a, b):
    return lax.dot_general(a.astype(BF16), b.astype(BF16), (((1,), (1,)), ((), ())),
                           preferred_element_type=F32)


def _dot_tn(a, b):
    return lax.dot_general(a.astype(BF16), b.astype(BF16), (((0,), (0,)), ((), ())),
                           preferred_element_type=F32)


SUB_ROWS = 256
COL_PIECE = 512


def _gate_terms(t, m, mu_ref, gi_ref, en_ref, ks_ref, dec_ref, rt_ref):
    rows = t.shape[0]
    c = ML_CHUNK
    li = t[:, :LANES]
    lf = _log_sigmoid(t[:, LANES:])
    row = lax.broadcasted_iota(jnp.int32, (rows, LANES), 0) % c

    def scan(x, op, fill):
        s = 1
        while s < c:
            x = op(x, jnp.where(row >= s, pltpu.roll(x, s, 0), fill))
            s *= 2
        return x

    b = scan(lf, jnp.add, 0.0)
    r = li - b
    rho = scan(r, jnp.maximum, -jnp.inf)
    for ci in range(rows // c):
        sl = slice(ci * c, (ci + 1) * c)
        b_c, r_c, rho_c = b[sl], r[sl], rho[sl]
        b_last = b_c[c - 1:c, :]
        m_new = b_last + jnp.maximum(m, rho_c[c - 1:c, :])
        mu = jnp.maximum(rho_c, m)
        mu_ref[sl, :] = mu
        gi_ref[sl, :] = jnp.exp(m - mu)
        en_ref[sl, :] = jnp.exp(-(b_c + mu))
        ks_ref[sl, :] = jnp.exp(b_last + r_c - m_new)
        dec_ref[ci:ci + 1, :] = jnp.exp(b_last + m - m_new)
        rt_ref[0, :, sl] = r_c.T
        m = m_new
    return m


def _in_proj_kernel(*refs, with_norm, with_gates, transposed, spans, layer, seq_blocks):
    kinds = {k for k, _ in spans}
    refs = list(refs)
    x_ref = refs.pop(0)
    g_ref = refs.pop(0) if with_norm else None
    w_ref = refs.pop(0)
    wg_ref = refs.pop(0) if with_gates else None
    lbl_ref = refs.pop(0) if "hgate" in kinds else None
    gb_ref = refs.pop(0) if with_gates else None
    o_ref = refs.pop(0)
    gate_refs = [refs.pop(0) for _ in range(6)] if with_gates else None
    hn_ref = refs.pop(0) if with_norm else x_ref
    m_ref = refs.pop(0) if with_gates else None
    mm = _dot_nt if transposed else functools.partial(jnp.dot, preferred_element_type=F32)
    tm, tn = o_ref.shape
    i, j = pl.program_id(0), pl.program_id(1)

    starts, c0 = [], 0
    for kind, width in spans:
        starts.append((kind, c0, width))
        c0 += width
    n = c0

    if lbl_ref is not None:
        lg = lbl_ref[...]
        e = jnp.exp(lg - jnp.max(lg, axis=0, keepdims=True))
        sm = e / jnp.sum(e, axis=0, keepdims=True)
        lb = jnp.sum(sm[:layer + 1], axis=0, keepdims=True)

    def activate(kind, r, c0):
        if kind == "silu":
            return _silu(r)
        if kind == "sigmoid":
            return _sigmoid(r)
        if kind == "hgate":
            lbp = lb[:, c0:c0 + r.shape[1]]
            return lbp + (1.0 - lbp) * _sigmoid(r)
        return r

    def run_tile(jt, blocked):
        tile_c0 = jt * tn
        pieces = []
        for kind, c0, width in starts:
            lo, hi = max(c0, tile_c0), min(c0 + width, tile_c0 + tn)
            for p0 in range(lo, hi, COL_PIECE if blocked else hi - lo):
                pieces.append((kind, p0 - tile_c0, min(COL_PIECE, hi - p0) if blocked else hi - lo,
                               p0 - c0))
        def gates():
            @pl.when(i % seq_blocks == 0)
            def _():
                m_ref[...] = jnp.zeros_like(m_ref)
            t = mm(hn_ref[...], wg_ref[...]) + gb_ref[...]
            m_ref[...] = _gate_terms(t, m_ref[...], *gate_refs)

        if with_gates and jt == 0 and not with_norm:
            gates()
        for r0 in range(0, tm, SUB_ROWS if blocked else tm):
            rows = slice(r0, r0 + (SUB_ROWS if blocked else tm))
            if with_norm and jt == 0:
                x = x_ref[rows, :]
                ms = jnp.mean(x * x, axis=-1, keepdims=True)
                hn = (x * lax.rsqrt(ms + EPS) * g_ref[...]).astype(BF16)
                hn_ref[rows, :] = hn
            else:
                hn = hn_ref[rows, :]
            for kind, off, width, c0 in pieces:
                wp = w_ref[off:off + width, :] if transposed else w_ref[:, off:off + width]
                o_ref[rows, off:off + width] = activate(kind, mm(hn, wp), c0)
        if with_gates and jt == 0 and with_norm:
            gates()

    if kinds == {"id"}:
        pl.when(j == 0)(functools.partial(run_tile, 0, with_norm))
        pl.when(j != 0)(functools.partial(run_tile, 1, False))
    else:
        for jt in range(n // tn):
            pl.when(j == jt)(functools.partial(run_tile, jt, True))


IN_PROJ_ROWS = 1024
IN_PROJ_VMEM = 46 * 1024 * 1024


def _in_proj_cols(n, tm, d, gw, x_bytes, with_norm):
    for tn in range(n, 0, -512):
        need = (2 * tm * d * x_bytes + (tm * d * 2 if with_norm else 0) + 2 * d * tn * 2
                + 2 * tm * tn * 4 + 2 * (d * gw * 2 + tm * gw * 4))
        if n % tn == 0 and need <= IN_PROJ_VMEM:
            return tn
    raise ValueError("no column tile fits")


def _in_proj(x, g, w, wg=None, gate_bias=None, *, spans, seq_rows=None, transposed=False,
             lb_logits=None, layer=0, casts=(), tm=IN_PROJ_ROWS):
    m, d = x.shape
    n = sum(width for _, width in spans)
    with_gates = wg is not None
    with_norm = g is not None
    gw = wg.shape[0 if transposed else 1] if with_gates else 0
    w_spec = (lambda cols, idx: pl.BlockSpec((cols, d), lambda i, j: (idx(j), 0)) if transposed
              else pl.BlockSpec((d, cols), lambda i, j: (0, idx(j))))
    const = lambda a: pl.BlockSpec(a.shape, lambda i, j: (0,) * a.ndim)
    tn = _in_proj_cols(n, tm, d, gw, x.dtype.itemsize, with_norm)
    in_specs = [pl.BlockSpec((tm, d), lambda i, j: (i, 0))]
    args = [x]
    if with_norm:
        args.append(g.reshape(1, d))
        in_specs.append(const(args[-1]))
    in_specs.append(w_spec(tn, lambda j: j))
    args.append(w)
    out_shape = [jax.ShapeDtypeStruct((m, n), F32)]
    out_specs = [pl.BlockSpec((tm, tn), lambda i, j: (i, j))]
    scratch = [pltpu.VMEM((tm, d), BF16)] if with_norm else []
    if with_gates:
        in_specs.append(w_spec(gw, lambda j: 0))
        args.append(wg)
    if lb_logits is not None:
        args.append(lb_logits)
        in_specs.append(const(lb_logits))
    if with_gates:
        assert gw == 2 * LANES and seq_rows % tm == 0 and tm % ML_CHUNK == 0
        spb = seq_rows // tm
        args.append(gate_bias)
        in_specs.append(const(gate_bias))
        col = pl.BlockSpec((tm, LANES), lambda i, j: (i, 0))
        out_specs += [col] * 4 + [pl.BlockSpec((tm // ML_CHUNK, LANES), lambda i, j: (i, 0)),
                                  pl.BlockSpec((1, LANES, tm), lambda i, j: (i // spb, 0, i % spb))]
        out_shape += ([jax.ShapeDtypeStruct((m, LANES), F32)] * 4
                      + [jax.ShapeDtypeStruct((m // ML_CHUNK, LANES), F32),
                         jax.ShapeDtypeStruct((m // seq_rows, LANES, seq_rows), F32)])
        scratch.append(pltpu.VMEM((1, LANES), F32))
    grid = (m // tm, n // tn)
    body = functools.partial(_in_proj_kernel, with_norm=with_norm, with_gates=with_gates,
                             transposed=transposed, spans=tuple(spans), layer=layer,
                             seq_blocks=seq_rows // tm if with_gates else 1)
    body, c_in, c_out, c_shapes = _with_casts(body, len(args), len(out_shape), casts, grid)
    res = pl.pallas_call(
        body,
        grid=grid,
        in_specs=in_specs + c_in,
        out_specs=out_specs + c_out,
        out_shape=out_shape + c_shapes,
        scratch_shapes=scratch,
        compiler_params=pltpu.CompilerParams(
            dimension_semantics=("arbitrary" if with_gates else "parallel", "arbitrary"),
            vmem_limit_bytes=VMEM_LIMIT),
        name="in_proj_gates" if with_gates else "in_proj",
    )(*args, *casts)
    return res if with_gates or casts else res[0]


HG_CHUNK = 128
HG_GROUPS = HG_CHUNK // SUBLANES
HG_INFLIGHT = 8


def _hgrn2_kernel(q_ref, f_ref, i_ref, z_ref, an_ref, o_ref,
                  st_ref, qs_ref, ks_ref, bs_ref, od_ref, tri_ref):
    s_len = q_ref.shape[1]
    dk = q_ref.shape[2]
    c = HG_CHUNK
    ng = HG_GROUPS
    assert c == dk == LANES and i_ref.shape[2] == dk

    an = an_ref[...]

    rowc = lax.broadcasted_iota(jnp.int32, (c, c), 0)
    colc = lax.broadcasted_iota(jnp.int32, (c, c), 1)
    tri_ref[...] = (colc <= rowc).astype(BF16)
    same_group = (colc % ng) == (rowc % ng)
    lane = lax.broadcasted_iota(jnp.int32, (SUBLANES, c), 1)
    diag_off = lane - lax.broadcasted_iota(jnp.int32, (SUBLANES, c), 0)
    zeros8 = jnp.zeros((SUBLANES, dk), F32)
    zeros16 = jnp.zeros((ng, dk), F32)

    st_ref[...] = jnp.zeros_like(st_ref)

    def rows8(x, r):
        return x[r * SUBLANES:(r + 1) * SUBLANES, :]

    def chunk(ci, slot):
        qs, ks, bs, od = qs_ref.at[slot], ks_ref.at[slot], bs_ref.at[slot], od_ref.at[slot]
        r0 = pl.multiple_of(ci * c, c)
        q = q_ref[0, pl.ds(r0, c), :]
        f = f_ref[0, pl.ds(r0, c), :]
        v = i_ref[0, pl.ds(r0, c), :]
        k = 1.0 - f

        lf = jnp.log2(f)
        hi = lf.astype(BF16)
        r1 = lf - hi.astype(F32)
        mid = r1.astype(BF16)
        lo = (r1 - mid.astype(F32)).astype(BF16)
        bb = jnp.dot(tri_ref[...], jnp.concatenate([hi, mid, lo], axis=1),
                     preferred_element_type=F32)
        yield
        b2 = bb[:, :dk] + bb[:, dk:2 * dk] + bb[:, 2 * dk:]
        qs[...] = q
        ks[...] = k
        bs[...] = b2
        b_last = bs[c - 1:c, :]
        upd = _dot_tn(v, k * jnp.exp2(b_last - b2))

        ps, rights = [], []
        for j in range(ng.bit_length() - 1):
            lhs, rhs, rights_j = [], [], []
            for r in range(ng):
                mid_grp = ((r >> (j + 1)) << (j + 1)) | ((1 << j) - 1)
                beta = bs[mid_grp * SUBLANES + SUBLANES - 1:(mid_grp + 1) * SUBLANES, :]
                if (r >> j) & 1:
                    lhs.append(rows8(q, r) * jnp.exp2(rows8(b2, r) - beta))
                    rhs.append(zeros8)
                    rights_j.append(r)
                else:
                    rhs.append(rows8(k, r) * jnp.exp2(beta - rows8(b2, r)))
            ps.append(_dot_nt(jnp.concatenate(lhs, axis=0), jnp.concatenate(rhs, axis=0)))
            rights.append(rights_j)
        yield

        st = st_ref[...]
        o = _dot_nt(q * jnp.exp2(b2), st)
        st_ref[...] = st * jnp.exp2(b_last) + upd

        def permuted(ref):
            return [ref[pl.ds(pos, ng, stride=SUBLANES), :] for pos in range(SUBLANES)]

        q_p, k_p, b_p = permuted(qs), permuted(ks), permuted(bs)
        lhs_slots, rhs_slots = [], []
        for j in range(SUBLANES.bit_length() - 1):
            for blk in range(SUBLANES >> (j + 1)):
                mid_pos = (blk << (j + 1)) | ((1 << j) - 1)
                lhs, rhs = [], []
                for pos in range(SUBLANES):
                    if pos >> (j + 1) != blk:
                        lhs.append(zeros16)
                        rhs.append(zeros16)
                    elif (pos >> j) & 1:
                        lhs.append(q_p[pos] * jnp.exp2(b_p[pos] - b_p[mid_pos]))
                        rhs.append(zeros16)
                    else:
                        lhs.append(zeros16)
                        rhs.append(k_p[pos] * jnp.exp2(b_p[mid_pos] - b_p[pos]))
                lhs_slots.append(jnp.concatenate(lhs, axis=0).astype(BF16))
                rhs_slots.append(jnp.concatenate(rhs, axis=0).astype(BF16))
        pd = _dot_nt(jnp.concatenate(lhs_slots, axis=1), jnp.concatenate(rhs_slots, axis=1))
        yield

        dg = jnp.sum(q * k, axis=1, keepdims=True)
        a_rows = [jnp.where(diag_off == r * SUBLANES, rows8(dg, r), 0.0) for r in range(ng)]
        for j, (p, rights_j) in enumerate(zip(ps, rights)):
            for i, r in enumerate(rights_j):
                right_start = (((r >> (j + 1)) << (j + 1)) | (1 << j)) * SUBLANES
                a_rows[r] = jnp.where(lane < right_start, rows8(p, i), a_rows[r])
        o = o + _dot(jnp.concatenate(a_rows, axis=0), v)
        v_p = [i_ref[0, pl.ds(r0 + pos, ng, stride=SUBLANES), :] for pos in range(SUBLANES)]
        od[...] = _dot(jnp.where(same_group, pd, 0.0), jnp.concatenate(v_p, axis=0))
        yield

        o = o + jnp.concatenate(
            [od[pl.ds(g, SUBLANES, stride=ng), :] for g in range(ng)], axis=0)
        yn = o * lax.rsqrt(jnp.mean(o * o, axis=-1, keepdims=True) + EPS) * an
        z = z_ref[0, pl.ds(r0, c), :]
        o_ref[0, pl.ds(r0, c), :] = (yn * z).astype(o_ref.dtype)

    def chunks(i, carry):
        stages = [chunk(i * HG_INFLIGHT + slot, slot) for slot in range(HG_INFLIGHT)]
        while stages:
            stages = [g for g in stages if next(g, stages) is not stages]
        return carry

    lax.fori_loop(0, s_len // (c * HG_INFLIGHT), chunks, 0)


def _hgrn2(u, a_norm, *, heads, dk, casts=()):
    bsz, s_len, _ = u.shape
    blk = lambda off: pl.BlockSpec((1, s_len, dk), lambda b, h, off=off: (b, 0, off + h))
    grid = (bsz, heads)
    body, c_in, c_out, c_shapes = _with_casts(_hgrn2_kernel, 5, 1, casts, grid)
    return pl.pallas_call(
        body,
        grid=grid,
        in_specs=[blk(0), blk(heads), blk(2 * heads), blk(3 * heads),
                  pl.BlockSpec((1, dk), lambda b, h: (0, h))] + c_in,
        out_specs=[pl.BlockSpec((1, s_len, dk), lambda b, h: (b, 0, h))] + c_out,
        out_shape=[jax.ShapeDtypeStruct((bsz, s_len, heads * dk), BF16)] + c_shapes,
        scratch_shapes=[pltpu.VMEM((dk, dk), F32)]
        + [pltpu.VMEM((HG_INFLIGHT, HG_CHUNK, dk), F32)] * 4
        + [pltpu.VMEM((HG_CHUNK, HG_CHUNK), BF16)],
        compiler_params=pltpu.CompilerParams(
            dimension_semantics=("parallel", "parallel"), vmem_limit_bytes=VMEM_LIMIT),
        name="hgrn2",
    )(u, u, u, u, a_norm.reshape(1, -1), *casts)


def _rglru_kernel(x_ref, z_ref, cw_ref, cb_ref, wr_ref, br_ref, wi_ref, bi_ref, lam_ref,
                  o_ref, a_ref, u_ref, h_ref, ca_ref, cu_ref):
    s_len = x_ref.shape[1]
    w = x_ref.shape[2]
    ng = s_len // SUBLANES
    xc = _causal_conv(x_ref, cw_ref, cb_ref)
    r = _sigmoid(_dot(xc, wr_ref[0]) + br_ref[...])
    ig = _sigmoid(_dot(xc, wi_ref[0]) + bi_ref[...])
    log_a = -LRU_C * r * _softplus(-lam_ref[...])
    a = jnp.exp(log_a)
    a_ref[...] = a
    t = -jnp.tanh(log_a) * (a * a + 1.0)
    u_ref[...] = jnp.where(t > 0.0, t * lax.rsqrt(t), 0.0) * (ig * xc)

    ca = cu = None
    for pos in range(SUBLANES):
        ap = a_ref[pl.ds(pos, ng, stride=SUBLANES), :]
        up = u_ref[pl.ds(pos, ng, stride=SUBLANES), :]
        ca, cu = (ap, up) if pos == 0 else (ap * ca, ap * cu + up)
        ca_ref[pos] = ca
        cu_ref[pos] = cu
    grp = lax.broadcasted_iota(jnp.int32, (ng, w), 0)
    ta, tu = ca, cu
    s = 1
    while s < ng:
        m = grp >= s
        tu = tu + ta * jnp.where(m, pltpu.roll(tu, s, 0), 0.0)
        ta = ta * jnp.where(m, pltpu.roll(ta, s, 0), 1.0)
        s *= 2
    h_in = jnp.where(grp >= 1, pltpu.roll(tu, 1, 0), 0.0)
    for pos in range(SUBLANES):
        h_ref[pl.ds(pos, ng, stride=SUBLANES), :] = ca_ref[pos] * h_in + cu_ref[pos]
    o_ref[0] = (h_ref[...] * z_ref[0]).astype(o_ref.dtype)


def _rglru(u, conv_w, conv_b, w_r, b_r, w_i, b_i, lam, *, x_off, z_off, casts=()):
    bsz, s_len, _ = u.shape
    nblk, blk, _ = w_r.shape
    vec = lambda rows: pl.BlockSpec((rows, blk), lambda b, n: (0, n))
    mat = pl.BlockSpec((1, blk, blk), lambda b, n: (n, 0, 0))
    grid = (bsz, nblk)
    body, c_in, c_out, c_shapes = _with_casts(_rglru_kernel, 9, 1, casts, grid)
    return pl.pallas_call(
        body,
        grid=grid,
        in_specs=[pl.BlockSpec((1, s_len, blk), lambda b, n: (b, 0, x_off + n)),
                  pl.BlockSpec((1, s_len, blk), lambda b, n: (b, 0, z_off + n)),
                  vec(CONV_W), vec(1), mat, vec(1), mat, vec(1), vec(1)] + c_in,
        out_specs=[pl.BlockSpec((1, s_len, blk), lambda b, n: (b, 0, n))] + c_out,
        out_shape=[jax.ShapeDtypeStruct((bsz, s_len, nblk * blk), BF16)] + c_shapes,
        scratch_shapes=[pltpu.VMEM((s_len, blk), F32)] * 3
        + [pltpu.VMEM((SUBLANES, s_len // SUBLANES, blk), F32)] * 2,
        compiler_params=pltpu.CompilerParams(
            dimension_semantics=("parallel", "parallel"), vmem_limit_bytes=VMEM_LIMIT),
        name="rglru",
    )(u, u, conv_w, conv_b.reshape(1, -1), w_r, b_r.reshape(1, -1), w_i, b_i.reshape(1, -1),
      lam.reshape(1, -1), *casts)


ML_CHUNK = 128
ML_UNROLL = 8


def _mlstm_kernel(q_ref, k_ref, v_ref, og_ref, z_ref, mu_ref, gi_ref, en_ref, ks_ref, dec_ref,
                  rt_ref, cwq_ref, cbq_ref, cwk_ref, cbk_ref, cn_ref, o_ref, qs_ref, ks_s_ref,
                  cst_ref, nst_ref):
    s_len = q_ref.shape[1]
    dk = q_ref.shape[2]
    c = ML_CHUNK
    assert c == dk == LANES
    h = pl.program_id(1)

    qs_ref[...] = _silu(_causal_conv(q_ref, cwq_ref, cbq_ref)) * (dk ** -0.5)
    ks_s_ref[...] = _silu(_causal_conv(k_ref, cwk_ref, cbk_ref))
    cst_ref[...] = jnp.zeros_like(cst_ref)
    nst_ref[...] = jnp.zeros_like(nst_ref)

    row = lax.broadcasted_iota(jnp.int32, (c, c), 0)
    col = lax.broadcasted_iota(jnp.int32, (c, c), 1)
    causal = col <= row
    head = col == h
    cn = cn_ref[...]

    def pick(x):
        return jnp.sum(jnp.where(head, x, 0.0), axis=1, keepdims=True)

    dec_all = dec_ref[0]
    chunk_id = lax.broadcasted_iota(jnp.int32, dec_all.shape, 0)
    head_lane = lax.broadcasted_iota(jnp.int32, (1, LANES), 1) == h
    head_row = lax.broadcasted_iota(jnp.int32, (SUBLANES, c), 0) == h

    def chunk(ci, carry):
        r0 = pl.multiple_of(ci * c, c)
        q = qs_ref[pl.ds(r0, c), :]
        k = ks_s_ref[pl.ds(r0, c), :]
        v = v_ref[0, pl.ds(r0, c), :]
        mu = pick(mu_ref[0, pl.ds(r0, c), :])
        g_inter = pick(gi_ref[0, pl.ds(r0, c), :])
        e_negm = pick(en_ref[0, pl.ds(r0, c), :])
        kscale = pick(ks_ref[0, pl.ds(r0, c), :])
        decay = jnp.sum(jnp.where(chunk_id == ci, dec_all, 0.0), axis=0, keepdims=True)
        decay = jnp.sum(jnp.where(head_lane, decay, 0.0), axis=1, keepdims=True)
        r_row = jnp.sum(jnp.where(head_row, rt_ref[0, :, pl.ds(r0, c)], 0.0),
                        axis=0, keepdims=True)
        wts = jnp.where(causal, jnp.exp(jnp.minimum(r_row - mu, 0.0)), 0.0)
        qk = _dot_nt(q, k)
        kw = k * kscale
        upd = _dot_tn(kw, v)

        cst = cst_ref[...]
        n_st = nst_ref[...]
        sc = qk * wts
        inter = _dot(q, cst)
        intra = _dot(sc, v)
        cst_ref[...] = decay * cst + upd
        nst_ref[...] = decay * n_st + jnp.sum(kw, axis=0, keepdims=True)

        num = g_inter * inter + intra
        den = (g_inter * jnp.sum(q * n_st, axis=1, keepdims=True)
               + jnp.sum(sc, axis=1, keepdims=True))
        inv = 1.0 / jnp.maximum(jnp.abs(den), e_negm)
        msq = jnp.mean(num * num, axis=-1, keepdims=True)
        yn = num * (inv * lax.rsqrt(inv * inv * msq + EPS)) * cn
        og = og_ref[0, pl.ds(r0, c), :]
        z = z_ref[0, pl.ds(r0, c), :]
        o_ref[0, pl.ds(r0, c), :] = (yn * _sigmoid(og) * _silu(z)).astype(o_ref.dtype)
        return carry

    lax.fori_loop(0, s_len // c, chunk, 0, unroll=ML_UNROLL)


def _mlstm(u, gate_terms, conv_w, conv_b, c_norm, *, heads, dk, dv, casts=()):
    bsz, s_len, _ = u.shape
    grid = (bsz, heads)
    body, c_in, c_out, c_shapes = _with_casts(_mlstm_kernel, 16, 1, casts, grid)
    nc = s_len // ML_CHUNK
    col_spec = pl.BlockSpec((1, s_len, LANES), lambda b, h: (b, 0, 0))
    qk_blk = lambda off: pl.BlockSpec((1, s_len, dk), lambda b, h, off=off: (b, 0, off + h))
    v_blk = lambda off: pl.BlockSpec((1, s_len, dv), lambda b, h, off=off: (b, 0, off + h))
    cw = lambda off: pl.BlockSpec((CONV_W, dk), lambda b, h, off=off: (0, off + h))
    cb = lambda off: pl.BlockSpec((1, dk), lambda b, h, off=off: (0, off + h))
    nqk = 2 * heads * dk // dv
    return pl.pallas_call(
        body,
        grid=grid,
        in_specs=[qk_blk(0), qk_blk(heads),
                  v_blk(nqk), v_blk(nqk + heads), v_blk(nqk + 2 * heads),
                  col_spec, col_spec, col_spec, col_spec,
                  pl.BlockSpec((1, nc, LANES), lambda b, h: (b, 0, 0)),
                  pl.BlockSpec((1, SUBLANES, s_len), lambda b, h: (b, 0, 0)),
                  cw(0), cb(0), cw(heads), cb(heads),
                  pl.BlockSpec((1, dv), lambda b, h: (0, h))] + c_in,
        out_specs=[pl.BlockSpec((1, s_len, dv), lambda b, h: (b, 0, h))] + c_out,
        out_shape=[jax.ShapeDtypeStruct((bsz, s_len, heads * dv), BF16)] + c_shapes,
        scratch_shapes=[pltpu.VMEM((s_len, dk), F32), pltpu.VMEM((s_len, dk), F32),
                        pltpu.VMEM((dk, dv), F32), pltpu.VMEM((1, dk), F32)],
        compiler_params=pltpu.CompilerParams(
            dimension_semantics=("parallel", "parallel"), vmem_limit_bytes=VMEM_LIMIT),
        name="mlstm",
    )(u, u, u, u, u, *gate_terms, conv_w, conv_b.reshape(1, -1), conv_w,
      conv_b.reshape(1, -1), c_norm.reshape(1, -1), *casts)


OUT_SUB_ROWS = 256


def _out_kernel(h_ref, *rest, n_y, final):
    y_refs = rest[:n_y]
    wo_refs = rest[n_y:2 * n_y]
    p_ref, pw_ref, pn_ref, gw_ref, norm_ref = rest[2 * n_y:2 * n_y + 5]
    out_refs = rest[2 * n_y + 5:]

    def rows_block(r0):
        rows = slice(r0, r0 + OUT_SUB_ROWS)
        mix = jnp.dot(y_refs[0][rows, :], wo_refs[0][...], preferred_element_type=F32)
        for y_ref, wo_ref in zip(y_refs[1:], wo_refs[1:]):
            mix = mix + jnp.dot(y_ref[rows, :], wo_ref[...], preferred_element_type=F32)
        yield
        h1 = h_ref[rows, :] + mix
        pe = _dot(p_ref[rows, :], pw_ref[...])
        gate = _dot(h1, gw_ref[...])
        yield
        pe = pe * lax.rsqrt(jnp.mean(pe * pe, axis=-1, keepdims=True) + EPS) * pn_ref[...]
        h2 = h1 + _sigmoid(gate) * pe
        hn = h2 * lax.rsqrt(jnp.mean(h2 * h2, axis=-1, keepdims=True) + EPS) * norm_ref[...]
        if final:
            out_refs[0][rows, :] = hn
        else:
            out_refs[0][rows, :] = h2
            out_refs[1][rows, :] = hn.astype(out_refs[1].dtype)

    stages = [rows_block(r0) for r0 in range(0, h_ref.shape[0], OUT_SUB_ROWS)]
    while stages:
        stages = [g for g in stages if next(g, stages) is not stages]


def _out_proj(h, ys, w_out, p, layer, ple_w, ple_norm, gate_w, norm, final, *, tm=512):
    m, d = h.shape
    const = lambda shape: pl.BlockSpec(shape, lambda i: (0, 0), pipeline_mode=pl.Buffered(1))
    in_specs = [pl.BlockSpec((tm, d), lambda i: (i, 0))]
    in_specs += [pl.BlockSpec((tm, y.shape[1]), lambda i: (i, 0)) for y in ys]
    assert all(y.shape[1] == ys[0].shape[1] for y in ys)
    in_specs += [pl.BlockSpec((y.shape[1], d), lambda i, r=r: (r, 0), pipeline_mode=pl.Buffered(1))
                 for r, y in enumerate(ys)]
    stacked = lambda a: pl.BlockSpec((None,) + a.shape[1:], lambda i: (layer, 0, 0),
                                     pipeline_mode=pl.Buffered(1))
    in_specs += [pl.BlockSpec((None, tm, p.shape[2]), lambda i: (layer, i, 0)),
                 stacked(ple_w), const((1, d)), stacked(gate_w), const((1, d))]
    args = [h, *ys, *([w_out] * len(ys)), p, ple_w, ple_norm.reshape(1, d), gate_w,
            norm.reshape(1, d)]
    row_spec = pl.BlockSpec((tm, d), lambda i: (i, 0))
    return pl.pallas_call(
        functools.partial(_out_kernel, n_y=len(ys), final=final),
        grid=(m // tm,),
        in_specs=in_specs,
        out_specs=row_spec if final else [row_spec, row_spec],
        out_shape=(jax.ShapeDtypeStruct((m, d), F32) if final else
                   [jax.ShapeDtypeStruct((m, d), F32), jax.ShapeDtypeStruct((m, d), BF16)]),
        compiler_params=pltpu.CompilerParams(
            dimension_semantics=("parallel",), vmem_limit_bytes=VMEM_LIMIT),
        name="out_proj_final" if final else "out_proj",
    )(*args)


def kernel(x, p, e_norm, e_w_in, a_lb_logits, a_norm, b_conv_w, b_conv_b, b_w_r, b_b_r, b_w_i, b_b_i, b_lambda, e_w_out, o_norm, o_w_in, c_conv_w, c_conv_b, c_b_i, c_b_f, c_norm, o_w_out, ple_w, ple_norm, ple_gate_w, final_norm):
    bsz, s_len, d = x.shape
    depth = p.shape[0]
    m = bsz * s_len
    a_width = a_norm.shape[1]
    b_width = b_lambda.shape[1]
    a_dk = LANES
    a_heads = a_width // a_dk
    c_heads = c_b_i.shape[1]
    c_dv = c_norm.shape[1] // c_heads
    c_dk = c_conv_w.shape[2] // (2 * c_heads)
    c_main = o_w_in.shape[2] - 2 * c_heads

    h = x.reshape(m, d)
    ready = {}
    take = lambda name, w: ready.pop(name) if name in ready else w.astype(BF16)
    w_in_t = lambda j: o_w_in[j].T
    hn = None
    layer_norm = lambda i: e_norm[i // 2] if i % 2 == 0 else o_norm[i // 2]
    for i in range(depth):
        j = i // 2
        last = i == depth - 1
        xin, g = (h, layer_norm(i)) if hn is None else (hn, None)
        if i % 2 == 0:
            a_qk = a_heads * a_dk
            spans = (("silu", a_qk), ("hgate", a_qk), ("id", a_width), ("silu", a_width),
                     ("id", b_width), ("silu", b_width))
            jobs = {("e_w_out", j): e_w_out[j]}
            if i == 0:
                jobs.update(ple_w=ple_w.reshape(-1, d),
                            gate_w=ple_gate_w.reshape(depth * d, d))
            if not last:
                jobs[("o_w_out", j)] = o_w_out[j]
            u, *done = _in_proj(xin, g, take(("e_w_in", j), e_w_in[j]), spans=spans,
                                lb_logits=a_lb_logits, layer=j, casts=list(jobs.values()))
            ready.update(zip(jobs, done))
            u = u.reshape(bsz, s_len, -1)
            jobs = {("o_w_in", j): w_in_t(j)} if not last else {}
            ya, *done = _hgrn2(u, a_norm[j], heads=a_heads, dk=a_dk, casts=list(jobs.values()))
            ready.update(zip(jobs, done))
            x_off = (2 * a_heads * a_dk + 2 * a_width) // LANES
            (yb,) = _rglru(u, b_conv_w[j], b_conv_b[j], b_w_r[j].astype(BF16), b_b_r[j],
                           b_w_i[j].astype(BF16), b_b_i[j], b_lambda[j], x_off=x_off,
                           z_off=x_off + b_width // LANES)
            if i == 0:
                gate_w_bf = ready.pop("gate_w").reshape(depth, d, d)
                ple_w_bf = ready.pop("ple_w").reshape(ple_w.shape)
            w_out = take(("e_w_out", j), e_w_out[j])
            ys = [ya.reshape(m, a_width), yb.reshape(m, b_width)]
        else:
            pad = (0, LANES - c_heads)
            gate_rows = lax.optimization_barrier(o_w_in[j][:, c_main:]).T
            wg_t = jnp.concatenate([jnp.pad(gate_rows[:c_heads], (pad, (0, 0))),
                                    jnp.pad(gate_rows[c_heads:], (pad, (0, 0)))]).astype(BF16)
            gate_bias = jnp.concatenate([jnp.pad(c_b_i[j], pad),
                                         jnp.pad(c_b_f[j], pad)]).reshape(1, 2 * LANES)
            u, mu, gi, en, ks, dec, rt = _in_proj(
                xin, g, take(("o_w_in", j), w_in_t(j)[:c_main]), wg_t, gate_bias,
                spans=(("id", c_main),), seq_rows=s_len, transposed=True)
            col = lambda a: a.reshape(bsz, s_len, LANES)
            gate_terms = (col(mu), col(gi), col(en), col(ks), dec.reshape(bsz, -1, LANES), rt)
            jobs = {} if ("o_w_out", j) in ready else {("o_w_out", j): o_w_out[j]}
            if not last:
                jobs[("e_w_in", j + 1)] = e_w_in[j + 1]
            yc, *done = _mlstm(u.reshape(bsz, s_len, -1), gate_terms, c_conv_w[j], c_conv_b[j],
                               c_norm[j], heads=c_heads, dk=c_dk, dv=c_dv,
                               casts=list(jobs.values()))
            ready.update(zip(jobs, done))
            ys = [yc.reshape(m, -1)]
            w_out = take(("o_w_out", j), o_w_out[j])
        res = _out_proj(h, ys, w_out, p.reshape(depth, m, -1), i, ple_w_bf, ple_norm[i],
                        gate_w_bf, final_norm if last else layer_norm(i + 1), last)
        h, hn = (res, None) if last else res
    return h.reshape(bsz, s_len, d)
```
